```python
import jax, jax.numpy as jnp
from jax import lax
import numpy as np

D_MODEL = 1024
BATCH = 8
SEQ = 2048
DEPTH = 1
DEC_BATCH = 128
DEC_SEQ = 8
PAST_LEN = 16384
PAGE_SIZE = 128

RET_H = 4
RET_DK = 64
RET_DV = 128
DN_H = 4
DN_DK = 128
DN_DV = 128
CONV_W = 4
CHUNK = 64
ROPE_BASE = 10000.0
N_GROUPS = 4
EXPERTS_PER_GROUP = 4
N_EXPERTS = N_GROUPS * EXPERTS_PER_GROUP
TOP_K = 2
D_EXPERT = 256
PLE_DIM = 256
EPS = 1e-6

RET_QK = RET_H * RET_DK
RET_VW = RET_H * RET_DV
DN_QK = DN_H * DN_DK
DN_VW = DN_H * DN_DV
DN_CONV_CH = 2 * DN_QK + DN_VW
IN_WIDTHS = (RET_QK, RET_QK, RET_VW, RET_VW, DN_CONV_CH, DN_VW, DN_H, DN_H, D_MODEL, D_MODEL)
IN_COLS = sum(IN_WIDTHS)

kernel_name = "hybrid_retention_gdn_hmoe_step"


def _split_points():
    pts, acc = [], 0
    for w in IN_WIDTHS[:-1]:
        acc += w
        pts.append(acc)
    return pts


def rmsnorm(x, g):
    xf = x.astype(jnp.float32)
    y = xf * lax.rsqrt(jnp.mean(xf * xf, axis=-1, keepdims=True) + EPS)
    return (y * g.astype(jnp.float32)).astype(x.dtype)


def rms_plain(xf):
    return xf * lax.rsqrt(jnp.mean(xf * xf, axis=-1, keepdims=True) + EPS)


def l2norm(xf):
    return xf * lax.rsqrt(jnp.sum(xf * xf, axis=-1, keepdims=True) + EPS)


def rotary(x, pos):
    half = x.shape[-1] // 2
    inv = 1.0 / (ROPE_BASE ** (jnp.arange(half, dtype=jnp.float32) / half))
    ang = pos.astype(jnp.float32)[:, None] * inv[None, :]
    cos, sin = jnp.cos(ang), jnp.sin(ang)
    x1, x2 = x[..., :half], x[..., half:]
    return jnp.concatenate([x1 * cos - x2 * sin, x1 * sin + x2 * cos], axis=-1)


def chunk_len(L):
    return CHUNK if L % CHUNK == 0 else L


def to_chunks(x, c):
    B, H, L = x.shape[:3]
    x = x.reshape((B, H, L // c, c) + x.shape[3:])
    return jnp.moveaxis(x, 2, 0)


def from_chunks(x):
    x = jnp.moveaxis(x, 0, 2)
    return x.reshape((x.shape[0], x.shape[1], -1) + x.shape[4:])


def retention(q, k, v, log_gamma, s0):
    c = chunk_len(q.shape[2])
    idx = jnp.arange(c, dtype=jnp.float32)
    diff = idx[:, None] - idx[None, :]
    causal = diff >= 0
    lg = log_gamma[:, None, None]
    decay = jnp.where(causal, jnp.exp(jnp.where(causal, diff, 0.0) * lg), 0.0)
    q_dec = jnp.exp((idx + 1.0)[None, :] * log_gamma[:, None])
    k_dec = jnp.exp((c - 1.0 - idx)[None, :] * log_gamma[:, None])
    chunk_dec = jnp.exp(c * log_gamma)

    def step(s, inp):
        qc, kc, vc = inp
        scores = jnp.einsum('bhid,bhjd->bhij', qc, kc) * decay
        o = (jnp.einsum('bhij,bhjv->bhiv', scores, vc)
             + jnp.einsum('bhid,bhdv->bhiv', qc * q_dec[..., None], s))
        s = s * chunk_dec[:, None, None] + jnp.einsum('bhjd,bhjv->bhdv', kc * k_dec[..., None], vc)
        return s, o

    s, o = lax.scan(step, s0, (to_chunks(q, c), to_chunks(k, c), to_chunks(v, c)))
    return from_chunks(o), s


def gated_delta(q, k, v, g, beta, s0):
    c = chunk_len(q.shape[2])
    dv = v.shape[-1]
    incl = jnp.tril(jnp.ones((c, c), dtype=bool))
    strict = jnp.tril(jnp.ones((c, c), dtype=bool), k=-1)
    eye = jnp.eye(c, dtype=jnp.float32)

    def step(s, inp):
        qc, kc, vc, gc, bc = inp
        G = jnp.cumsum(gc, axis=-1)
        diff = G[..., :, None] - G[..., None, :]
        dec = jnp.where(incl, jnp.exp(jnp.where(incl, diff, 0.0)), 0.0)
        kk = jnp.einsum('bhid,bhjd->bhij', kc, kc)
        a_mat = jnp.where(strict, kk * dec * bc[..., :, None], 0.0)
        eG = jnp.exp(G)[..., None]
        rhs = jnp.concatenate([bc[..., None] * vc, bc[..., None] * eG * kc], axis=-1)
        sol = lax.linalg.triangular_solve(eye + a_mat, rhs, left_side=True, lower=True,
                                          unit_diagonal=True)
        u_mat, w_mat = sol[..., :dv], sol[..., dv:]
        delta = u_mat - jnp.einsum('bhid,bhdv->bhiv', w_mat, s)
        qk = jnp.einsum('bhid,bhjd->bhij', qc, kc) * dec
        o = (jnp.einsum('bhid,bhdv->bhiv', qc * eG, s)
             + jnp.einsum('bhij,bhjv->bhiv', qk, delta))
        g_last = G[..., -1:]
        s = (s * jnp.exp(g_last)[..., None]
             + jnp.einsum('bhjd,bhjv->bhdv', kc * jnp.exp(g_last - G)[..., None], delta))
        return s, o

    s, o = lax.scan(step, s0, (to_chunks(q, c), to_chunks(k, c), to_chunks(v, c),
                               to_chunks(g, c), to_chunks(beta, c)))
    return from_chunks(o), s


def hybrid_layer(h, p_l, s_ret, s_dn, s_conv, positions,
                 norm_mix, w_in, conv_w, dn_a_log, dn_dt_bias, dn_norm,
                 w_ret_up, w_dn_up, w_out,
                 norm_ffn, w_router_group, b_router_group, w_router_expert, b_router_expert,
                 w_gate_up, w_down, norm_ple, w_ple_gate, w_ple_proj):
    B, L, _ = h.shape
    dt = h.dtype
    f32 = jnp.float32

    def heads(t, n):
        return t.reshape(B, L, n, -1).transpose(0, 2, 1, 3).astype(f32)

    u = rmsnorm(h, norm_mix)
    proj = jnp.einsum('bld,de->ble', u, w_in)
    rq, rk, rv, rg, dqkv, dz, da, db, gate_a, gate_b = jnp.split(proj, _split_points(), axis=-1)

    q = rotary(heads(rq, RET_H), positions) * (RET_DK ** -0.5)
    k = rotary(heads(rk, RET_H), positions)
    v = heads(rv, RET_H)
    log_gamma = jnp.log(1.0 - 2.0 ** (-5.0 - jnp.arange(RET_H, dtype=f32)))
    o_r, s_ret_new = retention(q, k, v, log_gamma, s_ret.astype(f32))
    o_r = rms_plain(o_r).transpose(0, 2, 1, 3).reshape(B, L, RET_VW)
    o_r = o_r * jax.nn.silu(rg.astype(f32))
    branch_a = jnp.einsum('blv,vd->bld', o_r.astype(dt), w_ret_up)

    xpad = jnp.concatenate([s_conv.astype(dt), dqkv], axis=1)
    conv = xpad[:, 0:L] * conv_w[0]
    for j in range(1, CONV_W):
        conv = conv + xpad[:, j:j + L] * conv_w[j]
    conv = jax.nn.silu(conv.astype(f32))
    conv_new = xpad[:, L:]
    cq, ck, cv = jnp.split(conv, [DN_QK, 2 * DN_QK], axis=-1)
    qd = l2norm(heads(cq, DN_H)) * (DN_DK ** -0.5)
    kd = l2norm(heads(ck, DN_H))
    vd = heads(cv, DN_H)
    g = -jnp.exp(dn_a_log.astype(f32)) * jax.nn.softplus(da.astype(f32) + dn_dt_bias.astype(f32))
    beta = jax.nn.sigmoid(db.astype(f32))
    o_d, s_dn_new = gated_delta(qd, kd, vd, g.transpose(0, 2, 1), beta.transpose(0, 2, 1),
                                s_dn.astype(f32))
    o_d = rms_plain(o_d) * dn_norm.astype(f32)
    o_d = o_d.transpose(0, 2, 1, 3).reshape(B, L, DN_VW) * jax.nn.silu(dz.astype(f32))
    branch_b = jnp.einsum('blv,vd->bld', o_d.astype(dt), w_dn_up)

    merged = jax.nn.sigmoid(gate_a) * branch_a + jax.nn.sigmoid(gate_b) * branch_b
    h = h + jnp.einsum('bld,de->ble', merged, w_out)

    u = rmsnorm(h, norm_ffn)
    grp_logits = (jnp.einsum('bld,dg->blg', u, w_router_group) + b_router_group).astype(f32)
    grp_prob = jax.nn.softmax(grp_logits, axis=-1)
    grp = jnp.argmax(grp_logits, axis=-1)
    p_grp = jnp.take_along_axis(grp_prob, grp[..., None], axis=-1)
    exp_logits = (jnp.einsum('bld,de->ble', u, w_router_expert) + b_router_expert).astype(f32)
    exp_logits = exp_logits.reshape(B, L, N_GROUPS, EXPERTS_PER_GROUP)
    in_grp = jnp.take_along_axis(exp_logits, grp[..., None, None], axis=2)[..., 0, :]
    top_v, top_i = lax.top_k(in_grp, TOP_K)
    w_top = jax.nn.softmax(top_v, axis=-1) * p_grp
    expert_id = grp[..., None] * EXPERTS_PER_GROUP + top_i
    combine = jnp.sum(jax.nn.one_hot(expert_id, N_EXPERTS, dtype=f32) * w_top[..., None], axis=-2)
    combine = combine.astype(dt)
    moe = jnp.zeros_like(h)
    for e in range(N_EXPERTS):
        gu = jnp.einsum('bld,df->blf', u, w_gate_up[e])
        hid = jax.nn.silu(gu[..., :D_EXPERT]) * gu[..., D_EXPERT:]
        moe = moe + combine[..., e:e + 1] * jnp.einsum('blf,fd->bld', hid, w_down[e])
    h = h + moe

    u = rmsnorm(h, norm_ple)
    gate = jax.nn.sigmoid(jnp.einsum('bld,de->ble', u, w_ple_gate))
    h = h + gate * jnp.einsum('blp,pd->bld', p_l, w_ple_proj)
    return h, s_ret_new, s_dn_new, conv_new


def setup_inputs(seed: int = 0) -> dict:
    key = jax.random.key(seed)
    ks = jax.random.split(key, 32)
    f32 = jnp.float32

    def nrm(k, shape, scale):
        return jax.random.normal(k, shape, f32) * scale

    def gain(k, shape):
        return 1.0 + 0.05 * jax.random.normal(k, shape, f32)

    dt_init = jax.random.uniform(ks[11], (DEPTH, DN_H), f32, 0.001, 0.1)
    return {
        "x_prompt": nrm(ks[0], (BATCH, SEQ, D_MODEL), 1.0),
        "x_sample": nrm(ks[1], (DEC_BATCH, DEC_SEQ, D_MODEL), 1.0),
        "p_prompt": nrm(ks[2], (DEPTH, BATCH, SEQ, PLE_DIM), 1.0),
        "p_sample": nrm(ks[3], (DEPTH, DEC_BATCH, DEC_SEQ, PLE_DIM), 1.0),
        "state_ret": nrm(ks[4], (DEPTH, DEC_BATCH, RET_H, RET_DK, RET_DV), 0.5),
        "state_dn": nrm(ks[5], (DEPTH, DEC_BATCH, DN_H, DN_DK, DN_DV), 0.1),
        "state_conv": nrm(ks[6], (DEPTH, DEC_BATCH, CONV_W - 1, DN_CONV_CH), 0.5),
        "norm_mix": gain(ks[7], (DEPTH, D_MODEL)),
        "w_in": nrm(ks[8], (DEPTH, D_MODEL, IN_COLS), D_MODEL ** -0.5),
        "conv_w": nrm(ks[9], (DEPTH, CONV_W, DN_CONV_CH), CONV_W ** -0.5),
        "dn_a_log": jnp.log(jax.random.uniform(ks[10], (DEPTH, DN_H), f32, 1.0, 16.0)),
        "dn_dt_bias": jnp.log(jnp.expm1(dt_init)),
        "dn_norm": gain(ks[12], (DEPTH, DN_DV)),
        "w_ret_up": nrm(ks[13], (DEPTH, RET_VW, D_MODEL), RET_VW ** -0.5),
        "w_dn_up": nrm(ks[14], (DEPTH, DN_VW, D_MODEL), DN_VW ** -0.5),
        "w_out": nrm(ks[15], (DEPTH, D_MODEL, D_MODEL), D_MODEL ** -0.5),
        "norm_ffn": gain(ks[16], (DEPTH, D_MODEL)),
        "w_router_group": nrm(ks[17], (DEPTH, D_MODEL, N_GROUPS), D_MODEL ** -0.5),
        "b_router_group": nrm(ks[18], (DEPTH, N_GROUPS), 0.01),
        "w_router_expert": nrm(ks[19], (DEPTH, D_MODEL, N_EXPERTS), D_MODEL ** -0.5),
        "b_router_expert": nrm(ks[20], (DEPTH, N_EXPERTS), 0.01),
        "w_gate_up": nrm(ks[21], (DEPTH, N_EXPERTS, D_MODEL, 2 * D_EXPERT), D_MODEL ** -0.5),
        "w_down": nrm(ks[22], (DEPTH, N_EXPERTS, D_EXPERT, D_MODEL), D_EXPERT ** -0.5),
        "norm_ple": gain(ks[23], (DEPTH, D_MODEL)),
        "w_ple_gate": nrm(ks[24], (DEPTH, D_MODEL, D_MODEL), D_MODEL ** -0.5),
        "w_ple_proj": nrm(ks[25], (DEPTH, PLE_DIM, D_MODEL), PLE_DIM ** -0.5),
        "norm_final": gain(ks[26], (D_MODEL,)),
    }


def reference(x_prompt, x_sample, p_prompt, p_sample, state_ret, state_dn, state_conv,
              norm_mix, w_in, conv_w, dn_a_log, dn_dt_bias, dn_norm,
              w_ret_up, w_dn_up, w_out,
              norm_ffn, w_router_group, b_router_group, w_router_expert, b_router_expert,
              w_gate_up, w_down, norm_ple, w_ple_gate, w_ple_proj, norm_final):
    f32 = jnp.float32
    bp, lp = x_prompt.shape[0], x_prompt.shape[1]
    ls = x_sample.shape[1]
    pos_prompt = jnp.arange(lp, dtype=jnp.int32)
    pos_sample = PAST_LEN + jnp.arange(ls, dtype=jnp.int32)
    zero_ret = jnp.zeros((bp, RET_H, RET_DK, RET_DV), f32)
    zero_dn = jnp.zeros((bp, DN_H, DN_DK, DN_DV), f32)
    zero_conv = jnp.zeros((bp, CONV_W - 1, DN_CONV_CH), x_prompt.dtype)

    hp, hs = x_prompt, x_sample
    ret_p, dn_p, conv_p, ret_s, dn_s, conv_s = [], [], [], [], [], []
    for i in range(DEPTH):
        weights = (norm_mix[i], w_in[i], conv_w[i], dn_a_log[i], dn_dt_bias[i], dn_norm[i],
                   w_ret_up[i], w_dn_up[i], w_out[i],
                   norm_ffn[i], w_router_group[i], b_router_group[i], w_router_expert[i],
                   b_router_expert[i], w_gate_up[i], w_down[i], norm_ple[i], w_ple_gate[i],
                   w_ple_proj[i])
        hp, r_p, d_p, c_p = hybrid_layer(hp, p_prompt[i], zero_ret, zero_dn, zero_conv,
                                         pos_prompt, *weights)
        hs, r_s, d_s, c_s = hybrid_layer(hs, p_sample[i], state_ret[i], state_dn[i],
                                         state_conv[i], pos_sample, *weights)
        ret_p.append(r_p)
        dn_p.append(d_p)
        conv_p.append(c_p)
        ret_s.append(r_s)
        dn_s.append(d_s)
        conv_s.append(c_s)

    y_prompt = rmsnorm(hp, norm_final)
    y_sample = rmsnorm(hs, norm_final)
    ret_prompt = jnp.stack(ret_p, axis=0)
    dn_prompt = jnp.stack(dn_p, axis=0)
    conv_prompt = jnp.stack(conv_p, axis=0)
    ret_sample = jnp.stack(ret_s, axis=0)
    dn_sample = jnp.stack(dn_s, axis=0)
    conv_sample = jnp.stack(conv_s, axis=0)
    return (y_prompt, y_sample, ret_prompt, dn_prompt, conv_prompt, ret_sample, dn_sample, conv_sample)
```

```python
import functools

import jax
import jax.numpy as jnp
from jax import lax
from jax.experimental import pallas as pl
from jax.experimental.pallas import tpu as pltpu

F32 = jnp.float32
BF16 = jnp.bfloat16

D_MODEL = 1024
RET_H, RET_DK, RET_DV = 4, 64, 128
DN_H, DN_DK, DN_DV = 4, 128, 128
CONV_W = 4
CHUNK = 64
ROPE_BASE = 10000.0
PAST_LEN = 16384
N_GROUPS, EXPERTS_PER_GROUP = 4, 4
N_EXPERTS = N_GROUPS * EXPERTS_PER_GROUP
D_EXPERT = 256
PLE_DIM = 256
EPS = 1e-6

RET_QK = RET_H * RET_DK
RET_VW = RET_H * RET_DV
DN_QK = DN_H * DN_DK
DN_VW = DN_H * DN_DV
DN_CONV_CH = 2 * DN_QK + DN_VW
IN_WIDTHS = (RET_QK, RET_QK, RET_VW, RET_VW, DN_CONV_CH, DN_VW, DN_H, DN_H, D_MODEL, D_MODEL)

LANES = 128
VMEM_LIMIT = 56 * 1024 * 1024

_C_Q, _C_K, _C_V, _C_RG = 0, 256, 512, 1024
_C_DQKV, _C_DZ, _C_GA, _C_GB, _C_END = 1536, 3072, 3584, 4608, 5632


def _silu(x):
    return x * jax.nn.sigmoid(x)


def _softplus(x):
    return jnp.maximum(x, 0.0) + jnp.log1p(jnp.exp(-jnp.abs(x)))


def _dot(a, b):
    return jnp.dot(a, b, preferred_element_type=F32)


def _dot_nt(a, b):
    return lax.dot_general(a, b, (((1,), (1,)), ((), ())), preferred_element_type=F32)


def _dot_tn(a, b):
    return lax.dot_general(a, b, (((0,), (0,)), ((), ())), preferred_element_type=F32)


def _dot_f32(a, b):
    return jnp.dot(a, b, preferred_element_type=F32, precision=lax.Precision.HIGHEST)


def _split3(x):
    hi = x.astype(BF16)
    r = x - hi.astype(F32)
    mid = r.astype(BF16)
    lo = (r - mid.astype(F32)).astype(BF16)
    return hi, mid, lo


def _rms_plain(x):
    return x * lax.rsqrt(jnp.mean(x * x, axis=-1, keepdims=True) + EPS)


def _proj_kernel(x_ref, gain_ref, w_ref, wab_ref, wabt_ref, cos_ref, sin_ref,
                 alog_c_ref, dtb_c_ref, alog_r_ref, dtb_r_ref,
                 q_ref, k_ref, v_ref, rg_ref, dqkv_ref, dz_ref, ga_ref, gb_ref, gbc_ref, gbr_ref):
    x = x_ref[...]
    u = x * lax.rsqrt(jnp.mean(x * x, axis=-1, keepdims=True) + EPS) * gain_ref[...]
    ub = u.astype(BF16)

    def mm(lo, hi):
        return _dot(ub, w_ref[:, lo:hi])

    cos = cos_ref[...]
    sin = sin_ref[...]
    lane = lax.broadcasted_iota(jnp.int32, cos.shape, 1)
    first_half = (lane % RET_DK) < (RET_DK // 2)

    def rot(t):
        swapped = jnp.where(first_half, pltpu.roll(t, RET_QK - RET_DK // 2, 1),
                            pltpu.roll(t, RET_DK // 2, 1))
        return t * cos + swapped * sin

    q_ref[...] = (rot(mm(_C_Q, _C_K)) * (RET_DK ** -0.5)).astype(BF16)
    k_ref[...] = rot(mm(_C_K, _C_V)).astype(BF16)
    v_ref[...] = mm(_C_V, _C_RG).astype(BF16)
    rg_ref[...] = _silu(mm(_C_RG, _C_DQKV)).astype(BF16)
    dqkv_ref[...] = mm(_C_DQKV, _C_DZ)
    dz_ref[...] = _silu(mm(_C_DZ, _C_GA)).astype(BF16)
    ga_ref[...] = jax.nn.sigmoid(mm(_C_GA, _C_GB)).astype(BF16)
    gb_ref[...] = jax.nn.sigmoid(mm(_C_GB, _C_END)).astype(BF16)

    dab = _dot(ub, wab_ref[...])
    lane_c = lax.broadcasted_iota(jnp.int32, dab.shape, 1)
    g_c = -jnp.exp(alog_c_ref[...]) * _softplus(dab + dtb_c_ref[...])
    gbc_ref[...] = jnp.where(lane_c < DN_H, g_c, jax.nn.sigmoid(dab))
    dabt = _dot_nt(wabt_ref[...], ub)
    row_r = lax.broadcasted_iota(jnp.int32, dabt.shape, 0)
    g_r = -jnp.exp(alog_r_ref[...]) * _softplus(dabt + dtb_r_ref[...])
    gbr_ref[...] = jnp.where(row_r < DN_H, g_r, jax.nn.sigmoid(dabt))


def _proj_call(x2d, gain, w_main, wab, wabt, cos_t, sin_t, table_blocks,
               alog_c, dtb_c, alog_r, dtb_r, tm):
    n = x2d.shape[0]
    grid = (n // tm,)
    tok = lambda w: pl.BlockSpec((tm, w), lambda i: (i, 0))
    const = lambda a: pl.BlockSpec(a.shape, lambda i: (0,) * a.ndim)
    tab = pl.BlockSpec((tm, RET_QK), lambda i: (i % table_blocks, 0))
    out_shapes = (
        jax.ShapeDtypeStruct((n, RET_QK), BF16), jax.ShapeDtypeStruct((n, RET_QK), BF16),
        jax.ShapeDtypeStruct((n, RET_VW), BF16), jax.ShapeDtypeStruct((n, RET_VW), BF16),
        jax.ShapeDtypeStruct((n, DN_CONV_CH), F32), jax.ShapeDtypeStruct((n, DN_VW), BF16),
        jax.ShapeDtypeStruct((n, D_MODEL), BF16), jax.ShapeDtypeStruct((n, D_MODEL), BF16),
        jax.ShapeDtypeStruct((n, LANES), F32), jax.ShapeDtypeStruct((8, n), F32),
    )
    out_specs = (tok(RET_QK), tok(RET_QK), tok(RET_VW), tok(RET_VW), tok(DN_CONV_CH), tok(DN_VW),
                 tok(D_MODEL), tok(D_MODEL), tok(LANES), pl.BlockSpec((8, tm), lambda i: (0, i)))
    return pl.pallas_call(
        _proj_kernel, grid=grid, name="proj",
        in_specs=[tok(D_MODEL), const(gain), const(w_main), const(wab), const(wabt), tab, tab,
                  const(alog_c), const(dtb_c), const(alog_r), const(dtb_r)],
        out_specs=out_specs, out_shape=out_shapes,
        compiler_params=pltpu.CompilerParams(dimension_semantics=("parallel",),
                                             vmem_limit_bytes=VMEM_LIMIT),
    )(x2d, gain, w_main, wab, wabt, cos_t, sin_t, alog_c, dtb_c, alog_r, dtb_r)


def _ret_kernel(q_ref, k_ref, v_ref, rg_ref, s0_ref, dec_ref, qd_ref, kd_ref, cd_ref,
                o_ref, s_ref, *, bt):
    @pl.when(pl.program_id(1) == 0)
    def _():
        s_ref[...] = s0_ref[...]

    for b in range(bt):
        q = q_ref[b]
        k = k_ref[b]
        v = v_ref[b]
        rg = rg_ref[b]
        for h in range(RET_H):
            qh = q[:, h * RET_DK:(h + 1) * RET_DK]
            kh = k[:, h * RET_DK:(h + 1) * RET_DK]
            vh = v[:, h * RET_DV:(h + 1) * RET_DV]
            s = s_ref[b, h]
            scores = _dot_nt(qh, kh) * dec_ref[h]
            q_dec = (qh.astype(F32) * qd_ref[h]).astype(BF16)
            o = _dot(scores.astype(BF16), vh) + _dot(q_dec, s.astype(BF16))
            k_dec = (kh.astype(F32) * kd_ref[h]).astype(BF16)
            s_ref[b, h] = s * cd_ref[h] + _dot_tn(k_dec, vh)
            gate = rg[:, h * RET_DV:(h + 1) * RET_DV].astype(F32)
            o_ref[b, :, h * RET_DV:(h + 1) * RET_DV] = (_rms_plain(o) * gate).astype(BF16)


def _ret_call(q, k, v, rg, s0, dec, qd, kd, cd, bt, tt):
    bsz, seq, _ = q.shape
    grid = (bsz // bt, seq // tt)
    tok = lambda w: pl.BlockSpec((bt, tt, w), lambda b, l: (b, l, 0))
    const = lambda a: pl.BlockSpec(a.shape, lambda b, l: (0,) * a.ndim)
    st = pl.BlockSpec((bt, RET_H, RET_DK, RET_DV), lambda b, l: (b, 0, 0, 0))
    return pl.pallas_call(
        functools.partial(_ret_kernel, bt=bt), grid=grid, name="ret",
        in_specs=[tok(RET_QK), tok(RET_QK), tok(RET_VW), tok(RET_VW), st,
                  const(dec), const(qd), const(kd), const(cd)],
        out_specs=(tok(RET_VW), st),
        out_shape=(jax.ShapeDtypeStruct((bsz, seq, RET_VW), BF16),
                   jax.ShapeDtypeStruct(s0.shape, F32)),
        compiler_params=pltpu.CompilerParams(dimension_semantics=("parallel", "arbitrary"),
                                             vmem_limit_bytes=VMEM_LIMIT),
    )(q, k, v, rg, s0, dec, qd, kd, cd)


def _dn_kernel(x_ref, gc_ref, gr_ref, dz_ref, cw_ref, dnorm_ref, s0_ref, c0_ref, lblk_ref, ublk_ref,
               o_ref, s_ref, c_ref, ext_ref, *, bt, tt, c):
    pad = 8
    tail = CONV_W - 1

    @pl.when(pl.program_id(1) == 0)
    def _():
        s_ref[...] = s0_ref[...]
        ext_ref[:, pad - tail:pad, :] = c0_ref[...]

    ri = lax.broadcasted_iota(jnp.int32, (c, c), 0)
    ci = lax.broadcasted_iota(jnp.int32, (c, c), 1)
    incl = ri >= ci
    strict = ri > ci
    eye = (ri == ci).astype(F32)
    n_sq = max(c.bit_length() - 2, 0)
    lblk = lblk_ref[...]
    ublk = ublk_ref[...]

    for b in range(bt):
        x = x_ref[b]
        ext_ref[b, pad:pad + tt, :] = x
        conv = ext_ref[b, pad - tail:pad - tail + tt, :] * cw_ref[0:1, :]
        for j in range(1, CONV_W):
            conv = conv + ext_ref[b, pad - tail + j:pad - tail + j + tt, :] * cw_ref[j:j + 1, :]
        new_tail = x_ref[b, tt - tail:tt, :]
        ext_ref[b, pad - tail:pad, :] = new_tail
        c_ref[b] = new_tail
        conv = _silu(conv)

        gc = gc_ref[b]
        gr = gr_ref[b]
        g_hi, g_mid, g_lo = _split3(gc)
        cum_c = _dot(lblk, g_hi) + _dot(lblk, g_mid) + _dot(lblk, g_lo)
        r_hi, r_mid, r_lo = _split3(gr)
        cum_r = _dot(r_hi, ublk) + _dot(r_mid, ublk) + _dot(r_lo, ublk)

        for h in range(DN_H):
            cq = conv[:, h * DN_DK:(h + 1) * DN_DK]
            ck = conv[:, DN_QK + h * DN_DK:DN_QK + (h + 1) * DN_DK]
            cv = conv[:, 2 * DN_QK + h * DN_DV:2 * DN_QK + (h + 1) * DN_DV]
            qn = cq * lax.rsqrt(jnp.sum(cq * cq, axis=-1, keepdims=True) + EPS) * (DN_DK ** -0.5)
            kn = ck * lax.rsqrt(jnp.sum(ck * ck, axis=-1, keepdims=True) + EPS)
            s = s_ref[b, h]
            for n in range(tt // c):
                r0 = n * c
                qc, kc, vc = qn[r0:r0 + c], kn[r0:r0 + c], cv[r0:r0 + c]
                g_col = cum_c[r0:r0 + c, h:h + 1]
                g_row = cum_r[h:h + 1, r0:r0 + c]
                beta = gc[r0:r0 + c, DN_H + h:DN_H + h + 1]
                dec = jnp.where(incl, jnp.exp(jnp.where(incl, g_col - g_row, 0.0)), 0.0)
                kcb = kc.astype(BF16)
                kk = _dot_nt(kcb, kcb)
                a_mat = jnp.where(strict, kk * dec * beta, 0.0)
                e_g = jnp.exp(g_col)
                rhs = jnp.concatenate([beta * vc, beta * e_g * kc], axis=-1)
                pw = -a_mat
                t_inv = eye + pw
                for _ in range(n_sq):
                    pw = _dot_f32(pw, pw)
                    t_inv = t_inv + _dot_f32(t_inv, pw)
                sol = _dot_f32(t_inv, rhs)
                u_mat, w_mat = sol[:, :DN_DV], sol[:, DN_DV:]
                sb = s.astype(BF16)
                delta = u_mat - _dot(w_mat.astype(BF16), sb)
                qk = _dot_nt(qc.astype(BF16), kcb) * dec
                o = _dot((qc * e_g).astype(BF16), sb) + _dot(qk.astype(BF16), delta.astype(BF16))
                g_last = g_col[c - 1:c, :]
                s = s * jnp.exp(g_last) + _dot_tn((kc * jnp.exp(g_last - g_col)).astype(BF16),
                                                  delta.astype(BF16))
                gate = dz_ref[b, r0:r0 + c, h * DN_DV:(h + 1) * DN_DV].astype(F32)
                o_ref[b, r0:r0 + c, h * DN_DV:(h + 1) * DN_DV] = (
                    _rms_plain(o) * dnorm_ref[...] * gate).astype(BF16)
            s_ref[b, h] = s


def _dn_call(dqkv, gbc, gbr, dz, conv_w, dn_norm, s0, c0, lblk, ublk, bt, tt, c):
    bsz, seq, _ = dqkv.shape
    grid = (bsz // bt, seq // tt)
    tok = lambda w: pl.BlockSpec((bt, tt, w), lambda b, l: (b, l, 0))
    const = lambda a: pl.BlockSpec(a.shape, lambda b, l: (0,) * a.ndim)
    st = pl.BlockSpec((bt, DN_H, DN_DK, DN_DV), lambda b, l: (b, 0, 0, 0))
    cst = pl.BlockSpec((bt, CONV_W - 1, DN_CONV_CH), lambda b, l: (b, 0, 0))
    return pl.pallas_call(
        functools.partial(_dn_kernel, bt=bt, tt=tt, c=c), grid=grid, name="dn",
        in_specs=[tok(DN_CONV_CH), tok(LANES), pl.BlockSpec((bt, 8, tt), lambda b, l: (b, 0, l)),
                  tok(DN_VW), const(conv_w), const(dn_norm), st, cst, const(lblk), const(ublk)],
        out_specs=(tok(DN_VW), st, cst),
        out_shape=(jax.ShapeDtypeStruct((bsz, seq, DN_VW), BF16),
                   jax.ShapeDtypeStruct(s0.shape, F32),
                   jax.ShapeDtypeStruct(c0.shape, F32)),
        scratch_shapes=[pltpu.VMEM((bt, tt + 8, DN_CONV_CH), F32)],
        compiler_params=pltpu.CompilerParams(dimension_semantics=("parallel", "arbitrary"),
                                             vmem_limit_bytes=VMEM_LIMIT),
    )(dqkv, gbc, gbr, dz, conv_w, dn_norm, s0, c0, lblk, ublk)


def _mix_kernel(x_ref, og_ref, od_ref, ga_ref, gb_ref, wru_ref, wdu_ref, wo_ref, h_ref):
    branch_a = _dot(og_ref[...], wru_ref[...])
    branch_b = _dot(od_ref[...], wdu_ref[...])
    merged = ga_ref[...].astype(F32) * branch_a + gb_ref[...].astype(F32) * branch_b
    h_ref[...] = x_ref[...] + _dot(merged.astype(BF16), wo_ref[...])


def _mix_call(x2d, og, od, ga, gb, wru, wdu, wo, tm):
    n = x2d.shape[0]
    tok = lambda w: pl.BlockSpec((tm, w), lambda i: (i, 0))
    const = lambda a: pl.BlockSpec(a.shape, lambda i: (0,) * a.ndim)
    return pl.pallas_call(
        _mix_kernel, grid=(n // tm,), name="mix",
        in_specs=[tok(D_MODEL), tok(RET_VW), tok(DN_VW), tok(D_MODEL), tok(D_MODEL),
                  const(wru), const(wdu), const(wo)],
        out_specs=tok(D_MODEL), out_shape=jax.ShapeDtypeStruct((n, D_MODEL), F32),
        compiler_params=pltpu.CompilerParams(dimension_semantics=("parallel",),
                                             vmem_limit_bytes=VMEM_LIMIT),
    )(x2d, og, od, ga, gb, wru, wdu, wo)


_R_EXP = N_GROUPS


def _ffn_kernel(h_ref, p_ref, nffn_ref, wr_ref, br_ref, wgu_ref, wdn_ref, nple_ref, wpg_ref, wpp_ref,
                nfin_ref, y_ref):
    h = h_ref[...]
    u = _rms_plain(h) * nffn_ref[...]
    ub = u.astype(BF16)

    logits = _dot(ub, wr_ref[...]) + br_ref[...]
    lane = lax.broadcasted_iota(jnp.int32, logits.shape, 1)
    neg = jnp.float32(-jnp.inf)
    big = jnp.int32(LANES)
    gl = jnp.where(lane < N_GROUPS, logits, neg)
    g_max = jnp.max(gl, axis=-1, keepdims=True)
    p_grp = 1.0 / jnp.sum(jnp.exp(gl - g_max), axis=-1, keepdims=True)
    grp = jnp.min(jnp.where(gl == g_max, lane, big), axis=-1, keepdims=True)
    e_idx = lane - _R_EXP
    in_grp = (e_idx >= 0) & (e_idx < N_EXPERTS) & ((e_idx // EXPERTS_PER_GROUP) == grp)
    el = jnp.where(in_grp, logits, neg)
    v1 = jnp.max(el, axis=-1, keepdims=True)
    i1 = jnp.min(jnp.where(el == v1, lane, big), axis=-1, keepdims=True)
    el2 = jnp.where(lane == i1, neg, el)
    v2 = jnp.max(el2, axis=-1, keepdims=True)
    i2 = jnp.min(jnp.where(el2 == v2, lane, big), axis=-1, keepdims=True)
    e2 = jnp.exp(v2 - v1)
    w1 = 1.0 / (1.0 + e2)
    w2 = e2 / (1.0 + e2)
    combine = jnp.where(lane == i1, w1 * p_grp, 0.0) + jnp.where(lane == i2, w2 * p_grp, 0.0)

    moe = jnp.zeros_like(h)
    for e in range(N_EXPERTS):
        gu = _dot(ub, wgu_ref[e])
        hid = _silu(gu[:, :D_EXPERT]) * gu[:, D_EXPERT:]
        moe = moe + combine[:, _R_EXP + e:_R_EXP + e + 1] * _dot(hid.astype(BF16), wdn_ref[e])
    h = h + moe

    u3 = (_rms_plain(h) * nple_ref[...]).astype(BF16)
    gate = jax.nn.sigmoid(_dot(u3, wpg_ref[...]))
    h = h + gate * _dot(p_ref[...].astype(BF16), wpp_ref[...])
    y_ref[...] = _rms_plain(h) * nfin_ref[...]


def _ffn_call(h2d, p2d, nffn, wr, br, wgu, wdn, nple, wpg, wpp, nfin, tm):
    n = h2d.shape[0]
    tok = lambda w: pl.BlockSpec((tm, w), lambda i: (i, 0))
    const = lambda a: pl.BlockSpec(a.shape, lambda i: (0,) * a.ndim, pipeline_mode=pl.Buffered(1))
    return pl.pallas_call(
        _ffn_kernel, grid=(n // tm,), name="ffn",
        in_specs=[tok(D_MODEL), tok(PLE_DIM), const(nffn), const(wr), const(br), const(wgu), const(wdn),
                  const(nple), const(wpg), const(wpp), const(nfin)],
        out_specs=tok(D_MODEL), out_shape=jax.ShapeDtypeStruct((n, D_MODEL), F32),
        compiler_params=pltpu.CompilerParams(dimension_semantics=("parallel",),
                                             vmem_limit_bytes=VMEM_LIMIT),
    )(h2d, p2d, nffn, wr, br, wgu, wdn, nple, wpg, wpp, nfin)


def _rope_tables(pos):
    half = RET_DK // 2
    inv = 1.0 / (ROPE_BASE ** (jnp.arange(half, dtype=F32) / half))
    ang = pos.astype(F32)[:, None] * inv[None, :]
    cos, sin = jnp.cos(ang), jnp.sin(ang)
    cos_t = jnp.tile(jnp.concatenate([cos, cos], axis=-1), (1, RET_H))
    sin_t = jnp.tile(jnp.concatenate([-sin, sin], axis=-1), (1, RET_H))
    return cos_t, sin_t


def _ret_tables(c):
    log_gamma = jnp.log(1.0 - 2.0 ** (-5.0 - jnp.arange(RET_H, dtype=F32)))
    idx = jnp.arange(c, dtype=F32)
    diff = idx[:, None] - idx[None, :]
    causal = diff >= 0
    lg = log_gamma[:, None, None]
    decay = jnp.where(causal, jnp.exp(jnp.where(causal, diff, 0.0) * lg), 0.0)
    q_dec = jnp.exp((idx + 1.0)[None, :] * log_gamma[:, None])[..., None]
    k_dec = jnp.exp((c - 1.0 - idx)[None, :] * log_gamma[:, None])[..., None]
    chunk_dec = jnp.exp(c * log_gamma)[:, None, None]
    return decay, q_dec, k_dec, chunk_dec


def _cumsum_tables(tt, c):
    i = jnp.arange(tt)
    same = (i[:, None] // c) == (i[None, :] // c)
    lblk = ((i[:, None] >= i[None, :]) & same).astype(BF16)
    return lblk, lblk.T


def _layer(x, p, s_ret, s_dn, s_conv, pos_table, table_blocks, wts, cfg):
    bsz, seq, _ = x.shape
    n = bsz * seq
    x2d = x.reshape(n, D_MODEL)
    cos_t, sin_t = pos_table
    (q, k, v, rg, dqkv, dz, ga, gb, gbc, gbr) = _proj_call(
        x2d, wts["norm_mix"], wts["w_main"], wts["wab"], wts["wabt"], cos_t, sin_t, table_blocks,
        wts["alog_c"], wts["dtb_c"], wts["alog_r"], wts["dtb_r"], cfg["tm"])

    r3 = lambda a: a.reshape(bsz, seq, a.shape[-1])
    dec, qd, kd, cd = _ret_tables(cfg["ret_tt"])
    og, ret_new = _ret_call(r3(q), r3(k), r3(v), r3(rg), s_ret, dec, qd, kd, cd,
                            cfg["bt"], cfg["ret_tt"])

    lblk, ublk = _cumsum_tables(cfg["dn_tt"], cfg["dn_c"])
    gbr3 = gbr.reshape(8, bsz, seq).transpose(1, 0, 2)
    od, dn_new, conv_new = _dn_call(r3(dqkv), r3(gbc), gbr3, r3(dz), wts["conv_w"], wts["dn_norm"],
                                    s_dn, s_conv, lblk, ublk, cfg["bt"], cfg["dn_tt"], cfg["dn_c"])

    h1 = _mix_call(x2d, og.reshape(n, RET_VW), od.reshape(n, DN_VW), ga, gb,
                   wts["w_ret_up"], wts["w_dn_up"], wts["w_out"], cfg["tm"])
    y = _ffn_call(h1, p.reshape(n, PLE_DIM), wts["norm_ffn"], wts["w_router"], wts["b_router"],
                  wts["w_gate_up"], wts["w_down"], wts["norm_ple"], wts["w_ple_gate"], wts["w_ple_proj"],
                  wts["norm_final"], cfg["tm"])
    return y.reshape(bsz, seq, D_MODEL), ret_new, dn_new, conv_new


def _prep_weights(norm_mix, w_in, conv_w, dn_a_log, dn_dt_bias, dn_norm, w_ret_up, w_dn_up, w_out,
                  norm_ffn, w_router_group, b_router_group, w_router_expert, b_router_expert,
                  w_gate_up, w_down, norm_ple, w_ple_gate, w_ple_proj, norm_final):
    pts, acc = [], 0
    for w in IN_WIDTHS[:-1]:
        acc += w
        pts.append(acc)
    rq, rk, rv, rg, dqkv, dz, da, db, ga, gb = jnp.split(w_in, pts, axis=-1)
    w_main = jnp.concatenate([rq, rk, rv, rg, dqkv, dz, ga, gb], axis=-1).astype(BF16)
    wab8 = jnp.concatenate([da, db], axis=-1).astype(BF16)
    wab = jnp.pad(wab8, ((0, 0), (0, LANES - 2 * DN_H)))
    pad_lanes = lambda a: jnp.pad(a.astype(F32), (0, LANES - a.shape[0]))[None, :]
    pad_rows = lambda a: jnp.pad(a.astype(F32), (0, 8 - a.shape[0]))[:, None]
    w_router = jnp.pad(jnp.concatenate([w_router_group, w_router_expert], axis=-1),
                       ((0, 0), (0, LANES - N_GROUPS - N_EXPERTS))).astype(BF16)
    b_router = pad_lanes(jnp.concatenate([b_router_group, b_router_expert]))
    row = lambda a: a.astype(F32)[None, :]
    return dict(
        norm_mix=row(norm_mix), w_main=w_main, wab=wab, wabt=wab8.T,
        alog_c=pad_lanes(dn_a_log), dtb_c=pad_lanes(dn_dt_bias),
        alog_r=pad_rows(dn_a_log), dtb_r=pad_rows(dn_dt_bias),
        conv_w=conv_w.astype(F32), dn_norm=row(dn_norm),
        w_ret_up=w_ret_up.astype(BF16), w_dn_up=w_dn_up.astype(BF16), w_out=w_out.astype(BF16),
        norm_ffn=row(norm_ffn), w_router=w_router, b_router=b_router,
        w_gate_up=w_gate_up.astype(BF16), w_down=w_down.astype(BF16),
        norm_ple=row(norm_ple), w_ple_gate=w_ple_gate.astype(BF16), w_ple_proj=w_ple_proj.astype(BF16),
        norm_final=row(norm_final),
    )


def kernel(x_prompt, x_sample, p_prompt, p_sample, state_ret, state_dn, state_conv, norm_mix, w_in, conv_w, dn_a_log, dn_dt_bias, dn_norm, w_ret_up, w_dn_up, w_out, norm_ffn, w_router_group, b_router_group, w_router_expert, b_router_expert, w_gate_up, w_down, norm_ple, w_ple_gate, w_ple_proj, norm_final):
    depth = w_in.shape[0]
    assert depth == 1, "one layer: the final norm is fused into the layer's last kernel"
    bp, lp, _ = x_prompt.shape
    bs, ls, _ = x_sample.shape
    wts = _prep_weights(norm_mix[0], w_in[0], conv_w[0], dn_a_log[0], dn_dt_bias[0], dn_norm[0],
                        w_ret_up[0], w_dn_up[0], w_out[0], norm_ffn[0], w_router_group[0],
                        b_router_group[0], w_router_expert[0], b_router_expert[0], w_gate_up[0],
                        w_down[0], norm_ple[0], w_ple_gate[0], w_ple_proj[0], norm_final)

    tm = 256
    cfg_p = dict(tm=tm, bt=1, ret_tt=256, dn_tt=256, dn_c=CHUNK if lp % CHUNK == 0 else lp)
    cfg_s = dict(tm=tm, bt=16, ret_tt=ls, dn_tt=ls, dn_c=CHUNK if ls % CHUNK == 0 else ls)

    pos_p = _rope_tables(jnp.arange(lp, dtype=jnp.int32))
    cos_s, sin_s = _rope_tables(PAST_LEN + jnp.arange(ls, dtype=jnp.int32))
    pos_s = (jnp.tile(cos_s, (bs, 1)), jnp.tile(sin_s, (bs, 1)))

    zero_ret = jnp.zeros((bp, RET_H, RET_DK, RET_DV), F32)
    zero_dn = jnp.zeros((bp, DN_H, DN_DK, DN_DV), F32)
    zero_conv = jnp.zeros((bp, CONV_W - 1, DN_CONV_CH), F32)

    y_p, r_p, d_p, c_p = _layer(x_prompt, p_prompt[0], zero_ret, zero_dn, zero_conv,
                                pos_p, lp // tm, wts, cfg_p)
    y_s, r_s, d_s, c_s = _layer(x_sample, p_sample[0], state_ret[0], state_dn[0], state_conv[0],
                                pos_s, (bs * ls) // tm, wts, cfg_s)
    return (y_p, y_s, r_p[None], d_p[None], c_p[None], r_s[None], d_s[None], c_s[None])
```

```python
import functools

import jax
import jax.numpy as jnp
from jax import lax
from jax.experimental import pallas as pl
from jax.experimental.pallas import tpu as pltpu

F32 = jnp.float32
BF16 = jnp.bfloat16

D_MODEL = 1024
RET_H, RET_DK, RET_DV = 4, 64, 128
DN_H, DN_DK, DN_DV = 4, 128, 128
CONV_W = 4
ROPE_BASE = 10000.0
PAST_LEN = 16384
N_GROUPS, EXPERTS_PER_GROUP = 4, 4
N_EXPERTS = N_GROUPS * EXPERTS_PER_GROUP
D_EXPERT = 256
PLE_DIM = 256
EPS = 1e-6

RET_QK = RET_H * RET_DK
RET_VW = RET_H * RET_DV
DN_QK = DN_H * DN_DK
DN_VW = DN_H * DN_DV
DN_CONV_CH = 2 * DN_QK + DN_VW
IN_WIDTHS = (RET_QK, RET_QK, RET_VW, RET_VW, DN_CONV_CH, DN_VW, DN_H, DN_H, D_MODEL, D_MODEL)

LANES = 128
VMEM_LIMIT = 56 * 1024 * 1024

_C_Q, _C_K, _C_V, _C_RG = 0, 256, 512, 1024
_C_DQKV, _C_DZ, _C_GA, _C_GB, _C_END = 1536, 3072, 3584, 4608, 5632


def _silu(x):
    return x * jax.nn.sigmoid(x)


def _softplus(x):
    return jnp.maximum(x, 0.0) + jnp.log1p(jnp.exp(-jnp.abs(x)))


def _dot(a, b):
    return jnp.dot(a, b, preferred_element_type=F32)


def _dot_nt(a, b):
    return lax.dot_general(a, b, (((1,), (1,)), ((), ())), preferred_element_type=F32)


def _dot_tn(a, b):
    return lax.dot_general(a, b, (((0,), (0,)), ((), ())), preferred_element_type=F32)


def _split3(x):
    hi = x.astype(BF16)
    r = x - hi.astype(F32)
    mid = r.astype(BF16)
    lo = (r - mid.astype(F32)).astype(BF16)
    return hi, mid, lo


def _rms_plain(x):
    return x * lax.rsqrt(jnp.mean(x * x, axis=-1, keepdims=True) + EPS)


def _proj_kernel(x_ref, gain_ref, w_ref, wab_ref, wabt_ref, cos_ref, sin_ref,
                 alog_c_ref, dtb_c_ref, alog_r_ref, dtb_r_ref,
                 q_ref, k_ref, v_ref, rg_ref, dqkv_ref, dz_ref, ga_ref, gb_ref, gbc_ref, gbr_ref):
    x = x_ref[...]
    u = x * lax.rsqrt(jnp.mean(x * x, axis=-1, keepdims=True) + EPS) * gain_ref[...]
    ub = u.astype(BF16)

    def mm(lo, hi):
        return _dot(ub, w_ref[:, lo:hi])

    cos = cos_ref[...]
    sin = sin_ref[...]
    lane = lax.broadcasted_iota(jnp.int32, cos.shape, 1)
    first_half = (lane % RET_DK) < (RET_DK // 2)

    def rot(t):
        swapped = jnp.where(first_half, pltpu.roll(t, RET_QK - RET_DK // 2, 1),
                            pltpu.roll(t, RET_DK // 2, 1))
        return t * cos + swapped * sin

    q_ref[...] = (rot(mm(_C_Q, _C_K)) * (RET_DK ** -0.5)).astype(BF16)
    k_ref[...] = rot(mm(_C_K, _C_V)).astype(BF16)
    v_ref[...] = mm(_C_V, _C_RG).astype(BF16)
    rg_ref[...] = _silu(mm(_C_RG, _C_DQKV)).astype(BF16)
    dqkv_ref[...] = mm(_C_DQKV, _C_DZ)
    dz_ref[...] = _silu(mm(_C_DZ, _C_GA)).astype(BF16)
    ga_ref[...] = jax.nn.sigmoid(mm(_C_GA, _C_GB)).astype(BF16)
    gb_ref[...] = jax.nn.sigmoid(mm(_C_GB, _C_END)).astype(BF16)

    dab = _dot(ub, wab_ref[...])
    lane_c = lax.broadcasted_iota(jnp.int32, dab.shape, 1)
    g_c = -jnp.exp(alog_c_ref[...]) * _softplus(dab + dtb_c_ref[...])
    gbc_ref[...] = jnp.where(lane_c < DN_H, g_c, jax.nn.sigmoid(dab))
    dabt = _dot_nt(wabt_ref[...], ub)
    row_r = lax.broadcasted_iota(jnp.int32, dabt.shape, 0)
    g_r = -jnp.exp(alog_r_ref[...]) * _softplus(dabt + dtb_r_ref[...])
    gbr_ref[...] = jnp.where(row_r < DN_H, g_r, jax.nn.sigmoid(dabt))


def _proj_call(x2d, gain, w_main, wab, wabt, cos_t, sin_t, table_blocks,
               alog_c, dtb_c, alog_r, dtb_r, tm):
    n = x2d.shape[0]
    grid = (n // tm,)
    tok = lambda w: pl.BlockSpec((tm, w), lambda i: (i, 0))
    const = lambda a: pl.BlockSpec(a.shape, lambda i: (0,) * a.ndim)
    tab = pl.BlockSpec((tm, RET_QK), lambda i: (i % table_blocks, 0))
    out_shapes = (
        jax.ShapeDtypeStruct((n, RET_QK), BF16), jax.ShapeDtypeStruct((n, RET_QK), BF16),
        jax.ShapeDtypeStruct((n, RET_VW), BF16), jax.ShapeDtypeStruct((n, RET_VW), BF16),
        jax.ShapeDtypeStruct((n, DN_CONV_CH), F32), jax.ShapeDtypeStruct((n, DN_VW), BF16),
        jax.ShapeDtypeStruct((n, D_MODEL), BF16), jax.ShapeDtypeStruct((n, D_MODEL), BF16),
        jax.ShapeDtypeStruct((n, LANES), F32), jax.ShapeDtypeStruct((8, n), F32),
    )
    out_specs = (tok(RET_QK), tok(RET_QK), tok(RET_VW), tok(RET_VW), tok(DN_CONV_CH), tok(DN_VW),
                 tok(D_MODEL), tok(D_MODEL), tok(LANES), pl.BlockSpec((8, tm), lambda i: (0, i)))
    return pl.pallas_call(
        _proj_kernel, grid=grid, name="proj",
        in_specs=[tok(D_MODEL), const(gain), const(w_main), const(wab), const(wabt), tab, tab,
                  const(alog_c), const(dtb_c), const(alog_r), const(dtb_r)],
        out_specs=out_specs, out_shape=out_shapes,
        compiler_params=pltpu.CompilerParams(dimension_semantics=("parallel",),
                                             vmem_limit_bytes=VMEM_LIMIT),
    )(x2d, gain, w_main, wab, wabt, cos_t, sin_t, alog_c, dtb_c, alog_r, dtb_r)


def _ret_kernel(q_ref, k_ref, v_ref, rg_ref, s0_ref, dec_ref, qd_ref, kd_ref, cd_ref,
                o_ref, s_ref, *, bt):
    @pl.when(pl.program_id(1) == 0)
    def _():
        s_ref[...] = s0_ref[...]

    for b in range(bt):
        q = q_ref[b]
        k = k_ref[b]
        v = v_ref[b]
        rg = rg_ref[b]
        for h in range(RET_H):
            qh = q[:, h * RET_DK:(h + 1) * RET_DK]
            kh = k[:, h * RET_DK:(h + 1) * RET_DK]
            vh = v[:, h * RET_DV:(h + 1) * RET_DV]
            s = s_ref[b, h]
            scores = _dot_nt(qh, kh) * dec_ref[h]
            q_dec = (qh.astype(F32) * qd_ref[h]).astype(BF16)
            o = _dot(scores.astype(BF16), vh) + _dot(q_dec, s.astype(BF16))
            k_dec = (kh.astype(F32) * kd_ref[h]).astype(BF16)
            s_ref[b, h] = s * cd_ref[h] + _dot_tn(k_dec, vh)
            gate = rg[:, h * RET_DV:(h + 1) * RET_DV].astype(F32)
            o_ref[b, :, h * RET_DV:(h + 1) * RET_DV] = (_rms_plain(o) * gate).astype(BF16)


def _ret_call(q, k, v, rg, s0, dec, qd, kd, cd, bt, tt):
    bsz, seq, _ = q.shape
    grid = (bsz // bt, seq // tt)
    tok = lambda w: pl.BlockSpec((bt, tt, w), lambda b, l: (b, l, 0))
    const = lambda a: pl.BlockSpec(a.shape, lambda b, l: (0,) * a.ndim)
    st = pl.BlockSpec((bt, RET_H, RET_DK, RET_DV), lambda b, l: (b, 0, 0, 0))
    return pl.pallas_call(
        functools.partial(_ret_kernel, bt=bt), grid=grid, name="ret",
        in_specs=[tok(RET_QK), tok(RET_QK), tok(RET_VW), tok(RET_VW), st,
                  const(dec), const(qd), const(kd), const(cd)],
        out_specs=(tok(RET_VW), st),
        out_shape=(jax.ShapeDtypeStruct((bsz, seq, RET_VW), BF16),
                   jax.ShapeDtypeStruct(s0.shape, F32)),
        compiler_params=pltpu.CompilerParams(dimension_semantics=("parallel", "arbitrary"),
                                             vmem_limit_bytes=VMEM_LIMIT),
    )(q, k, v, rg, s0, dec, qd, kd, cd)


def _dn_kernel(x_ref, gc_ref, gr_ref, dz_ref, cw_ref, dnorm_ref, s0_ref, c0_ref, lblk_ref, ublk_ref, lvl_ref,
               o_ref, s_ref, c_ref, ext_ref, *, bt, tt):
    pad = 8
    tail = CONV_W - 1
    r = bt * tt
    n_lvl = lvl_ref.shape[0]

    @pl.when(pl.program_id(1) == 0)
    def _():
        s_ref[...] = s0_ref[...]
        ext_ref[:, pad - tail:pad, :] = c0_ref[...]

    ext_ref[:, pad:pad + tt, :] = x_ref[...]
    conv = ext_ref[:, pad - tail:pad - tail + tt, :] * cw_ref[0:1, :]
    for j in range(1, CONV_W):
        conv = conv + ext_ref[:, pad - tail + j:pad - tail + j + tt, :] * cw_ref[j:j + 1, :]
    new_tail = x_ref[:, tt - tail:tt, :]
    ext_ref[:, pad - tail:pad, :] = new_tail
    c_ref[...] = new_tail
    conv = _silu(conv).reshape(r, DN_CONV_CH)

    ri = lax.broadcasted_iota(jnp.int32, (r, r), 0)
    ci = lax.broadcasted_iota(jnp.int32, (r, r), 1)
    same = (ri // tt) == (ci // tt)
    incl = same & (ri >= ci)
    strict = same & (ri > ci)

    gc = gc_ref[...]
    g_hi, g_mid, g_lo = _split3(gc)
    lblk = lblk_ref[...]
    cum_c = _dot(lblk, g_hi) + _dot(lblk, g_mid) + _dot(lblk, g_lo)
    r_hi, r_mid, r_lo = _split3(gr_ref[...])
    ublk = ublk_ref[...]
    cum_r = _dot(r_hi, ublk) + _dot(r_mid, ublk) + _dot(r_lo, ublk)

    for h in range(DN_H):
        cq = conv[:, h * DN_DK:(h + 1) * DN_DK]
        ck = conv[:, DN_QK + h * DN_DK:DN_QK + (h + 1) * DN_DK]
        vc = conv[:, 2 * DN_QK + h * DN_DV:2 * DN_QK + (h + 1) * DN_DV]
        qn = cq * lax.rsqrt(jnp.sum(cq * cq, axis=-1, keepdims=True) + EPS) * (DN_DK ** -0.5)
        kn = ck * lax.rsqrt(jnp.sum(ck * ck, axis=-1, keepdims=True) + EPS)
        g_col = cum_c[:, h:h + 1]
        g_row = cum_r[h:h + 1, :]
        beta = gc[:, DN_H + h:DN_H + h + 1]
        dec = jnp.where(incl, jnp.exp(jnp.where(incl, g_col - g_row, 0.0)), 0.0)
        kb = kn.astype(BF16)
        a_mat = jnp.where(strict, _dot_nt(kb, kb) * dec * beta, 0.0)
        ab = a_mat.astype(BF16)
        d = -(lvl_ref[0] * a_mat)
        for lv in range(1, n_lvl):
            db = d.astype(BF16)
            w = a_mat + _dot(db, ab)
            z = w + _dot(w.astype(BF16), db)
            d = d - lvl_ref[lv] * z
        e_g = jnp.exp(g_col)
        rhs = jnp.concatenate([beta * vc, beta * e_g * kn], axis=-1)
        sol = rhs + _dot(d.astype(BF16), rhs.astype(BF16))
        u_mat, w_mat = sol[:, :DN_DV], sol[:, DN_DV:]
        qe = qn * e_g

        deltas, qss = [], []
        for b in range(bt):
            r0 = b * tt
            sb = s_ref[b, h].astype(BF16)
            lhs = jnp.concatenate([w_mat[r0:r0 + tt], qe[r0:r0 + tt]], axis=0).astype(BF16)
            ws_qs = _dot(lhs, sb)
            deltas.append(u_mat[r0:r0 + tt] - ws_qs[:tt])
            qss.append(ws_qs[tt:])
        delta = deltas[0] if bt == 1 else jnp.concatenate(deltas, axis=0)
        q_s = qss[0] if bt == 1 else jnp.concatenate(qss, axis=0)
        delta_b = delta.astype(BF16)
        p_mat = _dot_nt(qn.astype(BF16), kb) * dec
        o = q_s + _dot(p_mat.astype(BF16), delta_b)
        for b in range(bt):
            r0 = b * tt
            g_last = g_col[r0 + tt - 1:r0 + tt, :]
            k_dec = (kn[r0:r0 + tt] * jnp.exp(g_last - g_col[r0:r0 + tt])).astype(BF16)
            s_ref[b, h] = s_ref[b, h] * jnp.exp(g_last) + _dot_tn(k_dec, delta_b[r0:r0 + tt])
        gate = dz_ref[:, h * DN_DV:(h + 1) * DN_DV].astype(F32)
        o_ref[:, h * DN_DV:(h + 1) * DN_DV] = (_rms_plain(o) * dnorm_ref[...] * gate).astype(BF16)


def _dn_call(dqkv, gbc, gbr, dz, conv_w, dn_norm, s0, c0, lblk, ublk, lvl, bt, tt):
    bsz, seq, _ = dqkv.shape
    nl = seq // tt
    r = bt * tt
    grid = (bsz // bt, nl)
    tok = lambda w: pl.BlockSpec((r, w), lambda b, l: (b * nl + l, 0))
    const = lambda a: pl.BlockSpec(a.shape, lambda b, l: (0,) * a.ndim)
    st = pl.BlockSpec((bt, DN_H, DN_DK, DN_DV), lambda b, l: (b, 0, 0, 0))
    cst = pl.BlockSpec((bt, CONV_W - 1, DN_CONV_CH), lambda b, l: (b, 0, 0))
    return pl.pallas_call(
        functools.partial(_dn_kernel, bt=bt, tt=tt), grid=grid, name="dn",
        in_specs=[pl.BlockSpec((bt, tt, DN_CONV_CH), lambda b, l: (b, l, 0)), tok(LANES),
                  pl.BlockSpec((8, r), lambda b, l: (0, b * nl + l)),
                  tok(DN_VW), const(conv_w), const(dn_norm), st, cst, const(lblk), const(ublk), const(lvl)],
        out_specs=(tok(DN_VW), st, cst),
        out_shape=(jax.ShapeDtypeStruct((bsz * seq, DN_VW), BF16),
                   jax.ShapeDtypeStruct(s0.shape, F32),
                   jax.ShapeDtypeStruct(c0.shape, F32)),
        scratch_shapes=[pltpu.VMEM((bt, tt + 8, DN_CONV_CH), F32)],
        compiler_params=pltpu.CompilerParams(dimension_semantics=("parallel", "arbitrary"),
                                             vmem_limit_bytes=VMEM_LIMIT),
    )(dqkv, gbc, gbr, dz, conv_w, dn_norm, s0, c0, lblk, ublk, lvl)


def _mix_kernel(x_ref, og_ref, od_ref, ga_ref, gb_ref, wru_ref, wdu_ref, wo_ref, h_ref):
    branch_a = _dot(og_ref[...], wru_ref[...])
    branch_b = _dot(od_ref[...], wdu_ref[...])
    merged = ga_ref[...].astype(F32) * branch_a + gb_ref[...].astype(F32) * branch_b
    h_ref[...] = x_ref[...] + _dot(merged.astype(BF16), wo_ref[...])


def _mix_call(x2d, og, od, ga, gb, wru, wdu, wo, tm):
    n = x2d.shape[0]
    tok = lambda w: pl.BlockSpec((tm, w), lambda i: (i, 0))
    const = lambda a: pl.BlockSpec(a.shape, lambda i: (0,) * a.ndim)
    return pl.pallas_call(
        _mix_kernel, grid=(n // tm,), name="mix",
        in_specs=[tok(D_MODEL), tok(RET_VW), tok(DN_VW), tok(D_MODEL), tok(D_MODEL),
                  const(wru), const(wdu), const(wo)],
        out_specs=tok(D_MODEL), out_shape=jax.ShapeDtypeStruct((n, D_MODEL), F32),
        compiler_params=pltpu.CompilerParams(dimension_semantics=("parallel",),
                                             vmem_limit_bytes=VMEM_LIMIT),
    )(x2d, og, od, ga, gb, wru, wdu, wo)


_R_EXP = N_GROUPS


def _ffn_kernel(h_ref, p_ref, nffn_ref, wr_ref, br_ref, wgu_ref, wdn_ref, nple_ref, wpg_ref, wpp_ref,
                nfin_ref, y_ref):
    h = h_ref[...]
    u = _rms_plain(h) * nffn_ref[...]
    ub = u.astype(BF16)

    logits = _dot(ub, wr_ref[...]) + br_ref[...]
    lane = lax.broadcasted_iota(jnp.int32, logits.shape, 1)
    neg = jnp.float32(-jnp.inf)
    big = jnp.int32(LANES)
    gl = jnp.where(lane < N_GROUPS, logits, neg)
    g_max = jnp.max(gl, axis=-1, keepdims=True)
    p_grp = 1.0 / jnp.sum(jnp.exp(gl - g_max), axis=-1, keepdims=True)
    grp = jnp.min(jnp.where(gl == g_max, lane, big), axis=-1, keepdims=True)
    e_idx = lane - _R_EXP
    in_grp = (e_idx >= 0) & (e_idx < N_EXPERTS) & ((e_idx // EXPERTS_PER_GROUP) == grp)
    el = jnp.where(in_grp, logits, neg)
    v1 = jnp.max(el, axis=-1, keepdims=True)
    i1 = jnp.min(jnp.where(el == v1, lane, big), axis=-1, keepdims=True)
    el2 = jnp.where(lane == i1, neg, el)
    v2 = jnp.max(el2, axis=-1, keepdims=True)
    i2 = jnp.min(jnp.where(el2 == v2, lane, big), axis=-1, keepdims=True)
    e2 = jnp.exp(v2 - v1)
    w1 = 1.0 / (1.0 + e2)
    w2 = e2 / (1.0 + e2)
    combine = jnp.where(lane == i1, w1 * p_grp, 0.0) + jnp.where(lane == i2, w2 * p_grp, 0.0)

    moe = jnp.zeros_like(h)
    for e in range(N_EXPERTS):
        gu = _dot(ub, wgu_ref[e])
        hid = _silu(gu[:, :D_EXPERT]) * gu[:, D_EXPERT:]
        moe = moe + combine[:, _R_EXP + e:_R_EXP + e + 1] * _dot(hid.astype(BF16), wdn_ref[e])
    h = h + moe

    u3 = (_rms_plain(h) * nple_ref[...]).astype(BF16)
    gate = jax.nn.sigmoid(_dot(u3, wpg_ref[...]))
    h = h + gate * _dot(p_ref[...].astype(BF16), wpp_ref[...])
    y_ref[...] = _rms_plain(h) * nfin_ref[...]


def _ffn_call(h2d, p2d, nffn, wr, br, wgu, wdn, nple, wpg, wpp, nfin, tm):
    n = h2d.shape[0]
    tok = lambda w: pl.BlockSpec((tm, w), lambda i: (i, 0))
    const = lambda a: pl.BlockSpec(a.shape, lambda i: (0,) * a.ndim, pipeline_mode=pl.Buffered(1))
    return pl.pallas_call(
        _ffn_kernel, grid=(n // tm,), name="ffn",
        in_specs=[tok(D_MODEL), tok(PLE_DIM), const(nffn), const(wr), const(br), const(wgu), const(wdn),
                  const(nple), const(wpg), const(wpp), const(nfin)],
        out_specs=tok(D_MODEL), out_shape=jax.ShapeDtypeStruct((n, D_MODEL), F32),
        compiler_params=pltpu.CompilerParams(dimension_semantics=("parallel",),
                                             vmem_limit_bytes=VMEM_LIMIT),
    )(h2d, p2d, nffn, wr, br, wgu, wdn, nple, wpg, wpp, nfin)


def _rope_tables(pos):
    half = RET_DK // 2
    inv = 1.0 / (ROPE_BASE ** (jnp.arange(half, dtype=F32) / half))
    ang = pos.astype(F32)[:, None] * inv[None, :]
    cos, sin = jnp.cos(ang), jnp.sin(ang)
    cos_t = jnp.tile(jnp.concatenate([cos, cos], axis=-1), (1, RET_H))
    sin_t = jnp.tile(jnp.concatenate([-sin, sin], axis=-1), (1, RET_H))
    return cos_t, sin_t


def _ret_tables(c):
    log_gamma = jnp.log(1.0 - 2.0 ** (-5.0 - jnp.arange(RET_H, dtype=F32)))
    idx = jnp.arange(c, dtype=F32)
    diff = idx[:, None] - idx[None, :]
    causal = diff >= 0
    lg = log_gamma[:, None, None]
    decay = jnp.where(causal, jnp.exp(jnp.where(causal, diff, 0.0) * lg), 0.0)
    q_dec = jnp.exp((idx + 1.0)[None, :] * log_gamma[:, None])[..., None]
    k_dec = jnp.exp((c - 1.0 - idx)[None, :] * log_gamma[:, None])[..., None]
    chunk_dec = jnp.exp(c * log_gamma)[:, None, None]
    return decay, q_dec, k_dec, chunk_dec


def _dn_tables(r, c):
    i = jnp.arange(r)
    ri, ci = i[:, None], i[None, :]
    same = (ri // c) == (ci // c)
    lblk = ((ri >= ci) & same).astype(BF16)
    x = ri ^ ci
    lvl, m = [], 1
    while m < c:
        lvl.append((same & (ri > ci) & (x >= m) & (x < 2 * m)).astype(F32))
        m *= 2
    return lblk, lblk.T, jnp.stack(lvl)


def _layer(x, p, s_ret, s_dn, s_conv, pos_table, table_blocks, wts, cfg):
    bsz, seq, _ = x.shape
    n = bsz * seq
    x2d = x.reshape(n, D_MODEL)
    cos_t, sin_t = pos_table
    (q, k, v, rg, dqkv, dz, ga, gb, gbc, gbr) = _proj_call(
        x2d, wts["norm_mix"], wts["w_main"], wts["wab"], wts["wabt"], cos_t, sin_t, table_blocks,
        wts["alog_c"], wts["dtb_c"], wts["alog_r"], wts["dtb_r"], cfg["tm"])

    r3 = lambda a: a.reshape(bsz, seq, a.shape[-1])
    dec, qd, kd, cd = _ret_tables(cfg["ret_tt"])
    og, ret_new = _ret_call(r3(q), r3(k), r3(v), r3(rg), s_ret, dec, qd, kd, cd,
                            cfg["bt"], cfg["ret_tt"])

    lblk, ublk, lvl = _dn_tables(cfg["bt"] * cfg["dn_tt"], cfg["dn_tt"])
    od, dn_new, conv_new = _dn_call(r3(dqkv), gbc, gbr, dz, wts["conv_w"], wts["dn_norm"],
                                    s_dn, s_conv, lblk, ublk, lvl, cfg["bt"], cfg["dn_tt"])

    h1 = _mix_call(x2d, og.reshape(n, RET_VW), od, ga, gb,
                   wts["w_ret_up"], wts["w_dn_up"], wts["w_out"], cfg["tm"])
    y = _ffn_call(h1, p.reshape(n, PLE_DIM), wts["norm_ffn"], wts["w_router"], wts["b_router"],
                  wts["w_gate_up"], wts["w_down"], wts["norm_ple"], wts["w_ple_gate"], wts["w_ple_proj"],
                  wts["norm_final"], cfg["tm"])
    return y.reshape(bsz, seq, D_MODEL), ret_new, dn_new, conv_new


def _prep_weights(norm_mix, w_in, conv_w, dn_a_log, dn_dt_bias, dn_norm, w_ret_up, w_dn_up, w_out,
                  norm_ffn, w_router_group, b_router_group, w_router_expert, b_router_expert,
                  w_gate_up, w_down, norm_ple, w_ple_gate, w_ple_proj, norm_final):
    pts, acc = [], 0
    for w in IN_WIDTHS[:-1]:
        acc += w
        pts.append(acc)
    rq, rk, rv, rg, dqkv, dz, da, db, ga, gb = jnp.split(w_in, pts, axis=-1)
    w_main = jnp.concatenate([rq, rk, rv, rg, dqkv, dz, ga, gb], axis=-1).astype(BF16)
    wab8 = jnp.concatenate([da, db], axis=-1).astype(BF16)
    wab = jnp.pad(wab8, ((0, 0), (0, LANES - 2 * DN_H)))
    pad_lanes = lambda a: jnp.pad(a.astype(F32), (0, LANES - a.shape[0]))[None, :]
    pad_rows = lambda a: jnp.pad(a.astype(F32), (0, 8 - a.shape[0]))[:, None]
    w_router = jnp.pad(jnp.concatenate([w_router_group, w_router_expert], axis=-1),
                       ((0, 0), (0, LANES - N_GROUPS - N_EXPERTS))).astype(BF16)
    b_router = pad_lanes(jnp.concatenate([b_router_group, b_router_expert]))
    row = lambda a: a.astype(F32)[None, :]
    return dict(
        norm_mix=row(norm_mix), w_main=w_main, wab=wab, wabt=wab8.T,
        alog_c=pad_lanes(dn_a_log), dtb_c=pad_lanes(dn_dt_bias),
        alog_r=pad_rows(dn_a_log), dtb_r=pad_rows(dn_dt_bias),
        conv_w=conv_w.astype(F32), dn_norm=row(dn_norm),
        w_ret_up=w_ret_up.astype(BF16), w_dn_up=w_dn_up.astype(BF16), w_out=w_out.astype(BF16),
        norm_ffn=row(norm_ffn), w_router=w_router, b_router=b_router,
        w_gate_up=w_gate_up.astype(BF16), w_down=w_down.astype(BF16),
        norm_ple=row(norm_ple), w_ple_gate=w_ple_gate.astype(BF16), w_ple_proj=w_ple_proj.astype(BF16),
        norm_final=row(norm_final),
    )


def kernel(x_prompt, x_sample, p_prompt, p_sample, state_ret, state_dn, state_conv, norm_mix, w_in, conv_w, dn_a_log, dn_dt_bias, dn_norm, w_ret_up, w_dn_up, w_out, norm_ffn, w_router_group, b_router_group, w_router_expert, b_router_expert, w_gate_up, w_down, norm_ple, w_ple_gate, w_ple_proj, norm_final):
    depth = w_in.shape[0]
    assert depth == 1, "one layer: the final norm is fused into the layer's last kernel"
    bp, lp, _ = x_prompt.shape
    bs, ls, _ = x_sample.shape
    wts = _prep_weights(norm_mix[0], w_in[0], conv_w[0], dn_a_log[0], dn_dt_bias[0], dn_norm[0],
                        w_ret_up[0], w_dn_up[0], w_out[0], norm_ffn[0], w_router_group[0],
                        b_router_group[0], w_router_expert[0], b_router_expert[0], w_gate_up[0],
                        w_down[0], norm_ple[0], w_ple_gate[0], w_ple_proj[0], norm_final)

    tm = 256
    cfg_p = dict(tm=tm, bt=1, ret_tt=256, dn_tt=256)
    cfg_s = dict(tm=tm, bt=16, ret_tt=ls, dn_tt=ls)

    pos_p = _rope_tables(jnp.arange(lp, dtype=jnp.int32))
    cos_s, sin_s = _rope_tables(PAST_LEN + jnp.arange(ls, dtype=jnp.int32))
    pos_s = (jnp.tile(cos_s, (bs, 1)), jnp.tile(sin_s, (bs, 1)))

    zero_ret = jnp.zeros((bp, RET_H, RET_DK, RET_DV), F32)
    zero_dn = jnp.zeros((bp, DN_H, DN_DK, DN_DV), F32)
    zero_conv = jnp.zeros((bp, CONV_W - 1, DN_CONV_CH), F32)

    y_p, r_p, d_p, c_p = _layer(x_prompt, p_prompt[0], zero_ret, zero_dn, zero_conv,
                                pos_p, lp // tm, wts, cfg_p)
    y_s, r_s, d_s, c_s = _layer(x_sample, p_sample[0], state_ret[0], state_dn[0], state_conv[0],
                                pos_s, (bs * ls) // tm, wts, cfg_s)
    return (y_p, y_s, r_p[None], d_p[None], c_p[None], r_s[None], d_s[None], c_s[None])
```

```python
import functools

import jax
import jax.numpy as jnp
from jax import lax
from jax.experimental import pallas as pl
from jax.experimental.pallas import tpu as pltpu

F32 = jnp.float32
BF16 = jnp.bfloat16

D_MODEL = 1024
RET_H, RET_DK, RET_DV = 4, 64, 128
DN_H, DN_DK, DN_DV = 4, 128, 128
CONV_W = 4
ROPE_BASE = 10000.0
PAST_LEN = 16384
N_GROUPS, EXPERTS_PER_GROUP = 4, 4
N_EXPERTS = N_GROUPS * EXPERTS_PER_GROUP
D_EXPERT = 256
PLE_DIM = 256
EPS = 1e-6

RET_QK = RET_H * RET_DK
RET_VW = RET_H * RET_DV
DN_QK = DN_H * DN_DK
DN_VW = DN_H * DN_DV
DN_CONV_CH = 2 * DN_QK + DN_VW
IN_WIDTHS = (RET_QK, RET_QK, RET_VW, RET_VW, DN_CONV_CH, DN_VW, DN_H, DN_H, D_MODEL, D_MODEL)

LANES = 128
VMEM_LIMIT = 56 * 1024 * 1024

_C_Q, _C_K, _C_V, _C_RG = 0, 256, 512, 1024
_C_DQKV, _C_DZ, _C_GA, _C_GB, _C_END = 1536, 3072, 3584, 4608, 5632


def _silu(x):
    return x * jax.nn.sigmoid(x)


def _softplus(x):
    return jnp.maximum(x, 0.0) + jnp.log1p(jnp.exp(-jnp.abs(x)))


def _dot(a, b):
    return jnp.dot(a, b, preferred_element_type=F32)


def _dot_nt(a, b):
    return lax.dot_general(a, b, (((1,), (1,)), ((), ())), preferred_element_type=F32)


def _dot_tn(a, b):
    return lax.dot_general(a, b, (((0,), (0,)), ((), ())), preferred_element_type=F32)


def _split3(x):
    hi = x.astype(BF16)
    r = x - hi.astype(F32)
    mid = r.astype(BF16)
    lo = (r - mid.astype(F32)).astype(BF16)
    return hi, mid, lo


def _rms_plain(x):
    return x * lax.rsqrt(jnp.mean(x * x, axis=-1, keepdims=True) + EPS)


def _proj_kernel(x_ref, gain_ref, w_ref, wab_ref, wabt_ref, cos_ref, sin_ref,
                 alog_c_ref, dtb_c_ref, alog_r_ref, dtb_r_ref,
                 q_ref, k_ref, v_ref, rg_ref, dqkv_ref, dz_ref, ga_ref, gb_ref, gbc_ref, gbr_ref):
    x = x_ref[...]
    u = x * lax.rsqrt(jnp.mean(x * x, axis=-1, keepdims=True) + EPS) * gain_ref[...]
    ub = u.astype(BF16)

    def mm(lo, hi):
        return _dot(ub, w_ref[:, lo:hi])

    cos = cos_ref[...]
    sin = sin_ref[...]
    lane = lax.broadcasted_iota(jnp.int32, cos.shape, 1)
    first_half = (lane % RET_DK) < (RET_DK // 2)

    def rot(t):
        swapped = jnp.where(first_half, pltpu.roll(t, RET_QK - RET_DK // 2, 1),
                            pltpu.roll(t, RET_DK // 2, 1))
        return t * cos + swapped * sin

    q_ref[...] = (rot(mm(_C_Q, _C_K)) * (RET_DK ** -0.5)).astype(BF16)
    k_ref[...] = rot(mm(_C_K, _C_V)).astype(BF16)
    v_ref[...] = mm(_C_V, _C_RG).astype(BF16)
    rg_ref[...] = _silu(mm(_C_RG, _C_DQKV)).astype(BF16)
    dqkv_ref[...] = mm(_C_DQKV, _C_DZ)
    dz_ref[...] = _silu(mm(_C_DZ, _C_GA)).astype(BF16)
    ga_ref[...] = jax.nn.sigmoid(mm(_C_GA, _C_GB)).astype(BF16)
    gb_ref[...] = jax.nn.sigmoid(mm(_C_GB, _C_END)).astype(BF16)

    dab = _dot(ub, wab_ref[...])
    lane_c = lax.broadcasted_iota(jnp.int32, dab.shape, 1)
    g_c = -jnp.exp(alog_c_ref[...]) * _softplus(dab + dtb_c_ref[...])
    gbc_ref[...] = jnp.where(lane_c < DN_H, g_c, jax.nn.sigmoid(dab))
    dabt = _dot_nt(wabt_ref[...], ub)
    row_r = lax.broadcasted_iota(jnp.int32, dabt.shape, 0)
    g_r = -jnp.exp(alog_r_ref[...]) * _softplus(dabt + dtb_r_ref[...])
    gbr_ref[...] = jnp.where(row_r < DN_H, g_r, jax.nn.sigmoid(dabt))


def _proj_call(x2d, gain, w_main, wab, wabt, cos_t, sin_t, table_blocks,
               alog_c, dtb_c, alog_r, dtb_r, tm):
    n = x2d.shape[0]
    grid = (n // tm,)
    tok = lambda w: pl.BlockSpec((tm, w), lambda i: (i, 0))
    const = lambda a: pl.BlockSpec(a.shape, lambda i: (0,) * a.ndim, pipeline_mode=pl.Buffered(1))
    tab = pl.BlockSpec((tm, RET_QK), lambda i: (i % table_blocks, 0))
    out_shapes = (
        jax.ShapeDtypeStruct((n, RET_QK), BF16), jax.ShapeDtypeStruct((n, RET_QK), BF16),
        jax.ShapeDtypeStruct((n, RET_VW), BF16), jax.ShapeDtypeStruct((n, RET_VW), BF16),
        jax.ShapeDtypeStruct((n, DN_CONV_CH), F32), jax.ShapeDtypeStruct((n, DN_VW), BF16),
        jax.ShapeDtypeStruct((n, D_MODEL), BF16), jax.ShapeDtypeStruct((n, D_MODEL), BF16),
        jax.ShapeDtypeStruct((n, LANES), F32), jax.ShapeDtypeStruct((8, n), F32),
    )
    out_specs = (tok(RET_QK), tok(RET_QK), tok(RET_VW), tok(RET_VW), tok(DN_CONV_CH), tok(DN_VW),
                 tok(D_MODEL), tok(D_MODEL), tok(LANES), pl.BlockSpec((8, tm), lambda i: (0, i)))
    return pl.pallas_call(
        _proj_kernel, grid=grid, name="proj",
        in_specs=[tok(D_MODEL), const(gain), const(w_main), const(wab), const(wabt), tab, tab,
                  const(alog_c), const(dtb_c), const(alog_r), const(dtb_r)],
        out_specs=out_specs, out_shape=out_shapes,
        compiler_params=pltpu.CompilerParams(dimension_semantics=("parallel",),
                                             vmem_limit_bytes=VMEM_LIMIT),
    )(x2d, gain, w_main, wab, wabt, cos_t, sin_t, alog_c, dtb_c, alog_r, dtb_r)


def _ret_kernel(q_ref, k_ref, v_ref, rg_ref, s0_ref, dec_ref, qd_ref, kd_ref, cd_ref,
                o_ref, s_ref, *, bt):
    @pl.when(pl.program_id(1) == 0)
    def _():
        s_ref[...] = s0_ref[...]

    for b in range(bt):
        q = q_ref[b]
        k = k_ref[b]
        v = v_ref[b]
        rg = rg_ref[b]
        for h in range(RET_H):
            qh = q[:, h * RET_DK:(h + 1) * RET_DK]
            kh = k[:, h * RET_DK:(h + 1) * RET_DK]
            vh = v[:, h * RET_DV:(h + 1) * RET_DV]
            s = s_ref[b, h]
            scores = _dot_nt(qh, kh) * dec_ref[h]
            q_dec = (qh.astype(F32) * qd_ref[h]).astype(BF16)
            o = _dot(scores.astype(BF16), vh) + _dot(q_dec, s.astype(BF16))
            k_dec = (kh.astype(F32) * kd_ref[h]).astype(BF16)
            s_ref[b, h] = s * cd_ref[h] + _dot_tn(k_dec, vh)
            gate = rg[:, h * RET_DV:(h + 1) * RET_DV].astype(F32)
            o_ref[b, :, h * RET_DV:(h + 1) * RET_DV] = (_rms_plain(o) * gate).astype(BF16)


def _ret_call(q, k, v, rg, s0, dec, qd, kd, cd, bt, tt):
    bsz, seq, _ = q.shape
    grid = (bsz // bt, seq // tt)
    tok = lambda w: pl.BlockSpec((bt, tt, w), lambda b, l: (b, l, 0))
    const = lambda a: pl.BlockSpec(a.shape, lambda b, l: (0,) * a.ndim)
    st = pl.BlockSpec((bt, RET_H, RET_DK, RET_DV), lambda b, l: (b, 0, 0, 0))
    return pl.pallas_call(
        functools.partial(_ret_kernel, bt=bt), grid=grid, name="ret",
        in_specs=[tok(RET_QK), tok(RET_QK), tok(RET_VW), tok(RET_VW), st,
                  const(dec), const(qd), const(kd), const(cd)],
        out_specs=(tok(RET_VW), st),
        out_shape=(jax.ShapeDtypeStruct((bsz, seq, RET_VW), BF16),
                   jax.ShapeDtypeStruct(s0.shape, F32)),
        compiler_params=pltpu.CompilerParams(dimension_semantics=("parallel", "arbitrary"),
                                             vmem_limit_bytes=VMEM_LIMIT),
    )(q, k, v, rg, s0, dec, qd, kd, cd)


def _dn_kernel(x_ref, gc_ref, gr_ref, dz_ref, cw_ref, dnorm_ref, s0_ref, c0_ref, lblk_ref, ublk_ref, lvl_ref,
               o_ref, s_ref, c_ref, ext_ref, *, bt, tt, cu):
    pad = 8
    tail = CONV_W - 1
    r = bt * tt
    blk = min(tt, cu)
    n_units = r // cu
    n_seg = cu // blk
    n_lvl = lvl_ref.shape[0]

    @pl.when(pl.program_id(1) == 0)
    def _():
        s_ref[...] = s0_ref[...]
        ext_ref[:, pad - tail:pad, :] = c0_ref[...]

    ext_ref[:, pad:pad + tt, :] = x_ref[...]
    conv = ext_ref[:, pad - tail:pad - tail + tt, :] * cw_ref[0:1, :]
    for j in range(1, CONV_W):
        conv = conv + ext_ref[:, pad - tail + j:pad - tail + j + tt, :] * cw_ref[j:j + 1, :]
    new_tail = x_ref[:, tt - tail:tt, :]
    ext_ref[:, pad - tail:pad, :] = new_tail
    c_ref[...] = new_tail
    conv = _silu(conv).reshape(r, DN_CONV_CH)

    ri = lax.broadcasted_iota(jnp.int32, (cu, cu), 0)
    ci = lax.broadcasted_iota(jnp.int32, (cu, cu), 1)
    same = (ri // blk) == (ci // blk)
    incl = same & (ri >= ci)
    strict = same & (ri > ci)

    gc = gc_ref[...]
    g_hi, g_mid, g_lo = _split3(gc)
    r_hi, r_mid, r_lo = _split3(gr_ref[...])
    lblk = lblk_ref[...]
    ublk = ublk_ref[...]

    units = [(n, h) for n in range(n_units) for h in range(DN_H)]
    cum_c, cum_r = [], []
    for n in range(n_units):
        rows = slice(n * cu, (n + 1) * cu)
        cum_c.append(_dot(lblk, g_hi[rows]) + _dot(lblk, g_mid[rows]) + _dot(lblk, g_lo[rows]))
        cum_r.append(_dot(r_hi[:, rows], ublk) + _dot(r_mid[:, rows], ublk) + _dot(r_lo[:, rows], ublk))

    qn, kn, kb, g_col, dec, a_mat, ab, d, rhs = {}, {}, {}, {}, {}, {}, {}, {}, {}
    for u in units:
        n, h = u
        rows = slice(n * cu, (n + 1) * cu)
        cq = conv[rows, h * DN_DK:(h + 1) * DN_DK]
        ck = conv[rows, DN_QK + h * DN_DK:DN_QK + (h + 1) * DN_DK]
        vc = conv[rows, 2 * DN_QK + h * DN_DV:2 * DN_QK + (h + 1) * DN_DV]
        qn_u = cq * lax.rsqrt(jnp.sum(cq * cq, axis=-1, keepdims=True) + EPS) * (DN_DK ** -0.5)
        kn[u] = ck * lax.rsqrt(jnp.sum(ck * ck, axis=-1, keepdims=True) + EPS)
        g_col[u] = cum_c[n][:, h:h + 1]
        g_row = cum_r[n][h:h + 1, :]
        beta = gc[rows, DN_H + h:DN_H + h + 1]
        dec[u] = jnp.where(incl, jnp.exp(jnp.where(incl, g_col[u] - g_row, 0.0)), 0.0)
        kb[u] = kn[u].astype(BF16)
        a_mat[u] = jnp.where(strict, _dot_nt(kb[u], kb[u]) * dec[u] * beta, 0.0)
        ab[u] = a_mat[u].astype(BF16)
        e_g = jnp.exp(g_col[u])
        rhs[u] = jnp.concatenate([beta * vc, beta * e_g * kn[u]], axis=-1)
        qn[u] = (qn_u, qn_u * e_g)
        d[u] = -(lvl_ref[0] * a_mat[u])
    for lv in range(1, n_lvl):
        for u in units:
            db = d[u].astype(BF16)
            w = a_mat[u] + _dot(db, ab[u])
            z = w + _dot(w.astype(BF16), db)
            d[u] = d[u] - lvl_ref[lv] * z
    sol, p_mat = {}, {}
    for u in units:
        sol[u] = rhs[u] + _dot(d[u].astype(BF16), rhs[u].astype(BF16))
        p_mat[u] = (_dot_nt(qn[u][0].astype(BF16), kb[u]) * dec[u]).astype(BF16)

    for n in range(n_units):
        for h in range(DN_H):
            u = (n, h)
            u_mat, w_mat = sol[u][:, :DN_DV], sol[u][:, DN_DV:]
            qe = qn[u][1]
            deltas, qss = [], []
            for j in range(n_seg):
                r0 = j * blk
                b = (n * cu + r0) // tt
                sb = s_ref[b, h].astype(BF16)
                lhs = jnp.concatenate([w_mat[r0:r0 + blk], qe[r0:r0 + blk]], axis=0).astype(BF16)
                ws_qs = _dot(lhs, sb)
                deltas.append(u_mat[r0:r0 + blk] - ws_qs[:blk])
                qss.append(ws_qs[blk:])
            delta = deltas[0] if n_seg == 1 else jnp.concatenate(deltas, axis=0)
            q_s = qss[0] if n_seg == 1 else jnp.concatenate(qss, axis=0)
            delta_b = delta.astype(BF16)
            o = q_s + _dot(p_mat[u], delta_b)
            for j in range(n_seg):
                r0 = j * blk
                b = (n * cu + r0) // tt
                g_last = g_col[u][r0 + blk - 1:r0 + blk, :]
                k_dec = (kn[u][r0:r0 + blk] * jnp.exp(g_last - g_col[u][r0:r0 + blk])).astype(BF16)
                s_ref[b, h] = s_ref[b, h] * jnp.exp(g_last) + _dot_tn(k_dec, delta_b[r0:r0 + blk])
            gate = dz_ref[n * cu:(n + 1) * cu, h * DN_DV:(h + 1) * DN_DV].astype(F32)
            o_ref[n * cu:(n + 1) * cu, h * DN_DV:(h + 1) * DN_DV] = (
                _rms_plain(o) * dnorm_ref[...] * gate).astype(BF16)


def _dn_call(dqkv, gbc, gbr, dz, conv_w, dn_norm, s0, c0, lblk, ublk, lvl, bt, tt, cu):
    bsz, seq, _ = dqkv.shape
    nl = seq // tt
    r = bt * tt
    grid = (bsz // bt, nl)
    tok = lambda w: pl.BlockSpec((r, w), lambda b, l: (b * nl + l, 0))
    const = lambda a: pl.BlockSpec(a.shape, lambda b, l: (0,) * a.ndim)
    st = pl.BlockSpec((bt, DN_H, DN_DK, DN_DV), lambda b, l: (b, 0, 0, 0))
    cst = pl.BlockSpec((bt, CONV_W - 1, DN_CONV_CH), lambda b, l: (b, 0, 0))
    return pl.pallas_call(
        functools.partial(_dn_kernel, bt=bt, tt=tt, cu=cu), grid=grid, name="dn",
        in_specs=[pl.BlockSpec((bt, tt, DN_CONV_CH), lambda b, l: (b, l, 0)), tok(LANES),
                  pl.BlockSpec((8, r), lambda b, l: (0, b * nl + l)),
                  tok(DN_VW), const(conv_w), const(dn_norm), st, cst, const(lblk), const(ublk), const(lvl)],
        out_specs=(tok(DN_VW), st, cst),
        out_shape=(jax.ShapeDtypeStruct((bsz * seq, DN_VW), BF16),
                   jax.ShapeDtypeStruct(s0.shape, F32),
                   jax.ShapeDtypeStruct(c0.shape, F32)),
        scratch_shapes=[pltpu.VMEM((bt, tt + 8, DN_CONV_CH), F32)],
        compiler_params=pltpu.CompilerParams(dimension_semantics=("parallel", "arbitrary"),
                                             vmem_limit_bytes=VMEM_LIMIT),
    )(dqkv, gbc, gbr, dz, conv_w, dn_norm, s0, c0, lblk, ublk, lvl)


def _mix_kernel(x_ref, og_ref, od_ref, ga_ref, gb_ref, wru_ref, wdu_ref, wo_ref, h_ref):
    branch_a = _dot(og_ref[...], wru_ref[...])
    branch_b = _dot(od_ref[...], wdu_ref[...])
    merged = ga_ref[...].astype(F32) * branch_a + gb_ref[...].astype(F32) * branch_b
    h_ref[...] = x_ref[...] + _dot(merged.astype(BF16), wo_ref[...])


def _mix_call(x2d, og, od, ga, gb, wru, wdu, wo, tm):
    n = x2d.shape[0]
    tok = lambda w: pl.BlockSpec((tm, w), lambda i: (i, 0))
    const = lambda a: pl.BlockSpec(a.shape, lambda i: (0,) * a.ndim, pipeline_mode=pl.Buffered(1))
    return pl.pallas_call(
        _mix_kernel, grid=(n // tm,), name="mix",
        in_specs=[tok(D_MODEL), tok(RET_VW), tok(DN_VW), tok(D_MODEL), tok(D_MODEL),
                  const(wru), const(wdu), const(wo)],
        out_specs=tok(D_MODEL), out_shape=jax.ShapeDtypeStruct((n, D_MODEL), F32),
        compiler_params=pltpu.CompilerParams(dimension_semantics=("parallel",),
                                             vmem_limit_bytes=VMEM_LIMIT),
    )(x2d, og, od, ga, gb, wru, wdu, wo)


_R_EXP = N_GROUPS


def _ffn_kernel(h_ref, p_ref, nffn_ref, wr_ref, br_ref, wgu_ref, wdn_ref, nple_ref, wpg_ref, wpp_ref,
                nfin_ref, y_ref, hid_ref):
    h = h_ref[...]
    u = _rms_plain(h) * nffn_ref[...]
    ub = u.astype(BF16)

    logits = _dot(ub, wr_ref[...]) + br_ref[...]
    lane = lax.broadcasted_iota(jnp.int32, logits.shape, 1)
    neg = jnp.float32(-jnp.inf)
    big = jnp.int32(LANES)
    gl = jnp.where(lane < N_GROUPS, logits, neg)
    g_max = jnp.max(gl, axis=-1, keepdims=True)
    p_grp = 1.0 / jnp.sum(jnp.exp(gl - g_max), axis=-1, keepdims=True)
    grp = jnp.min(jnp.where(gl == g_max, lane, big), axis=-1, keepdims=True)
    e_idx = lane - _R_EXP
    in_grp = (e_idx >= 0) & (e_idx < N_EXPERTS) & ((e_idx // EXPERTS_PER_GROUP) == grp)
    el = jnp.where(in_grp, logits, neg)
    v1 = jnp.max(el, axis=-1, keepdims=True)
    i1 = jnp.min(jnp.where(el == v1, lane, big), axis=-1, keepdims=True)
    el2 = jnp.where(lane == i1, neg, el)
    v2 = jnp.max(el2, axis=-1, keepdims=True)
    i2 = jnp.min(jnp.where(el2 == v2, lane, big), axis=-1, keepdims=True)
    e2 = jnp.exp(v2 - v1)
    w1 = 1.0 / (1.0 + e2)
    w2 = e2 / (1.0 + e2)
    combine = jnp.where(lane == i1, w1 * p_grp, 0.0) + jnp.where(lane == i2, w2 * p_grp, 0.0)

    for e in range(N_EXPERTS):
        gu = _dot(ub, wgu_ref[e])
        hid = _silu(gu[:, :D_EXPERT]) * gu[:, D_EXPERT:]
        hid_ref[:, e * D_EXPERT:(e + 1) * D_EXPERT] = (
            combine[:, _R_EXP + e:_R_EXP + e + 1] * hid).astype(BF16)
    h = h + _dot(hid_ref[...], wdn_ref[...])

    u3 = (_rms_plain(h) * nple_ref[...]).astype(BF16)
    gate = jax.nn.sigmoid(_dot(u3, wpg_ref[...]))
    h = h + gate * _dot(p_ref[...].astype(BF16), wpp_ref[...])
    y_ref[...] = _rms_plain(h) * nfin_ref[...]


def _ffn_call(h2d, p2d, nffn, wr, br, wgu, wdn, nple, wpg, wpp, nfin, tm):
    n = h2d.shape[0]
    tok = lambda w: pl.BlockSpec((tm, w), lambda i: (i, 0))
    const = lambda a: pl.BlockSpec(a.shape, lambda i: (0,) * a.ndim, pipeline_mode=pl.Buffered(1))
    return pl.pallas_call(
        _ffn_kernel, grid=(n // tm,), name="ffn",
        in_specs=[tok(D_MODEL), tok(PLE_DIM), const(nffn), const(wr), const(br), const(wgu), const(wdn),
                  const(nple), const(wpg), const(wpp), const(nfin)],
        out_specs=tok(D_MODEL), out_shape=jax.ShapeDtypeStruct((n, D_MODEL), F32),
        scratch_shapes=[pltpu.VMEM((tm, N_EXPERTS * D_EXPERT), BF16)],
        compiler_params=pltpu.CompilerParams(dimension_semantics=("parallel",),
                                             vmem_limit_bytes=VMEM_LIMIT),
    )(h2d, p2d, nffn, wr, br, wgu, wdn, nple, wpg, wpp, nfin)


def _rope_tables(pos):
    half = RET_DK // 2
    inv = 1.0 / (ROPE_BASE ** (jnp.arange(half, dtype=F32) / half))
    ang = pos.astype(F32)[:, None] * inv[None, :]
    cos, sin = jnp.cos(ang), jnp.sin(ang)
    cos_t = jnp.tile(jnp.concatenate([cos, cos], axis=-1), (1, RET_H))
    sin_t = jnp.tile(jnp.concatenate([-sin, sin], axis=-1), (1, RET_H))
    return cos_t, sin_t


def _ret_tables(c):
    log_gamma = jnp.log(1.0 - 2.0 ** (-5.0 - jnp.arange(RET_H, dtype=F32)))
    idx = jnp.arange(c, dtype=F32)
    diff = idx[:, None] - idx[None, :]
    causal = diff >= 0
    lg = log_gamma[:, None, None]
    decay = jnp.where(causal, jnp.exp(jnp.where(causal, diff, 0.0) * lg), 0.0)
    q_dec = jnp.exp((idx + 1.0)[None, :] * log_gamma[:, None])[..., None]
    k_dec = jnp.exp((c - 1.0 - idx)[None, :] * log_gamma[:, None])[..., None]
    chunk_dec = jnp.exp(c * log_gamma)[:, None, None]
    return decay, q_dec, k_dec, chunk_dec


def _dn_tables(r, c):
    i = jnp.arange(r)
    ri, ci = i[:, None], i[None, :]
    same = (ri // c) == (ci // c)
    lblk = ((ri >= ci) & same).astype(BF16)
    x = ri ^ ci
    lvl, m = [], 1
    while m < c:
        lvl.append((same & (ri > ci) & (x >= m) & (x < 2 * m)).astype(F32))
        m *= 2
    return lblk, lblk.T, jnp.stack(lvl)


def _layer(x, p, s_ret, s_dn, s_conv, pos_table, table_blocks, wts, cfg):
    bsz, seq, _ = x.shape
    n = bsz * seq
    x2d = x.reshape(n, D_MODEL)
    cos_t, sin_t = pos_table
    (q, k, v, rg, dqkv, dz, ga, gb, gbc, gbr) = _proj_call(
        x2d, wts["norm_mix"], wts["w_main"], wts["wab"], wts["wabt"], cos_t, sin_t, table_blocks,
        wts["alog_c"], wts["dtb_c"], wts["alog_r"], wts["dtb_r"], cfg["tm"])

    r3 = lambda a: a.reshape(bsz, seq, a.shape[-1])
    dec, qd, kd, cd = _ret_tables(cfg["ret_tt"])
    og, ret_new = _ret_call(r3(q), r3(k), r3(v), r3(rg), s_ret, dec, qd, kd, cd,
                            cfg["bt"], cfg["ret_tt"])

    lblk, ublk, lvl = _dn_tables(cfg["dn_cu"], min(cfg["dn_tt"], cfg["dn_cu"]))
    od, dn_new, conv_new = _dn_call(r3(dqkv), gbc, gbr, dz, wts["conv_w"], wts["dn_norm"],
                                    s_dn, s_conv, lblk, ublk, lvl, cfg["bt"], cfg["dn_tt"], cfg["dn_cu"])

    h1 = _mix_call(x2d, og.reshape(n, RET_VW), od, ga, gb,
                   wts["w_ret_up"], wts["w_dn_up"], wts["w_out"], cfg["tm"])
    y = _ffn_call(h1, p.reshape(n, PLE_DIM), wts["norm_ffn"], wts["w_router"], wts["b_router"],
                  wts["w_gate_up"], wts["w_down"], wts["norm_ple"], wts["w_ple_gate"], wts["w_ple_proj"],
                  wts["norm_final"], cfg["tm"])
    return y.reshape(bsz, seq, D_MODEL), ret_new, dn_new, conv_new


def _prep_weights(norm_mix, w_in, conv_w, dn_a_log, dn_dt_bias, dn_norm, w_ret_up, w_dn_up, w_out,
                  norm_ffn, w_router_group, b_router_group, w_router_expert, b_router_expert,
                  w_gate_up, w_down, norm_ple, w_ple_gate, w_ple_proj, norm_final):
    pts, acc = [], 0
    for w in IN_WIDTHS[:-1]:
        acc += w
        pts.append(acc)
    rq, rk, rv, rg, dqkv, dz, da, db, ga, gb = jnp.split(w_in, pts, axis=-1)
    w_main = jnp.concatenate([rq, rk, rv, rg, dqkv, dz, ga, gb], axis=-1).astype(BF16)
    wab8 = jnp.concatenate([da, db], axis=-1).astype(BF16)
    wab = jnp.pad(wab8, ((0, 0), (0, LANES - 2 * DN_H)))
    pad_lanes = lambda a: jnp.pad(a.astype(F32), (0, LANES - a.shape[0]))[None, :]
    pad_rows = lambda a: jnp.pad(a.astype(F32), (0, 8 - a.shape[0]))[:, None]
    w_router = jnp.pad(jnp.concatenate([w_router_group, w_router_expert], axis=-1),
                       ((0, 0), (0, LANES - N_GROUPS - N_EXPERTS))).astype(BF16)
    b_router = pad_lanes(jnp.concatenate([b_router_group, b_router_expert]))
    row = lambda a: a.astype(F32)[None, :]
    return dict(
        norm_mix=row(norm_mix), w_main=w_main, wab=wab, wabt=wab8.T,
        alog_c=pad_lanes(dn_a_log), dtb_c=pad_lanes(dn_dt_bias),
        alog_r=pad_rows(dn_a_log), dtb_r=pad_rows(dn_dt_bias),
        conv_w=conv_w.astype(F32), dn_norm=row(dn_norm),
        w_ret_up=w_ret_up.astype(BF16), w_dn_up=w_dn_up.astype(BF16), w_out=w_out.astype(BF16),
        norm_ffn=row(norm_ffn), w_router=w_router, b_router=b_router,
        w_gate_up=w_gate_up.astype(BF16),
        w_down=w_down.astype(BF16).reshape(N_EXPERTS * D_EXPERT, D_MODEL),
        norm_ple=row(norm_ple), w_ple_gate=w_ple_gate.astype(BF16), w_ple_proj=w_ple_proj.astype(BF16),
        norm_final=row(norm_final),
    )


def kernel(x_prompt, x_sample, p_prompt, p_sample, state_ret, state_dn, state_conv, norm_mix, w_in, conv_w, dn_a_log, dn_dt_bias, dn_norm, w_ret_up, w_dn_up, w_out, norm_ffn, w_router_group, b_router_group, w_router_expert, b_router_expert, w_gate_up, w_down, norm_ple, w_ple_gate, w_ple_proj, norm_final):
    depth = w_in.shape[0]
    assert depth == 1, "one layer: the final norm is fused into the layer's last kernel"
    bp, lp, _ = x_prompt.shape
    bs, ls, _ = x_sample.shape
    wts = _prep_weights(norm_mix[0], w_in[0], conv_w[0], dn_a_log[0], dn_dt_bias[0], dn_norm[0],
                        w_ret_up[0], w_dn_up[0], w_out[0], norm_ffn[0], w_router_group[0],
                        b_router_group[0], w_router_expert[0], b_router_expert[0], w_gate_up[0],
                        w_down[0], norm_ple[0], w_ple_gate[0], w_ple_proj[0], norm_final)

    tm = 512
    cfg_p = dict(tm=tm, bt=1, ret_tt=256, dn_tt=256, dn_cu=128)
    cfg_s = dict(tm=tm, bt=16, ret_tt=ls, dn_tt=ls, dn_cu=16 * ls)

    pos_p = _rope_tables(jnp.arange(lp, dtype=jnp.int32))
    cos_s, sin_s = _rope_tables(PAST_LEN + jnp.arange(ls, dtype=jnp.int32))
    pos_s = (jnp.tile(cos_s, (bs, 1)), jnp.tile(sin_s, (bs, 1)))

    zero_ret = jnp.zeros((bp, RET_H, RET_DK, RET_DV), F32)
    zero_dn = jnp.zeros((bp, DN_H, DN_DK, DN_DV), F32)
    zero_conv = jnp.zeros((bp, CONV_W - 1, DN_CONV_CH), F32)

    y_p, r_p, d_p, c_p = _layer(x_prompt, p_prompt[0], zero_ret, zero_dn, zero_conv,
                                pos_p, lp // tm, wts, cfg_p)
    y_s, r_s, d_s, c_s = _layer(x_sample, p_sample[0], state_ret[0], state_dn[0], state_conv[0],
                                pos_s, (bs * ls) // tm, wts, cfg_s)
    return (y_p, y_s, r_p[None], d_p[None], c_p[None], r_s[None], d_s[None], c_s[None])
```

```python
import functools

import jax
import jax.numpy as jnp
from jax import lax
from jax.experimental import pallas as pl
from jax.experimental.pallas import tpu as pltpu

F32 = jnp.float32
BF16 = jnp.bfloat16

D_MODEL = 1024
RET_H, RET_DK, RET_DV = 4, 64, 128
DN_H, DN_DK, DN_DV = 4, 128, 128
CONV_W = 4
ROPE_BASE = 10000.0
PAST_LEN = 16384
N_GROUPS, EXPERTS_PER_GROUP = 4, 4
N_EXPERTS = N_GROUPS * EXPERTS_PER_GROUP
D_EXPERT = 256
PLE_DIM = 256
EPS = 1e-6

RET_QK = RET_H * RET_DK
RET_VW = RET_H * RET_DV
DN_QK = DN_H * DN_DK
DN_VW = DN_H * DN_DV
DN_CONV_CH = 2 * DN_QK + DN_VW
IN_WIDTHS = (RET_QK, RET_QK, RET_VW, RET_VW, DN_CONV_CH, DN_VW, DN_H, DN_H, D_MODEL, D_MODEL)

LANES = 128
VMEM_LIMIT = 56 * 1024 * 1024

_C_Q, _C_K, _C_V, _C_RG = 0, 256, 512, 1024
_C_DQKV, _C_DZ, _C_GA, _C_GB, _C_END = 1536, 3072, 3584, 4608, 5632


def _silu(x):
    return x * jax.nn.sigmoid(x)


def _softplus(x):
    return jnp.maximum(x, 0.0) + jnp.log1p(jnp.exp(-jnp.abs(x)))


def _dot(a, b):
    return jnp.dot(a, b, preferred_element_type=F32)


def _dot_nt(a, b):
    return lax.dot_general(a, b, (((1,), (1,)), ((), ())), preferred_element_type=F32)


def _dot_tn(a, b):
    return lax.dot_general(a, b, (((0,), (0,)), ((), ())), preferred_element_type=F32)


def _split3(x):
    hi = x.astype(BF16)
    r = x - hi.astype(F32)
    mid = r.astype(BF16)
    lo = (r - mid.astype(F32)).astype(BF16)
    return hi, mid, lo


def _rms_plain(x):
    return x * lax.rsqrt(jnp.mean(x * x, axis=-1, keepdims=True) + EPS)


def _proj_kernel(x_ref, gain_ref, w_ref, wab_ref, wabt_ref, cos_ref, sin_ref,
                 alog_c_ref, dtb_c_ref, alog_r_ref, dtb_r_ref, cw_ref, c0_ref,
                 q_ref, k_ref, v_ref, rg_ref, dq_ref, dk_ref, dv_ref, dz_ref, ga_ref, gb_ref,
                 gbc_ref, gbr_ref, c_ref, ext_ref, *, bt, tt, nl):
    pad = 8
    tail = CONV_W - 1

    @pl.when(pl.program_id(0) % nl == 0)
    def _():
        ext_ref[:, pad - tail:pad, :] = c0_ref[...]

    x = x_ref[...]
    u = x * lax.rsqrt(jnp.mean(x * x, axis=-1, keepdims=True) + EPS) * gain_ref[...]
    ub = u.astype(BF16)

    def mm(lo, hi):
        return _dot(ub, w_ref[:, lo:hi])

    ext_ref[:, pad:pad + tt, :] = mm(_C_DQKV, _C_DZ).reshape(bt, tt, DN_CONV_CH)
    conv = ext_ref[:, pad - tail:pad - tail + tt, :] * cw_ref[0:1, :]
    for j in range(1, CONV_W):
        conv = conv + ext_ref[:, pad - tail + j:pad - tail + j + tt, :] * cw_ref[j:j + 1, :]
    new_tail = ext_ref[:, pad + tt - tail:pad + tt, :]
    ext_ref[:, pad - tail:pad, :] = new_tail
    c_ref[...] = new_tail
    conv = _silu(conv).reshape(bt * tt, DN_CONV_CH)
    for h in range(DN_H):
        cq = conv[:, h * DN_DK:(h + 1) * DN_DK]
        ck = conv[:, DN_QK + h * DN_DK:DN_QK + (h + 1) * DN_DK]
        dq_ref[:, h * DN_DK:(h + 1) * DN_DK] = (
            cq * lax.rsqrt(jnp.sum(cq * cq, axis=-1, keepdims=True) + EPS) * (DN_DK ** -0.5)).astype(BF16)
        dk_ref[:, h * DN_DK:(h + 1) * DN_DK] = (
            ck * lax.rsqrt(jnp.sum(ck * ck, axis=-1, keepdims=True) + EPS)).astype(BF16)
    dv_ref[...] = conv[:, 2 * DN_QK:].astype(BF16)

    cos = cos_ref[...]
    sin = sin_ref[...]
    lane = lax.broadcasted_iota(jnp.int32, cos.shape, 1)
    first_half = (lane % RET_DK) < (RET_DK // 2)

    def rot(t):
        swapped = jnp.where(first_half, pltpu.roll(t, RET_QK - RET_DK // 2, 1),
                            pltpu.roll(t, RET_DK // 2, 1))
        return t * cos + swapped * sin

    q_ref[...] = (rot(mm(_C_Q, _C_K)) * (RET_DK ** -0.5)).astype(BF16)
    k_ref[...] = rot(mm(_C_K, _C_V)).astype(BF16)
    v_ref[...] = mm(_C_V, _C_RG).astype(BF16)
    rg_ref[...] = _silu(mm(_C_RG, _C_DQKV)).astype(BF16)
    dz_ref[...] = _silu(mm(_C_DZ, _C_GA)).astype(BF16)
    ga_ref[...] = jax.nn.sigmoid(mm(_C_GA, _C_GB)).astype(BF16)
    gb_ref[...] = jax.nn.sigmoid(mm(_C_GB, _C_END)).astype(BF16)

    dab = _dot(ub, wab_ref[...])
    lane_c = lax.broadcasted_iota(jnp.int32, dab.shape, 1)
    g_c = -jnp.exp(alog_c_ref[...]) * _softplus(dab + dtb_c_ref[...])
    gbc_ref[...] = jnp.where(lane_c < DN_H, g_c, jax.nn.sigmoid(dab))
    dabt = _dot_nt(wabt_ref[...], ub)
    row_r = lax.broadcasted_iota(jnp.int32, dabt.shape, 0)
    g_r = -jnp.exp(alog_r_ref[...]) * _softplus(dabt + dtb_r_ref[...])
    gbr_ref[...] = jnp.where(row_r < DN_H, g_r, jax.nn.sigmoid(dabt))


def _proj_call(x2d, gain, w_main, wab, wabt, cos_t, sin_t, alog_c, dtb_c, alog_r, dtb_r, conv_w, c0, tm):
    n = x2d.shape[0]
    bsz = c0.shape[0]
    seq = n // bsz
    tt = min(tm, seq)
    bt = tm // tt
    nl = seq // tt
    table_blocks = cos_t.shape[0] // tm
    tok = lambda w: pl.BlockSpec((tm, w), lambda i: (i, 0))
    const = lambda a: pl.BlockSpec(a.shape, lambda i: (0,) * a.ndim, pipeline_mode=pl.Buffered(1))
    tab = pl.BlockSpec((tm, RET_QK), lambda i: (i % table_blocks, 0))
    cst = pl.BlockSpec((bt, CONV_W - 1, DN_CONV_CH), lambda i: (i // nl, 0, 0))
    bf = lambda w: jax.ShapeDtypeStruct((n, w), BF16)
    out_shapes = (bf(RET_QK), bf(RET_QK), bf(RET_VW), bf(RET_VW), bf(DN_QK), bf(DN_QK), bf(DN_VW),
                  bf(DN_VW), bf(D_MODEL), bf(D_MODEL),
                  jax.ShapeDtypeStruct((n, LANES), F32), jax.ShapeDtypeStruct((8, n), F32),
                  jax.ShapeDtypeStruct(c0.shape, F32))
    out_specs = (tok(RET_QK), tok(RET_QK), tok(RET_VW), tok(RET_VW), tok(DN_QK), tok(DN_QK), tok(DN_VW),
                 tok(DN_VW), tok(D_MODEL), tok(D_MODEL), tok(LANES),
                 pl.BlockSpec((8, tm), lambda i: (0, i)), cst)
    return pl.pallas_call(
        functools.partial(_proj_kernel, bt=bt, tt=tt, nl=nl), grid=(n // tm,), name="proj",
        in_specs=[tok(D_MODEL), const(gain), const(w_main), const(wab), const(wabt), tab, tab,
                  const(alog_c), const(dtb_c), const(alog_r), const(dtb_r), const(conv_w), cst],
        out_specs=out_specs, out_shape=out_shapes,
        scratch_shapes=[pltpu.VMEM((bt, tt + 8, DN_CONV_CH), F32)],
        compiler_params=pltpu.CompilerParams(dimension_semantics=("arbitrary",),
                                             vmem_limit_bytes=VMEM_LIMIT),
    )(x2d, gain, w_main, wab, wabt, cos_t, sin_t, alog_c, dtb_c, alog_r, dtb_r, conv_w, c0)


def _ret_kernel(q_ref, k_ref, v_ref, rg_ref, s0_ref, dec_ref, hm_ref, qd_ref, kd_ref, cd_ref,
                o_ref, s_ref, *, bt, tt, cu):
    r = bt * tt
    blk = min(tt, cu)
    n_units = r // cu
    n_seg = cu // blk

    @pl.when(pl.program_id(1) == 0)
    def _():
        s_ref[...] = s0_ref[...]

    heads = range(RET_H)
    hrow = lambda a, h, m: a[h * m:(h + 1) * m]
    for n in range(n_units):
        rows = slice(n * cu, (n + 1) * cu)
        qf = q_ref[rows, :].astype(F32)
        kf = k_ref[rows, :].astype(F32)
        vf = v_ref[rows, :].astype(F32)
        q_m = [qf * hm_ref[h] for h in heads]
        q_d = [qf * qd_ref[h] for h in heads]
        k_d = [kf * kd_ref[h] for h in heads]
        v_h = [vf[:, h * RET_DV:(h + 1) * RET_DV] for h in heads]
        scores = (_dot_nt(jnp.concatenate(q_m, axis=0).astype(BF16), k_ref[rows, :])
                  * dec_ref[...]).astype(BF16)
        intra = [_dot(hrow(scores, h, cu), v_h[h].astype(BF16)) for h in heads]
        inters = []
        for j in range(n_seg):
            seg = slice(j * blk, (j + 1) * blk)
            b = (n * cu + j * blk) // tt
            s = s_ref[b]
            q_seg = jnp.concatenate([a[seg] for a in q_d], axis=0).astype(BF16)
            k_seg = jnp.concatenate([a[seg] for a in k_d], axis=0).astype(BF16)
            v_seg = jnp.concatenate([a[seg] for a in v_h], axis=0).astype(BF16)
            inters.append(_dot(q_seg, s.astype(BF16)))
            s_ref[b] = s * cd_ref[...] + _dot_tn(k_seg, v_seg)
        for h in heads:
            inter = jnp.concatenate([hrow(a, h, blk) for a in inters], axis=0)
            gate = rg_ref[rows, h * RET_DV:(h + 1) * RET_DV].astype(F32)
            o_ref[rows, h * RET_DV:(h + 1) * RET_DV] = (
                _rms_plain(intra[h] + inter) * gate).astype(BF16)


def _ret_call(q, k, v, rg, s0, tables, bsz, bt, tt, cu):
    seq = q.shape[0] // bsz
    nl = seq // tt
    r = bt * tt
    tok = lambda w: pl.BlockSpec((r, w), lambda b, l: (b * nl + l, 0))
    const = lambda a: pl.BlockSpec(a.shape, lambda b, l: (0,) * a.ndim)
    st = pl.BlockSpec((bt, RET_QK, RET_DV), lambda b, l: (b, 0, 0))
    return pl.pallas_call(
        functools.partial(_ret_kernel, bt=bt, tt=tt, cu=cu), grid=(bsz // bt, nl), name="ret",
        in_specs=[tok(RET_QK), tok(RET_QK), tok(RET_VW), tok(RET_VW), st] + [const(t) for t in tables],
        out_specs=(tok(RET_VW), st),
        out_shape=(jax.ShapeDtypeStruct((bsz * seq, RET_VW), BF16),
                   jax.ShapeDtypeStruct(s0.shape, F32)),
        compiler_params=pltpu.CompilerParams(dimension_semantics=("parallel", "arbitrary"),
                                             vmem_limit_bytes=VMEM_LIMIT),
    )(q, k, v, rg, s0, *tables)


def _dn_kernel(q_ref, k_ref, v_ref, gc_ref, gr_ref, dz_ref, dnorm_ref, s0_ref, lblk_ref, ublk_ref, lvl_ref,
               o_ref, s_ref, *, bt, tt, cu):
    r = bt * tt
    blk = min(tt, cu)
    n_units = r // cu
    n_seg = cu // blk
    n_lvl = lvl_ref.shape[0]

    @pl.when(pl.program_id(1) == 0)
    def _():
        s_ref[...] = s0_ref[...]

    ri = lax.broadcasted_iota(jnp.int32, (cu, cu), 0)
    ci = lax.broadcasted_iota(jnp.int32, (cu, cu), 1)
    same = (ri // blk) == (ci // blk)
    incl = same & (ri >= ci)
    strict = same & (ri > ci)

    gc = gc_ref[...]
    g_hi, g_mid, g_lo = _split3(gc)
    r_hi, r_mid, r_lo = _split3(gr_ref[...])
    lblk = lblk_ref[...]
    ublk = ublk_ref[...]

    units = [(n, h) for n in range(n_units) for h in range(DN_H)]
    cum_c, cum_r = [], []
    for n in range(n_units):
        rows = slice(n * cu, (n + 1) * cu)
        cum_c.append(_dot(lblk, g_hi[rows]) + _dot(lblk, g_mid[rows]) + _dot(lblk, g_lo[rows]))
        cum_r.append(_dot(r_hi[:, rows], ublk) + _dot(r_mid[:, rows], ublk) + _dot(r_lo[:, rows], ublk))

    qn, kn, kb, g_col, dec, a_mat, ab, d, rhs = {}, {}, {}, {}, {}, {}, {}, {}, {}
    for u in units:
        n, h = u
        rows = slice(n * cu, (n + 1) * cu)
        qb = q_ref[rows, h * DN_DK:(h + 1) * DN_DK]
        kb[u] = k_ref[rows, h * DN_DK:(h + 1) * DN_DK]
        kn[u] = kb[u].astype(F32)
        vc = v_ref[rows, h * DN_DV:(h + 1) * DN_DV].astype(F32)
        g_col[u] = cum_c[n][:, h:h + 1]
        g_row = cum_r[n][h:h + 1, :]
        beta = gc[rows, DN_H + h:DN_H + h + 1]
        dec[u] = jnp.where(incl, jnp.exp(jnp.where(incl, g_col[u] - g_row, 0.0)), 0.0)
        a_mat[u] = jnp.where(strict, _dot_nt(kb[u], kb[u]) * dec[u] * beta, 0.0)
        ab[u] = a_mat[u].astype(BF16)
        e_g = jnp.exp(g_col[u])
        rhs[u] = jnp.concatenate([beta * vc, beta * e_g * kn[u]], axis=-1)
        qn[u] = (qb, qb.astype(F32) * e_g)
        d[u] = -(lvl_ref[0] * a_mat[u])
    for lv in range(1, n_lvl):
        for u in units:
            db = d[u].astype(BF16)
            w = a_mat[u] + _dot(db, ab[u])
            z = w + _dot(w.astype(BF16), db)
            d[u] = d[u] - lvl_ref[lv] * z
    sol, p_mat = {}, {}
    for u in units:
        sol[u] = rhs[u] + _dot(d[u].astype(BF16), rhs[u].astype(BF16))
        p_mat[u] = (_dot_nt(qn[u][0], kb[u]) * dec[u]).astype(BF16)

    for n in range(n_units):
        for h in range(DN_H):
            u = (n, h)
            u_mat, w_mat = sol[u][:, :DN_DV], sol[u][:, DN_DV:]
            qe = qn[u][1]
            deltas, qss = [], []
            for j in range(n_seg):
                r0 = j * blk
                b = (n * cu + r0) // tt
                sb = s_ref[b, h].astype(BF16)
                lhs = jnp.concatenate([w_mat[r0:r0 + blk], qe[r0:r0 + blk]], axis=0).astype(BF16)
                ws_qs = _dot(lhs, sb)
                deltas.append(u_mat[r0:r0 + blk] - ws_qs[:blk])
                qss.append(ws_qs[blk:])
            delta = deltas[0] if n_seg == 1 else jnp.concatenate(deltas, axis=0)
            q_s = qss[0] if n_seg == 1 else jnp.concatenate(qss, axis=0)
            delta_b = delta.astype(BF16)
            o = q_s + _dot(p_mat[u], delta_b)
            for j in range(n_seg):
                r0 = j * blk
                b = (n * cu + r0) // tt
                g_last = g_col[u][r0 + blk - 1:r0 + blk, :]
                k_dec = (kn[u][r0:r0 + blk] * jnp.exp(g_last - g_col[u][r0:r0 + blk])).astype(BF16)
                s_ref[b, h] = s_ref[b, h] * jnp.exp(g_last) + _dot_tn(k_dec, delta_b[r0:r0 + blk])
            gate = dz_ref[n * cu:(n + 1) * cu, h * DN_DV:(h + 1) * DN_DV].astype(F32)
            o_ref[n * cu:(n + 1) * cu, h * DN_DV:(h + 1) * DN_DV] = (
                _rms_plain(o) * dnorm_ref[...] * gate).astype(BF16)


def _dn_call(dq, dk, dv, gbc, gbr, dz, dn_norm, s0, lblk, ublk, lvl, bt, tt, cu):
    bsz = s0.shape[0]
    seq = dq.shape[0] // bsz
    nl = seq // tt
    r = bt * tt
    tok = lambda w: pl.BlockSpec((r, w), lambda b, l: (b * nl + l, 0))
    const = lambda a: pl.BlockSpec(a.shape, lambda b, l: (0,) * a.ndim)
    st = pl.BlockSpec((bt, DN_H, DN_DK, DN_DV), lambda b, l: (b, 0, 0, 0))
    return pl.pallas_call(
        functools.partial(_dn_kernel, bt=bt, tt=tt, cu=cu), grid=(bsz // bt, nl), name="dn",
        in_specs=[tok(DN_QK), tok(DN_QK), tok(DN_VW), tok(LANES),
                  pl.BlockSpec((8, r), lambda b, l: (0, b * nl + l)),
                  tok(DN_VW), const(dn_norm), st, const(lblk), const(ublk), const(lvl)],
        out_specs=(tok(DN_VW), st),
        out_shape=(jax.ShapeDtypeStruct((bsz * seq, DN_VW), BF16),
                   jax.ShapeDtypeStruct(s0.shape, F32)),
        compiler_params=pltpu.CompilerParams(dimension_semantics=("parallel", "arbitrary"),
                                             vmem_limit_bytes=VMEM_LIMIT),
    )(dq, dk, dv, gbc, gbr, dz, dn_norm, s0, lblk, ublk, lvl)


def _mix_kernel(x_ref, og_ref, od_ref, ga_ref, gb_ref, wru_ref, wdu_ref, wo_ref, h_ref):
    branch_a = _dot(og_ref[...], wru_ref[...])
    branch_b = _dot(od_ref[...], wdu_ref[...])
    merged = ga_ref[...].astype(F32) * branch_a + gb_ref[...].astype(F32) * branch_b
    h_ref[...] = x_ref[...] + _dot(merged.astype(BF16), wo_ref[...])


def _mix_call(x2d, og, od, ga, gb, wru, wdu, wo, tm):
    n = x2d.shape[0]
    tok = lambda w: pl.BlockSpec((tm, w), lambda i: (i, 0))
    const = lambda a: pl.BlockSpec(a.shape, lambda i: (0,) * a.ndim, pipeline_mode=pl.Buffered(1))
    return pl.pallas_call(
        _mix_kernel, grid=(n // tm,), name="mix",
        in_specs=[tok(D_MODEL), tok(RET_VW), tok(DN_VW), tok(D_MODEL), tok(D_MODEL),
                  const(wru), const(wdu), const(wo)],
        out_specs=tok(D_MODEL), out_shape=jax.ShapeDtypeStruct((n, D_MODEL), F32),
        compiler_params=pltpu.CompilerParams(dimension_semantics=("parallel",),
                                             vmem_limit_bytes=VMEM_LIMIT),
    )(x2d, og, od, ga, gb, wru, wdu, wo)


_R_EXP = N_GROUPS


def _ffn_kernel(h_ref, p_ref, nffn_ref, wr_ref, br_ref, wgu_ref, wdn_ref, nple_ref, wpg_ref, wpp_ref,
                nfin_ref, y_ref, hid_ref):
    h = h_ref[...]
    u = _rms_plain(h) * nffn_ref[...]
    ub = u.astype(BF16)

    logits = _dot(ub, wr_ref[...]) + br_ref[...]
    lane = lax.broadcasted_iota(jnp.int32, logits.shape, 1)
    neg = jnp.float32(-jnp.inf)
    big = jnp.int32(LANES)
    gl = jnp.where(lane < N_GROUPS, logits, neg)
    g_max = jnp.max(gl, axis=-1, keepdims=True)
    p_grp = 1.0 / jnp.sum(jnp.exp(gl - g_max), axis=-1, keepdims=True)
    grp = jnp.min(jnp.where(gl == g_max, lane, big), axis=-1, keepdims=True)
    e_idx = lane - _R_EXP
    in_grp = (e_idx >= 0) & (e_idx < N_EXPERTS) & ((e_idx // EXPERTS_PER_GROUP) == grp)
    el = jnp.where(in_grp, logits, neg)
    v1 = jnp.max(el, axis=-1, keepdims=True)
    i1 = jnp.min(jnp.where(el == v1, lane, big), axis=-1, keepdims=True)
    el2 = jnp.where(lane == i1, neg, el)
    v2 = jnp.max(el2, axis=-1, keepdims=True)
    i2 = jnp.min(jnp.where(el2 == v2, lane, big), axis=-1, keepdims=True)
    e2 = jnp.exp(v2 - v1)
    w1 = 1.0 / (1.0 + e2)
    w2 = e2 / (1.0 + e2)
    combine = jnp.where(lane == i1, w1 * p_grp, 0.0) + jnp.where(lane == i2, w2 * p_grp, 0.0)

    for e in range(N_EXPERTS):
        gu = _dot(ub, wgu_ref[e])
        hid = _silu(gu[:, :D_EXPERT]) * gu[:, D_EXPERT:]
        hid_ref[:, e * D_EXPERT:(e + 1) * D_EXPERT] = (
            combine[:, _R_EXP + e:_R_EXP + e + 1] * hid).astype(BF16)
    h = h + _dot(hid_ref[...], wdn_ref[...])

    u3 = (_rms_plain(h) * nple_ref[...]).astype(BF16)
    gate = jax.nn.sigmoid(_dot(u3, wpg_ref[...]))
    h = h + gate * _dot(p_ref[...].astype(BF16), wpp_ref[...])
    y_ref[...] = _rms_plain(h) * nfin_ref[...]


def _ffn_call(h2d, p2d, nffn, wr, br, wgu, wdn, nple, wpg, wpp, nfin, tm):
    n = h2d.shape[0]
    tok = lambda w: pl.BlockSpec((tm, w), lambda i: (i, 0))
    const = lambda a: pl.BlockSpec(a.shape, lambda i: (0,) * a.ndim, pipeline_mode=pl.Buffered(1))
    return pl.pallas_call(
        _ffn_kernel, grid=(n // tm,), name="ffn",
        in_specs=[tok(D_MODEL), tok(PLE_DIM), const(nffn), const(wr), const(br), const(wgu), const(wdn),
                  const(nple), const(wpg), const(wpp), const(nfin)],
        out_specs=tok(D_MODEL), out_shape=jax.ShapeDtypeStruct((n, D_MODEL), F32),
        scratch_shapes=[pltpu.VMEM((tm, N_EXPERTS * D_EXPERT), BF16)],
        compiler_params=pltpu.CompilerParams(dimension_semantics=("parallel",),
                                             vmem_limit_bytes=VMEM_LIMIT),
    )(h2d, p2d, nffn, wr, br, wgu, wdn, nple, wpg, wpp, nfin)


def _rope_tables(pos):
    half = RET_DK // 2
    inv = 1.0 / (ROPE_BASE ** (jnp.arange(half, dtype=F32) / half))
    ang = pos.astype(F32)[:, None] * inv[None, :]
    cos, sin = jnp.cos(ang), jnp.sin(ang)
    cos_t = jnp.tile(jnp.concatenate([cos, cos], axis=-1), (1, RET_H))
    sin_t = jnp.tile(jnp.concatenate([-sin, sin], axis=-1), (1, RET_H))
    return cos_t, sin_t


def _ret_tables(r, c):
    log_gamma = jnp.log(1.0 - 2.0 ** (-5.0 - jnp.arange(RET_H, dtype=F32)))
    row = jnp.arange(r)
    idx = (row % c).astype(F32)
    diff = idx[:, None] - idx[None, :]
    causal = (diff >= 0) & ((row[:, None] // c) == (row[None, :] // c))
    lg = log_gamma[:, None, None]
    decay = jnp.where(causal, jnp.exp(jnp.where(causal, diff, 0.0) * lg), 0.0)
    q_dec = jnp.exp((idx + 1.0)[None, :] * log_gamma[:, None])[..., None]
    k_dec = jnp.exp((c - 1.0 - idx)[None, :] * log_gamma[:, None])[..., None]
    chunk_dec = jnp.exp(c * log_gamma)
    lane_head = jnp.arange(RET_QK) // RET_DK
    head_mask = (lane_head[None, :] == jnp.arange(RET_H)[:, None]).astype(F32)[:, None, :]
    return (decay.reshape(RET_H * r, r), head_mask, q_dec * head_mask, k_dec * head_mask,
            jnp.broadcast_to(chunk_dec[lane_head][:, None], (RET_QK, RET_DV)))


def _dn_tables(r, c):
    i = jnp.arange(r)
    ri, ci = i[:, None], i[None, :]
    same = (ri // c) == (ci // c)
    lblk = ((ri >= ci) & same).astype(BF16)
    x = ri ^ ci
    lvl, m = [], 1
    while m < c:
        lvl.append((same & (ri > ci) & (x >= m) & (x < 2 * m)).astype(F32))
        m *= 2
    return lblk, lblk.T, jnp.stack(lvl)


def _layer(x, p, s_ret, s_dn, s_conv, pos_table, wts, cfg):
    bsz, seq, _ = x.shape
    n = bsz * seq
    x2d = x.reshape(n, D_MODEL)
    cos_t, sin_t = pos_table
    (q, k, v, rg, dq, dk, dv, dz, ga, gb, gbc, gbr, conv_new) = _proj_call(
        x2d, wts["norm_mix"], wts["w_main"], wts["wab"], wts["wabt"], cos_t, sin_t,
        wts["alog_c"], wts["dtb_c"], wts["alog_r"], wts["dtb_r"], wts["conv_w"], s_conv, cfg["tm_proj"])

    ret_tables = _ret_tables(cfg["ret_cu"], min(cfg["ret_tt"], cfg["ret_cu"]))
    og, ret_new = _ret_call(q, k, v, rg, s_ret.reshape(bsz, RET_QK, RET_DV), ret_tables,
                            bsz, cfg["bt"], cfg["ret_tt"], cfg["ret_cu"])
    ret_new = ret_new.reshape(bsz, RET_H, RET_DK, RET_DV)

    lblk, ublk, lvl = _dn_tables(cfg["dn_cu"], min(cfg["dn_tt"], cfg["dn_cu"]))
    od, dn_new = _dn_call(dq, dk, dv, gbc, gbr, dz, wts["dn_norm"], s_dn, lblk, ublk, lvl,
                          cfg["bt"], cfg["dn_tt"], cfg["dn_cu"])

    h1 = _mix_call(x2d, og, od, ga, gb,
                   wts["w_ret_up"], wts["w_dn_up"], wts["w_out"], cfg["tm"])
    y = _ffn_call(h1, p.reshape(n, PLE_DIM), wts["norm_ffn"], wts["w_router"], wts["b_router"],
                  wts["w_gate_up"], wts["w_down"], wts["norm_ple"], wts["w_ple_gate"], wts["w_ple_proj"],
                  wts["norm_final"], cfg["tm"])
    return y.reshape(bsz, seq, D_MODEL), ret_new, dn_new, conv_new


def _prep_weights(norm_mix, w_in, conv_w, dn_a_log, dn_dt_bias, dn_norm, w_ret_up, w_dn_up, w_out,
                  norm_ffn, w_router_group, b_router_group, w_router_expert, b_router_expert,
                  w_gate_up, w_down, norm_ple, w_ple_gate, w_ple_proj, norm_final):
    pts, acc = [], 0
    for w in IN_WIDTHS[:-1]:
        acc += w
        pts.append(acc)
    rq, rk, rv, rg, dqkv, dz, da, db, ga, gb = jnp.split(w_in, pts, axis=-1)
    w_main = jnp.concatenate([rq, rk, rv, rg, dqkv, dz, ga, gb], axis=-1).astype(BF16)
    wab8 = jnp.concatenate([da, db], axis=-1).astype(BF16)
    wab = jnp.pad(wab8, ((0, 0), (0, LANES - 2 * DN_H)))
    pad_lanes = lambda a: jnp.pad(a.astype(F32), (0, LANES - a.shape[0]))[None, :]
    pad_rows = lambda a: jnp.pad(a.astype(F32), (0, 8 - a.shape[0]))[:, None]
    w_router = jnp.pad(jnp.concatenate([w_router_group, w_router_expert], axis=-1),
                       ((0, 0), (0, LANES - N_GROUPS - N_EXPERTS))).astype(BF16)
    b_router = pad_lanes(jnp.concatenate([b_router_group, b_router_expert]))
    row = lambda a: a.astype(F32)[None, :]
    return dict(
        norm_mix=row(norm_mix), w_main=w_main, wab=wab, wabt=wab8.T,
        alog_c=pad_lanes(dn_a_log), dtb_c=pad_lanes(dn_dt_bias),
        alog_r=pad_rows(dn_a_log), dtb_r=pad_rows(dn_dt_bias),
        conv_w=conv_w.astype(F32), dn_norm=row(dn_norm),
        w_ret_up=w_ret_up.astype(BF16), w_dn_up=w_dn_up.astype(BF16), w_out=w_out.astype(BF16),
        norm_ffn=row(norm_ffn), w_router=w_router, b_router=b_router,
        w_gate_up=w_gate_up.astype(BF16),
        w_down=w_down.astype(BF16).reshape(N_EXPERTS * D_EXPERT, D_MODEL),
        norm_ple=row(norm_ple), w_ple_gate=w_ple_gate.astype(BF16), w_ple_proj=w_ple_proj.astype(BF16),
        norm_final=row(norm_final),
    )


def kernel(x_prompt, x_sample, p_prompt, p_sample, state_ret, state_dn, state_conv, norm_mix, w_in, conv_w, dn_a_log, dn_dt_bias, dn_norm, w_ret_up, w_dn_up, w_out, norm_ffn, w_router_group, b_router_group, w_router_expert, b_router_expert, w_gate_up, w_down, norm_ple, w_ple_gate, w_ple_proj, norm_final):
    depth = w_in.shape[0]
    assert depth == 1, "one layer: the final norm is fused into the layer's last kernel"
    bp, lp, _ = x_prompt.shape
    bs, ls, _ = x_sample.shape
    wts = _prep_weights(norm_mix[0], w_in[0], conv_w[0], dn_a_log[0], dn_dt_bias[0], dn_norm[0],
                        w_ret_up[0], w_dn_up[0], w_out[0], norm_ffn[0], w_router_group[0],
                        b_router_group[0], w_router_expert[0], b_router_expert[0], w_gate_up[0],
                        w_down[0], norm_ple[0], w_ple_gate[0], w_ple_proj[0], norm_final)

    tm = 512
    cfg_p = dict(tm=tm, tm_proj=256, bt=1, ret_tt=512, ret_cu=256, dn_tt=256, dn_cu=128)
    cfg_s = dict(tm=tm, tm_proj=256, bt=16, ret_tt=ls, ret_cu=16 * ls, dn_tt=ls, dn_cu=16 * ls)

    pos_p = _rope_tables(jnp.arange(lp, dtype=jnp.int32))
    cos_s, sin_s = _rope_tables(PAST_LEN + jnp.arange(ls, dtype=jnp.int32))
    pos_s = (jnp.tile(cos_s, (bs, 1)), jnp.tile(sin_s, (bs, 1)))

    zero_ret = jnp.zeros((bp, RET_H, RET_DK, RET_DV), F32)
    zero_dn = jnp.zeros((bp, DN_H, DN_DK, DN_DV), F32)
    zero_conv = jnp.zeros((bp, CONV_W - 1, DN_CONV_CH), F32)

    y_p, r_p, d_p, c_p = _layer(x_prompt, p_prompt[0], zero_ret, zero_dn, zero_conv,
                                pos_p, wts, cfg_p)
    y_s, r_s, d_s, c_s = _layer(x_sample, p_sample[0], state_ret[0], state_dn[0], state_conv[0],
                                pos_s, wts, cfg_s)
    return (y_p, y_s, r_p[None], d_p[None], c_p[None], r_s[None], d_s[None], c_s[None])
```

```python
import functools

import jax
import jax.numpy as jnp
from jax import lax
from jax.experimental import pallas as pl
from jax.experimental.pallas import tpu as pltpu

F32 = jnp.float32
BF16 = jnp.bfloat16

D_MODEL = 1024
RET_H, RET_DK, RET_DV = 4, 64, 128
DN_H, DN_DK, DN_DV = 4, 128, 128
CONV_W = 4
ROPE_BASE = 10000.0
PAST_LEN = 16384
N_GROUPS, EXPERTS_PER_GROUP = 4, 4
N_EXPERTS = N_GROUPS * EXPERTS_PER_GROUP
D_EXPERT = 256
PLE_DIM = 256
EPS = 1e-6

RET_QK = RET_H * RET_DK
RET_VW = RET_H * RET_DV
DN_QK = DN_H * DN_DK
DN_VW = DN_H * DN_DV
DN_CONV_CH = 2 * DN_QK + DN_VW
IN_WIDTHS = (RET_QK, RET_QK, RET_VW, RET_VW, DN_CONV_CH, DN_VW, DN_H, DN_H, D_MODEL, D_MODEL)

LANES = 128
VMEM_LIMIT = 56 * 1024 * 1024

_C_Q, _C_K, _C_V, _C_RG = 0, 256, 512, 1024
_C_DQKV, _C_DZ, _C_FRONT = 1536, 3072, 3584
_C_GATES = _C_FRONT + 2 * DN_H


def _silu(x):
    return x * jax.nn.sigmoid(x)


def _softplus(x):
    return jnp.maximum(x, 0.0) + jnp.log1p(jnp.exp(-jnp.abs(x)))


def _dot(a, b):
    return jnp.dot(a, b, preferred_element_type=F32)


def _dot_nt(a, b):
    return lax.dot_general(a, b, (((1,), (1,)), ((), ())), preferred_element_type=F32)


def _dot_tn(a, b):
    return lax.dot_general(a, b, (((0,), (0,)), ((), ())), preferred_element_type=F32)


def _split3(x):
    hi = x.astype(BF16)
    r = x - hi.astype(F32)
    mid = r.astype(BF16)
    lo = (r - mid.astype(F32)).astype(BF16)
    return hi, mid, lo


def _rms_plain(x):
    return x * lax.rsqrt(jnp.mean(x * x, axis=-1, keepdims=True) + EPS)


def _proj_kernel(x_ref, gain_ref, w_ref, wg_ref, wab_ref, wabt_ref, cos_ref, sin_ref,
                 alog_c_ref, dtb_c_ref, alog_r_ref, dtb_r_ref, cw_ref, c0_ref,
                 q_ref, k_ref, v_ref, rg_ref, dq_ref, dk_ref, dv_ref, dz_ref, ga_ref, gb_ref,
                 gbc_ref, gbr_ref, c_ref, ext_ref, *, bt, tt, nl):
    pad = 8
    tail = CONV_W - 1

    @pl.when(pl.program_id(0) % nl == 0)
    def _():
        ext_ref[:, pad - tail:pad, :] = c0_ref[...]

    x = x_ref[...]
    u = x * lax.rsqrt(jnp.mean(x * x, axis=-1, keepdims=True) + EPS) * gain_ref[...]
    ub = u.astype(BF16)

    def mm(lo, hi):
        return _dot(ub, w_ref[:, lo:hi])

    ext_ref[:, pad:pad + tt, :] = mm(_C_DQKV, _C_DZ).reshape(bt, tt, DN_CONV_CH)
    conv = ext_ref[:, pad - tail:pad - tail + tt, :] * cw_ref[0:1, :]
    for j in range(1, CONV_W):
        conv = conv + ext_ref[:, pad - tail + j:pad - tail + j + tt, :] * cw_ref[j:j + 1, :]
    new_tail = ext_ref[:, pad + tt - tail:pad + tt, :]
    ext_ref[:, pad - tail:pad, :] = new_tail
    c_ref[...] = new_tail
    conv = _silu(conv).reshape(bt * tt, DN_CONV_CH)
    for h in range(DN_H):
        cq = conv[:, h * DN_DK:(h + 1) * DN_DK]
        ck = conv[:, DN_QK + h * DN_DK:DN_QK + (h + 1) * DN_DK]
        dq_ref[:, h * DN_DK:(h + 1) * DN_DK] = (
            cq * lax.rsqrt(jnp.sum(cq * cq, axis=-1, keepdims=True) + EPS) * (DN_DK ** -0.5)).astype(BF16)
        dk_ref[:, h * DN_DK:(h + 1) * DN_DK] = (
            ck * lax.rsqrt(jnp.sum(ck * ck, axis=-1, keepdims=True) + EPS)).astype(BF16)
    dv_ref[...] = conv[:, 2 * DN_QK:].astype(BF16)

    cos = cos_ref[...]
    sin = sin_ref[...]
    lane = lax.broadcasted_iota(jnp.int32, cos.shape, 1)
    first_half = (lane % RET_DK) < (RET_DK // 2)

    def rot(t):
        swapped = jnp.where(first_half, pltpu.roll(t, RET_QK - RET_DK // 2, 1),
                            pltpu.roll(t, RET_DK // 2, 1))
        return t * cos + swapped * sin

    q_ref[...] = (rot(mm(_C_Q, _C_K)) * (RET_DK ** -0.5)).astype(BF16)
    k_ref[...] = rot(mm(_C_K, _C_V)).astype(BF16)
    v_ref[...] = mm(_C_V, _C_RG).astype(BF16)
    rg_ref[...] = _silu(mm(_C_RG, _C_DQKV)).astype(BF16)
    dz_ref[...] = _silu(mm(_C_DZ, _C_FRONT)).astype(BF16)
    ga_ref[...] = _dot(ub, wg_ref[:, :D_MODEL]).astype(BF16)
    gb_ref[...] = _dot(ub, wg_ref[:, D_MODEL:]).astype(BF16)

    dab = _dot(ub, wab_ref[...])
    lane_c = lax.broadcasted_iota(jnp.int32, dab.shape, 1)
    g_c = -jnp.exp(alog_c_ref[...]) * _softplus(dab + dtb_c_ref[...])
    gbc_ref[...] = jnp.where(lane_c < DN_H, g_c, jax.nn.sigmoid(dab))
    dabt = _dot_nt(wabt_ref[...], ub)
    row_r = lax.broadcasted_iota(jnp.int32, dabt.shape, 0)
    g_r = -jnp.exp(alog_r_ref[...]) * _softplus(dabt + dtb_r_ref[...])
    gbr_ref[...] = jnp.where(row_r < DN_H, g_r, jax.nn.sigmoid(dabt))


def _proj_call(x2d, gain, w_in, w_gates, wab, wabt, cos_t, sin_t, alog_c, dtb_c, alog_r, dtb_r, conv_w, c0, tm):
    n = x2d.shape[0]
    bsz = c0.shape[0]
    seq = n // bsz
    tt = min(tm, seq)
    bt = tm // tt
    nl = seq // tt
    table_blocks = cos_t.shape[0] // tm
    tok = lambda w: pl.BlockSpec((tm, w), lambda i: (i, 0))
    const = lambda a: pl.BlockSpec(a.shape, lambda i: (0,) * a.ndim, pipeline_mode=pl.Buffered(1))
    tab = pl.BlockSpec((tm, RET_QK), lambda i: (i % table_blocks, 0))
    cst = pl.BlockSpec((bt, CONV_W - 1, DN_CONV_CH), lambda i: (i // nl, 0, 0))
    bf = lambda w: jax.ShapeDtypeStruct((n, w), BF16)
    out_shapes = (bf(RET_QK), bf(RET_QK), bf(RET_VW), bf(RET_VW), bf(DN_QK), bf(DN_QK), bf(DN_VW),
                  bf(DN_VW), bf(D_MODEL), bf(D_MODEL),
                  jax.ShapeDtypeStruct((n, LANES), F32), jax.ShapeDtypeStruct((8, n), F32),
                  jax.ShapeDtypeStruct(c0.shape, F32))
    out_specs = (tok(RET_QK), tok(RET_QK), tok(RET_VW), tok(RET_VW), tok(DN_QK), tok(DN_QK), tok(DN_VW),
                 tok(DN_VW), tok(D_MODEL), tok(D_MODEL), tok(LANES),
                 pl.BlockSpec((8, tm), lambda i: (0, i)), cst)
    return pl.pallas_call(
        functools.partial(_proj_kernel, bt=bt, tt=tt, nl=nl), grid=(n // tm,), name="proj",
        in_specs=[tok(D_MODEL), const(gain),
                  pl.BlockSpec((D_MODEL, _C_FRONT), lambda i: (0, 0), pipeline_mode=pl.Buffered(1)),
                  const(w_gates), const(wab), const(wabt), tab, tab,
                  const(alog_c), const(dtb_c), const(alog_r), const(dtb_r), const(conv_w), cst],
        out_specs=out_specs, out_shape=out_shapes,
        scratch_shapes=[pltpu.VMEM((bt, tt + 8, DN_CONV_CH), F32)],
        compiler_params=pltpu.CompilerParams(dimension_semantics=("arbitrary",),
                                             vmem_limit_bytes=VMEM_LIMIT),
    )(x2d, gain, w_in, w_gates, wab, wabt, cos_t, sin_t, alog_c, dtb_c, alog_r, dtb_r, conv_w, c0)


def _ret_kernel(q_ref, k_ref, v_ref, rg_ref, s0_ref, dec_ref, hm_ref, qd_ref, kd_ref, cd_ref,
                o_ref, s_ref, *, bt, tt, cu):
    r = bt * tt
    blk = min(tt, cu)
    n_units = r // cu
    n_seg = cu // blk

    @pl.when(pl.program_id(1) == 0)
    def _():
        s_ref[...] = s0_ref[...]

    heads = range(RET_H)
    hrow = lambda a, h, m: a[h * m:(h + 1) * m]
    for n in range(n_units):
        rows = slice(n * cu, (n + 1) * cu)
        qf = q_ref[rows, :].astype(F32)
        kf = k_ref[rows, :].astype(F32)
        vf = v_ref[rows, :].astype(F32)
        q_m = [qf * hm_ref[h] for h in heads]
        q_d = [qf * qd_ref[h] for h in heads]
        k_d = [kf * kd_ref[h] for h in heads]
        v_h = [vf[:, h * RET_DV:(h + 1) * RET_DV] for h in heads]
        scores = (_dot_nt(jnp.concatenate(q_m, axis=0).astype(BF16), k_ref[rows, :])
                  * dec_ref[...]).astype(BF16)
        intra = [_dot(hrow(scores, h, cu), v_h[h].astype(BF16)) for h in heads]
        inters = []
        for j in range(n_seg):
            seg = slice(j * blk, (j + 1) * blk)
            b = (n * cu + j * blk) // tt
            s = s_ref[b]
            q_seg = jnp.concatenate([a[seg] for a in q_d], axis=0).astype(BF16)
            k_seg = jnp.concatenate([a[seg] for a in k_d], axis=0).astype(BF16)
            v_seg = jnp.concatenate([a[seg] for a in v_h], axis=0).astype(BF16)
            inters.append(_dot(q_seg, s.astype(BF16)))
            s_ref[b] = s * cd_ref[...] + _dot_tn(k_seg, v_seg)
        for h in heads:
            inter = jnp.concatenate([hrow(a, h, blk) for a in inters], axis=0)
            gate = rg_ref[rows, h * RET_DV:(h + 1) * RET_DV].astype(F32)
            o_ref[rows, h * RET_DV:(h + 1) * RET_DV] = (
                _rms_plain(intra[h] + inter) * gate).astype(BF16)


def _ret_call(q, k, v, rg, s0, tables, bsz, bt, tt, cu):
    seq = q.shape[0] // bsz
    nl = seq // tt
    r = bt * tt
    tok = lambda w: pl.BlockSpec((r, w), lambda b, l: (b * nl + l, 0))
    const = lambda a: pl.BlockSpec(a.shape, lambda b, l: (0,) * a.ndim)
    st = pl.BlockSpec((bt, RET_QK, RET_DV), lambda b, l: (b, 0, 0))
    return pl.pallas_call(
        functools.partial(_ret_kernel, bt=bt, tt=tt, cu=cu), grid=(bsz // bt, nl), name="ret",
        in_specs=[tok(RET_QK), tok(RET_QK), tok(RET_VW), tok(RET_VW), st] + [const(t) for t in tables],
        out_specs=(tok(RET_VW), st),
        out_shape=(jax.ShapeDtypeStruct((bsz * seq, RET_VW), BF16),
                   jax.ShapeDtypeStruct(s0.shape, F32)),
        compiler_params=pltpu.CompilerParams(dimension_semantics=("parallel", "arbitrary"),
                                             vmem_limit_bytes=VMEM_LIMIT),
    )(q, k, v, rg, s0, *tables)


def _dn_kernel(q_ref, k_ref, v_ref, gc_ref, gr_ref, dz_ref, dnorm_ref, s0_ref, lblk_ref, ublk_ref, lvl_ref,
               o_ref, s_ref, *, bt, tt, cu):
    r = bt * tt
    blk = min(tt, cu)
    n_units = r // cu
    n_seg = cu // blk
    n_lvl = lvl_ref.shape[0]

    @pl.when(pl.program_id(1) == 0)
    def _():
        s_ref[...] = s0_ref[...]

    ri = lax.broadcasted_iota(jnp.int32, (cu, cu), 0)
    ci = lax.broadcasted_iota(jnp.int32, (cu, cu), 1)
    same = (ri // blk) == (ci // blk)
    incl = same & (ri >= ci)
    strict = same & (ri > ci)

    gc = gc_ref[...]
    g_hi, g_mid, g_lo = _split3(gc)
    r_hi, r_mid, r_lo = _split3(gr_ref[...])
    lblk = lblk_ref[...]
    ublk = ublk_ref[...]

    units = [(n, h) for n in range(n_units) for h in range(DN_H)]
    cum_c, cum_r = [], []
    for n in range(n_units):
        rows = slice(n * cu, (n + 1) * cu)
        cum_c.append(_dot(lblk, g_hi[rows]) + _dot(lblk, g_mid[rows]) + _dot(lblk, g_lo[rows]))
        cum_r.append(_dot(r_hi[:, rows], ublk) + _dot(r_mid[:, rows], ublk) + _dot(r_lo[:, rows], ublk))

    qn, kn, kb, g_col, dec, a_mat, ab, d, rhs = {}, {}, {}, {}, {}, {}, {}, {}, {}
    for u in units:
        n, h = u
        rows = slice(n * cu, (n + 1) * cu)
        qb = q_ref[rows, h * DN_DK:(h + 1) * DN_DK]
        kb[u] = k_ref[rows, h * DN_DK:(h + 1) * DN_DK]
        kn[u] = kb[u].astype(F32)
        vc = v_ref[rows, h * DN_DV:(h + 1) * DN_DV].astype(F32)
        g_col[u] = cum_c[n][:, h:h + 1]
        g_row = cum_r[n][h:h + 1, :]
        beta = gc[rows, DN_H + h:DN_H + h + 1]
        dec[u] = jnp.where(incl, jnp.exp(jnp.where(incl, g_col[u] - g_row, 0.0)), 0.0)
        a_mat[u] = jnp.where(strict, _dot_nt(kb[u], kb[u]) * dec[u] * beta, 0.0)
        ab[u] = a_mat[u].astype(BF16)
        e_g = jnp.exp(g_col[u])
        rhs[u] = jnp.concatenate([beta * vc, beta * e_g * kn[u]], axis=-1)
        qn[u] = (qb, qb.astype(F32) * e_g)
        d[u] = -(lvl_ref[0] * a_mat[u])
    for lv in range(1, n_lvl):
        for u in units:
            db = d[u].astype(BF16)
            w = a_mat[u] + _dot(db, ab[u])
            z = w + _dot(w.astype(BF16), db)
            d[u] = d[u] - lvl_ref[lv] * z
    sol, p_mat = {}, {}
    for u in units:
        sol[u] = rhs[u] + _dot(d[u].astype(BF16), rhs[u].astype(BF16))
        p_mat[u] = (_dot_nt(qn[u][0], kb[u]) * dec[u]).astype(BF16)

    for n in range(n_units):
        for h in range(DN_H):
            u = (n, h)
            u_mat, w_mat = sol[u][:, :DN_DV], sol[u][:, DN_DV:]
            qe = qn[u][1]
            deltas, qss = [], []
            for j in range(n_seg):
                r0 = j * blk
                b = (n * cu + r0) // tt
                sb = s_ref[b, h].astype(BF16)
                lhs = jnp.concatenate([w_mat[r0:r0 + blk], qe[r0:r0 + blk]], axis=0).astype(BF16)
                ws_qs = _dot(lhs, sb)
                deltas.append(u_mat[r0:r0 + blk] - ws_qs[:blk])
                qss.append(ws_qs[blk:])
            delta = deltas[0] if n_seg == 1 else jnp.concatenate(deltas, axis=0)
            q_s = qss[0] if n_seg == 1 else jnp.concatenate(qss, axis=0)
            delta_b = delta.astype(BF16)
            o = q_s + _dot(p_mat[u], delta_b)
            for j in range(n_seg):
                r0 = j * blk
                b = (n * cu + r0) // tt
                g_last = g_col[u][r0 + blk - 1:r0 + blk, :]
                k_dec = (kn[u][r0:r0 + blk] * jnp.exp(g_last - g_col[u][r0:r0 + blk])).astype(BF16)
                s_ref[b, h] = s_ref[b, h] * jnp.exp(g_last) + _dot_tn(k_dec, delta_b[r0:r0 + blk])
            gate = dz_ref[n * cu:(n + 1) * cu, h * DN_DV:(h + 1) * DN_DV].astype(F32)
            o_ref[n * cu:(n + 1) * cu, h * DN_DV:(h + 1) * DN_DV] = (
                _rms_plain(o) * dnorm_ref[...] * gate).astype(BF16)


def _dn_call(dq, dk, dv, gbc, gbr, dz, dn_norm, s0, lblk, ublk, lvl, bt, tt, cu):
    bsz = s0.shape[0]
    seq = dq.shape[0] // bsz
    nl = seq // tt
    r = bt * tt
    tok = lambda w: pl.BlockSpec((r, w), lambda b, l: (b * nl + l, 0))
    const = lambda a: pl.BlockSpec(a.shape, lambda b, l: (0,) * a.ndim)
    st = pl.BlockSpec((bt, DN_H, DN_DK, DN_DV), lambda b, l: (b, 0, 0, 0))
    return pl.pallas_call(
        functools.partial(_dn_kernel, bt=bt, tt=tt, cu=cu), grid=(bsz // bt, nl), name="dn",
        in_specs=[tok(DN_QK), tok(DN_QK), tok(DN_VW), tok(LANES),
                  pl.BlockSpec((8, r), lambda b, l: (0, b * nl + l)),
                  tok(DN_VW), const(dn_norm), st, const(lblk), const(ublk), const(lvl)],
        out_specs=(tok(DN_VW), st),
        out_shape=(jax.ShapeDtypeStruct((bsz * seq, DN_VW), BF16),
                   jax.ShapeDtypeStruct(s0.shape, F32)),
        compiler_params=pltpu.CompilerParams(dimension_semantics=("parallel", "arbitrary"),
                                             vmem_limit_bytes=VMEM_LIMIT),
    )(dq, dk, dv, gbc, gbr, dz, dn_norm, s0, lblk, ublk, lvl)


def _mix_kernel(x_ref, og_ref, od_ref, ga_ref, gb_ref, wru_ref, wdu_ref, wo_ref, h_ref):
    branch_a = _dot(og_ref[...], wru_ref[...])
    branch_b = _dot(od_ref[...], wdu_ref[...])
    merged = (jax.nn.sigmoid(ga_ref[...].astype(F32)) * branch_a
              + jax.nn.sigmoid(gb_ref[...].astype(F32)) * branch_b)
    h_ref[...] = x_ref[...] + _dot(merged.astype(BF16), wo_ref[...])


def _mix_call(x2d, og, od, ga, gb, wru, wdu, wo, tm):
    n = x2d.shape[0]
    tok = lambda w: pl.BlockSpec((tm, w), lambda i: (i, 0))
    const = lambda a: pl.BlockSpec(a.shape, lambda i: (0,) * a.ndim, pipeline_mode=pl.Buffered(1))
    return pl.pallas_call(
        _mix_kernel, grid=(n // tm,), name="mix",
        in_specs=[tok(D_MODEL), tok(RET_VW), tok(DN_VW), tok(D_MODEL), tok(D_MODEL),
                  const(wru), const(wdu), const(wo)],
        out_specs=tok(D_MODEL), out_shape=jax.ShapeDtypeStruct((n, D_MODEL), F32),
        compiler_params=pltpu.CompilerParams(dimension_semantics=("parallel",),
                                             vmem_limit_bytes=VMEM_LIMIT),
    )(x2d, og, od, ga, gb, wru, wdu, wo)


_R_EXP = N_GROUPS


def _ffn_kernel(h_ref, p_ref, nffn_ref, wr_ref, br_ref, wgu_ref, wdn_ref, nple_ref, wpg_ref, wpp_ref,
                nfin_ref, y_ref, hid_ref):
    h = h_ref[...]
    u = _rms_plain(h) * nffn_ref[...]
    ub = u.astype(BF16)

    logits = _dot(ub, wr_ref[...]) + br_ref[...]
    lane = lax.broadcasted_iota(jnp.int32, logits.shape, 1)
    neg = jnp.float32(-jnp.inf)
    big = jnp.int32(LANES)
    gl = jnp.where(lane < N_GROUPS, logits, neg)
    g_max = jnp.max(gl, axis=-1, keepdims=True)
    p_grp = 1.0 / jnp.sum(jnp.exp(gl - g_max), axis=-1, keepdims=True)
    grp = jnp.min(jnp.where(gl == g_max, lane, big), axis=-1, keepdims=True)
    e_idx = lane - _R_EXP
    in_grp = (e_idx >= 0) & (e_idx < N_EXPERTS) & ((e_idx // EXPERTS_PER_GROUP) == grp)
    el = jnp.where(in_grp, logits, neg)
    v1 = jnp.max(el, axis=-1, keepdims=True)
    i1 = jnp.min(jnp.where(el == v1, lane, big), axis=-1, keepdims=True)
    el2 = jnp.where(lane == i1, neg, el)
    v2 = jnp.max(el2, axis=-1, keepdims=True)
    i2 = jnp.min(jnp.where(el2 == v2, lane, big), axis=-1, keepdims=True)
    e2 = jnp.exp(v2 - v1)
    w1 = 1.0 / (1.0 + e2)
    w2 = e2 / (1.0 + e2)
    combine = jnp.where(lane == i1, w1 * p_grp, 0.0) + jnp.where(lane == i2, w2 * p_grp, 0.0)

    for e in range(N_EXPERTS):
        gu = _dot(ub, wgu_ref[e])
        hid = _silu(gu[:, :D_EXPERT]) * gu[:, D_EXPERT:]
        hid_ref[:, e * D_EXPERT:(e + 1) * D_EXPERT] = (
            combine[:, _R_EXP + e:_R_EXP + e + 1] * hid).astype(BF16)
    h = h + _dot(hid_ref[...], wdn_ref[...])

    u3 = (_rms_plain(h) * nple_ref[...]).astype(BF16)
    gate = jax.nn.sigmoid(_dot(u3, wpg_ref[...]))
    h = h + gate * _dot(p_ref[...].astype(BF16), wpp_ref[...])
    y_ref[...] = _rms_plain(h) * nfin_ref[...]


def _ffn_call(h2d, p2d, nffn, wr, br, wgu, wdn, nple, wpg, wpp, nfin, tm):
    n = h2d.shape[0]
    tok = lambda w: pl.BlockSpec((tm, w), lambda i: (i, 0))
    const = lambda a: pl.BlockSpec(a.shape, lambda i: (0,) * a.ndim, pipeline_mode=pl.Buffered(1))
    return pl.pallas_call(
        _ffn_kernel, grid=(n // tm,), name="ffn",
        in_specs=[tok(D_MODEL), tok(PLE_DIM), const(nffn), const(wr), const(br), const(wgu), const(wdn),
                  const(nple), const(wpg), const(wpp), const(nfin)],
        out_specs=tok(D_MODEL), out_shape=jax.ShapeDtypeStruct((n, D_MODEL), F32),
        scratch_shapes=[pltpu.VMEM((tm, N_EXPERTS * D_EXPERT), BF16)],
        compiler_params=pltpu.CompilerParams(dimension_semantics=("parallel",),
                                             vmem_limit_bytes=VMEM_LIMIT),
    )(h2d, p2d, nffn, wr, br, wgu, wdn, nple, wpg, wpp, nfin)


def _rope_tables(pos):
    half = RET_DK // 2
    inv = 1.0 / (ROPE_BASE ** (jnp.arange(half, dtype=F32) / half))
    ang = pos.astype(F32)[:, None] * inv[None, :]
    cos, sin = jnp.cos(ang), jnp.sin(ang)
    cos_t = jnp.tile(jnp.concatenate([cos, cos], axis=-1), (1, RET_H))
    sin_t = jnp.tile(jnp.concatenate([-sin, sin], axis=-1), (1, RET_H))
    return cos_t, sin_t


def _ret_tables(r, c):
    log_gamma = jnp.log(1.0 - 2.0 ** (-5.0 - jnp.arange(RET_H, dtype=F32)))
    row = jnp.arange(r)
    idx = (row % c).astype(F32)
    diff = idx[:, None] - idx[None, :]
    causal = (diff >= 0) & ((row[:, None] // c) == (row[None, :] // c))
    lg = log_gamma[:, None, None]
    decay = jnp.where(causal, jnp.exp(jnp.where(causal, diff, 0.0) * lg), 0.0)
    q_dec = jnp.exp((idx + 1.0)[None, :] * log_gamma[:, None])[..., None]
    k_dec = jnp.exp((c - 1.0 - idx)[None, :] * log_gamma[:, None])[..., None]
    chunk_dec = jnp.exp(c * log_gamma)
    lane_head = jnp.arange(RET_QK) // RET_DK
    head_mask = (lane_head[None, :] == jnp.arange(RET_H)[:, None]).astype(F32)[:, None, :]
    return (decay.reshape(RET_H * r, r), head_mask, q_dec * head_mask, k_dec * head_mask,
            jnp.broadcast_to(chunk_dec[lane_head][:, None], (RET_QK, RET_DV)))


def _dn_tables(r, c):
    i = jnp.arange(r)
    ri, ci = i[:, None], i[None, :]
    same = (ri // c) == (ci // c)
    lblk = ((ri >= ci) & same).astype(BF16)
    x = ri ^ ci
    lvl, m = [], 1
    while m < c:
        lvl.append((same & (ri > ci) & (x >= m) & (x < 2 * m)).astype(F32))
        m *= 2
    return lblk, lblk.T, jnp.stack(lvl)


def _layer(x, p, s_ret, s_dn, s_conv, pos_table, wts, cfg):
    bsz, seq, _ = x.shape
    n = bsz * seq
    x2d = x.reshape(n, D_MODEL)
    cos_t, sin_t = pos_table
    (q, k, v, rg, dq, dk, dv, dz, ga, gb, gbc, gbr, conv_new) = _proj_call(
        x2d, wts["norm_mix"], wts["w_in"], wts["w_gates"], wts["wab"], wts["wabt"], cos_t, sin_t,
        wts["alog_c"], wts["dtb_c"], wts["alog_r"], wts["dtb_r"], wts["conv_w"], s_conv, cfg["tm_proj"])

    ret_tables = _ret_tables(cfg["ret_cu"], min(cfg["ret_tt"], cfg["ret_cu"]))
    og, ret_new = _ret_call(q, k, v, rg, s_ret.reshape(bsz, RET_QK, RET_DV), ret_tables,
                            bsz, cfg["bt"], cfg["ret_tt"], cfg["ret_cu"])
    ret_new = ret_new.reshape(bsz, RET_H, RET_DK, RET_DV)

    lblk, ublk, lvl = _dn_tables(cfg["dn_cu"], min(cfg["dn_tt"], cfg["dn_cu"]))
    od, dn_new = _dn_call(dq, dk, dv, gbc, gbr, dz, wts["dn_norm"], s_dn, lblk, ublk, lvl,
                          cfg["bt"], cfg["dn_tt"], cfg["dn_cu"])

    h1 = _mix_call(x2d, og, od, ga, gb,
                   wts["w_ret_up"], wts["w_dn_up"], wts["w_out"], cfg["tm"])
    y = _ffn_call(h1, p.reshape(n, PLE_DIM), wts["norm_ffn"], wts["w_router"], wts["b_router"],
                  wts["w_gate_up"], wts["w_down"], wts["norm_ple"], wts["w_ple_gate"], wts["w_ple_proj"],
                  wts["norm_final"], cfg["tm"])
    return y.reshape(bsz, seq, D_MODEL), ret_new, dn_new, conv_new


def _prep_weights(norm_mix, w_in, conv_w, dn_a_log, dn_dt_bias, dn_norm, w_ret_up, w_dn_up, w_out,
                  norm_ffn, w_router_group, b_router_group, w_router_expert, b_router_expert,
                  w_gate_up, w_down, norm_ple, w_ple_gate, w_ple_proj, norm_final):
    assert sum(IN_WIDTHS[:6]) == _C_FRONT and sum(IN_WIDTHS[:8]) == _C_GATES
    w_in = w_in.astype(BF16)
    w_gates = w_in[:, _C_GATES:]
    wab8 = w_in[:, _C_FRONT:_C_GATES]
    wab = jnp.pad(wab8, ((0, 0), (0, LANES - 2 * DN_H)))
    pad_lanes = lambda a: jnp.pad(a.astype(F32), (0, LANES - a.shape[0]))[None, :]
    pad_rows = lambda a: jnp.pad(a.astype(F32), (0, 8 - a.shape[0]))[:, None]
    w_router = jnp.pad(jnp.concatenate([w_router_group, w_router_expert], axis=-1),
                       ((0, 0), (0, LANES - N_GROUPS - N_EXPERTS))).astype(BF16)
    b_router = pad_lanes(jnp.concatenate([b_router_group, b_router_expert]))
    row = lambda a: a.astype(F32)[None, :]
    return dict(
        norm_mix=row(norm_mix), w_in=w_in, w_gates=w_gates, wab=wab, wabt=wab8.T,
        alog_c=pad_lanes(dn_a_log), dtb_c=pad_lanes(dn_dt_bias),
        alog_r=pad_rows(dn_a_log), dtb_r=pad_rows(dn_dt_bias),
        conv_w=conv_w.astype(F32), dn_norm=row(dn_norm),
        w_ret_up=w_ret_up.astype(BF16), w_dn_up=w_dn_up.astype(BF16), w_out=w_out.astype(BF16),
        norm_ffn=row(norm_ffn), w_router=w_router, b_router=b_router,
        w_gate_up=w_gate_up.astype(BF16),
        w_down=w_down.astype(BF16).reshape(N_EXPERTS * D_EXPERT, D_MODEL),
        norm_ple=row(norm_ple), w_ple_gate=w_ple_gate.astype(BF16), w_ple_proj=w_ple_proj.astype(BF16),
        norm_final=row(norm_final),
    )


def kernel(x_prompt, x_sample, p_prompt, p_sample, state_ret, state_dn, state_conv, norm_mix, w_in, conv_w, dn_a_log, dn_dt_bias, dn_norm, w_ret_up, w_dn_up, w_out, norm_ffn, w_router_group, b_router_group, w_router_expert, b_router_expert, w_gate_up, w_down, norm_ple, w_ple_gate, w_ple_proj, norm_final):
    depth = w_in.shape[0]
    assert depth == 1, "one layer: the final norm is fused into the layer's last kernel"
    bp, lp, _ = x_prompt.shape
    bs, ls, _ = x_sample.shape
    wts = _prep_weights(norm_mix[0], w_in[0], conv_w[0], dn_a_log[0], dn_dt_bias[0], dn_norm[0],
                        w_ret_up[0], w_dn_up[0], w_out[0], norm_ffn[0], w_router_group[0],
                        b_router_group[0], w_router_expert[0], b_router_expert[0], w_gate_up[0],
                        w_down[0], norm_ple[0], w_ple_gate[0], w_ple_proj[0], norm_final)

    tm = 512
    cfg_p = dict(tm=tm, tm_proj=256, bt=1, ret_tt=512, ret_cu=256, dn_tt=256, dn_cu=128)
    cfg_s = dict(tm=tm, tm_proj=256, bt=16, ret_tt=ls, ret_cu=16 * ls, dn_tt=ls, dn_cu=16 * ls)

    pos_p = _rope_tables(jnp.arange(lp, dtype=jnp.int32))
    cos_s, sin_s = _rope_tables(PAST_LEN + jnp.arange(ls, dtype=jnp.int32))
    pos_s = (jnp.tile(cos_s, (bs, 1)), jnp.tile(sin_s, (bs, 1)))

    zero_ret = jnp.zeros((bp, RET_H, RET_DK, RET_DV), F32)
    zero_dn = jnp.zeros((bp, DN_H, DN_DK, DN_DV), F32)
    zero_conv = jnp.zeros((bp, CONV_W - 1, DN_CONV_CH), F32)

    y_p, r_p, d_p, c_p = _layer(x_prompt, p_prompt[0], zero_ret, zero_dn, zero_conv,
                                pos_p, wts, cfg_p)
    y_s, r_s, d_s, c_s = _layer(x_sample, p_sample[0], state_ret[0], state_dn[0], state_conv[0],
                                pos_s, wts, cfg_s)
    return (y_p, y_s, r_p[None], d_p[None], c_p[None], r_s[None], d_s[None], c_s[None])
```

```python
import functools

import jax
import jax.numpy as jnp
from jax import lax
from jax.experimental import pallas as pl
from jax.experimental.pallas import tpu as pltpu

F32 = jnp.float32
BF16 = jnp.bfloat16

D_MODEL = 1024
RET_H, RET_DK, RET_DV = 4, 64, 128
DN_H, DN_DK, DN_DV = 4, 128, 128
CONV_W = 4
ROPE_BASE = 10000.0
PAST_LEN = 16384
N_GROUPS, EXPERTS_PER_GROUP = 4, 4
N_EXPERTS = N_GROUPS * EXPERTS_PER_GROUP
D_EXPERT = 256
PLE_DIM = 256
EPS = 1e-6

RET_QK = RET_H * RET_DK
RET_VW = RET_H * RET_DV
DN_QK = DN_H * DN_DK
DN_VW = DN_H * DN_DV
DN_CONV_CH = 2 * DN_QK + DN_VW
IN_WIDTHS = (RET_QK, RET_QK, RET_VW, RET_VW, DN_CONV_CH, DN_VW, DN_H, DN_H, D_MODEL, D_MODEL)

LANES = 128
VMEM_LIMIT = 56 * 1024 * 1024

_C_Q, _C_K, _C_V, _C_RG = 0, 256, 512, 1024
_C_DQKV, _C_DZ, _C_FRONT = 1536, 3072, 3584
_C_GATES = _C_FRONT + 2 * DN_H


def _silu(x):
    return x * jax.nn.sigmoid(x)


def _softplus(x):
    return jnp.maximum(x, 0.0) + jnp.log1p(jnp.exp(-jnp.abs(x)))


def _dot(a, b):
    return jnp.dot(a, b, preferred_element_type=F32)


def _dot_nt(a, b):
    return lax.dot_general(a, b, (((1,), (1,)), ((), ())), preferred_element_type=F32)


def _dot_tn(a, b):
    return lax.dot_general(a, b, (((0,), (0,)), ((), ())), preferred_element_type=F32)


def _split3(x):
    hi = x.astype(BF16)
    r = x - hi.astype(F32)
    mid = r.astype(BF16)
    lo = (r - mid.astype(F32)).astype(BF16)
    return hi, mid, lo


def _rms_plain(x):
    return x * lax.rsqrt(jnp.mean(x * x, axis=-1, keepdims=True) + EPS)


def _proj_kernel(x_ref, gain_ref, w_ref, wg_ref, wab_ref, wabt_ref, cos_ref, sin_ref,
                 alog_c_ref, dtb_c_ref, alog_r_ref, dtb_r_ref, cw_ref, c0_ref,
                 q_ref, k_ref, v_ref, rg_ref, dq_ref, dk_ref, dv_ref, dz_ref, ga_ref, gb_ref,
                 gbc_ref, gbr_ref, c_ref, ext_ref, *, bt, tt, nl):
    pad = 8
    tail = CONV_W - 1

    @pl.when(pl.program_id(0) % nl == 0)
    def _():
        ext_ref[:, pad - tail:pad, :] = c0_ref[...]

    x = x_ref[...]
    u = x * lax.rsqrt(jnp.mean(x * x, axis=-1, keepdims=True) + EPS) * gain_ref[...]
    ub = u.astype(BF16)

    def mm(lo, hi):
        return _dot(ub, w_ref[:, lo:hi])

    ext_ref[:, pad:pad + tt, :] = mm(_C_DQKV, _C_DZ).reshape(bt, tt, DN_CONV_CH)
    conv = ext_ref[:, pad - tail:pad - tail + tt, :] * cw_ref[0:1, :]
    for j in range(1, CONV_W):
        conv = conv + ext_ref[:, pad - tail + j:pad - tail + j + tt, :] * cw_ref[j:j + 1, :]
    new_tail = ext_ref[:, pad + tt - tail:pad + tt, :]
    ext_ref[:, pad - tail:pad, :] = new_tail
    c_ref[...] = new_tail
    conv = _silu(conv).reshape(bt * tt, DN_CONV_CH)
    for h in range(DN_H):
        cq = conv[:, h * DN_DK:(h + 1) * DN_DK]
        ck = conv[:, DN_QK + h * DN_DK:DN_QK + (h + 1) * DN_DK]
        dq_ref[:, h * DN_DK:(h + 1) * DN_DK] = (
            cq * lax.rsqrt(jnp.sum(cq * cq, axis=-1, keepdims=True) + EPS) * (DN_DK ** -0.5)).astype(BF16)
        dk_ref[:, h * DN_DK:(h + 1) * DN_DK] = (
            ck * lax.rsqrt(jnp.sum(ck * ck, axis=-1, keepdims=True) + EPS)).astype(BF16)
    dv_ref[...] = conv[:, 2 * DN_QK:].astype(BF16)

    cos = cos_ref[...]
    sin = sin_ref[...]
    lane = lax.broadcasted_iota(jnp.int32, cos.shape, 1)
    first_half = (lane % RET_DK) < (RET_DK // 2)

    def rot(t):
        swapped = jnp.where(first_half, pltpu.roll(t, RET_QK - RET_DK // 2, 1),
                            pltpu.roll(t, RET_DK // 2, 1))
        return t * cos + swapped * sin

    q_ref[...] = (rot(mm(_C_Q, _C_K)) * (RET_DK ** -0.5)).astype(BF16)
    k_ref[...] = rot(mm(_C_K, _C_V)).astype(BF16)
    v_ref[...] = mm(_C_V, _C_RG).astype(BF16)
    rg_ref[...] = _silu(mm(_C_RG, _C_DQKV)).astype(BF16)
    dz_ref[...] = _silu(mm(_C_DZ, _C_FRONT)).astype(BF16)
    ga_ref[...] = _dot(ub, wg_ref[:, :D_MODEL]).astype(BF16)
    gb_ref[...] = _dot(ub, wg_ref[:, D_MODEL:]).astype(BF16)

    dab = _dot(ub, wab_ref[...])
    lane_c = lax.broadcasted_iota(jnp.int32, dab.shape, 1)
    g_c = -jnp.exp(alog_c_ref[...]) * _softplus(dab + dtb_c_ref[...])
    gbc_ref[...] = jnp.where(lane_c < DN_H, g_c, jax.nn.sigmoid(dab))
    dabt = _dot_nt(wabt_ref[...], ub)
    row_r = lax.broadcasted_iota(jnp.int32, dabt.shape, 0)
    g_r = -jnp.exp(alog_r_ref[...]) * _softplus(dabt + dtb_r_ref[...])
    gbr_ref[...] = jnp.where(row_r < DN_H, g_r, jax.nn.sigmoid(dabt))


def _proj_call(x2d, gain, w_in, w_gates, wab, wabt, cos_t, sin_t, alog_c, dtb_c, alog_r, dtb_r, conv_w, c0, tm):
    n = x2d.shape[0]
    bsz = c0.shape[0]
    seq = n // bsz
    tt = min(tm, seq)
    bt = tm // tt
    nl = seq // tt
    table_blocks = cos_t.shape[0] // tm
    tok = lambda w: pl.BlockSpec((tm, w), lambda i: (i, 0))
    const = lambda a: pl.BlockSpec(a.shape, lambda i: (0,) * a.ndim, pipeline_mode=pl.Buffered(1))
    tab = pl.BlockSpec((tm, RET_QK), lambda i: (i % table_blocks, 0))
    cst = pl.BlockSpec((bt, CONV_W - 1, DN_CONV_CH), lambda i: (i // nl, 0, 0))
    bf = lambda w: jax.ShapeDtypeStruct((n, w), BF16)
    out_shapes = (bf(RET_QK), bf(RET_QK), bf(RET_VW), bf(RET_VW), bf(DN_QK), bf(DN_QK), bf(DN_VW),
                  bf(DN_VW), bf(D_MODEL), bf(D_MODEL),
                  jax.ShapeDtypeStruct((n, LANES), F32), jax.ShapeDtypeStruct((8, n), F32),
                  jax.ShapeDtypeStruct(c0.shape, F32))
    out_specs = (tok(RET_QK), tok(RET_QK), tok(RET_VW), tok(RET_VW), tok(DN_QK), tok(DN_QK), tok(DN_VW),
                 tok(DN_VW), tok(D_MODEL), tok(D_MODEL), tok(LANES),
                 pl.BlockSpec((8, tm), lambda i: (0, i)), cst)
    return pl.pallas_call(
        functools.partial(_proj_kernel, bt=bt, tt=tt, nl=nl), grid=(n // tm,), name="proj",
        in_specs=[tok(D_MODEL), const(gain),
                  pl.BlockSpec((D_MODEL, _C_FRONT), lambda i: (0, 0), pipeline_mode=pl.Buffered(1)),
                  const(w_gates), const(wab), const(wabt), tab, tab,
                  const(alog_c), const(dtb_c), const(alog_r), const(dtb_r), const(conv_w), cst],
        out_specs=out_specs, out_shape=out_shapes,
        scratch_shapes=[pltpu.VMEM((bt, tt + 8, DN_CONV_CH), F32)],
        compiler_params=pltpu.CompilerParams(dimension_semantics=("arbitrary",),
                                             vmem_limit_bytes=VMEM_LIMIT),
    )(x2d, gain, w_in, w_gates, wab, wabt, cos_t, sin_t, alog_c, dtb_c, alog_r, dtb_r, conv_w, c0)


def _ret_tile(q_ref, k_ref, v_ref, rg_ref, dec_ref, hm_ref, qd_ref, kd_ref, cd_ref,
              o_ref, s_ref, *, bt, tt, cu):
    r = bt * tt
    blk = min(tt, cu)
    n_units = r // cu
    n_seg = cu // blk

    heads = range(RET_H)
    hrow = lambda a, h, m: a[h * m:(h + 1) * m]
    for n in range(n_units):
        rows = slice(n * cu, (n + 1) * cu)
        qf = q_ref[rows, :].astype(F32)
        kf = k_ref[rows, :].astype(F32)
        vf = v_ref[rows, :].astype(F32)
        q_m = [qf * hm_ref[h] for h in heads]
        q_d = [qf * qd_ref[h] for h in heads]
        k_d = [kf * kd_ref[h] for h in heads]
        v_h = [vf[:, h * RET_DV:(h + 1) * RET_DV] for h in heads]
        scores = (_dot_nt(jnp.concatenate(q_m, axis=0).astype(BF16), k_ref[rows, :])
                  * dec_ref[...]).astype(BF16)
        intra = []
        for h in heads:
            intra.append(_dot(hrow(scores, h, cu), v_h[h].astype(BF16)))
        inters = []
        for j in range(n_seg):
            seg = slice(j * blk, (j + 1) * blk)
            b = (n * cu + j * blk) // tt
            s = s_ref[b]
            q_seg = jnp.concatenate([a[seg] for a in q_d], axis=0).astype(BF16)
            k_seg = jnp.concatenate([a[seg] for a in k_d], axis=0).astype(BF16)
            v_seg = jnp.concatenate([a[seg] for a in v_h], axis=0).astype(BF16)
            inters.append(_dot(q_seg, s.astype(BF16)))
            s_ref[b] = s * cd_ref[...] + _dot_tn(k_seg, v_seg)
        for h in heads:
            inter = jnp.concatenate([hrow(a, h, blk) for a in inters], axis=0)
            gate = rg_ref[rows, h * RET_DV:(h + 1) * RET_DV].astype(F32)
            o_ref[rows, h * RET_DV:(h + 1) * RET_DV] = (
                _rms_plain(intra[h] + inter) * gate).astype(BF16)


def _dn_tile(q_ref, k_ref, v_ref, gc_ref, gr_ref, dz_ref, dnorm_ref, lblk_ref, ublk_ref, lvl_ref,
             o_ref, s_ref, *, bt, tt, cu):
    r = bt * tt
    blk = min(tt, cu)
    n_units = r // cu
    n_seg = cu // blk
    n_lvl = lvl_ref.shape[0]

    ri = lax.broadcasted_iota(jnp.int32, (cu, cu), 0)
    ci = lax.broadcasted_iota(jnp.int32, (cu, cu), 1)
    same = (ri // blk) == (ci // blk)
    incl = same & (ri >= ci)
    strict = same & (ri > ci)

    gc = gc_ref[...]
    g_hi, g_mid, g_lo = _split3(gc)
    r_hi, r_mid, r_lo = _split3(gr_ref[...])
    lblk = lblk_ref[...]
    ublk = ublk_ref[...]

    units = [(n, h) for n in range(n_units) for h in range(DN_H)]
    cum_c, cum_r = [], []
    for n in range(n_units):
        rows = slice(n * cu, (n + 1) * cu)
        cum_c.append(_dot(lblk, g_hi[rows]) + _dot(lblk, g_mid[rows]) + _dot(lblk, g_lo[rows]))
        cum_r.append(_dot(r_hi[:, rows], ublk) + _dot(r_mid[:, rows], ublk) + _dot(r_lo[:, rows], ublk))

    qn, kn, kb, g_col, dec, a_mat, ab, d, rhs = {}, {}, {}, {}, {}, {}, {}, {}, {}
    for u in units:
        n, h = u
        rows = slice(n * cu, (n + 1) * cu)
        qb = q_ref[rows, h * DN_DK:(h + 1) * DN_DK]
        kb[u] = k_ref[rows, h * DN_DK:(h + 1) * DN_DK]
        kn[u] = kb[u].astype(F32)
        vc = v_ref[rows, h * DN_DV:(h + 1) * DN_DV].astype(F32)
        g_col[u] = cum_c[n][:, h:h + 1]
        g_row = cum_r[n][h:h + 1, :]
        beta = gc[rows, DN_H + h:DN_H + h + 1]
        dec[u] = jnp.where(incl, jnp.exp(jnp.where(incl, g_col[u] - g_row, 0.0)), 0.0)
        a_mat[u] = jnp.where(strict, _dot_nt(kb[u], kb[u]) * dec[u] * beta, 0.0)
        ab[u] = a_mat[u].astype(BF16)
        e_g = jnp.exp(g_col[u])
        rhs[u] = jnp.concatenate([beta * vc, beta * e_g * kn[u]], axis=-1)
        qn[u] = (qb, qb.astype(F32) * e_g)
        d[u] = -(lvl_ref[0] * a_mat[u])
    for lv in range(1, n_lvl):
        for u in units:
            db = d[u].astype(BF16)
            w = a_mat[u] + _dot(db, ab[u])
            z = w + _dot(w.astype(BF16), db)
            d[u] = d[u] - lvl_ref[lv] * z
    sol, p_mat = {}, {}
    for u in units:
        sol[u] = rhs[u] + _dot(d[u].astype(BF16), rhs[u].astype(BF16))
        p_mat[u] = (_dot_nt(qn[u][0], kb[u]) * dec[u]).astype(BF16)

    for n in range(n_units):
        for h in range(DN_H):
            u = (n, h)
            u_mat, w_mat = sol[u][:, :DN_DV], sol[u][:, DN_DV:]
            qe = qn[u][1]
            deltas, qss = [], []
            for j in range(n_seg):
                r0 = j * blk
                b = (n * cu + r0) // tt
                sb = s_ref[b, h].astype(BF16)
                lhs = jnp.concatenate([w_mat[r0:r0 + blk], qe[r0:r0 + blk]], axis=0).astype(BF16)
                ws_qs = _dot(lhs, sb)
                deltas.append(u_mat[r0:r0 + blk] - ws_qs[:blk])
                qss.append(ws_qs[blk:])
            delta = deltas[0] if n_seg == 1 else jnp.concatenate(deltas, axis=0)
            q_s = qss[0] if n_seg == 1 else jnp.concatenate(qss, axis=0)
            delta_b = delta.astype(BF16)
            o = q_s + _dot(p_mat[u], delta_b)
            for j in range(n_seg):
                r0 = j * blk
                b = (n * cu + r0) // tt
                g_last = g_col[u][r0 + blk - 1:r0 + blk, :]
                k_dec = (kn[u][r0:r0 + blk] * jnp.exp(g_last - g_col[u][r0:r0 + blk])).astype(BF16)
                s_ref[b, h] = s_ref[b, h] * jnp.exp(g_last) + _dot_tn(k_dec, delta_b[r0:r0 + blk])
            gate = dz_ref[n * cu:(n + 1) * cu, h * DN_DV:(h + 1) * DN_DV].astype(F32)
            o_ref[n * cu:(n + 1) * cu, h * DN_DV:(h + 1) * DN_DV] = (
                _rms_plain(o) * dnorm_ref[...] * gate).astype(BF16)


def _seq_kernel(rq_ref, rk_ref, rv_ref, rg_ref, rs0_ref, dec_ref, hm_ref, qd_ref, kd_ref, cd_ref,
                dq_ref, dk_ref, dv_ref, gc_ref, gr_ref, dz_ref, dnorm_ref, ds0_ref, lblk_ref, ublk_ref,
                lvl_ref, og_ref, rs_ref, od_ref, ds_ref, *, bt, tt, ret_cu, dn_cu):
    @pl.when(pl.program_id(1) == 0)
    def _():
        rs_ref[...] = rs0_ref[...]
        ds_ref[...] = ds0_ref[...]

    _dn_tile(dq_ref, dk_ref, dv_ref, gc_ref, gr_ref, dz_ref, dnorm_ref, lblk_ref, ublk_ref, lvl_ref,
             od_ref, ds_ref, bt=bt, tt=tt, cu=dn_cu)
    _ret_tile(rq_ref, rk_ref, rv_ref, rg_ref, dec_ref, hm_ref, qd_ref, kd_ref, cd_ref,
              og_ref, rs_ref, bt=bt, tt=tt, cu=ret_cu)


def _seq_call(rq, rk, rv, rg, rs0, ret_tables, dq, dk, dv, gbc, gbr, dz, dn_norm, ds0, dn_tables,
              bt, tt, ret_cu, dn_cu):
    bsz = ds0.shape[0]
    seq = dq.shape[0] // bsz
    nl = seq // tt
    r = bt * tt
    tok = lambda w: pl.BlockSpec((r, w), lambda b, l: (b * nl + l, 0))
    const = lambda a: pl.BlockSpec(a.shape, lambda b, l: (0,) * a.ndim)
    rst = pl.BlockSpec((bt, RET_QK, RET_DV), lambda b, l: (b, 0, 0))
    dst = pl.BlockSpec((bt, DN_H, DN_DK, DN_DV), lambda b, l: (b, 0, 0, 0))
    return pl.pallas_call(
        functools.partial(_seq_kernel, bt=bt, tt=tt, ret_cu=ret_cu, dn_cu=dn_cu),
        grid=(bsz // bt, nl), name="seq",
        in_specs=([tok(RET_QK), tok(RET_QK), tok(RET_VW), tok(RET_VW), rst]
                  + [const(t) for t in ret_tables]
                  + [tok(DN_QK), tok(DN_QK), tok(DN_VW), tok(LANES),
                     pl.BlockSpec((8, r), lambda b, l: (0, b * nl + l)),
                     tok(DN_VW), const(dn_norm), dst] + [const(t) for t in dn_tables]),
        out_specs=(tok(RET_VW), rst, tok(DN_VW), dst),
        out_shape=(jax.ShapeDtypeStruct((bsz * seq, RET_VW), BF16), jax.ShapeDtypeStruct(rs0.shape, F32),
                   jax.ShapeDtypeStruct((bsz * seq, DN_VW), BF16), jax.ShapeDtypeStruct(ds0.shape, F32)),
        compiler_params=pltpu.CompilerParams(dimension_semantics=("parallel", "arbitrary"),
                                             vmem_limit_bytes=VMEM_LIMIT),
    )(rq, rk, rv, rg, rs0, *ret_tables, dq, dk, dv, gbc, gbr, dz, dn_norm, ds0, *dn_tables)


def _mix_kernel(x_ref, og_ref, od_ref, ga_ref, gb_ref, wru_ref, wdu_ref, wo_ref, h_ref):
    branch_a = _dot(og_ref[...], wru_ref[...])
    branch_b = _dot(od_ref[...], wdu_ref[...])
    merged = (jax.nn.sigmoid(ga_ref[...].astype(F32)) * branch_a
              + jax.nn.sigmoid(gb_ref[...].astype(F32)) * branch_b)
    h_ref[...] = x_ref[...] + _dot(merged.astype(BF16), wo_ref[...])


def _mix_call(x2d, og, od, ga, gb, wru, wdu, wo, tm):
    n = x2d.shape[0]
    tok = lambda w: pl.BlockSpec((tm, w), lambda i: (i, 0))
    const = lambda a: pl.BlockSpec(a.shape, lambda i: (0,) * a.ndim, pipeline_mode=pl.Buffered(1))
    return pl.pallas_call(
        _mix_kernel, grid=(n // tm,), name="mix",
        in_specs=[tok(D_MODEL), tok(RET_VW), tok(DN_VW), tok(D_MODEL), tok(D_MODEL),
                  const(wru), const(wdu), const(wo)],
        out_specs=tok(D_MODEL), out_shape=jax.ShapeDtypeStruct((n, D_MODEL), F32),
        compiler_params=pltpu.CompilerParams(dimension_semantics=("parallel",),
                                             vmem_limit_bytes=VMEM_LIMIT),
    )(x2d, og, od, ga, gb, wru, wdu, wo)


_R_EXP = N_GROUPS


def _ffn_kernel(h_ref, p_ref, nffn_ref, wr_ref, br_ref, wgu_ref, wdn_ref, nple_ref, wpg_ref, wpp_ref,
                nfin_ref, y_ref, hid_ref):
    h = h_ref[...]
    u = _rms_plain(h) * nffn_ref[...]
    ub = u.astype(BF16)

    logits = _dot(ub, wr_ref[...]) + br_ref[...]
    lane = lax.broadcasted_iota(jnp.int32, logits.shape, 1)
    neg = jnp.float32(-jnp.inf)
    big = jnp.int32(LANES)
    gl = jnp.where(lane < N_GROUPS, logits, neg)
    g_max = jnp.max(gl, axis=-1, keepdims=True)
    p_grp = 1.0 / jnp.sum(jnp.exp(gl - g_max), axis=-1, keepdims=True)
    grp = jnp.min(jnp.where(gl == g_max, lane, big), axis=-1, keepdims=True)
    e_idx = lane - _R_EXP
    in_grp = (e_idx >= 0) & (e_idx < N_EXPERTS) & ((e_idx // EXPERTS_PER_GROUP) == grp)
    el = jnp.where(in_grp, logits, neg)
    v1 = jnp.max(el, axis=-1, keepdims=True)
    i1 = jnp.min(jnp.where(el == v1, lane, big), axis=-1, keepdims=True)
    el2 = jnp.where(lane == i1, neg, el)
    v2 = jnp.max(el2, axis=-1, keepdims=True)
    i2 = jnp.min(jnp.where(el2 == v2, lane, big), axis=-1, keepdims=True)
    e2 = jnp.exp(v2 - v1)
    w1 = 1.0 / (1.0 + e2)
    w2 = e2 / (1.0 + e2)
    combine = jnp.where(lane == i1, w1 * p_grp, 0.0) + jnp.where(lane == i2, w2 * p_grp, 0.0)

    for e in range(N_EXPERTS):
        gu = _dot(ub, wgu_ref[e])
        hid = _silu(gu[:, :D_EXPERT]) * gu[:, D_EXPERT:]
        hid_ref[:, e * D_EXPERT:(e + 1) * D_EXPERT] = (
            combine[:, _R_EXP + e:_R_EXP + e + 1] * hid).astype(BF16)
    h = h + _dot(hid_ref[...], wdn_ref[...])

    u3 = (_rms_plain(h) * nple_ref[...]).astype(BF16)
    gate = jax.nn.sigmoid(_dot(u3, wpg_ref[...]))
    h = h + gate * _dot(p_ref[...].astype(BF16), wpp_ref[...])
    y_ref[...] = _rms_plain(h) * nfin_ref[...]


def _ffn_call(h2d, p2d, nffn, wr, br, wgu, wdn, nple, wpg, wpp, nfin, tm):
    n = h2d.shape[0]
    tok = lambda w: pl.BlockSpec((tm, w), lambda i: (i, 0))
    const = lambda a: pl.BlockSpec(a.shape, lambda i: (0,) * a.ndim, pipeline_mode=pl.Buffered(1))
    return pl.pallas_call(
        _ffn_kernel, grid=(n // tm,), name="ffn",
        in_specs=[tok(D_MODEL), tok(PLE_DIM), const(nffn), const(wr), const(br), const(wgu), const(wdn),
                  const(nple), const(wpg), const(wpp), const(nfin)],
        out_specs=tok(D_MODEL), out_shape=jax.ShapeDtypeStruct((n, D_MODEL), F32),
        scratch_shapes=[pltpu.VMEM((tm, N_EXPERTS * D_EXPERT), BF16)],
        compiler_params=pltpu.CompilerParams(dimension_semantics=("parallel",),
                                             vmem_limit_bytes=VMEM_LIMIT),
    )(h2d, p2d, nffn, wr, br, wgu, wdn, nple, wpg, wpp, nfin)


def _rope_tables(pos):
    half = RET_DK // 2
    inv = 1.0 / (ROPE_BASE ** (jnp.arange(half, dtype=F32) / half))
    ang = pos.astype(F32)[:, None] * inv[None, :]
    cos, sin = jnp.cos(ang), jnp.sin(ang)
    cos_t = jnp.tile(jnp.concatenate([cos, cos], axis=-1), (1, RET_H))
    sin_t = jnp.tile(jnp.concatenate([-sin, sin], axis=-1), (1, RET_H))
    return cos_t, sin_t


def _ret_tables(r, c):
    log_gamma = jnp.log(1.0 - 2.0 ** (-5.0 - jnp.arange(RET_H, dtype=F32)))
    row = jnp.arange(r)
    idx = (row % c).astype(F32)
    diff = idx[:, None] - idx[None, :]
    causal = (diff >= 0) & ((row[:, None] // c) == (row[None, :] // c))
    lg = log_gamma[:, None, None]
    decay = jnp.where(causal, jnp.exp(jnp.where(causal, diff, 0.0) * lg), 0.0)
    q_dec = jnp.exp((idx + 1.0)[None, :] * log_gamma[:, None])[..., None]
    k_dec = jnp.exp((c - 1.0 - idx)[None, :] * log_gamma[:, None])[..., None]
    chunk_dec = jnp.exp(c * log_gamma)
    lane_head = jnp.arange(RET_QK) // RET_DK
    head_mask = (lane_head[None, :] == jnp.arange(RET_H)[:, None]).astype(F32)[:, None, :]
    return (decay.reshape(RET_H * r, r), head_mask, q_dec * head_mask, k_dec * head_mask,
            jnp.broadcast_to(chunk_dec[lane_head][:, None], (RET_QK, RET_DV)))


def _dn_tables(r, c):
    i = jnp.arange(r)
    ri, ci = i[:, None], i[None, :]
    same = (ri // c) == (ci // c)
    lblk = ((ri >= ci) & same).astype(BF16)
    x = ri ^ ci
    lvl, m = [], 1
    while m < c:
        lvl.append((same & (ri > ci) & (x >= m) & (x < 2 * m)).astype(F32))
        m *= 2
    return lblk, lblk.T, jnp.stack(lvl)


def _layer(x, p, s_ret, s_dn, s_conv, pos_table, wts, cfg):
    bsz, seq, _ = x.shape
    n = bsz * seq
    x2d = x.reshape(n, D_MODEL)
    cos_t, sin_t = pos_table
    (q, k, v, rg, dq, dk, dv, dz, ga, gb, gbc, gbr, conv_new) = _proj_call(
        x2d, wts["norm_mix"], wts["w_in"], wts["w_gates"], wts["wab"], wts["wabt"], cos_t, sin_t,
        wts["alog_c"], wts["dtb_c"], wts["alog_r"], wts["dtb_r"], wts["conv_w"], s_conv, cfg["tm_proj"])

    ret_tables = _ret_tables(cfg["ret_cu"], min(cfg["tt"], cfg["ret_cu"]))
    dn_tables = _dn_tables(cfg["dn_cu"], min(cfg["tt"], cfg["dn_cu"]))
    og, ret_new, od, dn_new = _seq_call(
        q, k, v, rg, s_ret.reshape(bsz, RET_QK, RET_DV), ret_tables,
        dq, dk, dv, gbc, gbr, dz, wts["dn_norm"], s_dn, dn_tables,
        cfg["bt"], cfg["tt"], cfg["ret_cu"], cfg["dn_cu"])
    ret_new = ret_new.reshape(bsz, RET_H, RET_DK, RET_DV)

    h1 = _mix_call(x2d, og, od, ga, gb,
                   wts["w_ret_up"], wts["w_dn_up"], wts["w_out"], cfg["tm"])
    y = _ffn_call(h1, p.reshape(n, PLE_DIM), wts["norm_ffn"], wts["w_router"], wts["b_router"],
                  wts["w_gate_up"], wts["w_down"], wts["norm_ple"], wts["w_ple_gate"], wts["w_ple_proj"],
                  wts["norm_final"], cfg["tm"])
    return y.reshape(bsz, seq, D_MODEL), ret_new, dn_new, conv_new


def _prep_weights(norm_mix, w_in, conv_w, dn_a_log, dn_dt_bias, dn_norm, w_ret_up, w_dn_up, w_out,
                  norm_ffn, w_router_group, b_router_group, w_router_expert, b_router_expert,
                  w_gate_up, w_down, norm_ple, w_ple_gate, w_ple_proj, norm_final):
    assert sum(IN_WIDTHS[:6]) == _C_FRONT and sum(IN_WIDTHS[:8]) == _C_GATES
    w_in = w_in.astype(BF16)
    w_gates = w_in[:, _C_GATES:]
    wab8 = w_in[:, _C_FRONT:_C_GATES]
    wab = jnp.pad(wab8, ((0, 0), (0, LANES - 2 * DN_H)))
    pad_lanes = lambda a: jnp.pad(a.astype(F32), (0, LANES - a.shape[0]))[None, :]
    pad_rows = lambda a: jnp.pad(a.astype(F32), (0, 8 - a.shape[0]))[:, None]
    w_router = jnp.pad(jnp.concatenate([w_router_group, w_router_expert], axis=-1),
                       ((0, 0), (0, LANES - N_GROUPS - N_EXPERTS))).astype(BF16)
    b_router = pad_lanes(jnp.concatenate([b_router_group, b_router_expert]))
    row = lambda a: a.astype(F32)[None, :]
    return dict(
        norm_mix=row(norm_mix), w_in=w_in, w_gates=w_gates, wab=wab, wabt=wab8.T,
        alog_c=pad_lanes(dn_a_log), dtb_c=pad_lanes(dn_dt_bias),
        alog_r=pad_rows(dn_a_log), dtb_r=pad_rows(dn_dt_bias),
        conv_w=conv_w.astype(F32), dn_norm=row(dn_norm),
        w_ret_up=w_ret_up.astype(BF16), w_dn_up=w_dn_up.astype(BF16), w_out=w_out.astype(BF16),
        norm_ffn=row(norm_ffn), w_router=w_router, b_router=b_router,
        w_gate_up=w_gate_up.astype(BF16),
        w_down=w_down.astype(BF16).reshape(N_EXPERTS * D_EXPERT, D_MODEL),
        norm_ple=row(norm_ple), w_ple_gate=w_ple_gate.astype(BF16), w_ple_proj=w_ple_proj.astype(BF16),
        norm_final=row(norm_final),
    )


def kernel(x_prompt, x_sample, p_prompt, p_sample, state_ret, state_dn, state_conv, norm_mix, w_in, conv_w, dn_a_log, dn_dt_bias, dn_norm, w_ret_up, w_dn_up, w_out, norm_ffn, w_router_group, b_router_group, w_router_expert, b_router_expert, w_gate_up, w_down, norm_ple, w_ple_gate, w_ple_proj, norm_final):
    depth = w_in.shape[0]
    assert depth == 1, "one layer: the final norm is fused into the layer's last kernel"
    bp, lp, _ = x_prompt.shape
    bs, ls, _ = x_sample.shape
    wts = _prep_weights(norm_mix[0], w_in[0], conv_w[0], dn_a_log[0], dn_dt_bias[0], dn_norm[0],
                        w_ret_up[0], w_dn_up[0], w_out[0], norm_ffn[0], w_router_group[0],
                        b_router_group[0], w_router_expert[0], b_router_expert[0], w_gate_up[0],
                        w_down[0], norm_ple[0], w_ple_gate[0], w_ple_proj[0], norm_final)

    tm = 512
    cfg_p = dict(tm=tm, tm_proj=256, bt=1, tt=512, ret_cu=256, dn_cu=128)
    cfg_s = dict(tm=tm, tm_proj=256, bt=16, tt=ls, ret_cu=16 * ls, dn_cu=16 * ls)

    pos_p = _rope_tables(jnp.arange(lp, dtype=jnp.int32))
    cos_s, sin_s = _rope_tables(PAST_LEN + jnp.arange(ls, dtype=jnp.int32))
    pos_s = (jnp.tile(cos_s, (bs, 1)), jnp.tile(sin_s, (bs, 1)))

    zero_ret = jnp.zeros((bp, RET_H, RET_DK, RET_DV), F32)
    zero_dn = jnp.zeros((bp, DN_H, DN_DK, DN_DV), F32)
    zero_conv = jnp.zeros((bp, CONV_W - 1, DN_CONV_CH), F32)

    y_p, r_p, d_p, c_p = _layer(x_prompt, p_prompt[0], zero_ret, zero_dn, zero_conv,
                                pos_p, wts, cfg_p)
    y_s, r_s, d_s, c_s = _layer(x_sample, p_sample[0], state_ret[0], state_dn[0], state_conv[0],
                                pos_s, wts, cfg_s)
    return (y_p, y_s, r_p[None], d_p[None], c_p[None], r_s[None], d_s[None], c_s[None])
```

```python
import functools

import jax
import jax.numpy as jnp
from jax import lax
from jax.experimental import pallas as pl
from jax.experimental.pallas import tpu as pltpu

F32 = jnp.float32
BF16 = jnp.bfloat16

D_MODEL = 1024
RET_H, RET_DK, RET_DV = 4, 64, 128
DN_H, DN_DK, DN_DV = 4, 128, 128
CONV_W = 4
ROPE_BASE = 10000.0
PAST_LEN = 16384
N_GROUPS, EXPERTS_PER_GROUP = 4, 4
N_EXPERTS = N_GROUPS * EXPERTS_PER_GROUP
D_EXPERT = 256
PLE_DIM = 256
EPS = 1e-6

RET_QK = RET_H * RET_DK
RET_VW = RET_H * RET_DV
DN_QK = DN_H * DN_DK
DN_VW = DN_H * DN_DV
DN_CONV_CH = 2 * DN_QK + DN_VW
IN_WIDTHS = (RET_QK, RET_QK, RET_VW, RET_VW, DN_CONV_CH, DN_VW, DN_H, DN_H, D_MODEL, D_MODEL)

LANES = 128
VMEM_LIMIT = 56 * 1024 * 1024

_C_Q, _C_K, _C_V, _C_RG = 0, 256, 512, 1024
_C_DQKV, _C_DZ, _C_FRONT = 1536, 3072, 3584
_C_GATES = _C_FRONT + 2 * DN_H


def _silu(x):
    return x * jax.nn.sigmoid(x)


def _softplus(x):
    return jnp.maximum(x, 0.0) + jnp.log1p(jnp.exp(-jnp.abs(x)))


def _dot(a, b):
    return jnp.dot(a, b, preferred_element_type=F32)


def _dot_nt(a, b):
    return lax.dot_general(a, b, (((1,), (1,)), ((), ())), preferred_element_type=F32)


def _dot_tn(a, b):
    return lax.dot_general(a, b, (((0,), (0,)), ((), ())), preferred_element_type=F32)


def _split3(x):
    hi = x.astype(BF16)
    r = x - hi.astype(F32)
    mid = r.astype(BF16)
    lo = (r - mid.astype(F32)).astype(BF16)
    return hi, mid, lo


def _rms_plain(x):
    return x * lax.rsqrt(jnp.mean(x * x, axis=-1, keepdims=True) + EPS)


def _proj_kernel(x_ref, gain_ref, w_ref, wg_ref, wab_ref, wabt_ref, cos_ref, sin_ref,
                 alog_c_ref, dtb_c_ref, alog_r_ref, dtb_r_ref, cw_ref, c0_ref,
                 q_ref, k_ref, v_ref, rg_ref, dq_ref, dk_ref, dv_ref, dz_ref, ga_ref, gb_ref,
                 gbc_ref, gbr_ref, c_ref, ext_ref, *, bt, tt, nl):
    pad = 8
    tail = CONV_W - 1

    @pl.when(pl.program_id(0) % nl == 0)
    def _():
        ext_ref[:, pad - tail:pad, :] = c0_ref[...]

    x = x_ref[...]
    u = x * lax.rsqrt(jnp.mean(x * x, axis=-1, keepdims=True) + EPS) * gain_ref[...]
    ub = u.astype(BF16)

    def mm(lo, hi):
        return _dot(ub, w_ref[:, lo:hi])

    ext_ref[:, pad:pad + tt, :] = mm(_C_DQKV, _C_DZ).reshape(bt, tt, DN_CONV_CH)
    conv = ext_ref[:, pad - tail:pad - tail + tt, :] * cw_ref[0:1, :]
    for j in range(1, CONV_W):
        conv = conv + ext_ref[:, pad - tail + j:pad - tail + j + tt, :] * cw_ref[j:j + 1, :]
    new_tail = ext_ref[:, pad + tt - tail:pad + tt, :]
    ext_ref[:, pad - tail:pad, :] = new_tail
    c_ref[...] = new_tail
    conv = _silu(conv).reshape(bt * tt, DN_CONV_CH)
    for h in range(DN_H):
        cq = conv[:, h * DN_DK:(h + 1) * DN_DK]
        ck = conv[:, DN_QK + h * DN_DK:DN_QK + (h + 1) * DN_DK]
        dq_ref[:, h * DN_DK:(h + 1) * DN_DK] = (
            cq * lax.rsqrt(jnp.sum(cq * cq, axis=-1, keepdims=True) + EPS) * (DN_DK ** -0.5)).astype(BF16)
        dk_ref[:, h * DN_DK:(h + 1) * DN_DK] = (
            ck * lax.rsqrt(jnp.sum(ck * ck, axis=-1, keepdims=True) + EPS)).astype(BF16)
    dv_ref[...] = conv[:, 2 * DN_QK:].astype(BF16)

    cos = cos_ref[...]
    sin = sin_ref[...]
    lane = lax.broadcasted_iota(jnp.int32, cos.shape, 1)
    first_half = (lane % RET_DK) < (RET_DK // 2)

    def rot(t):
        swapped = jnp.where(first_half, pltpu.roll(t, RET_QK - RET_DK // 2, 1),
                            pltpu.roll(t, RET_DK // 2, 1))
        return t * cos + swapped * sin

    q_ref[...] = (rot(mm(_C_Q, _C_K)) * (RET_DK ** -0.5)).astype(BF16)
    k_ref[...] = rot(mm(_C_K, _C_V)).astype(BF16)
    v_ref[...] = mm(_C_V, _C_RG).astype(BF16)
    rg_ref[...] = _silu(mm(_C_RG, _C_DQKV)).astype(BF16)
    dz_ref[...] = _silu(mm(_C_DZ, _C_FRONT)).astype(BF16)
    ga_ref[...] = _dot(ub, wg_ref[:, :D_MODEL]).astype(BF16)
    gb_ref[...] = _dot(ub, wg_ref[:, D_MODEL:]).astype(BF16)

    dab = _dot(ub, wab_ref[...])
    lane_c = lax.broadcasted_iota(jnp.int32, dab.shape, 1)
    g_c = -jnp.exp(alog_c_ref[...]) * _softplus(dab + dtb_c_ref[...])
    gbc_ref[...] = jnp.where(lane_c < DN_H, g_c, jax.nn.sigmoid(dab))
    dabt = _dot_nt(wabt_ref[...], ub)
    row_r = lax.broadcasted_iota(jnp.int32, dabt.shape, 0)
    g_r = -jnp.exp(alog_r_ref[...]) * _softplus(dabt + dtb_r_ref[...])
    gbr_ref[...] = jnp.where(row_r < DN_H, g_r, jax.nn.sigmoid(dabt))


def _proj_call(x2d, gain, w_in, w_gates, wab, wabt, cos_t, sin_t, alog_c, dtb_c, alog_r, dtb_r, conv_w, c0, tm):
    n = x2d.shape[0]
    bsz = c0.shape[0]
    seq = n // bsz
    tt = min(tm, seq)
    bt = tm // tt
    nl = seq // tt
    table_blocks = cos_t.shape[0] // tm
    tok = lambda w: pl.BlockSpec((tm, w), lambda i: (i, 0))
    const = lambda a: pl.BlockSpec(a.shape, lambda i: (0,) * a.ndim, pipeline_mode=pl.Buffered(1))
    tab = pl.BlockSpec((tm, RET_QK), lambda i: (i % table_blocks, 0))
    cst = pl.BlockSpec((bt, CONV_W - 1, DN_CONV_CH), lambda i: (i // nl, 0, 0))
    bf = lambda w: jax.ShapeDtypeStruct((n, w), BF16)
    out_shapes = (bf(RET_QK), bf(RET_QK), bf(RET_VW), bf(RET_VW), bf(DN_QK), bf(DN_QK), bf(DN_VW),
                  bf(DN_VW), bf(D_MODEL), bf(D_MODEL),
                  jax.ShapeDtypeStruct((n, LANES), F32), jax.ShapeDtypeStruct((8, n), F32),
                  jax.ShapeDtypeStruct(c0.shape, F32))
    out_specs = (tok(RET_QK), tok(RET_QK), tok(RET_VW), tok(RET_VW), tok(DN_QK), tok(DN_QK), tok(DN_VW),
                 tok(DN_VW), tok(D_MODEL), tok(D_MODEL), tok(LANES),
                 pl.BlockSpec((8, tm), lambda i: (0, i)), cst)
    return pl.pallas_call(
        functools.partial(_proj_kernel, bt=bt, tt=tt, nl=nl), grid=(n // tm,), name="proj",
        in_specs=[tok(D_MODEL), const(gain),
                  pl.BlockSpec((D_MODEL, _C_FRONT), lambda i: (0, 0), pipeline_mode=pl.Buffered(1)),
                  const(w_gates), const(wab), const(wabt), tab, tab,
                  const(alog_c), const(dtb_c), const(alog_r), const(dtb_r), const(conv_w), cst],
        out_specs=out_specs, out_shape=out_shapes,
        scratch_shapes=[pltpu.VMEM((bt, tt + 8, DN_CONV_CH), F32)],
        compiler_params=pltpu.CompilerParams(dimension_semantics=("arbitrary",),
                                             vmem_limit_bytes=VMEM_LIMIT),
    )(x2d, gain, w_in, w_gates, wab, wabt, cos_t, sin_t, alog_c, dtb_c, alog_r, dtb_r, conv_w, c0)


def _ret_tile(q_ref, k_ref, v_ref, rg_ref, dec_ref, hm_ref, qd_ref, kd_ref, cd_ref,
              o_ref, s_ref, *, bt, tt, cu):
    r = bt * tt
    blk = min(tt, cu)
    n_units = r // cu
    n_seg = cu // blk

    heads = range(RET_H)
    hrow = lambda a, h, m: a[h * m:(h + 1) * m]
    for n in range(n_units):
        rows = slice(n * cu, (n + 1) * cu)
        qf = q_ref[rows, :].astype(F32)
        kf = k_ref[rows, :].astype(F32)
        vf = v_ref[rows, :].astype(F32)
        q_m = [qf * hm_ref[h] for h in heads]
        q_d = [qf * qd_ref[h] for h in heads]
        k_d = [kf * kd_ref[h] for h in heads]
        v_h = [vf[:, h * RET_DV:(h + 1) * RET_DV] for h in heads]
        scores = (_dot_nt(jnp.concatenate(q_m, axis=0).astype(BF16), k_ref[rows, :])
                  * dec_ref[...]).astype(BF16)
        intra = []
        for h in heads:
            intra.append(_dot(hrow(scores, h, cu), v_h[h].astype(BF16)))
        inters = []
        for j in range(n_seg):
            seg = slice(j * blk, (j + 1) * blk)
            b = (n * cu + j * blk) // tt
            s = s_ref[b]
            q_seg = jnp.concatenate([a[seg] for a in q_d], axis=0).astype(BF16)
            k_seg = jnp.concatenate([a[seg] for a in k_d], axis=0).astype(BF16)
            v_seg = jnp.concatenate([a[seg] for a in v_h], axis=0).astype(BF16)
            inters.append(_dot(q_seg, s.astype(BF16)))
            s_ref[b] = s * cd_ref[...] + _dot_tn(k_seg, v_seg)
        for h in heads:
            inter = jnp.concatenate([hrow(a, h, blk) for a in inters], axis=0)
            gate = rg_ref[rows, h * RET_DV:(h + 1) * RET_DV].astype(F32)
            o_ref[rows, h * RET_DV:(h + 1) * RET_DV] = (
                _rms_plain(intra[h] + inter) * gate).astype(BF16)


def _dn_tile(q_ref, k_ref, v_ref, gc_ref, gr_ref, dz_ref, dnorm_ref, lblk_ref, ublk_ref, lvl_ref,
             o_ref, s_ref, *, bt, tt, cu):
    r = bt * tt
    blk = min(tt, cu)
    n_units = r // cu
    n_seg = cu // blk
    n_lvl = lvl_ref.shape[0]

    ri = lax.broadcasted_iota(jnp.int32, (cu, cu), 0)
    ci = lax.broadcasted_iota(jnp.int32, (cu, cu), 1)
    same = (ri // blk) == (ci // blk)
    incl = same & (ri >= ci)
    strict = same & (ri > ci)

    gc = gc_ref[...]
    g_hi, g_mid, g_lo = _split3(gc)
    r_hi, r_mid, r_lo = _split3(gr_ref[...])
    lblk = lblk_ref[...]
    ublk = ublk_ref[...]

    units = [(n, h) for n in range(n_units) for h in range(DN_H)]
    cum_c, cum_r = [], []
    for n in range(n_units):
        rows = slice(n * cu, (n + 1) * cu)
        cum_c.append(_dot(lblk, g_hi[rows]) + _dot(lblk, g_mid[rows]) + _dot(lblk, g_lo[rows]))
        cum_r.append(_dot(r_hi[:, rows], ublk) + _dot(r_mid[:, rows], ublk) + _dot(r_lo[:, rows], ublk))

    qn, kn, kb, g_col, dec, a_mat, ab, d, rhs = {}, {}, {}, {}, {}, {}, {}, {}, {}
    for u in units:
        n, h = u
        rows = slice(n * cu, (n + 1) * cu)
        qb = q_ref[rows, h * DN_DK:(h + 1) * DN_DK]
        kb[u] = k_ref[rows, h * DN_DK:(h + 1) * DN_DK]
        kn[u] = kb[u].astype(F32)
        vc = v_ref[rows, h * DN_DV:(h + 1) * DN_DV].astype(F32)
        g_col[u] = cum_c[n][:, h:h + 1]
        g_row = cum_r[n][h:h + 1, :]
        beta = gc[rows, DN_H + h:DN_H + h + 1]
        dec[u] = jnp.where(incl, jnp.exp(jnp.where(incl, g_col[u] - g_row, 0.0)), 0.0)
        a_mat[u] = jnp.where(strict, _dot_nt(kb[u], kb[u]) * dec[u] * beta, 0.0)
        ab[u] = a_mat[u].astype(BF16)
        e_g = jnp.exp(g_col[u])
        rhs[u] = jnp.concatenate([beta * vc, beta * e_g * kn[u]], axis=-1)
        qn[u] = (qb, qb.astype(F32) * e_g)
        d[u] = -(lvl_ref[0] * a_mat[u])
    for lv in range(1, n_lvl):
        for u in units:
            db = d[u].astype(BF16)
            w = a_mat[u] + _dot(db, ab[u])
            z = w + _dot(w.astype(BF16), db)
            d[u] = d[u] - lvl_ref[lv] * z
    sol, p_mat = {}, {}
    for u in units:
        sol[u] = rhs[u] + _dot(d[u].astype(BF16), rhs[u].astype(BF16))
        p_mat[u] = (_dot_nt(qn[u][0], kb[u]) * dec[u]).astype(BF16)

    for n in range(n_units):
        for h in range(DN_H):
            u = (n, h)
            u_mat, w_mat = sol[u][:, :DN_DV], sol[u][:, DN_DV:]
            qe = qn[u][1]
            deltas, qss = [], []
            for j in range(n_seg):
                r0 = j * blk
                b = (n * cu + r0) // tt
                sb = s_ref[b, h].astype(BF16)
                lhs = jnp.concatenate([w_mat[r0:r0 + blk], qe[r0:r0 + blk]], axis=0).astype(BF16)
                ws_qs = _dot(lhs, sb)
                deltas.append(u_mat[r0:r0 + blk] - ws_qs[:blk])
                qss.append(ws_qs[blk:])
            delta = deltas[0] if n_seg == 1 else jnp.concatenate(deltas, axis=0)
            q_s = qss[0] if n_seg == 1 else jnp.concatenate(qss, axis=0)
            delta_b = delta.astype(BF16)
            o = q_s + _dot(p_mat[u], delta_b)
            for j in range(n_seg):
                r0 = j * blk
                b = (n * cu + r0) // tt
                g_last = g_col[u][r0 + blk - 1:r0 + blk, :]
                k_dec = (kn[u][r0:r0 + blk] * jnp.exp(g_last - g_col[u][r0:r0 + blk])).astype(BF16)
                s_ref[b, h] = s_ref[b, h] * jnp.exp(g_last) + _dot_tn(k_dec, delta_b[r0:r0 + blk])
            gate = dz_ref[n * cu:(n + 1) * cu, h * DN_DV:(h + 1) * DN_DV].astype(F32)
            o_ref[n * cu:(n + 1) * cu, h * DN_DV:(h + 1) * DN_DV] = (
                _rms_plain(o) * dnorm_ref[...] * gate).astype(BF16)


def _seq_kernel(rq_ref, rk_ref, rv_ref, rg_ref, rs0_ref, dec_ref, hm_ref, qd_ref, kd_ref, cd_ref,
                dq_ref, dk_ref, dv_ref, gc_ref, gr_ref, dz_ref, dnorm_ref, ds0_ref, lblk_ref, ublk_ref,
                lvl_ref, og_ref, rs_ref, od_ref, ds_ref, *, bt, tt, ret_cu, dn_cu):
    @pl.when(pl.program_id(1) == 0)
    def _():
        rs_ref[...] = rs0_ref[...]
        ds_ref[...] = ds0_ref[...]

    _dn_tile(dq_ref, dk_ref, dv_ref, gc_ref, gr_ref, dz_ref, dnorm_ref, lblk_ref, ublk_ref, lvl_ref,
             od_ref, ds_ref, bt=bt, tt=tt, cu=dn_cu)
    _ret_tile(rq_ref, rk_ref, rv_ref, rg_ref, dec_ref, hm_ref, qd_ref, kd_ref, cd_ref,
              og_ref, rs_ref, bt=bt, tt=tt, cu=ret_cu)


def _seq_call(rq, rk, rv, rg, rs0, ret_tables, dq, dk, dv, gbc, gbr, dz, dn_norm, ds0, dn_tables,
              bt, tt, ret_cu, dn_cu):
    bsz = ds0.shape[0]
    seq = dq.shape[0] // bsz
    nl = seq // tt
    r = bt * tt
    tok = lambda w: pl.BlockSpec((r, w), lambda b, l: (b * nl + l, 0))
    const = lambda a: pl.BlockSpec(a.shape, lambda b, l: (0,) * a.ndim)
    rst = pl.BlockSpec((bt, RET_QK, RET_DV), lambda b, l: (b, 0, 0))
    dst = pl.BlockSpec((bt, DN_H, DN_DK, DN_DV), lambda b, l: (b, 0, 0, 0))
    return pl.pallas_call(
        functools.partial(_seq_kernel, bt=bt, tt=tt, ret_cu=ret_cu, dn_cu=dn_cu),
        grid=(bsz // bt, nl), name="seq",
        in_specs=([tok(RET_QK), tok(RET_QK), tok(RET_VW), tok(RET_VW), rst]
                  + [const(t) for t in ret_tables]
                  + [tok(DN_QK), tok(DN_QK), tok(DN_VW), tok(LANES),
                     pl.BlockSpec((8, r), lambda b, l: (0, b * nl + l)),
                     tok(DN_VW), const(dn_norm), dst] + [const(t) for t in dn_tables]),
        out_specs=(tok(RET_VW), rst, tok(DN_VW), dst),
        out_shape=(jax.ShapeDtypeStruct((bsz * seq, RET_VW), BF16), jax.ShapeDtypeStruct(rs0.shape, F32),
                   jax.ShapeDtypeStruct((bsz * seq, DN_VW), BF16), jax.ShapeDtypeStruct(ds0.shape, F32)),
        compiler_params=pltpu.CompilerParams(dimension_semantics=("parallel", "arbitrary"),
                                             vmem_limit_bytes=VMEM_LIMIT),
    )(rq, rk, rv, rg, rs0, *ret_tables, dq, dk, dv, gbc, gbr, dz, dn_norm, ds0, *dn_tables)


_R_EXP = N_GROUPS


def _ffn_kernel(x_ref, og_ref, od_ref, ga_ref, gb_ref, p_ref,
                wru_ref, wdu_ref, wo_ref, nffn_ref, wr_ref, br_ref, wgu_ref, wdn_ref, nple_ref, wpg_ref,
                wpp_ref, nfin_ref, y_ref, hid_ref):
    branch_a = _dot(og_ref[...], wru_ref[...])
    branch_b = _dot(od_ref[...], wdu_ref[...])
    merged = (jax.nn.sigmoid(ga_ref[...].astype(F32)) * branch_a
              + jax.nn.sigmoid(gb_ref[...].astype(F32)) * branch_b)
    h = x_ref[...] + _dot(merged.astype(BF16), wo_ref[...])

    u = _rms_plain(h) * nffn_ref[...]
    ub = u.astype(BF16)

    logits = _dot(ub, wr_ref[...]) + br_ref[...]
    lane = lax.broadcasted_iota(jnp.int32, logits.shape, 1)
    neg = jnp.float32(-jnp.inf)
    big = jnp.int32(LANES)
    gl = jnp.where(lane < N_GROUPS, logits, neg)
    g_max = jnp.max(gl, axis=-1, keepdims=True)
    p_grp = 1.0 / jnp.sum(jnp.exp(gl - g_max), axis=-1, keepdims=True)
    grp = jnp.min(jnp.where(gl == g_max, lane, big), axis=-1, keepdims=True)
    e_idx = lane - _R_EXP
    in_grp = (e_idx >= 0) & (e_idx < N_EXPERTS) & ((e_idx // EXPERTS_PER_GROUP) == grp)
    el = jnp.where(in_grp, logits, neg)
    v1 = jnp.max(el, axis=-1, keepdims=True)
    i1 = jnp.min(jnp.where(el == v1, lane, big), axis=-1, keepdims=True)
    el2 = jnp.where(lane == i1, neg, el)
    v2 = jnp.max(el2, axis=-1, keepdims=True)
    i2 = jnp.min(jnp.where(el2 == v2, lane, big), axis=-1, keepdims=True)
    e2 = jnp.exp(v2 - v1)
    w1 = 1.0 / (1.0 + e2)
    w2 = e2 / (1.0 + e2)
    combine = jnp.where(lane == i1, w1 * p_grp, 0.0) + jnp.where(lane == i2, w2 * p_grp, 0.0)

    for e in range(N_EXPERTS):
        gu = _dot(ub, wgu_ref[e])
        hid = _silu(gu[:, :D_EXPERT]) * gu[:, D_EXPERT:]
        hid_ref[:, e * D_EXPERT:(e + 1) * D_EXPERT] = (
            combine[:, _R_EXP + e:_R_EXP + e + 1] * hid).astype(BF16)
    h = h + _dot(hid_ref[...], wdn_ref[...])

    u3 = (_rms_plain(h) * nple_ref[...]).astype(BF16)
    gate = jax.nn.sigmoid(_dot(u3, wpg_ref[...]))
    h = h + gate * _dot(p_ref[...].astype(BF16), wpp_ref[...])
    y_ref[...] = _rms_plain(h) * nfin_ref[...]


def _ffn_call(tok_in, weights, tm):
    n = tok_in[0].shape[0]
    tok = lambda a: pl.BlockSpec((tm, a.shape[1]), lambda i: (i, 0))
    const = lambda a: pl.BlockSpec(a.shape, lambda i: (0,) * a.ndim, pipeline_mode=pl.Buffered(1))
    return pl.pallas_call(
        _ffn_kernel, grid=(n // tm,), name="ffn",
        in_specs=[tok(a) for a in tok_in] + [const(w) for w in weights],
        out_specs=tok(tok_in[0]), out_shape=jax.ShapeDtypeStruct((n, D_MODEL), F32),
        scratch_shapes=[pltpu.VMEM((tm, N_EXPERTS * D_EXPERT), BF16)],
        compiler_params=pltpu.CompilerParams(dimension_semantics=("parallel",),
                                             vmem_limit_bytes=VMEM_LIMIT),
    )(*tok_in, *weights)


def _rope_tables(pos):
    half = RET_DK // 2
    inv = 1.0 / (ROPE_BASE ** (jnp.arange(half, dtype=F32) / half))
    ang = pos.astype(F32)[:, None] * inv[None, :]
    cos, sin = jnp.cos(ang), jnp.sin(ang)
    cos_t = jnp.tile(jnp.concatenate([cos, cos], axis=-1), (1, RET_H))
    sin_t = jnp.tile(jnp.concatenate([-sin, sin], axis=-1), (1, RET_H))
    return cos_t, sin_t


def _ret_tables(r, c):
    log_gamma = jnp.log(1.0 - 2.0 ** (-5.0 - jnp.arange(RET_H, dtype=F32)))
    row = jnp.arange(r)
    idx = (row % c).astype(F32)
    diff = idx[:, None] - idx[None, :]
    causal = (diff >= 0) & ((row[:, None] // c) == (row[None, :] // c))
    lg = log_gamma[:, None, None]
    decay = jnp.where(causal, jnp.exp(jnp.where(causal, diff, 0.0) * lg), 0.0)
    q_dec = jnp.exp((idx + 1.0)[None, :] * log_gamma[:, None])[..., None]
    k_dec = jnp.exp((c - 1.0 - idx)[None, :] * log_gamma[:, None])[..., None]
    chunk_dec = jnp.exp(c * log_gamma)
    lane_head = jnp.arange(RET_QK) // RET_DK
    head_mask = (lane_head[None, :] == jnp.arange(RET_H)[:, None]).astype(F32)[:, None, :]
    return (decay.reshape(RET_H * r, r), head_mask, q_dec * head_mask, k_dec * head_mask,
            jnp.broadcast_to(chunk_dec[lane_head][:, None], (RET_QK, RET_DV)))


def _dn_tables(r, c):
    i = jnp.arange(r)
    ri, ci = i[:, None], i[None, :]
    same = (ri // c) == (ci // c)
    lblk = ((ri >= ci) & same).astype(BF16)
    x = ri ^ ci
    lvl, m = [], 1
    while m < c:
        lvl.append((same & (ri > ci) & (x >= m) & (x < 2 * m)).astype(F32))
        m *= 2
    return lblk, lblk.T, jnp.stack(lvl)


def _token_mixers(x, p, s_ret, s_dn, s_conv, pos_table, wts, cfg):
    bsz, seq, _ = x.shape
    n = bsz * seq
    x2d = x.reshape(n, D_MODEL)
    cos_t, sin_t = pos_table
    (q, k, v, rg, dq, dk, dv, dz, ga, gb, gbc, gbr, conv_new) = _proj_call(
        x2d, wts["norm_mix"], wts["w_in"], wts["w_gates"], wts["wab"], wts["wabt"], cos_t, sin_t,
        wts["alog_c"], wts["dtb_c"], wts["alog_r"], wts["dtb_r"], wts["conv_w"], s_conv, cfg["tm_proj"])

    ret_tables = _ret_tables(cfg["ret_cu"], min(cfg["tt"], cfg["ret_cu"]))
    dn_tables = _dn_tables(cfg["dn_cu"], min(cfg["tt"], cfg["dn_cu"]))
    og, ret_new, od, dn_new = _seq_call(
        q, k, v, rg, s_ret.reshape(bsz, RET_QK, RET_DV), ret_tables,
        dq, dk, dv, gbc, gbr, dz, wts["dn_norm"], s_dn, dn_tables,
        cfg["bt"], cfg["tt"], cfg["ret_cu"], cfg["dn_cu"])
    ret_new = ret_new.reshape(bsz, RET_H, RET_DK, RET_DV)

    return (x2d, og, od, ga, gb, p.reshape(n, PLE_DIM)), ret_new, dn_new, conv_new


def _prep_weights(norm_mix, w_in, conv_w, dn_a_log, dn_dt_bias, dn_norm, w_ret_up, w_dn_up, w_out,
                  norm_ffn, w_router_group, b_router_group, w_router_expert, b_router_expert,
                  w_gate_up, w_down, norm_ple, w_ple_gate, w_ple_proj, norm_final):
    assert sum(IN_WIDTHS[:6]) == _C_FRONT and sum(IN_WIDTHS[:8]) == _C_GATES
    w_in = w_in.astype(BF16)
    w_gates = w_in[:, _C_GATES:]
    wab8 = w_in[:, _C_FRONT:_C_GATES]
    wab = jnp.pad(wab8, ((0, 0), (0, LANES - 2 * DN_H)))
    pad_lanes = lambda a: jnp.pad(a.astype(F32), (0, LANES - a.shape[0]))[None, :]
    pad_rows = lambda a: jnp.pad(a.astype(F32), (0, 8 - a.shape[0]))[:, None]
    w_router = jnp.pad(jnp.concatenate([w_router_group, w_router_expert], axis=-1),
                       ((0, 0), (0, LANES - N_GROUPS - N_EXPERTS))).astype(BF16)
    b_router = pad_lanes(jnp.concatenate([b_router_group, b_router_expert]))
    row = lambda a: a.astype(F32)[None, :]
    return dict(
        norm_mix=row(norm_mix), w_in=w_in, w_gates=w_gates, wab=wab, wabt=wab8.T,
        alog_c=pad_lanes(dn_a_log), dtb_c=pad_lanes(dn_dt_bias),
        alog_r=pad_rows(dn_a_log), dtb_r=pad_rows(dn_dt_bias),
        conv_w=conv_w.astype(F32), dn_norm=row(dn_norm),
        w_ret_up=w_ret_up.astype(BF16), w_dn_up=w_dn_up.astype(BF16), w_out=w_out.astype(BF16),
        norm_ffn=row(norm_ffn), w_router=w_router, b_router=b_router,
        w_gate_up=w_gate_up.astype(BF16),
        w_down=w_down.astype(BF16).reshape(N_EXPERTS * D_EXPERT, D_MODEL),
        norm_ple=row(norm_ple), w_ple_gate=w_ple_gate.astype(BF16), w_ple_proj=w_ple_proj.astype(BF16),
        norm_final=row(norm_final),
    )


def kernel(x_prompt, x_sample, p_prompt, p_sample, state_ret, state_dn, state_conv, norm_mix, w_in, conv_w, dn_a_log, dn_dt_bias, dn_norm, w_ret_up, w_dn_up, w_out, norm_ffn, w_router_group, b_router_group, w_router_expert, b_router_expert, w_gate_up, w_down, norm_ple, w_ple_gate, w_ple_proj, norm_final):
    depth = w_in.shape[0]
    assert depth == 1, "one layer: the final norm is fused into the layer's last kernel"
    bp, lp, _ = x_prompt.shape
    bs, ls, _ = x_sample.shape
    wts = _prep_weights(norm_mix[0], w_in[0], conv_w[0], dn_a_log[0], dn_dt_bias[0], dn_norm[0],
                        w_ret_up[0], w_dn_up[0], w_out[0], norm_ffn[0], w_router_group[0],
                        b_router_group[0], w_router_expert[0], b_router_expert[0], w_gate_up[0],
                        w_down[0], norm_ple[0], w_ple_gate[0], w_ple_proj[0], norm_final)

    tm = 512
    cfg_p = dict(tm=tm, tm_proj=256, bt=1, tt=512, ret_cu=256, dn_cu=128)
    cfg_s = dict(tm=tm, tm_proj=256, bt=16, tt=ls, ret_cu=16 * ls, dn_cu=16 * ls)

    pos_p = _rope_tables(jnp.arange(lp, dtype=jnp.int32))
    cos_s, sin_s = _rope_tables(PAST_LEN + jnp.arange(ls, dtype=jnp.int32))
    pos_s = (jnp.tile(cos_s, (bs, 1)), jnp.tile(sin_s, (bs, 1)))

    zero_ret = jnp.zeros((bp, RET_H, RET_DK, RET_DV), F32)
    zero_dn = jnp.zeros((bp, DN_H, DN_DK, DN_DV), F32)
    zero_conv = jnp.zeros((bp, CONV_W - 1, DN_CONV_CH), F32)

    tok_p, r_p, d_p, c_p = _token_mixers(x_prompt, p_prompt[0], zero_ret, zero_dn, zero_conv,
                                         pos_p, wts, cfg_p)
    tok_s, r_s, d_s, c_s = _token_mixers(x_sample, p_sample[0], state_ret[0], state_dn[0], state_conv[0],
                                         pos_s, wts, cfg_s)
    ffn_weights = [wts[k] for k in ("w_ret_up", "w_dn_up", "w_out", "norm_ffn", "w_router", "b_router",
                                    "w_gate_up", "w_down", "norm_ple", "w_ple_gate", "w_ple_proj",
                                    "norm_final")]
    y_p = _ffn_call(tok_p, ffn_weights, tm)
    y_s = _ffn_call(tok_s, ffn_weights, tm)
    return (y_p.reshape(x_prompt.shape), y_s.reshape(x_sample.shape),
            r_p[None], d_p[None], c_p[None], r_s[None], d_s[None], c_s[None])
```

```python
import functools

import jax
import jax.numpy as jnp
from jax import lax
from jax.experimental import pallas as pl
from jax.experimental.pallas import tpu as pltpu

F32 = jnp.float32
BF16 = jnp.bfloat16

D_MODEL = 1024
RET_H, RET_DK, RET_DV = 4, 64, 128
DN_H, DN_DK, DN_DV = 4, 128, 128
CONV_W = 4
ROPE_BASE = 10000.0
PAST_LEN = 16384
N_GROUPS, EXPERTS_PER_GROUP = 4, 4
N_EXPERTS = N_GROUPS * EXPERTS_PER_GROUP
D_EXPERT = 256
PLE_DIM = 256
EPS = 1e-6

RET_QK = RET_H * RET_DK
RET_VW = RET_H * RET_DV
DN_QK = DN_H * DN_DK
DN_VW = DN_H * DN_DV
DN_CONV_CH = 2 * DN_QK + DN_VW
IN_WIDTHS = (RET_QK, RET_QK, RET_VW, RET_VW, DN_CONV_CH, DN_VW, DN_H, DN_H, D_MODEL, D_MODEL)

LANES = 128
VMEM_LIMIT = 56 * 1024 * 1024

_C_Q, _C_K, _C_V, _C_RG = 0, 256, 512, 1024
_C_DQKV, _C_DZ, _C_FRONT = 1536, 3072, 3584
_C_GATES = _C_FRONT + 2 * DN_H


def _silu(x):
    return x * jax.nn.sigmoid(x)


def _softplus(x):
    return jnp.maximum(x, 0.0) + jnp.log1p(jnp.exp(-jnp.abs(x)))


def _dot(a, b):
    return jnp.dot(a, b, preferred_element_type=F32)


def _dot_nt(a, b):
    return lax.dot_general(a, b, (((1,), (1,)), ((), ())), preferred_element_type=F32)


def _dot_tn(a, b):
    return lax.dot_general(a, b, (((0,), (0,)), ((), ())), preferred_element_type=F32)


def _split3(x):
    hi = x.astype(BF16)
    r = x - hi.astype(F32)
    mid = r.astype(BF16)
    lo = (r - mid.astype(F32)).astype(BF16)
    return hi, mid, lo


def _rms_plain(x):
    return x * lax.rsqrt(jnp.mean(x * x, axis=-1, keepdims=True) + EPS)


def _proj_kernel(x_ref, gain_ref, w_ref, wg_ref, cos_ref, sin_ref,
                 alog_c_ref, dtb_c_ref, alog_r_ref, dtb_r_ref, cw_ref, c0_ref,
                 q_ref, k_ref, v_ref, rg_ref, dq_ref, dk_ref, dv_ref, dz_ref, ga_ref, gb_ref,
                 gbc_ref, gbr_ref, c_ref, ext_ref, *, bt, tt, nl):
    pad = 8
    tail = CONV_W - 1

    @pl.when(pl.program_id(0) % nl == 0)
    def _():
        ext_ref[:, pad - tail:pad, :] = c0_ref[...]

    x = x_ref[...]
    u = x * lax.rsqrt(jnp.mean(x * x, axis=-1, keepdims=True) + EPS) * gain_ref[...]
    ub = u.astype(BF16)

    def mm(lo, hi):
        return _dot(ub, w_ref[:, lo:hi])

    ext_ref[:, pad:pad + tt, :] = mm(_C_DQKV, _C_DZ).reshape(bt, tt, DN_CONV_CH)

    def conv_silu(c0, c1):
        acc = ext_ref[:, pad - tail:pad - tail + tt, c0:c1] * cw_ref[0:1, c0:c1]
        for j in range(1, CONV_W):
            acc = acc + ext_ref[:, pad - tail + j:pad - tail + j + tt, c0:c1] * cw_ref[j:j + 1, c0:c1]
        return _silu(acc).reshape(bt * tt, c1 - c0)

    def l2_heads(out_ref, c0, scale):
        units = []
        for h in range(DN_H):
            ch = conv_silu(c0 + h * DN_DK, c0 + (h + 1) * DN_DK)
            unit = ch * lax.rsqrt(jnp.sum(ch * ch, axis=-1, keepdims=True) + EPS)
            units.append(unit if scale is None else unit * scale)
        out_ref[...] = jnp.concatenate(units, axis=-1).astype(BF16)

    cos = cos_ref[...]
    sin = sin_ref[...]
    lane = lax.broadcasted_iota(jnp.int32, cos.shape, 1)
    first_half = (lane % RET_DK) < (RET_DK // 2)

    def rot(t):
        swapped = jnp.where(first_half, pltpu.roll(t, RET_QK - RET_DK // 2, 1),
                            pltpu.roll(t, RET_DK // 2, 1))
        return t * cos + swapped * sin

    gates = _dot(ub, wg_ref[...])
    ga_ref[...] = gates[:, :D_MODEL].astype(BF16)
    gb_ref[...] = gates[:, D_MODEL:2 * D_MODEL].astype(BF16)
    dab = gates[:, 2 * D_MODEL:]
    l2_heads(dq_ref, 0, DN_DK ** -0.5)
    l2_heads(dk_ref, DN_QK, None)
    v_ref[...] = mm(_C_V, _C_RG).astype(BF16)
    dv_ref[...] = conv_silu(2 * DN_QK, DN_CONV_CH).astype(BF16)
    new_tail = ext_ref[:, pad + tt - tail:pad + tt, :]
    ext_ref[:, pad - tail:pad, :] = new_tail
    c_ref[...] = new_tail
    qk = mm(_C_Q, _C_V)
    q_ref[...] = (rot(qk[:, :RET_QK]) * (RET_DK ** -0.5)).astype(BF16)
    k_ref[...] = rot(qk[:, RET_QK:]).astype(BF16)
    rg_ref[...] = _silu(mm(_C_RG, _C_DQKV)).astype(BF16)
    dz_ref[...] = _silu(mm(_C_DZ, _C_FRONT)).astype(BF16)

    lane_c = lax.broadcasted_iota(jnp.int32, dab.shape, 1)
    g_c = -jnp.exp(alog_c_ref[...]) * _softplus(dab + dtb_c_ref[...])
    gbc_ref[...] = jnp.where(lane_c < DN_H, g_c, jax.nn.sigmoid(dab))
    dabt = dab.T[:8]
    row_r = lax.broadcasted_iota(jnp.int32, dabt.shape, 0)
    g_r = -jnp.exp(alog_r_ref[...]) * _softplus(dabt + dtb_r_ref[...])
    gbr_ref[...] = jnp.where(row_r < DN_H, g_r, jax.nn.sigmoid(dabt))


def _proj_call(x2d, gain, w_in, w_gates, cos_t, sin_t, alog_c, dtb_c, alog_r, dtb_r, conv_w, c0, tm):
    n = x2d.shape[0]
    bsz = c0.shape[0]
    seq = n // bsz
    tt = min(tm, seq)
    bt = tm // tt
    nl = seq // tt
    table_blocks = cos_t.shape[0] // tm
    tok = lambda w: pl.BlockSpec((tm, w), lambda i: (i, 0))
    const = lambda a: pl.BlockSpec(a.shape, lambda i: (0,) * a.ndim, pipeline_mode=pl.Buffered(1))
    tab = pl.BlockSpec((tm, RET_QK), lambda i: (i % table_blocks, 0))
    cst = pl.BlockSpec((bt, CONV_W - 1, DN_CONV_CH), lambda i: (i // nl, 0, 0))
    bf = lambda w: jax.ShapeDtypeStruct((n, w), BF16)
    out_shapes = (bf(RET_QK), bf(RET_QK), bf(RET_VW), bf(RET_VW), bf(DN_QK), bf(DN_QK), bf(DN_VW),
                  bf(DN_VW), bf(D_MODEL), bf(D_MODEL),
                  jax.ShapeDtypeStruct((n, LANES), F32), jax.ShapeDtypeStruct((8, n), F32),
                  jax.ShapeDtypeStruct(c0.shape, F32))
    out_specs = (tok(RET_QK), tok(RET_QK), tok(RET_VW), tok(RET_VW), tok(DN_QK), tok(DN_QK), tok(DN_VW),
                 tok(DN_VW), tok(D_MODEL), tok(D_MODEL), tok(LANES),
                 pl.BlockSpec((8, tm), lambda i: (0, i)), cst)
    return pl.pallas_call(
        functools.partial(_proj_kernel, bt=bt, tt=tt, nl=nl), grid=(n // tm,), name="proj",
        in_specs=[tok(D_MODEL), const(gain),
                  pl.BlockSpec((D_MODEL, _C_FRONT), lambda i: (0, 0), pipeline_mode=pl.Buffered(1)),
                  const(w_gates), tab, tab,
                  const(alog_c), const(dtb_c), const(alog_r), const(dtb_r), const(conv_w), cst],
        out_specs=out_specs, out_shape=out_shapes,
        scratch_shapes=[pltpu.VMEM((bt, tt + 8, DN_CONV_CH), F32)],
        compiler_params=pltpu.CompilerParams(dimension_semantics=("arbitrary",),
                                             vmem_limit_bytes=VMEM_LIMIT),
    )(x2d, gain, w_in, w_gates, cos_t, sin_t, alog_c, dtb_c, alog_r, dtb_r, conv_w, c0)


def _ret_tile(q_ref, k_ref, v_ref, rg_ref, dec_ref, hm_ref, qd_ref, kd_ref, cd_ref,
              o_ref, s_ref, *, bt, tt, cu):
    r = bt * tt
    blk = min(tt, cu)
    n_units = r // cu
    n_seg = cu // blk

    heads = range(RET_H)
    hrow = lambda a, h, m: a[h * m:(h + 1) * m]
    for n in range(n_units):
        rows = slice(n * cu, (n + 1) * cu)
        qf = q_ref[rows, :].astype(F32)
        kf = k_ref[rows, :].astype(F32)
        vf = v_ref[rows, :].astype(F32)
        q_m = [qf * hm_ref[h] for h in heads]
        q_d = [qf * qd_ref[h] for h in heads]
        k_d = [kf * kd_ref[h] for h in heads]
        v_h = [vf[:, h * RET_DV:(h + 1) * RET_DV] for h in heads]
        scores = (_dot_nt(jnp.concatenate(q_m, axis=0).astype(BF16), k_ref[rows, :])
                  * dec_ref[...]).astype(BF16)
        intra = []
        for h in heads:
            intra.append(_dot(hrow(scores, h, cu), v_h[h].astype(BF16)))
        inters = []
        for j in range(n_seg):
            seg = slice(j * blk, (j + 1) * blk)
            b = (n * cu + j * blk) // tt
            s = s_ref[b]
            q_seg = jnp.concatenate([a[seg] for a in q_d], axis=0).astype(BF16)
            k_seg = jnp.concatenate([a[seg] for a in k_d], axis=0).astype(BF16)
            v_seg = jnp.concatenate([a[seg] for a in v_h], axis=0).astype(BF16)
            inters.append(_dot(q_seg, s.astype(BF16)))
            s_ref[b] = s * cd_ref[...] + _dot_tn(k_seg, v_seg)
        for h in heads:
            inter = jnp.concatenate([hrow(a, h, blk) for a in inters], axis=0)
            gate = rg_ref[rows, h * RET_DV:(h + 1) * RET_DV].astype(F32)
            o_ref[rows, h * RET_DV:(h + 1) * RET_DV] = (
                _rms_plain(intra[h] + inter) * gate).astype(BF16)


def _dn_tile(q_ref, k_ref, v_ref, gc_ref, gr_ref, dz_ref, dnorm_ref, lblk_ref, ublk_ref, lvl_ref,
             o_ref, s_ref, *, bt, tt, cu):
    r = bt * tt
    blk = min(tt, cu)
    n_units = r // cu
    n_seg = cu // blk
    n_lvl = lvl_ref.shape[0]

    ri = lax.broadcasted_iota(jnp.int32, (cu, cu), 0)
    ci = lax.broadcasted_iota(jnp.int32, (cu, cu), 1)
    same = (ri // blk) == (ci // blk)
    incl = same & (ri >= ci)
    strict = same & (ri > ci)

    gc = gc_ref[...]
    g_hi, g_mid, g_lo = _split3(gc)
    r_hi, r_mid, r_lo = _split3(gr_ref[...])
    lblk = lblk_ref[...]
    ublk = ublk_ref[...]

    units = [(n, h) for n in range(n_units) for h in range(DN_H)]
    cum_c, cum_r = [], []
    for n in range(n_units):
        rows = slice(n * cu, (n + 1) * cu)
        cum_c.append(_dot(lblk, g_hi[rows]) + _dot(lblk, g_mid[rows]) + _dot(lblk, g_lo[rows]))
        cum_r.append(_dot(r_hi[:, rows], ublk) + _dot(r_mid[:, rows], ublk) + _dot(r_lo[:, rows], ublk))

    qn, kn, kb, g_col, dec, a_mat, ab, d, rhs = {}, {}, {}, {}, {}, {}, {}, {}, {}
    for u in units:
        n, h = u
        rows = slice(n * cu, (n + 1) * cu)
        qb = q_ref[rows, h * DN_DK:(h + 1) * DN_DK]
        kb[u] = k_ref[rows, h * DN_DK:(h + 1) * DN_DK]
        kn[u] = kb[u].astype(F32)
        vc = v_ref[rows, h * DN_DV:(h + 1) * DN_DV].astype(F32)
        g_col[u] = cum_c[n][:, h:h + 1]
        g_row = cum_r[n][h:h + 1, :]
        beta = gc[rows, DN_H + h:DN_H + h + 1]
        dec[u] = jnp.where(incl, jnp.exp(jnp.where(incl, g_col[u] - g_row, 0.0)), 0.0)
        a_mat[u] = jnp.where(strict, _dot_nt(kb[u], kb[u]) * dec[u] * beta, 0.0)
        ab[u] = a_mat[u].astype(BF16)
        e_g = jnp.exp(g_col[u])
        rhs[u] = jnp.concatenate([beta * vc, beta * e_g * kn[u]], axis=-1)
        qn[u] = (qb, qb.astype(F32) * e_g)
        d[u] = -(lvl_ref[0] * a_mat[u])
    for lv in range(1, n_lvl):
        for u in units:
            db = d[u].astype(BF16)
            w = a_mat[u] + _dot(db, ab[u])
            z = w + _dot(w.astype(BF16), db)
            d[u] = d[u] - lvl_ref[lv] * z
    sol, p_mat = {}, {}
    for u in units:
        sol[u] = rhs[u] + _dot(d[u].astype(BF16), rhs[u].astype(BF16))
        p_mat[u] = (_dot_nt(qn[u][0], kb[u]) * dec[u]).astype(BF16)

    for n in range(n_units):
        for h in range(DN_H):
            u = (n, h)
            u_mat, w_mat = sol[u][:, :DN_DV], sol[u][:, DN_DV:]
            qe = qn[u][1]
            deltas, qss = [], []
            for j in range(n_seg):
                r0 = j * blk
                b = (n * cu + r0) // tt
                sb = s_ref[b, h].astype(BF16)
                lhs = jnp.concatenate([w_mat[r0:r0 + blk], qe[r0:r0 + blk]], axis=0).astype(BF16)
                ws_qs = _dot(lhs, sb)
                deltas.append(u_mat[r0:r0 + blk] - ws_qs[:blk])
                qss.append(ws_qs[blk:])
            delta = deltas[0] if n_seg == 1 else jnp.concatenate(deltas, axis=0)
            q_s = qss[0] if n_seg == 1 else jnp.concatenate(qss, axis=0)
            delta_b = delta.astype(BF16)
            o = q_s + _dot(p_mat[u], delta_b)
            for j in range(n_seg):
                r0 = j * blk
                b = (n * cu + r0) // tt
                g_last = g_col[u][r0 + blk - 1:r0 + blk, :]
                k_dec = (kn[u][r0:r0 + blk] * jnp.exp(g_last - g_col[u][r0:r0 + blk])).astype(BF16)
                s_ref[b, h] = s_ref[b, h] * jnp.exp(g_last) + _dot_tn(k_dec, delta_b[r0:r0 + blk])
            gate = dz_ref[n * cu:(n + 1) * cu, h * DN_DV:(h + 1) * DN_DV].astype(F32)
            o_ref[n * cu:(n + 1) * cu, h * DN_DV:(h + 1) * DN_DV] = (
                _rms_plain(o) * dnorm_ref[...] * gate).astype(BF16)


def _seq_kernel(rq_ref, rk_ref, rv_ref, rg_ref, rs0_ref, dec_ref, hm_ref, qd_ref, kd_ref, cd_ref,
                dq_ref, dk_ref, dv_ref, gc_ref, gr_ref, dz_ref, dnorm_ref, ds0_ref, lblk_ref, ublk_ref,
                lvl_ref, og_ref, rs_ref, od_ref, ds_ref, *, bt, tt, ret_cu, dn_cu):
    @pl.when(pl.program_id(1) == 0)
    def _():
        rs_ref[...] = rs0_ref[...]
        ds_ref[...] = ds0_ref[...]

    _dn_tile(dq_ref, dk_ref, dv_ref, gc_ref, gr_ref, dz_ref, dnorm_ref, lblk_ref, ublk_ref, lvl_ref,
             od_ref, ds_ref, bt=bt, tt=tt, cu=dn_cu)
    _ret_tile(rq_ref, rk_ref, rv_ref, rg_ref, dec_ref, hm_ref, qd_ref, kd_ref, cd_ref,
              og_ref, rs_ref, bt=bt, tt=tt, cu=ret_cu)


def _seq_call(rq, rk, rv, rg, rs0, ret_tables, dq, dk, dv, gbc, gbr, dz, dn_norm, ds0, dn_tables,
              bt, tt, ret_cu, dn_cu):
    bsz = ds0.shape[0]
    seq = dq.shape[0] // bsz
    nl = seq // tt
    r = bt * tt
    tok = lambda w: pl.BlockSpec((r, w), lambda b, l: (b * nl + l, 0))
    const = lambda a: pl.BlockSpec(a.shape, lambda b, l: (0,) * a.ndim)
    rst = pl.BlockSpec((bt, RET_QK, RET_DV), lambda b, l: (b, 0, 0))
    dst = pl.BlockSpec((bt, DN_H, DN_DK, DN_DV), lambda b, l: (b, 0, 0, 0))
    return pl.pallas_call(
        functools.partial(_seq_kernel, bt=bt, tt=tt, ret_cu=ret_cu, dn_cu=dn_cu),
        grid=(bsz // bt, nl), name="seq",
        in_specs=([tok(RET_QK), tok(RET_QK), tok(RET_VW), tok(RET_VW), rst]
                  + [const(t) for t in ret_tables]
                  + [tok(DN_QK), tok(DN_QK), tok(DN_VW), tok(LANES),
                     pl.BlockSpec((8, r), lambda b, l: (0, b * nl + l)),
                     tok(DN_VW), const(dn_norm), dst] + [const(t) for t in dn_tables]),
        out_specs=(tok(RET_VW), rst, tok(DN_VW), dst),
        out_shape=(jax.ShapeDtypeStruct((bsz * seq, RET_VW), BF16), jax.ShapeDtypeStruct(rs0.shape, F32),
                   jax.ShapeDtypeStruct((bsz * seq, DN_VW), BF16), jax.ShapeDtypeStruct(ds0.shape, F32)),
        compiler_params=pltpu.CompilerParams(dimension_semantics=("parallel", "arbitrary"),
                                             vmem_limit_bytes=VMEM_LIMIT),
    )(rq, rk, rv, rg, rs0, *ret_tables, dq, dk, dv, gbc, gbr, dz, dn_norm, ds0, *dn_tables)


_R_EXP = N_GROUPS


def _ffn_kernel(x_ref, og_ref, od_ref, ga_ref, gb_ref, p_ref,
                wru_ref, wdu_ref, wo_ref, nffn_ref, wr_ref, br_ref, wgu_ref, wdn_ref, nple_ref, wpg_ref,
                wpp_ref, nfin_ref, y_ref, hid_ref):
    branch_a = _dot(og_ref[...], wru_ref[...])
    branch_b = _dot(od_ref[...], wdu_ref[...])
    merged = (jax.nn.sigmoid(ga_ref[...].astype(F32)) * branch_a
              + jax.nn.sigmoid(gb_ref[...].astype(F32)) * branch_b)
    h = x_ref[...] + _dot(merged.astype(BF16), wo_ref[...])

    u = _rms_plain(h) * nffn_ref[...]
    ub = u.astype(BF16)

    logits = _dot(ub, wr_ref[...]) + br_ref[...]
    lane = lax.broadcasted_iota(jnp.int32, logits.shape, 1)
    neg = jnp.float32(-jnp.inf)
    big = jnp.int32(LANES)
    gl = jnp.where(lane < N_GROUPS, logits, neg)
    g_max = jnp.max(gl, axis=-1, keepdims=True)
    p_grp = 1.0 / jnp.sum(jnp.exp(gl - g_max), axis=-1, keepdims=True)
    grp = jnp.min(jnp.where(gl == g_max, lane, big), axis=-1, keepdims=True)
    e_idx = lane - _R_EXP
    in_grp = (e_idx >= 0) & (e_idx < N_EXPERTS) & ((e_idx // EXPERTS_PER_GROUP) == grp)
    el = jnp.where(in_grp, logits, neg)
    v1 = jnp.max(el, axis=-1, keepdims=True)
    i1 = jnp.min(jnp.where(el == v1, lane, big), axis=-1, keepdims=True)
    el2 = jnp.where(lane == i1, neg, el)
    v2 = jnp.max(el2, axis=-1, keepdims=True)
    i2 = jnp.min(jnp.where(el2 == v2, lane, big), axis=-1, keepdims=True)
    e2 = jnp.exp(v2 - v1)
    w1 = 1.0 / (1.0 + e2)
    w2 = e2 / (1.0 + e2)
    combine = jnp.where(lane == i1, w1 * p_grp, 0.0) + jnp.where(lane == i2, w2 * p_grp, 0.0)

    for e in range(N_EXPERTS):
        gu = _dot(ub, wgu_ref[e])
        hid = _silu(gu[:, :D_EXPERT]) * gu[:, D_EXPERT:]
        hid_ref[:, e * D_EXPERT:(e + 1) * D_EXPERT] = (
            combine[:, _R_EXP + e:_R_EXP + e + 1] * hid).astype(BF16)
    h = h + _dot(hid_ref[...], wdn_ref[...])

    u3 = (_rms_plain(h) * nple_ref[...]).astype(BF16)
    gate = jax.nn.sigmoid(_dot(u3, wpg_ref[...]))
    h = h + gate * _dot(p_ref[...].astype(BF16), wpp_ref[...])
    y_ref[...] = _rms_plain(h) * nfin_ref[...]


def _ffn_call(tok_in, weights, tm):
    n = tok_in[0].shape[0]
    tok = lambda a: pl.BlockSpec((tm, a.shape[1]), lambda i: (i, 0))
    const = lambda a: pl.BlockSpec(a.shape, lambda i: (0,) * a.ndim, pipeline_mode=pl.Buffered(1))
    return pl.pallas_call(
        _ffn_kernel, grid=(n // tm,), name="ffn",
        in_specs=[tok(a) for a in tok_in] + [const(w) for w in weights],
        out_specs=tok(tok_in[0]), out_shape=jax.ShapeDtypeStruct((n, D_MODEL), F32),
        scratch_shapes=[pltpu.VMEM((tm, N_EXPERTS * D_EXPERT), BF16)],
        compiler_params=pltpu.CompilerParams(dimension_semantics=("parallel",),
                                             vmem_limit_bytes=VMEM_LIMIT),
    )(*tok_in, *weights)


def _rope_tables(pos):
    half = RET_DK // 2
    inv = 1.0 / (ROPE_BASE ** (jnp.arange(half, dtype=F32) / half))
    ang = pos.astype(F32)[:, None] * inv[None, :]
    cos, sin = jnp.cos(ang), jnp.sin(ang)
    cos_t = jnp.tile(jnp.concatenate([cos, cos], axis=-1), (1, RET_H))
    sin_t = jnp.tile(jnp.concatenate([-sin, sin], axis=-1), (1, RET_H))
    return cos_t, sin_t


def _ret_tables(r, c):
    log_gamma = jnp.log(1.0 - 2.0 ** (-5.0 - jnp.arange(RET_H, dtype=F32)))
    row = jnp.arange(r)
    idx = (row % c).astype(F32)
    diff = idx[:, None] - idx[None, :]
    causal = (diff >= 0) & ((row[:, None] // c) == (row[None, :] // c))
    lg = log_gamma[:, None, None]
    decay = jnp.where(causal, jnp.exp(jnp.where(causal, diff, 0.0) * lg), 0.0)
    q_dec = jnp.exp((idx + 1.0)[None, :] * log_gamma[:, None])[..., None]
    k_dec = jnp.exp((c - 1.0 - idx)[None, :] * log_gamma[:, None])[..., None]
    chunk_dec = jnp.exp(c * log_gamma)
    lane_head = jnp.arange(RET_QK) // RET_DK
    head_mask = (lane_head[None, :] == jnp.arange(RET_H)[:, None]).astype(F32)[:, None, :]
    return (decay.reshape(RET_H * r, r), head_mask, q_dec * head_mask, k_dec * head_mask,
            jnp.broadcast_to(chunk_dec[lane_head][:, None], (RET_QK, RET_DV)))


def _dn_tables(r, c):
    i = jnp.arange(r)
    ri, ci = i[:, None], i[None, :]
    same = (ri // c) == (ci // c)
    lblk = ((ri >= ci) & same).astype(BF16)
    x = ri ^ ci
    lvl, m = [], 1
    while m < c:
        lvl.append((same & (ri > ci) & (x >= m) & (x < 2 * m)).astype(F32))
        m *= 2
    return lblk, lblk.T, jnp.stack(lvl)


def _token_mixers(x, p, s_ret, s_dn, s_conv, pos_table, wts, cfg):
    bsz, seq, _ = x.shape
    n = bsz * seq
    x2d = x.reshape(n, D_MODEL)
    cos_t, sin_t = pos_table
    (q, k, v, rg, dq, dk, dv, dz, ga, gb, gbc, gbr, conv_new) = _proj_call(
        x2d, wts["norm_mix"], wts["w_in"], wts["w_gates"], cos_t, sin_t,
        wts["alog_c"], wts["dtb_c"], wts["alog_r"], wts["dtb_r"], wts["conv_w"], s_conv, cfg["tm_proj"])

    ret_tables = _ret_tables(cfg["ret_cu"], min(cfg["tt"], cfg["ret_cu"]))
    dn_tables = _dn_tables(cfg["dn_cu"], min(cfg["tt"], cfg["dn_cu"]))
    og, ret_new, od, dn_new = _seq_call(
        q, k, v, rg, s_ret.reshape(bsz, RET_QK, RET_DV), ret_tables,
        dq, dk, dv, gbc, gbr, dz, wts["dn_norm"], s_dn, dn_tables,
        cfg["bt"], cfg["tt"], cfg["ret_cu"], cfg["dn_cu"])
    ret_new = ret_new.reshape(bsz, RET_H, RET_DK, RET_DV)

    return (x2d, og, od, ga, gb, p.reshape(n, PLE_DIM)), ret_new, dn_new, conv_new


def _prep_weights(norm_mix, w_in, conv_w, dn_a_log, dn_dt_bias, dn_norm, w_ret_up, w_dn_up, w_out,
                  norm_ffn, w_router_group, b_router_group, w_router_expert, b_router_expert,
                  w_gate_up, w_down, norm_ple, w_ple_gate, w_ple_proj, norm_final):
    assert sum(IN_WIDTHS[:6]) == _C_FRONT and sum(IN_WIDTHS[:8]) == _C_GATES
    w_in = w_in.astype(BF16)
    w_gates = jnp.pad(jnp.concatenate([w_in[:, _C_GATES:], w_in[:, _C_FRONT:_C_GATES]], axis=-1),
                      ((0, 0), (0, LANES - 2 * DN_H)))
    pad_lanes = lambda a: jnp.pad(a.astype(F32), (0, LANES - a.shape[0]))[None, :]
    pad_rows = lambda a: jnp.pad(a.astype(F32), (0, 8 - a.shape[0]))[:, None]
    w_router = jnp.pad(jnp.concatenate([w_router_group, w_router_expert], axis=-1),
                       ((0, 0), (0, LANES - N_GROUPS - N_EXPERTS))).astype(BF16)
    b_router = pad_lanes(jnp.concatenate([b_router_group, b_router_expert]))
    row = lambda a: a.astype(F32)[None, :]
    return dict(
        norm_mix=row(norm_mix), w_in=w_in, w_gates=w_gates,
        alog_c=pad_lanes(dn_a_log), dtb_c=pad_lanes(dn_dt_bias),
        alog_r=pad_rows(dn_a_log), dtb_r=pad_rows(dn_dt_bias),
        conv_w=conv_w.astype(F32), dn_norm=row(dn_norm),
        w_ret_up=w_ret_up.astype(BF16), w_dn_up=w_dn_up.astype(BF16), w_out=w_out.astype(BF16),
        norm_ffn=row(norm_ffn), w_router=w_router, b_router=b_router,
        w_gate_up=w_gate_up.astype(BF16),
        w_down=w_down.astype(BF16).reshape(N_EXPERTS * D_EXPERT, D_MODEL),
        norm_ple=row(norm_ple), w_ple_gate=w_ple_gate.astype(BF16), w_ple_proj=w_ple_proj.astype(BF16),
        norm_final=row(norm_final),
    )


def kernel(x_prompt, x_sample, p_prompt, p_sample, state_ret, state_dn, state_conv, norm_mix, w_in, conv_w, dn_a_log, dn_dt_bias, dn_norm, w_ret_up, w_dn_up, w_out, norm_ffn, w_router_group, b_router_group, w_router_expert, b_router_expert, w_gate_up, w_down, norm_ple, w_ple_gate, w_ple_proj, norm_final):
    depth = w_in.shape[0]
    assert depth == 1, "one layer: the final norm is fused into the layer's last kernel"
    bp, lp, _ = x_prompt.shape
    bs, ls, _ = x_sample.shape
    wts = _prep_weights(norm_mix[0], w_in[0], conv_w[0], dn_a_log[0], dn_dt_bias[0], dn_norm[0],
                        w_ret_up[0], w_dn_up[0], w_out[0], norm_ffn[0], w_router_group[0],
                        b_router_group[0], w_router_expert[0], b_router_expert[0], w_gate_up[0],
                        w_down[0], norm_ple[0], w_ple_gate[0], w_ple_proj[0], norm_final)

    tm = 512
    cfg_p = dict(tm=tm, tm_proj=256, bt=1, tt=512, ret_cu=256, dn_cu=128)
    cfg_s = dict(tm=tm, tm_proj=256, bt=16, tt=ls, ret_cu=16 * ls, dn_cu=16 * ls)

    pos_p = _rope_tables(jnp.arange(lp, dtype=jnp.int32))
    cos_s, sin_s = _rope_tables(PAST_LEN + jnp.arange(ls, dtype=jnp.int32))
    pos_s = (jnp.tile(cos_s, (bs, 1)), jnp.tile(sin_s, (bs, 1)))

    zero_ret = jnp.zeros((bp, RET_H, RET_DK, RET_DV), F32)
    zero_dn = jnp.zeros((bp, DN_H, DN_DK, DN_DV), F32)
    zero_conv = jnp.zeros((bp, CONV_W - 1, DN_CONV_CH), F32)

    tok_p, r_p, d_p, c_p = _token_mixers(x_prompt, p_prompt[0], zero_ret, zero_dn, zero_conv,
                                         pos_p, wts, cfg_p)
    tok_s, r_s, d_s, c_s = _token_mixers(x_sample, p_sample[0], state_ret[0], state_dn[0], state_conv[0],
                                         pos_s, wts, cfg_s)
    ffn_weights = [wts[k] for k in ("w_ret_up", "w_dn_up", "w_out", "norm_ffn", "w_router", "b_router",
                                    "w_gate_up", "w_down", "norm_ple", "w_ple_gate", "w_ple_proj",
                                    "norm_final")]
    y_p = _ffn_call(tok_p, ffn_weights, tm)
    y_s = _ffn_call(tok_s, ffn_weights, tm)
    return (y_p.reshape(x_prompt.shape), y_s.reshape(x_sample.shape),
            r_p[None], d_p[None], c_p[None], r_s[None], d_s[None], c_s[None])
```

```python
import functools

import jax
import jax.numpy as jnp
import numpy as np
from jax import lax
from jax.experimental import pallas as pl
from jax.experimental.pallas import tpu as pltpu

F32 = jnp.float32
BF16 = jnp.bfloat16

D_MODEL = 1024
RET_H, RET_DK, RET_DV = 4, 64, 128
DN_H, DN_DK, DN_DV = 4, 128, 128
CONV_W = 4
ROPE_BASE = 10000.0
PAST_LEN = 16384
N_GROUPS, EXPERTS_PER_GROUP = 4, 4
N_EXPERTS = N_GROUPS * EXPERTS_PER_GROUP
D_EXPERT = 256
PLE_DIM = 256
EPS = 1e-6

RET_QK = RET_H * RET_DK
RET_VW = RET_H * RET_DV
DN_QK = DN_H * DN_DK
DN_VW = DN_H * DN_DV
DN_CONV_CH = 2 * DN_QK + DN_VW
IN_WIDTHS = (RET_QK, RET_QK, RET_VW, RET_VW, DN_CONV_CH, DN_VW, DN_H, DN_H, D_MODEL, D_MODEL)

LANES = 128
SUBLANES = 8
VMEM_LIMIT = 56 * 1024 * 1024
GB_ROWS = 2 * DN_H
assert GB_ROWS == SUBLANES

_C_Q, _C_K, _C_V, _C_RG = 0, 256, 512, 1024
_C_DQKV, _C_DZ, _C_FRONT = 1536, 3072, 3584
_C_GATES = _C_FRONT + GB_ROWS


def _silu(x):
    return x * jax.nn.sigmoid(x)


def _softplus(x):
    return jnp.maximum(x, 0.0) + jnp.log1p(jnp.exp(-jnp.abs(x)))


def _dot(a, b):
    return jnp.dot(a, b, preferred_element_type=F32)


def _dot_nt(a, b):
    return lax.dot_general(a, b, (((1,), (1,)), ((), ())), preferred_element_type=F32)


def _dot_tn(a, b):
    return lax.dot_general(a, b, (((0,), (0,)), ((), ())), preferred_element_type=F32)


def _split3(x):
    hi = x.astype(BF16)
    r = x - hi.astype(F32)
    mid = r.astype(BF16)
    lo = (r - mid.astype(F32)).astype(BF16)
    return hi, mid, lo


def _rms_plain(x):
    return x * lax.rsqrt(jnp.mean(x * x, axis=-1, keepdims=True) + EPS)


def _proj_kernel(x_ref, gain_ref, w_ref, wg_ref, cos_ref, sin_ref,
                 alog_c_ref, dtb_c_ref, alog_r_ref, dtb_r_ref, cw_ref, c0_ref,
                 q_ref, k_ref, v_ref, rg_ref, dq_ref, dk_ref, dv_ref, dz_ref, ga_ref, gb_ref,
                 gbc_ref, gbr_ref, c_ref, ext_ref, *, bt, tt, nl):
    pad = SUBLANES
    tail = CONV_W - 1

    @pl.when(pl.program_id(0) % nl == 0)
    def _():
        ext_ref[:, pad - tail:pad, :] = c0_ref[...]

    x = x_ref[...]
    u = x * lax.rsqrt(jnp.mean(x * x, axis=-1, keepdims=True) + EPS) * gain_ref[...]
    ub = u.astype(BF16)

    def mm(lo, hi):
        return _dot(ub, w_ref[:, lo:hi])

    ext_ref[:, pad:pad + tt, :] = mm(_C_DQKV, _C_DZ).reshape(bt, tt, DN_CONV_CH)

    def conv_silu(c0, c1):
        acc = ext_ref[:, pad - tail:pad - tail + tt, c0:c1] * cw_ref[0:1, c0:c1]
        for j in range(1, CONV_W):
            acc = acc + ext_ref[:, pad - tail + j:pad - tail + j + tt, c0:c1] * cw_ref[j:j + 1, c0:c1]
        return _silu(acc).reshape(bt * tt, c1 - c0)

    def l2_heads(out_ref, c0, scale):
        units = []
        for h in range(DN_H):
            ch = conv_silu(c0 + h * DN_DK, c0 + (h + 1) * DN_DK)
            unit = ch * lax.rsqrt(jnp.sum(ch * ch, axis=-1, keepdims=True) + EPS)
            units.append(unit if scale is None else unit * scale)
        out_ref[...] = jnp.concatenate(units, axis=-1).astype(BF16)

    cos = cos_ref[...]
    sin = sin_ref[...]
    lane = lax.broadcasted_iota(jnp.int32, cos.shape, 1)
    first_half = (lane % RET_DK) < (RET_DK // 2)

    def rot(t):
        swapped = jnp.where(first_half, pltpu.roll(t, RET_QK - RET_DK // 2, 1),
                            pltpu.roll(t, RET_DK // 2, 1))
        return t * cos + swapped * sin

    gates = _dot(ub, wg_ref[...])
    ga_ref[...] = gates[:, :D_MODEL].astype(BF16)
    gb_ref[...] = gates[:, D_MODEL:2 * D_MODEL].astype(BF16)
    dab = gates[:, 2 * D_MODEL:]
    l2_heads(dq_ref, 0, DN_DK ** -0.5)
    l2_heads(dk_ref, DN_QK, None)
    v_ref[...] = mm(_C_V, _C_RG).astype(BF16)
    dv_ref[...] = conv_silu(2 * DN_QK, DN_CONV_CH).astype(BF16)
    new_tail = ext_ref[:, pad + tt - tail:pad + tt, :]
    ext_ref[:, pad - tail:pad, :] = new_tail
    c_ref[...] = new_tail
    qk = mm(_C_Q, _C_V)
    q_ref[...] = (rot(qk[:, :RET_QK]) * (RET_DK ** -0.5)).astype(BF16)
    k_ref[...] = rot(qk[:, RET_QK:]).astype(BF16)
    rg_ref[...] = _silu(mm(_C_RG, _C_DQKV)).astype(BF16)
    dz_ref[...] = _silu(mm(_C_DZ, _C_FRONT)).astype(BF16)

    lane_c = lax.broadcasted_iota(jnp.int32, dab.shape, 1)
    g_c = -jnp.exp(alog_c_ref[...]) * _softplus(dab + dtb_c_ref[...])
    gbc_ref[...] = jnp.where(lane_c < DN_H, g_c, jax.nn.sigmoid(dab))
    dabt = dab.T[:GB_ROWS]
    row_r = lax.broadcasted_iota(jnp.int32, dabt.shape, 0)
    g_r = -jnp.exp(alog_r_ref[...]) * _softplus(dabt + dtb_r_ref[...])
    gbr_ref[...] = jnp.where(row_r < DN_H, g_r, jax.nn.sigmoid(dabt))


def _proj_call(x2d, gain, w_in, w_gates, cos_t, sin_t, alog_c, dtb_c, alog_r, dtb_r, conv_w, c0, tm):
    n = x2d.shape[0]
    bsz = c0.shape[0]
    seq = n // bsz
    tt = min(tm, seq)
    bt = tm // tt
    nl = seq // tt
    table_blocks = cos_t.shape[0] // tm
    tok = lambda w: pl.BlockSpec((tm, w), lambda i: (i, 0))
    const = lambda a: pl.BlockSpec(a.shape, lambda i: (0,) * a.ndim, pipeline_mode=pl.Buffered(1))
    tab = pl.BlockSpec((tm, RET_QK), lambda i: (i % table_blocks, 0))
    cst = pl.BlockSpec((bt, CONV_W - 1, DN_CONV_CH), lambda i: (i // nl, 0, 0))
    bf = lambda w: jax.ShapeDtypeStruct((n, w), BF16)
    out_shapes = (bf(RET_QK), bf(RET_QK), bf(RET_VW), bf(RET_VW), bf(DN_QK), bf(DN_QK), bf(DN_VW),
                  bf(DN_VW), bf(D_MODEL), bf(D_MODEL),
                  jax.ShapeDtypeStruct((n, LANES), F32), jax.ShapeDtypeStruct((GB_ROWS, n), F32),
                  jax.ShapeDtypeStruct(c0.shape, F32))
    out_specs = (tok(RET_QK), tok(RET_QK), tok(RET_VW), tok(RET_VW), tok(DN_QK), tok(DN_QK), tok(DN_VW),
                 tok(DN_VW), tok(D_MODEL), tok(D_MODEL), tok(LANES),
                 pl.BlockSpec((GB_ROWS, tm), lambda i: (0, i)), cst)
    return pl.pallas_call(
        functools.partial(_proj_kernel, bt=bt, tt=tt, nl=nl), grid=(n // tm,), name="proj",
        in_specs=[tok(D_MODEL), const(gain),
                  pl.BlockSpec((D_MODEL, _C_FRONT), lambda i: (0, 0), pipeline_mode=pl.Buffered(1)),
                  const(w_gates), tab, tab,
                  const(alog_c), const(dtb_c), const(alog_r), const(dtb_r), const(conv_w), cst],
        out_specs=out_specs, out_shape=out_shapes,
        scratch_shapes=[pltpu.VMEM((bt, tt + SUBLANES, DN_CONV_CH), F32)],
        compiler_params=pltpu.CompilerParams(dimension_semantics=("arbitrary",),
                                             vmem_limit_bytes=VMEM_LIMIT),
    )(x2d, gain, w_in, w_gates, cos_t, sin_t, alog_c, dtb_c, alog_r, dtb_r, conv_w, c0)


def _ret_tile(q_ref, k_ref, v_ref, rg_ref, dec_ref, hm_ref, qd_ref, kd_ref, cd_ref,
              o_ref, s_ref, *, bt, tt, cu):
    r = bt * tt
    blk = min(tt, cu)
    n_units = r // cu
    n_seg = cu // blk

    heads = range(RET_H)
    hrow = lambda a, h, m: a[h * m:(h + 1) * m]
    for n in range(n_units):
        rows = slice(n * cu, (n + 1) * cu)
        qf = q_ref[rows, :].astype(F32)
        kf = k_ref[rows, :].astype(F32)
        vf = v_ref[rows, :].astype(F32)
        q_m = [qf * hm_ref[h] for h in heads]
        q_d = [qf * qd_ref[h] for h in heads]
        k_d = [kf * kd_ref[h] for h in heads]
        v_h = [vf[:, h * RET_DV:(h + 1) * RET_DV] for h in heads]
        scores = (_dot_nt(jnp.concatenate(q_m, axis=0).astype(BF16), k_ref[rows, :])
                  * dec_ref[...]).astype(BF16)
        intra = []
        for h in heads:
            intra.append(_dot(hrow(scores, h, cu), v_h[h].astype(BF16)))
        inters = []
        for j in range(n_seg):
            seg = slice(j * blk, (j + 1) * blk)
            b = (n * cu + j * blk) // tt
            s = s_ref[b]
            q_seg = jnp.concatenate([a[seg] for a in q_d], axis=0).astype(BF16)
            k_seg = jnp.concatenate([a[seg] for a in k_d], axis=0).astype(BF16)
            v_seg = jnp.concatenate([a[seg] for a in v_h], axis=0).astype(BF16)
            inters.append(_dot(q_seg, s.astype(BF16)))
            s_ref[b] = s * cd_ref[...] + _dot_tn(k_seg, v_seg)
        for h in heads:
            inter = jnp.concatenate([hrow(a, h, blk) for a in inters], axis=0)
            gate = rg_ref[rows, h * RET_DV:(h + 1) * RET_DV].astype(F32)
            o_ref[rows, h * RET_DV:(h + 1) * RET_DV] = (
                _rms_plain(intra[h] + inter) * gate).astype(BF16)


def _dn_tile(q_ref, k_ref, v_ref, gc_ref, gr_ref, dz_ref, dnorm_ref, lblk_ref, ublk_ref, lvl_ref,
             o_ref, s_ref, *, bt, tt, cu):
    r = bt * tt
    blk = min(tt, cu)
    n_units = r // cu
    n_seg = cu // blk
    n_lvl = lvl_ref.shape[0]

    ri = lax.broadcasted_iota(jnp.int32, (cu, cu), 0)
    ci = lax.broadcasted_iota(jnp.int32, (cu, cu), 1)
    same = (ri // blk) == (ci // blk)
    incl = same & (ri >= ci)
    strict = same & (ri > ci)

    gc = gc_ref[...]
    g_hi, g_mid, g_lo = _split3(gc)
    r_hi, r_mid, r_lo = _split3(gr_ref[...])
    lblk = lblk_ref[...]
    ublk = ublk_ref[...]

    units = [(n, h) for n in range(n_units) for h in range(DN_H)]
    cum_c, cum_r = [], []
    for n in range(n_units):
        rows = slice(n * cu, (n + 1) * cu)
        cum_c.append(_dot(lblk, g_hi[rows]) + _dot(lblk, g_mid[rows]) + _dot(lblk, g_lo[rows]))
        cum_r.append(_dot(r_hi[:, rows], ublk) + _dot(r_mid[:, rows], ublk) + _dot(r_lo[:, rows], ublk))

    qn, kn, kb, g_col, dec, a_mat, ab, d, rhs = {}, {}, {}, {}, {}, {}, {}, {}, {}
    for u in units:
        n, h = u
        rows = slice(n * cu, (n + 1) * cu)
        qb = q_ref[rows, h * DN_DK:(h + 1) * DN_DK]
        kb[u] = k_ref[rows, h * DN_DK:(h + 1) * DN_DK]
        kn[u] = kb[u].astype(F32)
        vc = v_ref[rows, h * DN_DV:(h + 1) * DN_DV].astype(F32)
        g_col[u] = cum_c[n][:, h:h + 1]
        g_row = cum_r[n][h:h + 1, :]
        beta = gc[rows, DN_H + h:DN_H + h + 1]
        dec[u] = jnp.where(incl, jnp.exp(jnp.where(incl, g_col[u] - g_row, 0.0)), 0.0)
        a_mat[u] = jnp.where(strict, _dot_nt(kb[u], kb[u]) * dec[u] * beta, 0.0)
        ab[u] = a_mat[u].astype(BF16)
        e_g = jnp.exp(g_col[u])
        rhs[u] = jnp.concatenate([beta * vc, beta * e_g * kn[u]], axis=-1)
        qn[u] = (qb, qb.astype(F32) * e_g)
        d[u] = -(lvl_ref[0] * a_mat[u])
    for lv in range(1, n_lvl):
        for u in units:
            db = d[u].astype(BF16)
            w = a_mat[u] + _dot(db, ab[u])
            z = w + _dot(w.astype(BF16), db)
            d[u] = d[u] - lvl_ref[lv] * z
    sol, p_mat = {}, {}
    for u in units:
        sol[u] = rhs[u] + _dot(d[u].astype(BF16), rhs[u].astype(BF16))
        p_mat[u] = (_dot_nt(qn[u][0], kb[u]) * dec[u]).astype(BF16)

    for n in range(n_units):
        for h in range(DN_H):
            u = (n, h)
            u_mat, w_mat = sol[u][:, :DN_DV], sol[u][:, DN_DV:]
            qe = qn[u][1]
            deltas, qss = [], []
            for j in range(n_seg):
                r0 = j * blk
                b = (n * cu + r0) // tt
                sb = s_ref[b, h].astype(BF16)
                lhs = jnp.concatenate([w_mat[r0:r0 + blk], qe[r0:r0 + blk]], axis=0).astype(BF16)
                ws_qs = _dot(lhs, sb)
                deltas.append(u_mat[r0:r0 + blk] - ws_qs[:blk])
                qss.append(ws_qs[blk:])
            delta = deltas[0] if n_seg == 1 else jnp.concatenate(deltas, axis=0)
            q_s = qss[0] if n_seg == 1 else jnp.concatenate(qss, axis=0)
            delta_b = delta.astype(BF16)
            o = q_s + _dot(p_mat[u], delta_b)
            for j in range(n_seg):
                r0 = j * blk
                b = (n * cu + r0) // tt
                g_last = g_col[u][r0 + blk - 1:r0 + blk, :]
                k_dec = (kn[u][r0:r0 + blk] * jnp.exp(g_last - g_col[u][r0:r0 + blk])).astype(BF16)
                s_ref[b, h] = s_ref[b, h] * jnp.exp(g_last) + _dot_tn(k_dec, delta_b[r0:r0 + blk])
            gate = dz_ref[n * cu:(n + 1) * cu, h * DN_DV:(h + 1) * DN_DV].astype(F32)
            o_ref[n * cu:(n + 1) * cu, h * DN_DV:(h + 1) * DN_DV] = (
                _rms_plain(o) * dnorm_ref[...] * gate).astype(BF16)


def _seq_kernel(rq_ref, rk_ref, rv_ref, rg_ref, rs0_ref, dec_ref, hm_ref, qd_ref, kd_ref, cd_ref,
                dq_ref, dk_ref, dv_ref, gc_ref, gr_ref, dz_ref, dnorm_ref, ds0_ref, lblk_ref, ublk_ref,
                lvl_ref, og_ref, rs_ref, od_ref, ds_ref, *, bt, tt, ret_cu, dn_cu):
    @pl.when(pl.program_id(1) == 0)
    def _():
        rs_ref[...] = rs0_ref[...]
        ds_ref[...] = ds0_ref[...]

    _dn_tile(dq_ref, dk_ref, dv_ref, gc_ref, gr_ref, dz_ref, dnorm_ref, lblk_ref, ublk_ref, lvl_ref,
             od_ref, ds_ref, bt=bt, tt=tt, cu=dn_cu)
    _ret_tile(rq_ref, rk_ref, rv_ref, rg_ref, dec_ref, hm_ref, qd_ref, kd_ref, cd_ref,
              og_ref, rs_ref, bt=bt, tt=tt, cu=ret_cu)


def _seq_call(rq, rk, rv, rg, rs0, ret_tables, dq, dk, dv, gbc, gbr, dz, dn_norm, ds0, dn_tables,
              bt, tt, ret_cu, dn_cu):
    bsz = ds0.shape[0]
    seq = dq.shape[0] // bsz
    nl = seq // tt
    r = bt * tt
    tok = lambda w: pl.BlockSpec((r, w), lambda b, l: (b * nl + l, 0))
    const = lambda a: pl.BlockSpec(a.shape, lambda b, l: (0,) * a.ndim)
    rst = pl.BlockSpec((bt, RET_QK, RET_DV), lambda b, l: (b, 0, 0))
    dst = pl.BlockSpec((bt, DN_H, DN_DK, DN_DV), lambda b, l: (b, 0, 0, 0))
    return pl.pallas_call(
        functools.partial(_seq_kernel, bt=bt, tt=tt, ret_cu=ret_cu, dn_cu=dn_cu),
        grid=(bsz // bt, nl), name="seq",
        in_specs=([tok(RET_QK), tok(RET_QK), tok(RET_VW), tok(RET_VW), rst]
                  + [const(t) for t in ret_tables]
                  + [tok(DN_QK), tok(DN_QK), tok(DN_VW), tok(LANES),
                     pl.BlockSpec((GB_ROWS, r), lambda b, l: (0, b * nl + l)),
                     tok(DN_VW), const(dn_norm), dst] + [const(t) for t in dn_tables]),
        out_specs=(tok(RET_VW), rst, tok(DN_VW), dst),
        out_shape=(jax.ShapeDtypeStruct((bsz * seq, RET_VW), BF16), jax.ShapeDtypeStruct(rs0.shape, F32),
                   jax.ShapeDtypeStruct((bsz * seq, DN_VW), BF16), jax.ShapeDtypeStruct(ds0.shape, F32)),
        compiler_params=pltpu.CompilerParams(dimension_semantics=("parallel", "arbitrary"),
                                             vmem_limit_bytes=VMEM_LIMIT),
    )(rq, rk, rv, rg, rs0, *ret_tables, dq, dk, dv, gbc, gbr, dz, dn_norm, ds0, *dn_tables)


_R_EXP = N_GROUPS


def _ffn_kernel(x_ref, og_ref, od_ref, ga_ref, gb_ref, p_ref,
                wru_ref, wdu_ref, wo_ref, nffn_ref, wr_ref, br_ref, wgu_ref, wdn_ref, nple_ref, wpg_ref,
                wpp_ref, nfin_ref, y_ref, hid_ref):
    branch_a = _dot(og_ref[...], wru_ref[...])
    branch_b = _dot(od_ref[...], wdu_ref[...])
    merged = (jax.nn.sigmoid(ga_ref[...].astype(F32)) * branch_a
              + jax.nn.sigmoid(gb_ref[...].astype(F32)) * branch_b)
    h = x_ref[...] + _dot(merged.astype(BF16), wo_ref[...])

    u = _rms_plain(h) * nffn_ref[...]
    ub = u.astype(BF16)

    logits = _dot(ub, wr_ref[...]) + br_ref[...]
    lane = lax.broadcasted_iota(jnp.int32, logits.shape, 1)
    neg = jnp.float32(-jnp.inf)
    big = jnp.int32(LANES)
    gl = jnp.where(lane < N_GROUPS, logits, neg)
    g_max = jnp.max(gl, axis=-1, keepdims=True)
    p_grp = 1.0 / jnp.sum(jnp.exp(gl - g_max), axis=-1, keepdims=True)
    grp = jnp.min(jnp.where(gl == g_max, lane, big), axis=-1, keepdims=True)
    e_idx = lane - _R_EXP
    in_grp = (e_idx >= 0) & (e_idx < N_EXPERTS) & ((e_idx // EXPERTS_PER_GROUP) == grp)
    el = jnp.where(in_grp, logits, neg)
    v1 = jnp.max(el, axis=-1, keepdims=True)
    i1 = jnp.min(jnp.where(el == v1, lane, big), axis=-1, keepdims=True)
    el2 = jnp.where(lane == i1, neg, el)
    v2 = jnp.max(el2, axis=-1, keepdims=True)
    i2 = jnp.min(jnp.where(el2 == v2, lane, big), axis=-1, keepdims=True)
    e2 = jnp.exp(v2 - v1)
    w1 = 1.0 / (1.0 + e2)
    w2 = e2 / (1.0 + e2)
    combine = jnp.where(lane == i1, w1 * p_grp, 0.0) + jnp.where(lane == i2, w2 * p_grp, 0.0)

    for e in range(N_EXPERTS):
        gu = _dot(ub, wgu_ref[e])
        hid = _silu(gu[:, :D_EXPERT]) * gu[:, D_EXPERT:]
        hid_ref[:, e * D_EXPERT:(e + 1) * D_EXPERT] = (
            combine[:, _R_EXP + e:_R_EXP + e + 1] * hid).astype(BF16)
    h = h + _dot(hid_ref[...], wdn_ref[...])

    u3 = (_rms_plain(h) * nple_ref[...]).astype(BF16)
    gate = jax.nn.sigmoid(_dot(u3, wpg_ref[...]))
    h = h + gate * _dot(p_ref[...].astype(BF16), wpp_ref[...])
    y_ref[...] = _rms_plain(h) * nfin_ref[...]


def _ffn_call(tok_in, weights, tm):
    n = tok_in[0].shape[0]
    tok = lambda a: pl.BlockSpec((tm, a.shape[1]), lambda i: (i, 0))
    const = lambda a: pl.BlockSpec(a.shape, lambda i: (0,) * a.ndim, pipeline_mode=pl.Buffered(1))
    return pl.pallas_call(
        _ffn_kernel, grid=(n // tm,), name="ffn",
        in_specs=[tok(a) for a in tok_in] + [const(w) for w in weights],
        out_specs=tok(tok_in[0]), out_shape=jax.ShapeDtypeStruct((n, D_MODEL), F32),
        scratch_shapes=[pltpu.VMEM((tm, N_EXPERTS * D_EXPERT), BF16)],
        compiler_params=pltpu.CompilerParams(dimension_semantics=("parallel",),
                                             vmem_limit_bytes=VMEM_LIMIT),
    )(*tok_in, *weights)


def _rope_tables(pos):
    half = RET_DK // 2
    inv = 1.0 / (ROPE_BASE ** (jnp.arange(half, dtype=F32) / half))
    ang = pos.astype(F32)[:, None] * inv[None, :]
    cos, sin = jnp.cos(ang), jnp.sin(ang)
    cos_t = jnp.tile(jnp.concatenate([cos, cos], axis=-1), (1, RET_H))
    sin_t = jnp.tile(jnp.concatenate([-sin, sin], axis=-1), (1, RET_H))
    return cos_t, sin_t


def _ret_tables(r, c):
    log_gamma = jnp.log(1.0 - 2.0 ** (-5.0 - jnp.arange(RET_H, dtype=F32)))
    row = jnp.arange(r)
    idx = (row % c).astype(F32)
    diff = idx[:, None] - idx[None, :]
    causal = (diff >= 0) & ((row[:, None] // c) == (row[None, :] // c))
    lg = log_gamma[:, None, None]
    decay = jnp.where(causal, jnp.exp(jnp.where(causal, diff, 0.0) * lg), 0.0)
    q_dec = jnp.exp((idx + 1.0)[None, :] * log_gamma[:, None])[..., None]
    k_dec = jnp.exp((c - 1.0 - idx)[None, :] * log_gamma[:, None])[..., None]
    chunk_dec = jnp.exp(c * log_gamma)
    lane_head = np.arange(RET_QK) // RET_DK
    head_mask = jnp.asarray((lane_head[None, :] == np.arange(RET_H)[:, None])[:, None, :], F32)
    return (decay.reshape(RET_H * r, r), head_mask, q_dec * head_mask, k_dec * head_mask,
            jnp.broadcast_to(chunk_dec[lane_head][:, None], (RET_QK, RET_DV)))


def _dn_tables(r, c):
    i = np.arange(r)
    ri, ci = i[:, None], i[None, :]
    same = (ri // c) == (ci // c)
    lblk = ((ri >= ci) & same).astype(np.float32)
    x = ri ^ ci
    lvl, m = [], 1
    while m < c:
        lvl.append((same & (ri > ci) & (x >= m) & (x < 2 * m)).astype(np.float32))
        m *= 2
    return jnp.asarray(lblk, BF16), jnp.asarray(lblk.T, BF16), jnp.asarray(np.stack(lvl))


def _token_mixers(x, p, s_ret, s_dn, s_conv, pos_table, wts, cfg):
    bsz, seq, _ = x.shape
    n = bsz * seq
    x2d = x.reshape(n, D_MODEL)
    cos_t, sin_t = pos_table
    (q, k, v, rg, dq, dk, dv, dz, ga, gb, gbc, gbr, conv_new) = _proj_call(
        x2d, wts["norm_mix"], wts["w_in"], wts["w_gates"], cos_t, sin_t,
        wts["alog_c"], wts["dtb_c"], wts["alog_r"], wts["dtb_r"], wts["conv_w"], s_conv, cfg["tm_proj"])

    ret_tables = _ret_tables(cfg["ret_cu"], min(cfg["tt"], cfg["ret_cu"]))
    dn_tables = _dn_tables(cfg["dn_cu"], min(cfg["tt"], cfg["dn_cu"]))
    og, ret_new, od, dn_new = _seq_call(
        q, k, v, rg, s_ret.reshape(bsz, RET_QK, RET_DV), ret_tables,
        dq, dk, dv, gbc, gbr, dz, wts["dn_norm"], s_dn, dn_tables,
        cfg["bt"], cfg["tt"], cfg["ret_cu"], cfg["dn_cu"])
    ret_new = ret_new.reshape(bsz, RET_H, RET_DK, RET_DV)

    return (x2d, og, od, ga, gb, p.reshape(n, PLE_DIM)), ret_new, dn_new, conv_new


def _prep_weights(norm_mix, w_in, conv_w, dn_a_log, dn_dt_bias, dn_norm, w_ret_up, w_dn_up, w_out,
                  norm_ffn, w_router_group, b_router_group, w_router_expert, b_router_expert,
                  w_gate_up, w_down, norm_ple, w_ple_gate, w_ple_proj, norm_final):
    assert sum(IN_WIDTHS[:6]) == _C_FRONT and sum(IN_WIDTHS[:8]) == _C_GATES
    w_in = w_in.astype(BF16)
    w_gates = jnp.pad(jnp.concatenate([w_in[:, _C_GATES:], w_in[:, _C_FRONT:_C_GATES]], axis=-1),
                      ((0, 0), (0, LANES - GB_ROWS)))
    pad_lanes = lambda a: jnp.pad(a.astype(F32), (0, LANES - a.shape[0]))[None, :]
    pad_rows = lambda a: jnp.pad(a.astype(F32), (0, GB_ROWS - a.shape[0]))[:, None]
    w_router = jnp.pad(jnp.concatenate([w_router_group, w_router_expert], axis=-1),
                       ((0, 0), (0, LANES - N_GROUPS - N_EXPERTS))).astype(BF16)
    b_router = pad_lanes(jnp.concatenate([b_router_group, b_router_expert]))
    row = lambda a: a.astype(F32)[None, :]
    return dict(
        norm_mix=row(norm_mix), w_in=w_in, w_gates=w_gates,
        alog_c=pad_lanes(dn_a_log), dtb_c=pad_lanes(dn_dt_bias),
        alog_r=pad_rows(dn_a_log), dtb_r=pad_rows(dn_dt_bias),
        conv_w=conv_w.astype(F32), dn_norm=row(dn_norm),
        w_ret_up=w_ret_up.astype(BF16), w_dn_up=w_dn_up.astype(BF16), w_out=w_out.astype(BF16),
        norm_ffn=row(norm_ffn), w_router=w_router, b_router=b_router,
        w_gate_up=w_gate_up.astype(BF16),
        w_down=w_down.astype(BF16).reshape(N_EXPERTS * D_EXPERT, D_MODEL),
        norm_ple=row(norm_ple), w_ple_gate=w_ple_gate.astype(BF16), w_ple_proj=w_ple_proj.astype(BF16),
        norm_final=row(norm_final),
    )


def kernel(x_prompt, x_sample, p_prompt, p_sample, state_ret, state_dn, state_conv, norm_mix, w_in, conv_w, dn_a_log, dn_dt_bias, dn_norm, w_ret_up, w_dn_up, w_out, norm_ffn, w_router_group, b_router_group, w_router_expert, b_router_expert, w_gate_up, w_down, norm_ple, w_ple_gate, w_ple_proj, norm_final):
    depth = w_in.shape[0]
    assert depth == 1, "one layer: the final norm is fused into the layer's last kernel"
    bp, lp, _ = x_prompt.shape
    bs, ls, _ = x_sample.shape
    wts = _prep_weights(norm_mix[0], w_in[0], conv_w[0], dn_a_log[0], dn_dt_bias[0], dn_norm[0],
                        w_ret_up[0], w_dn_up[0], w_out[0], norm_ffn[0], w_router_group[0],
                        b_router_group[0], w_router_expert[0], b_router_expert[0], w_gate_up[0],
                        w_down[0], norm_ple[0], w_ple_gate[0], w_ple_proj[0], norm_final)

    tm = 512
    bt_s = LANES // ls
    cfg_p = dict(tm=tm, tm_proj=256, bt=1, tt=512, ret_cu=256, dn_cu=128)
    cfg_s = dict(tm=tm, tm_proj=256, bt=bt_s, tt=ls, ret_cu=bt_s * ls, dn_cu=bt_s * ls)

    pos_p = _rope_tables(jnp.arange(lp, dtype=jnp.int32))
    cos_s, sin_s = _rope_tables(PAST_LEN + jnp.arange(ls, dtype=jnp.int32))
    pos_s = (jnp.tile(cos_s, (bs, 1)), jnp.tile(sin_s, (bs, 1)))

    zero_ret = jnp.zeros((bp, RET_H, RET_DK, RET_DV), F32)
    zero_dn = jnp.zeros((bp, DN_H, DN_DK, DN_DV), F32)
    zero_conv = jnp.zeros((bp, CONV_W - 1, DN_CONV_CH), F32)

    tok_p, r_p, d_p, c_p = _token_mixers(x_prompt, p_prompt[0], zero_ret, zero_dn, zero_conv,
                                         pos_p, wts, cfg_p)
    tok_s, r_s, d_s, c_s = _token_mixers(x_sample, p_sample[0], state_ret[0], state_dn[0], state_conv[0],
                                         pos_s, wts, cfg_s)
    ffn_weights = [wts[k] for k in ("w_ret_up", "w_dn_up", "w_out", "norm_ffn", "w_router", "b_router",
                                    "w_gate_up", "w_down", "norm_ple", "w_ple_gate", "w_ple_proj",
                                    "norm_final")]
    y_p = _ffn_call(tok_p, ffn_weights, tm)
    y_s = _ffn_call(tok_s, ffn_weights, tm)
    return (y_p.reshape(x_prompt.shape), y_s.reshape(x_sample.shape),
            r_p[None], d_p[None], c_p[None], r_s[None], d_s[None], c_s[None])
```

```python
import functools

import jax
import jax.numpy as jnp
import numpy as np
from jax import lax
from jax.experimental import pallas as pl
from jax.experimental.pallas import tpu as pltpu

F32 = jnp.float32
BF16 = jnp.bfloat16

D_MODEL = 1024
RET_H, RET_DK, RET_DV = 4, 64, 128
DN_H, DN_DK, DN_DV = 4, 128, 128
CONV_W = 4
ROPE_BASE = 10000.0
PAST_LEN = 16384
N_GROUPS, EXPERTS_PER_GROUP = 4, 4
N_EXPERTS = N_GROUPS * EXPERTS_PER_GROUP
D_EXPERT = 256
PLE_DIM = 256
EPS = 1e-6

RET_QK = RET_H * RET_DK
RET_VW = RET_H * RET_DV
DN_QK = DN_H * DN_DK
DN_VW = DN_H * DN_DV
DN_CONV_CH = 2 * DN_QK + DN_VW
IN_WIDTHS = (RET_QK, RET_QK, RET_VW, RET_VW, DN_CONV_CH, DN_VW, DN_H, DN_H, D_MODEL, D_MODEL)

LANES = 128
SUBLANES = 8
VMEM_LIMIT = 56 * 1024 * 1024
GB_ROWS = 2 * DN_H
assert GB_ROWS == SUBLANES

_C_Q, _C_K, _C_V, _C_RG = 0, 256, 512, 1024
_C_DQKV, _C_DZ, _C_FRONT = 1536, 3072, 3584
_C_GATES = _C_FRONT + GB_ROWS


def _silu(x):
    return x * jax.nn.sigmoid(x)


def _softplus(x):
    return jnp.maximum(x, 0.0) + jnp.log1p(jnp.exp(-jnp.abs(x)))


def _dot(a, b):
    return jnp.dot(a, b, preferred_element_type=F32)


def _dot_nt(a, b):
    return lax.dot_general(a, b, (((1,), (1,)), ((), ())), preferred_element_type=F32)


def _dot_tn(a, b):
    return lax.dot_general(a, b, (((0,), (0,)), ((), ())), preferred_element_type=F32)


def _split3(x):
    hi = x.astype(BF16)
    r = x - hi.astype(F32)
    mid = r.astype(BF16)
    lo = (r - mid.astype(F32)).astype(BF16)
    return hi, mid, lo


def _rms_plain(x):
    return x * lax.rsqrt(jnp.mean(x * x, axis=-1, keepdims=True) + EPS)


def _proj_kernel(x_ref, gain_ref, w_ref, wg_ref, cos_ref, sin_ref,
                 alog_c_ref, dtb_c_ref, alog_r_ref, dtb_r_ref, cw_ref, c0_ref,
                 q_ref, k_ref, v_ref, rg_ref, dq_ref, dk_ref, dv_ref, dz_ref, ga_ref, gb_ref,
                 gbc_ref, gbr_ref, c_ref, ext_ref, *, bt, tt, nl):
    pad = SUBLANES
    tail = CONV_W - 1

    @pl.when(pl.program_id(0) % nl == 0)
    def _():
        ext_ref[:, pad - tail:pad, :] = c0_ref[...]

    x = x_ref[...]
    u = x * lax.rsqrt(jnp.mean(x * x, axis=-1, keepdims=True) + EPS) * gain_ref[...]
    ub = u.astype(BF16)

    def mm(lo, hi):
        return _dot(ub, w_ref[:, lo:hi])

    ext_ref[:, pad:pad + tt, :] = mm(_C_DQKV, _C_DZ).reshape(bt, tt, DN_CONV_CH)

    def conv_silu(c0, c1):
        acc = ext_ref[:, pad - tail:pad - tail + tt, c0:c1] * cw_ref[0:1, c0:c1]
        for j in range(1, CONV_W):
            acc = acc + ext_ref[:, pad - tail + j:pad - tail + j + tt, c0:c1] * cw_ref[j:j + 1, c0:c1]
        return _silu(acc).reshape(bt * tt, c1 - c0)

    def l2_heads(out_ref, c0, scale):
        units = []
        for h in range(DN_H):
            ch = conv_silu(c0 + h * DN_DK, c0 + (h + 1) * DN_DK)
            unit = ch * lax.rsqrt(jnp.sum(ch * ch, axis=-1, keepdims=True) + EPS)
            units.append(unit if scale is None else unit * scale)
        out_ref[...] = jnp.concatenate(units, axis=-1).astype(BF16)

    cos = cos_ref[...]
    sin = sin_ref[...]
    lane = lax.broadcasted_iota(jnp.int32, cos.shape, 1)
    first_half = (lane % RET_DK) < (RET_DK // 2)

    def rot(t):
        swapped = jnp.where(first_half, pltpu.roll(t, RET_QK - RET_DK // 2, 1),
                            pltpu.roll(t, RET_DK // 2, 1))
        return t * cos + swapped * sin

    gates = _dot(ub, wg_ref[...])
    ga_ref[...] = gates[:, :D_MODEL].astype(BF16)
    gb_ref[...] = gates[:, D_MODEL:2 * D_MODEL].astype(BF16)
    dab = gates[:, 2 * D_MODEL:]
    l2_heads(dq_ref, 0, DN_DK ** -0.5)
    l2_heads(dk_ref, DN_QK, None)
    v_ref[...] = mm(_C_V, _C_RG).astype(BF16)
    dv_ref[...] = conv_silu(2 * DN_QK, DN_CONV_CH).astype(BF16)
    new_tail = ext_ref[:, pad + tt - tail:pad + tt, :]
    ext_ref[:, pad - tail:pad, :] = new_tail
    c_ref[...] = new_tail
    qk = mm(_C_Q, _C_V)
    q_ref[...] = (rot(qk[:, :RET_QK]) * (RET_DK ** -0.5)).astype(BF16)
    k_ref[...] = rot(qk[:, RET_QK:]).astype(BF16)
    rg_ref[...] = _silu(mm(_C_RG, _C_DQKV)).astype(BF16)
    dz_ref[...] = _silu(mm(_C_DZ, _C_FRONT)).astype(BF16)

    lane_c = lax.broadcasted_iota(jnp.int32, dab.shape, 1)
    g_c = -jnp.exp(alog_c_ref[...]) * _softplus(dab + dtb_c_ref[...])
    gbc_ref[...] = jnp.where(lane_c < DN_H, g_c, jax.nn.sigmoid(dab))
    dabt = dab.T[:GB_ROWS]
    row_r = lax.broadcasted_iota(jnp.int32, dabt.shape, 0)
    g_r = -jnp.exp(alog_r_ref[...]) * _softplus(dabt + dtb_r_ref[...])
    gbr_ref[...] = jnp.where(row_r < DN_H, g_r, jax.nn.sigmoid(dabt))


def _proj_call(x2d, gain, w_in, w_gates, cos_t, sin_t, alog_c, dtb_c, alog_r, dtb_r, conv_w, c0, tm):
    n = x2d.shape[0]
    bsz = c0.shape[0]
    seq = n // bsz
    tt = min(tm, seq)
    bt = tm // tt
    nl = seq // tt
    table_blocks = cos_t.shape[0] // tm
    tok = lambda w: pl.BlockSpec((tm, w), lambda i: (i, 0))
    const = lambda a: pl.BlockSpec(a.shape, lambda i: (0,) * a.ndim, pipeline_mode=pl.Buffered(1))
    tab = pl.BlockSpec((tm, RET_QK), lambda i: (i % table_blocks, 0))
    cst = pl.BlockSpec((bt, CONV_W - 1, DN_CONV_CH), lambda i: (i // nl, 0, 0))
    bf = lambda w: jax.ShapeDtypeStruct((n, w), BF16)
    out_shapes = (bf(RET_QK), bf(RET_QK), bf(RET_VW), bf(RET_VW), bf(DN_QK), bf(DN_QK), bf(DN_VW),
                  bf(DN_VW), bf(D_MODEL), bf(D_MODEL),
                  jax.ShapeDtypeStruct((n, LANES), F32), jax.ShapeDtypeStruct((GB_ROWS, n), F32),
                  jax.ShapeDtypeStruct(c0.shape, F32))
    out_specs = (tok(RET_QK), tok(RET_QK), tok(RET_VW), tok(RET_VW), tok(DN_QK), tok(DN_QK), tok(DN_VW),
                 tok(DN_VW), tok(D_MODEL), tok(D_MODEL), tok(LANES),
                 pl.BlockSpec((GB_ROWS, tm), lambda i: (0, i)), cst)
    return pl.pallas_call(
        functools.partial(_proj_kernel, bt=bt, tt=tt, nl=nl), grid=(n // tm,), name="proj",
        in_specs=[tok(D_MODEL), const(gain),
                  pl.BlockSpec((D_MODEL, _C_FRONT), lambda i: (0, 0), pipeline_mode=pl.Buffered(1)),
                  const(w_gates), tab, tab,
                  const(alog_c), const(dtb_c), const(alog_r), const(dtb_r), const(conv_w), cst],
        out_specs=out_specs, out_shape=out_shapes,
        scratch_shapes=[pltpu.VMEM((bt, tt + SUBLANES, DN_CONV_CH), F32)],
        compiler_params=pltpu.CompilerParams(dimension_semantics=("arbitrary",),
                                             vmem_limit_bytes=VMEM_LIMIT),
    )(x2d, gain, w_in, w_gates, cos_t, sin_t, alog_c, dtb_c, alog_r, dtb_r, conv_w, c0)


def _ret_tile(q_ref, k_ref, v_ref, rg_ref, dec_ref, hm_ref, qd_ref, kd_ref, cd_ref,
              o_ref, s_ref, *, bt, tt, cu):
    r = bt * tt
    blk = min(tt, cu)
    n_units = r // cu
    n_seg = cu // blk

    heads = range(RET_H)
    hrow = lambda a, h, m: a[h * m:(h + 1) * m]
    for n in range(n_units):
        rows = slice(n * cu, (n + 1) * cu)
        qf = q_ref[rows, :].astype(F32)
        kf = k_ref[rows, :].astype(F32)
        vf = v_ref[rows, :].astype(F32)
        q_m = [qf * hm_ref[h] for h in heads]
        q_d = [qf * qd_ref[h] for h in heads]
        k_d = [kf * kd_ref[h] for h in heads]
        v_h = [vf[:, h * RET_DV:(h + 1) * RET_DV] for h in heads]
        scores = (_dot_nt(jnp.concatenate(q_m, axis=0).astype(BF16), k_ref[rows, :])
                  * dec_ref[...]).astype(BF16)
        intra = []
        for h in heads:
            intra.append(_dot(hrow(scores, h, cu), v_h[h].astype(BF16)))
        inters = []
        for j in range(n_seg):
            seg = slice(j * blk, (j + 1) * blk)
            b = (n * cu + j * blk) // tt
            s = s_ref[b]
            q_seg = jnp.concatenate([a[seg] for a in q_d], axis=0).astype(BF16)
            k_seg = jnp.concatenate([a[seg] for a in k_d], axis=0).astype(BF16)
            v_seg = jnp.concatenate([a[seg] for a in v_h], axis=0).astype(BF16)
            inters.append(_dot(q_seg, s.astype(BF16)))
            s_ref[b] = s * cd_ref[...] + _dot_tn(k_seg, v_seg)
        for h in heads:
            inter = jnp.concatenate([hrow(a, h, blk) for a in inters], axis=0)
            gate = rg_ref[rows, h * RET_DV:(h + 1) * RET_DV].astype(F32)
            o_ref[rows, h * RET_DV:(h + 1) * RET_DV] = (
                _rms_plain(intra[h] + inter) * gate).astype(BF16)


def _dn_tile(q_ref, k_ref, v_ref, gc_ref, gr_ref, dz_ref, dnorm_ref, lblk_ref, ublk_ref, lvl_ref,
             o_ref, s_ref, *, bt, tt, cu):
    r = bt * tt
    blk = min(tt, cu)
    n_units = r // cu
    n_seg = cu // blk
    n_lvl = lvl_ref.shape[0]

    ri = lax.broadcasted_iota(jnp.int32, (cu, cu), 0)
    ci = lax.broadcasted_iota(jnp.int32, (cu, cu), 1)
    same = (ri // blk) == (ci // blk)
    incl = same & (ri >= ci)
    strict = same & (ri > ci)

    gc = gc_ref[...]
    g_hi, g_mid, g_lo = _split3(gc)
    r_hi, r_mid, r_lo = _split3(gr_ref[...])
    lblk = lblk_ref[...]
    ublk = ublk_ref[...]

    units = [(n, h) for n in range(n_units) for h in range(DN_H)]
    cum_c, cum_r = [], []
    for n in range(n_units):
        rows = slice(n * cu, (n + 1) * cu)
        cum_c.append(_dot(lblk, g_hi[rows]) + _dot(lblk, g_mid[rows]) + _dot(lblk, g_lo[rows]))
        cum_r.append(_dot(r_hi[:, rows], ublk) + _dot(r_mid[:, rows], ublk) + _dot(r_lo[:, rows], ublk))

    qn, kn, kb, g_col, dec, a_mat, ab, d, rhs = {}, {}, {}, {}, {}, {}, {}, {}, {}
    for u in units:
        n, h = u
        rows = slice(n * cu, (n + 1) * cu)
        qb = q_ref[rows, h * DN_DK:(h + 1) * DN_DK]
        kb[u] = k_ref[rows, h * DN_DK:(h + 1) * DN_DK]
        kn[u] = kb[u].astype(F32)
        vc = v_ref[rows, h * DN_DV:(h + 1) * DN_DV].astype(F32)
        g_col[u] = cum_c[n][:, h:h + 1]
        g_row = cum_r[n][h:h + 1, :]
        beta = gc[rows, DN_H + h:DN_H + h + 1]
        dec[u] = jnp.where(incl, jnp.exp(jnp.where(incl, g_col[u] - g_row, 0.0)), 0.0)
        a_mat[u] = jnp.where(strict, _dot_nt(kb[u], kb[u]) * dec[u] * beta, 0.0)
        ab[u] = a_mat[u].astype(BF16)
        e_g = jnp.exp(g_col[u])
        rhs[u] = jnp.concatenate([beta * vc, beta * e_g * kn[u]], axis=-1)
        qn[u] = (qb, qb.astype(F32) * e_g)
        d[u] = -(lvl_ref[0] * a_mat[u])
    for lv in range(1, n_lvl):
        m = 1 << lv
        by_rows = m % SUBLANES == 0

        def lower(t):
            return t.reshape(cu // (2 * m), 2, m, t.shape[-1])[:, 1].reshape(cu // 2, t.shape[-1])

        for u in units:
            db = d[u].astype(BF16)
            if not by_rows:
                w = a_mat[u] + _dot(db, ab[u])
                z = w + _dot(w.astype(BF16), db)
                d[u] = d[u] - lvl_ref[lv] * z
                continue
            w = lower(a_mat[u]) + _dot(lower(d[u]).astype(BF16), ab[u])
            z = w + _dot(w.astype(BF16), db)
            d_low = lower(d[u]) - lower(lvl_ref[lv]) * z
            d4 = d[u].reshape(cu // (2 * m), 2, m, cu)
            d[u] = jnp.concatenate([d4[:, :1], d_low.reshape(cu // (2 * m), 1, m, cu)],
                                   axis=1).reshape(cu, cu)
    sol, p_mat = {}, {}
    for u in units:
        sol[u] = rhs[u] + _dot(d[u].astype(BF16), rhs[u].astype(BF16))
        p_mat[u] = (_dot_nt(qn[u][0], kb[u]) * dec[u]).astype(BF16)

    for n in range(n_units):
        for h in range(DN_H):
            u = (n, h)
            u_mat, w_mat = sol[u][:, :DN_DV], sol[u][:, DN_DV:]
            qe = qn[u][1]
            deltas, qss = [], []
            for j in range(n_seg):
                r0 = j * blk
                b = (n * cu + r0) // tt
                sb = s_ref[b, h].astype(BF16)
                lhs = jnp.concatenate([w_mat[r0:r0 + blk], qe[r0:r0 + blk]], axis=0).astype(BF16)
                ws_qs = _dot(lhs, sb)
                deltas.append(u_mat[r0:r0 + blk] - ws_qs[:blk])
                qss.append(ws_qs[blk:])
            delta = deltas[0] if n_seg == 1 else jnp.concatenate(deltas, axis=0)
            q_s = qss[0] if n_seg == 1 else jnp.concatenate(qss, axis=0)
            delta_b = delta.astype(BF16)
            o = q_s + _dot(p_mat[u], delta_b)
            for j in range(n_seg):
                r0 = j * blk
                b = (n * cu + r0) // tt
                g_last = g_col[u][r0 + blk - 1:r0 + blk, :]
                k_dec = (kn[u][r0:r0 + blk] * jnp.exp(g_last - g_col[u][r0:r0 + blk])).astype(BF16)
                s_ref[b, h] = s_ref[b, h] * jnp.exp(g_last) + _dot_tn(k_dec, delta_b[r0:r0 + blk])
            gate = dz_ref[n * cu:(n + 1) * cu, h * DN_DV:(h + 1) * DN_DV].astype(F32)
            o_ref[n * cu:(n + 1) * cu, h * DN_DV:(h + 1) * DN_DV] = (
                _rms_plain(o) * dnorm_ref[...] * gate).astype(BF16)


def _seq_kernel(rq_ref, rk_ref, rv_ref, rg_ref, rs0_ref, dec_ref, hm_ref, qd_ref, kd_ref, cd_ref,
                dq_ref, dk_ref, dv_ref, gc_ref, gr_ref, dz_ref, dnorm_ref, ds0_ref, lblk_ref, ublk_ref,
                lvl_ref, og_ref, rs_ref, od_ref, ds_ref, *, bt, tt, ret_cu, dn_cu):
    @pl.when(pl.program_id(1) == 0)
    def _():
        rs_ref[...] = rs0_ref[...]
        ds_ref[...] = ds0_ref[...]

    _dn_tile(dq_ref, dk_ref, dv_ref, gc_ref, gr_ref, dz_ref, dnorm_ref, lblk_ref, ublk_ref, lvl_ref,
             od_ref, ds_ref, bt=bt, tt=tt, cu=dn_cu)
    _ret_tile(rq_ref, rk_ref, rv_ref, rg_ref, dec_ref, hm_ref, qd_ref, kd_ref, cd_ref,
              og_ref, rs_ref, bt=bt, tt=tt, cu=ret_cu)


def _seq_call(rq, rk, rv, rg, rs0, ret_tables, dq, dk, dv, gbc, gbr, dz, dn_norm, ds0, dn_tables,
              bt, tt, ret_cu, dn_cu):
    bsz = ds0.shape[0]
    seq = dq.shape[0] // bsz
    nl = seq // tt
    r = bt * tt
    tok = lambda w: pl.BlockSpec((r, w), lambda b, l: (b * nl + l, 0))
    const = lambda a: pl.BlockSpec(a.shape, lambda b, l: (0,) * a.ndim)
    rst = pl.BlockSpec((bt, RET_QK, RET_DV), lambda b, l: (b, 0, 0))
    dst = pl.BlockSpec((bt, DN_H, DN_DK, DN_DV), lambda b, l: (b, 0, 0, 0))
    return pl.pallas_call(
        functools.partial(_seq_kernel, bt=bt, tt=tt, ret_cu=ret_cu, dn_cu=dn_cu),
        grid=(bsz // bt, nl), name="seq",
        in_specs=([tok(RET_QK), tok(RET_QK), tok(RET_VW), tok(RET_VW), rst]
                  + [const(t) for t in ret_tables]
                  + [tok(DN_QK), tok(DN_QK), tok(DN_VW), tok(LANES),
                     pl.BlockSpec((GB_ROWS, r), lambda b, l: (0, b * nl + l)),
                     tok(DN_VW), const(dn_norm), dst] + [const(t) for t in dn_tables]),
        out_specs=(tok(RET_VW), rst, tok(DN_VW), dst),
        out_shape=(jax.ShapeDtypeStruct((bsz * seq, RET_VW), BF16), jax.ShapeDtypeStruct(rs0.shape, F32),
                   jax.ShapeDtypeStruct((bsz * seq, DN_VW), BF16), jax.ShapeDtypeStruct(ds0.shape, F32)),
        compiler_params=pltpu.CompilerParams(dimension_semantics=("parallel", "arbitrary"),
                                             vmem_limit_bytes=VMEM_LIMIT),
    )(rq, rk, rv, rg, rs0, *ret_tables, dq, dk, dv, gbc, gbr, dz, dn_norm, ds0, *dn_tables)


_R_EXP = N_GROUPS


def _ffn_kernel(x_ref, og_ref, od_ref, ga_ref, gb_ref, p_ref,
                wru_ref, wdu_ref, wo_ref, nffn_ref, wr_ref, br_ref, wgu_ref, wdn_ref, nple_ref, wpg_ref,
                wpp_ref, nfin_ref, y_ref, hid_ref):
    branch_a = _dot(og_ref[...], wru_ref[...])
    branch_b = _dot(od_ref[...], wdu_ref[...])
    merged = (jax.nn.sigmoid(ga_ref[...].astype(F32)) * branch_a
              + jax.nn.sigmoid(gb_ref[...].astype(F32)) * branch_b)
    h = x_ref[...] + _dot(merged.astype(BF16), wo_ref[...])

    u = _rms_plain(h) * nffn_ref[...]
    ub = u.astype(BF16)

    logits = _dot(ub, wr_ref[...]) + br_ref[...]
    lane = lax.broadcasted_iota(jnp.int32, logits.shape, 1)
    neg = jnp.float32(-jnp.inf)
    big = jnp.int32(LANES)
    gl = jnp.where(lane < N_GROUPS, logits, neg)
    g_max = jnp.max(gl, axis=-1, keepdims=True)
    p_grp = 1.0 / jnp.sum(jnp.exp(gl - g_max), axis=-1, keepdims=True)
    grp = jnp.min(jnp.where(gl == g_max, lane, big), axis=-1, keepdims=True)
    e_idx = lane - _R_EXP
    in_grp = (e_idx >= 0) & (e_idx < N_EXPERTS) & ((e_idx // EXPERTS_PER_GROUP) == grp)
    el = jnp.where(in_grp, logits, neg)
    v1 = jnp.max(el, axis=-1, keepdims=True)
    i1 = jnp.min(jnp.where(el == v1, lane, big), axis=-1, keepdims=True)
    el2 = jnp.where(lane == i1, neg, el)
    v2 = jnp.max(el2, axis=-1, keepdims=True)
    i2 = jnp.min(jnp.where(el2 == v2, lane, big), axis=-1, keepdims=True)
    e2 = jnp.exp(v2 - v1)
    w1 = 1.0 / (1.0 + e2)
    w2 = e2 / (1.0 + e2)
    combine = jnp.where(lane == i1, w1 * p_grp, 0.0) + jnp.where(lane == i2, w2 * p_grp, 0.0)

    for e in range(N_EXPERTS):
        gu = _dot(ub, wgu_ref[e])
        hid = _silu(gu[:, :D_EXPERT]) * gu[:, D_EXPERT:]
        hid_ref[:, e * D_EXPERT:(e + 1) * D_EXPERT] = (
            combine[:, _R_EXP + e:_R_EXP + e + 1] * hid).astype(BF16)
    h = h + _dot(hid_ref[...], wdn_ref[...])

    u3 = (_rms_plain(h) * nple_ref[...]).astype(BF16)
    gate = jax.nn.sigmoid(_dot(u3, wpg_ref[...]))
    h = h + gate * _dot(p_ref[...].astype(BF16), wpp_ref[...])
    y_ref[...] = _rms_plain(h) * nfin_ref[...]


def _ffn_call(tok_in, weights, tm):
    n = tok_in[0].shape[0]
    tok = lambda a: pl.BlockSpec((tm, a.shape[1]), lambda i: (i, 0))
    const = lambda a: pl.BlockSpec(a.shape, lambda i: (0,) * a.ndim, pipeline_mode=pl.Buffered(1))
    return pl.pallas_call(
        _ffn_kernel, grid=(n // tm,), name="ffn",
        in_specs=[tok(a) for a in tok_in] + [const(w) for w in weights],
        out_specs=tok(tok_in[0]), out_shape=jax.ShapeDtypeStruct((n, D_MODEL), F32),
        scratch_shapes=[pltpu.VMEM((tm, N_EXPERTS * D_EXPERT), BF16)],
        compiler_params=pltpu.CompilerParams(dimension_semantics=("parallel",),
                                             vmem_limit_bytes=VMEM_LIMIT),
    )(*tok_in, *weights)


def _rope_tables(pos):
    half = RET_DK // 2
    inv = 1.0 / (ROPE_BASE ** (jnp.arange(half, dtype=F32) / half))
    ang = pos.astype(F32)[:, None] * inv[None, :]
    cos, sin = jnp.cos(ang), jnp.sin(ang)
    cos_t = jnp.tile(jnp.concatenate([cos, cos], axis=-1), (1, RET_H))
    sin_t = jnp.tile(jnp.concatenate([-sin, sin], axis=-1), (1, RET_H))
    return cos_t, sin_t


def _ret_tables(r, c):
    log_gamma = jnp.log(1.0 - 2.0 ** (-5.0 - jnp.arange(RET_H, dtype=F32)))
    row = jnp.arange(r)
    idx = (row % c).astype(F32)
    diff = idx[:, None] - idx[None, :]
    causal = (diff >= 0) & ((row[:, None] // c) == (row[None, :] // c))
    lg = log_gamma[:, None, None]
    decay = jnp.where(causal, jnp.exp(jnp.where(causal, diff, 0.0) * lg), 0.0)
    q_dec = jnp.exp((idx + 1.0)[None, :] * log_gamma[:, None])[..., None]
    k_dec = jnp.exp((c - 1.0 - idx)[None, :] * log_gamma[:, None])[..., None]
    chunk_dec = jnp.exp(c * log_gamma)
    lane_head = np.arange(RET_QK) // RET_DK
    head_mask = jnp.asarray((lane_head[None, :] == np.arange(RET_H)[:, None])[:, None, :], F32)
    return (decay.reshape(RET_H * r, r), head_mask, q_dec * head_mask, k_dec * head_mask,
            jnp.broadcast_to(chunk_dec[lane_head][:, None], (RET_QK, RET_DV)))


def _dn_tables(r, c):
    i = np.arange(r)
    ri, ci = i[:, None], i[None, :]
    same = (ri // c) == (ci // c)
    lblk = ((ri >= ci) & same).astype(np.float32)
    x = ri ^ ci
    lvl, m = [], 1
    while m < c:
        lvl.append((same & (ri > ci) & (x >= m) & (x < 2 * m)).astype(np.float32))
        m *= 2
    return jnp.asarray(lblk, BF16), jnp.asarray(lblk.T, BF16), jnp.asarray(np.stack(lvl))


def _token_mixers(x, p, s_ret, s_dn, s_conv, pos_table, wts, cfg):
    bsz, seq, _ = x.shape
    n = bsz * seq
    x2d = x.reshape(n, D_MODEL)
    cos_t, sin_t = pos_table
    (q, k, v, rg, dq, dk, dv, dz, ga, gb, gbc, gbr, conv_new) = _proj_call(
        x2d, wts["norm_mix"], wts["w_in"], wts["w_gates"], cos_t, sin_t,
        wts["alog_c"], wts["dtb_c"], wts["alog_r"], wts["dtb_r"], wts["conv_w"], s_conv, cfg["tm_proj"])

    ret_tables = _ret_tables(cfg["ret_cu"], min(cfg["tt"], cfg["ret_cu"]))
    dn_tables = _dn_tables(cfg["dn_cu"], min(cfg["tt"], cfg["dn_cu"]))
    og, ret_new, od, dn_new = _seq_call(
        q, k, v, rg, s_ret.reshape(bsz, RET_QK, RET_DV), ret_tables,
        dq, dk, dv, gbc, gbr, dz, wts["dn_norm"], s_dn, dn_tables,
        cfg["bt"], cfg["tt"], cfg["ret_cu"], cfg["dn_cu"])
    ret_new = ret_new.reshape(bsz, RET_H, RET_DK, RET_DV)

    return (x2d, og, od, ga, gb, p.reshape(n, PLE_DIM)), ret_new, dn_new, conv_new


def _prep_weights(norm_mix, w_in, conv_w, dn_a_log, dn_dt_bias, dn_norm, w_ret_up, w_dn_up, w_out,
                  norm_ffn, w_router_group, b_router_group, w_router_expert, b_router_expert,
                  w_gate_up, w_down, norm_ple, w_ple_gate, w_ple_proj, norm_final):
    assert sum(IN_WIDTHS[:6]) == _C_FRONT and sum(IN_WIDTHS[:8]) == _C_GATES
    w_in = w_in.astype(BF16)
    w_gates = jnp.pad(jnp.concatenate([w_in[:, _C_GATES:], w_in[:, _C_FRONT:_C_GATES]], axis=-1),
                      ((0, 0), (0, LANES - GB_ROWS)))
    pad_lanes = lambda a: jnp.pad(a.astype(F32), (0, LANES - a.shape[0]))[None, :]
    pad_rows = lambda a: jnp.pad(a.astype(F32), (0, GB_ROWS - a.shape[0]))[:, None]
    w_router = jnp.pad(jnp.concatenate([w_router_group, w_router_expert], axis=-1),
                       ((0, 0), (0, LANES - N_GROUPS - N_EXPERTS))).astype(BF16)
    b_router = pad_lanes(jnp.concatenate([b_router_group, b_router_expert]))
    row = lambda a: a.astype(F32)[None, :]
    return dict(
        norm_mix=row(norm_mix), w_in=w_in, w_gates=w_gates,
        alog_c=pad_lanes(dn_a_log), dtb_c=pad_lanes(dn_dt_bias),
        alog_r=pad_rows(dn_a_log), dtb_r=pad_rows(dn_dt_bias),
        conv_w=conv_w.astype(F32), dn_norm=row(dn_norm),
        w_ret_up=w_ret_up.astype(BF16), w_dn_up=w_dn_up.astype(BF16), w_out=w_out.astype(BF16),
        norm_ffn=row(norm_ffn), w_router=w_router, b_router=b_router,
        w_gate_up=w_gate_up.astype(BF16),
        w_down=w_down.astype(BF16).reshape(N_EXPERTS * D_EXPERT, D_MODEL),
        norm_ple=row(norm_ple), w_ple_gate=w_ple_gate.astype(BF16), w_ple_proj=w_ple_proj.astype(BF16),
        norm_final=row(norm_final),
    )


def kernel(x_prompt, x_sample, p_prompt, p_sample, state_ret, state_dn, state_conv, norm_mix, w_in, conv_w, dn_a_log, dn_dt_bias, dn_norm, w_ret_up, w_dn_up, w_out, norm_ffn, w_router_group, b_router_group, w_router_expert, b_router_expert, w_gate_up, w_down, norm_ple, w_ple_gate, w_ple_proj, norm_final):
    depth = w_in.shape[0]
    assert depth == 1, "one layer: the final norm is fused into the layer's last kernel"
    bp, lp, _ = x_prompt.shape
    bs, ls, _ = x_sample.shape
    wts = _prep_weights(norm_mix[0], w_in[0], conv_w[0], dn_a_log[0], dn_dt_bias[0], dn_norm[0],
                        w_ret_up[0], w_dn_up[0], w_out[0], norm_ffn[0], w_router_group[0],
                        b_router_group[0], w_router_expert[0], b_router_expert[0], w_gate_up[0],
                        w_down[0], norm_ple[0], w_ple_gate[0], w_ple_proj[0], norm_final)

    tm = 512
    bt_s = LANES // ls
    cfg_p = dict(tm=tm, tm_proj=256, bt=1, tt=512, ret_cu=256, dn_cu=128)
    cfg_s = dict(tm=tm, tm_proj=256, bt=bt_s, tt=ls, ret_cu=bt_s * ls, dn_cu=bt_s * ls)

    pos_p = _rope_tables(jnp.arange(lp, dtype=jnp.int32))
    cos_s, sin_s = _rope_tables(PAST_LEN + jnp.arange(ls, dtype=jnp.int32))
    pos_s = (jnp.tile(cos_s, (bs, 1)), jnp.tile(sin_s, (bs, 1)))

    zero_ret = jnp.zeros((bp, RET_H, RET_DK, RET_DV), F32)
    zero_dn = jnp.zeros((bp, DN_H, DN_DK, DN_DV), F32)
    zero_conv = jnp.zeros((bp, CONV_W - 1, DN_CONV_CH), F32)

    tok_p, r_p, d_p, c_p = _token_mixers(x_prompt, p_prompt[0], zero_ret, zero_dn, zero_conv,
                                         pos_p, wts, cfg_p)
    tok_s, r_s, d_s, c_s = _token_mixers(x_sample, p_sample[0], state_ret[0], state_dn[0], state_conv[0],
                                         pos_s, wts, cfg_s)
    ffn_weights = [wts[k] for k in ("w_ret_up", "w_dn_up", "w_out", "norm_ffn", "w_router", "b_router",
                                    "w_gate_up", "w_down", "norm_ple", "w_ple_gate", "w_ple_proj",
                                    "norm_final")]
    y_p = _ffn_call(tok_p, ffn_weights, tm)
    y_s = _ffn_call(tok_s, ffn_weights, tm)
    return (y_p.reshape(x_prompt.shape), y_s.reshape(x_sample.shape),
            r_p[None], d_p[None], c_p[None], r_s[None], d_s[None], c_s[None])
```

```python
import functools

import jax
import jax.numpy as jnp
import numpy as np
from jax import lax
from jax.experimental import pallas as pl
from jax.experimental.pallas import tpu as pltpu

F32 = jnp.float32
BF16 = jnp.bfloat16

D_MODEL = 1024
RET_H, RET_DK, RET_DV = 4, 64, 128
DN_H, DN_DK, DN_DV = 4, 128, 128
CONV_W = 4
ROPE_BASE = 10000.0
PAST_LEN = 16384
N_GROUPS, EXPERTS_PER_GROUP = 4, 4
N_EXPERTS = N_GROUPS * EXPERTS_PER_GROUP
D_EXPERT = 256
PLE_DIM = 256
EPS = 1e-6

RET_QK = RET_H * RET_DK
RET_VW = RET_H * RET_DV
DN_QK = DN_H * DN_DK
DN_VW = DN_H * DN_DV
DN_CONV_CH = 2 * DN_QK + DN_VW
IN_WIDTHS = (RET_QK, RET_QK, RET_VW, RET_VW, DN_CONV_CH, DN_VW, DN_H, DN_H, D_MODEL, D_MODEL)

LANES = 128
SUBLANES = 8
VMEM_LIMIT = 56 * 1024 * 1024
GB_ROWS = 2 * DN_H
assert GB_ROWS == SUBLANES

_C_Q, _C_K, _C_V, _C_RG = 0, 256, 512, 1024
_C_DQKV, _C_DZ, _C_FRONT = 1536, 3072, 3584
_C_GATES = _C_FRONT + GB_ROWS


def _silu(x):
    return x * jax.nn.sigmoid(x)


def _softplus(x):
    return jnp.maximum(x, 0.0) + jnp.log1p(jnp.exp(-jnp.abs(x)))


def _dot(a, b):
    return jnp.dot(a, b, preferred_element_type=F32)


def _dot_nt(a, b):
    return lax.dot_general(a, b, (((1,), (1,)), ((), ())), preferred_element_type=F32)


def _dot_tn(a, b):
    return lax.dot_general(a, b, (((0,), (0,)), ((), ())), preferred_element_type=F32)


def _split3(x):
    hi = x.astype(BF16)
    r = x - hi.astype(F32)
    mid = r.astype(BF16)
    lo = (r - mid.astype(F32)).astype(BF16)
    return hi, mid, lo


def _rms_plain(x):
    return x * lax.rsqrt(jnp.mean(x * x, axis=-1, keepdims=True) + EPS)


def _proj_kernel(x_ref, gain_ref, w_ref, wg_ref, cos_ref, sin_ref,
                 alog_c_ref, dtb_c_ref, alog_r_ref, dtb_r_ref, cw_ref, c0_ref,
                 q_ref, k_ref, v_ref, rg_ref, dq_ref, dk_ref, dv_ref, dz_ref, ga_ref, gb_ref,
                 gbc_ref, gbr_ref, c_ref, ext_ref, *, bt, tt, nl):
    pad = SUBLANES
    tail = CONV_W - 1

    @pl.when(pl.program_id(0) % nl == 0)
    def _():
        ext_ref[:, pad - tail:pad, :] = c0_ref[...]

    x = x_ref[...]
    u = x * lax.rsqrt(jnp.mean(x * x, axis=-1, keepdims=True) + EPS) * gain_ref[...]
    ub = u.astype(BF16)

    def mm(lo, hi):
        return _dot(ub, w_ref[:, lo:hi])

    ext_ref[:, pad:pad + tt, :] = mm(_C_DQKV, _C_DZ).reshape(bt, tt, DN_CONV_CH)

    def conv_silu(c0, c1):
        acc = ext_ref[:, pad - tail:pad - tail + tt, c0:c1] * cw_ref[0:1, c0:c1]
        for j in range(1, CONV_W):
            acc = acc + ext_ref[:, pad - tail + j:pad - tail + j + tt, c0:c1] * cw_ref[j:j + 1, c0:c1]
        return _silu(acc).reshape(bt * tt, c1 - c0)

    def l2_heads(out_ref, c0, scale):
        units = []
        for h in range(DN_H):
            ch = conv_silu(c0 + h * DN_DK, c0 + (h + 1) * DN_DK)
            unit = ch * lax.rsqrt(jnp.sum(ch * ch, axis=-1, keepdims=True) + EPS)
            units.append(unit if scale is None else unit * scale)
        out_ref[...] = jnp.concatenate(units, axis=-1).astype(BF16)

    cos = cos_ref[...]
    sin = sin_ref[...]
    lane = lax.broadcasted_iota(jnp.int32, cos.shape, 1)
    first_half = (lane % RET_DK) < (RET_DK // 2)

    def rot(t):
        swapped = jnp.where(first_half, pltpu.roll(t, RET_QK - RET_DK // 2, 1),
                            pltpu.roll(t, RET_DK // 2, 1))
        return t * cos + swapped * sin

    gates = _dot(ub, wg_ref[...])
    ga_ref[...] = gates[:, :D_MODEL].astype(BF16)
    gb_ref[...] = gates[:, D_MODEL:2 * D_MODEL].astype(BF16)
    dab = gates[:, 2 * D_MODEL:]
    l2_heads(dq_ref, 0, DN_DK ** -0.5)
    l2_heads(dk_ref, DN_QK, None)
    v_ref[...] = mm(_C_V, _C_RG).astype(BF16)
    dv_ref[...] = conv_silu(2 * DN_QK, DN_CONV_CH).astype(BF16)
    new_tail = ext_ref[:, pad + tt - tail:pad + tt, :]
    ext_ref[:, pad - tail:pad, :] = new_tail
    c_ref[...] = new_tail
    qk = mm(_C_Q, _C_V)
    q_ref[...] = (rot(qk[:, :RET_QK]) * (RET_DK ** -0.5)).astype(BF16)
    k_ref[...] = rot(qk[:, RET_QK:]).astype(BF16)
    rg_ref[...] = _silu(mm(_C_RG, _C_DQKV)).astype(BF16)
    dz_ref[...] = _silu(mm(_C_DZ, _C_FRONT)).astype(BF16)

    lane_c = lax.broadcasted_iota(jnp.int32, dab.shape, 1)
    g_c = -jnp.exp(alog_c_ref[...]) * _softplus(dab + dtb_c_ref[...])
    gbc_ref[...] = jnp.where(lane_c < DN_H, g_c, jax.nn.sigmoid(dab))
    dabt = dab.T[:GB_ROWS]
    row_r = lax.broadcasted_iota(jnp.int32, dabt.shape, 0)
    g_r = -jnp.exp(alog_r_ref[...]) * _softplus(dabt + dtb_r_ref[...])
    gbr_ref[...] = jnp.where(row_r < DN_H, g_r, jax.nn.sigmoid(dabt))


def _proj_call(x2d, gain, w_in, w_gates, cos_t, sin_t, alog_c, dtb_c, alog_r, dtb_r, conv_w, c0, tm):
    n = x2d.shape[0]
    bsz = c0.shape[0]
    seq = n // bsz
    tt = min(tm, seq)
    bt = tm // tt
    nl = seq // tt
    table_blocks = cos_t.shape[0] // tm
    tok = lambda w: pl.BlockSpec((tm, w), lambda i: (i, 0))
    const = lambda a: pl.BlockSpec(a.shape, lambda i: (0,) * a.ndim, pipeline_mode=pl.Buffered(1))
    tab = pl.BlockSpec((tm, RET_QK), lambda i: (i % table_blocks, 0))
    cst = pl.BlockSpec((bt, CONV_W - 1, DN_CONV_CH), lambda i: (i // nl, 0, 0))
    bf = lambda w: jax.ShapeDtypeStruct((n, w), BF16)
    out_shapes = (bf(RET_QK), bf(RET_QK), bf(RET_VW), bf(RET_VW), bf(DN_QK), bf(DN_QK), bf(DN_VW),
                  bf(DN_VW), bf(D_MODEL), bf(D_MODEL),
                  jax.ShapeDtypeStruct((n, LANES), F32), jax.ShapeDtypeStruct((GB_ROWS, n), F32),
                  jax.ShapeDtypeStruct(c0.shape, F32))
    out_specs = (tok(RET_QK), tok(RET_QK), tok(RET_VW), tok(RET_VW), tok(DN_QK), tok(DN_QK), tok(DN_VW),
                 tok(DN_VW), tok(D_MODEL), tok(D_MODEL), tok(LANES),
                 pl.BlockSpec((GB_ROWS, tm), lambda i: (0, i)), cst)
    return pl.pallas_call(
        functools.partial(_proj_kernel, bt=bt, tt=tt, nl=nl), grid=(n // tm,), name="proj",
        in_specs=[tok(D_MODEL), const(gain),
                  pl.BlockSpec((D_MODEL, _C_FRONT), lambda i: (0, 0), pipeline_mode=pl.Buffered(1)),
                  const(w_gates), tab, tab,
                  const(alog_c), const(dtb_c), const(alog_r), const(dtb_r), const(conv_w), cst],
        out_specs=out_specs, out_shape=out_shapes,
        scratch_shapes=[pltpu.VMEM((bt, tt + SUBLANES, DN_CONV_CH), F32)],
        compiler_params=pltpu.CompilerParams(dimension_semantics=("arbitrary",),
                                             vmem_limit_bytes=VMEM_LIMIT),
    )(x2d, gain, w_in, w_gates, cos_t, sin_t, alog_c, dtb_c, alog_r, dtb_r, conv_w, c0)


def _ret_tile(q_ref, k_ref, v_ref, rg_ref, dec_ref, hm_ref, qd_ref, kd_ref, cd_ref,
              o_ref, s_ref, *, bt, tt, cu):
    r = bt * tt
    blk = min(tt, cu)
    n_units = r // cu
    n_seg = cu // blk

    heads = range(RET_H)
    hrow = lambda a, h, m: a[h * m:(h + 1) * m]
    for n in range(n_units):
        rows = slice(n * cu, (n + 1) * cu)
        qf = q_ref[rows, :].astype(F32)
        kf = k_ref[rows, :].astype(F32)
        vf = v_ref[rows, :].astype(F32)
        q_m = [qf * hm_ref[h] for h in heads]
        q_d = [qf * qd_ref[h] for h in heads]
        k_d = [kf * kd_ref[h] for h in heads]
        v_h = [vf[:, h * RET_DV:(h + 1) * RET_DV] for h in heads]
        scores = (_dot_nt(jnp.concatenate(q_m, axis=0).astype(BF16), k_ref[rows, :])
                  * dec_ref[...]).astype(BF16)
        intra = []
        for h in heads:
            intra.append(_dot(hrow(scores, h, cu), v_h[h].astype(BF16)))
        inters = []
        for j in range(n_seg):
            seg = slice(j * blk, (j + 1) * blk)
            b = (n * cu + j * blk) // tt
            s = s_ref[b]
            q_seg = jnp.concatenate([a[seg] for a in q_d], axis=0).astype(BF16)
            k_seg = jnp.concatenate([a[seg] for a in k_d], axis=0).astype(BF16)
            v_seg = jnp.concatenate([a[seg] for a in v_h], axis=0).astype(BF16)
            inters.append(_dot(q_seg, s.astype(BF16)))
            s_ref[b] = s * cd_ref[...] + _dot_tn(k_seg, v_seg)
        for h in heads:
            inter = jnp.concatenate([hrow(a, h, blk) for a in inters], axis=0)
            gate = rg_ref[rows, h * RET_DV:(h + 1) * RET_DV].astype(F32)
            o_ref[rows, h * RET_DV:(h + 1) * RET_DV] = (
                _rms_plain(intra[h] + inter) * gate).astype(BF16)


def _dn_tile(q_ref, k_ref, v_ref, gc_ref, gr_ref, dz_ref, dnorm_ref, lblk_ref, ublk_ref, lvl_ref,
             o_ref, s_ref, *, bt, tt, cu):
    r = bt * tt
    blk = min(tt, cu)
    n_units = r // cu
    n_seg = cu // blk
    n_lvl = lvl_ref.shape[0]

    ri = lax.broadcasted_iota(jnp.int32, (cu, cu), 0)
    ci = lax.broadcasted_iota(jnp.int32, (cu, cu), 1)
    same = (ri // blk) == (ci // blk)
    incl = same & (ri >= ci)
    strict = same & (ri > ci)

    gc = gc_ref[...]
    g_hi, g_mid, g_lo = _split3(gc)
    r_hi, r_mid, r_lo = _split3(gr_ref[...])
    lblk = lblk_ref[...]
    ublk = ublk_ref[...]

    units = [(n, h) for n in range(n_units) for h in range(DN_H)]
    cum_c, cum_r = [], []
    for n in range(n_units):
        rows = slice(n * cu, (n + 1) * cu)
        cum_c.append(_dot(lblk, g_hi[rows]) + _dot(lblk, g_mid[rows]) + _dot(lblk, g_lo[rows]))
        cum_r.append(_dot(r_hi[:, rows], ublk) + _dot(r_mid[:, rows], ublk) + _dot(r_lo[:, rows], ublk))

    qn, kn, kb, g_col, dec, a_mat, ab, d, rhs = {}, {}, {}, {}, {}, {}, {}, {}, {}
    for u in units:
        n, h = u
        rows = slice(n * cu, (n + 1) * cu)
        qb = q_ref[rows, h * DN_DK:(h + 1) * DN_DK]
        kb[u] = k_ref[rows, h * DN_DK:(h + 1) * DN_DK]
        kn[u] = kb[u].astype(F32)
        vc = v_ref[rows, h * DN_DV:(h + 1) * DN_DV].astype(F32)
        g_col[u] = cum_c[n][:, h:h + 1]
        g_row = cum_r[n][h:h + 1, :]
        beta = gc[rows, DN_H + h:DN_H + h + 1]
        dec[u] = jnp.where(incl, jnp.exp(jnp.where(incl, g_col[u] - g_row, 0.0)), 0.0)
        a_mat[u] = jnp.where(strict, _dot_nt(kb[u], kb[u]) * dec[u] * beta, 0.0)
        ab[u] = a_mat[u].astype(BF16)
        e_g = jnp.exp(g_col[u])
        rhs[u] = jnp.concatenate([beta * vc, beta * e_g * kn[u]], axis=-1)
        qn[u] = (qb, qb.astype(F32) * e_g)
        d[u] = -(lvl_ref[0] * a_mat[u])
    for lv in range(1, n_lvl):
        m = 1 << lv
        by_rows = m % SUBLANES == 0

        def lower(t):
            return t.reshape(cu // (2 * m), 2, m, t.shape[-1])[:, 1].reshape(cu // 2, t.shape[-1])

        for u in units:
            db = d[u].astype(BF16)
            if not by_rows:
                w = a_mat[u] + _dot(db, ab[u])
                z = w + _dot(w.astype(BF16), db)
                d[u] = d[u] - lvl_ref[lv] * z
                continue
            w = lower(a_mat[u]) + _dot(lower(d[u]).astype(BF16), ab[u])
            z = w + _dot(w.astype(BF16), db)
            d_low = lower(d[u]) - lower(lvl_ref[lv]) * z
            d4 = d[u].reshape(cu // (2 * m), 2, m, cu)
            d[u] = jnp.concatenate([d4[:, :1], d_low.reshape(cu // (2 * m), 1, m, cu)],
                                   axis=1).reshape(cu, cu)
    sol, p_mat = {}, {}
    for u in units:
        sol[u] = rhs[u] + _dot(d[u].astype(BF16), rhs[u].astype(BF16))
        p_mat[u] = (_dot_nt(qn[u][0], kb[u]) * dec[u]).astype(BF16)

    for n in range(n_units):
        for h in range(DN_H):
            u = (n, h)
            u_mat, w_mat = sol[u][:, :DN_DV], sol[u][:, DN_DV:]
            qe = qn[u][1]
            deltas, qss = [], []
            for j in range(n_seg):
                r0 = j * blk
                b = (n * cu + r0) // tt
                sb = s_ref[b, h].astype(BF16)
                lhs = jnp.concatenate([w_mat[r0:r0 + blk], qe[r0:r0 + blk]], axis=0).astype(BF16)
                ws_qs = _dot(lhs, sb)
                deltas.append(u_mat[r0:r0 + blk] - ws_qs[:blk])
                qss.append(ws_qs[blk:])
            delta = deltas[0] if n_seg == 1 else jnp.concatenate(deltas, axis=0)
            q_s = qss[0] if n_seg == 1 else jnp.concatenate(qss, axis=0)
            delta_b = delta.astype(BF16)
            o = q_s + _dot(p_mat[u], delta_b)
            for j in range(n_seg):
                r0 = j * blk
                b = (n * cu + r0) // tt
                g_last = g_col[u][r0 + blk - 1:r0 + blk, :]
                k_dec = (kn[u][r0:r0 + blk] * jnp.exp(g_last - g_col[u][r0:r0 + blk])).astype(BF16)
                s_ref[b, h] = s_ref[b, h] * jnp.exp(g_last) + _dot_tn(k_dec, delta_b[r0:r0 + blk])
            gate = dz_ref[n * cu:(n + 1) * cu, h * DN_DV:(h + 1) * DN_DV].astype(F32)
            o_ref[n * cu:(n + 1) * cu, h * DN_DV:(h + 1) * DN_DV] = (
                _rms_plain(o) * dnorm_ref[...] * gate).astype(BF16)


def _seq_kernel(rq_ref, rk_ref, rv_ref, rg_ref, rs0_ref, dec_ref, hm_ref, qd_ref, kd_ref, cd_ref,
                dq_ref, dk_ref, dv_ref, gc_ref, gr_ref, dz_ref, dnorm_ref, ds0_ref, lblk_ref, ublk_ref,
                lvl_ref, *rest, bt, tt, ret_cu, dn_cu, n_cast):
    cast_in, (og_ref, rs_ref, od_ref, ds_ref), cast_out = rest[:n_cast], rest[n_cast:n_cast + 4], rest[n_cast + 4:]
    for src, dst in zip(cast_in, cast_out):
        dst[...] = src[...].astype(BF16)

    @pl.when(pl.program_id(1) == 0)
    def _():
        rs_ref[...] = rs0_ref[...]
        ds_ref[...] = ds0_ref[...]

    _dn_tile(dq_ref, dk_ref, dv_ref, gc_ref, gr_ref, dz_ref, dnorm_ref, lblk_ref, ublk_ref, lvl_ref,
             od_ref, ds_ref, bt=bt, tt=tt, cu=dn_cu)
    _ret_tile(rq_ref, rk_ref, rv_ref, rg_ref, dec_ref, hm_ref, qd_ref, kd_ref, cd_ref,
              og_ref, rs_ref, bt=bt, tt=tt, cu=ret_cu)


def _seq_call(rq, rk, rv, rg, rs0, ret_tables, dq, dk, dv, gbc, gbr, dz, dn_norm, ds0, dn_tables,
              bt, tt, ret_cu, dn_cu, cast_weights=()):
    bsz = ds0.shape[0]
    seq = dq.shape[0] // bsz
    nl = seq // tt
    r = bt * tt
    n_steps = (bsz // bt) * nl
    tok = lambda w: pl.BlockSpec((r, w), lambda b, l: (b * nl + l, 0))
    const = lambda a: pl.BlockSpec(a.shape, lambda b, l: (0,) * a.ndim)
    rst = pl.BlockSpec((bt, RET_QK, RET_DV), lambda b, l: (b, 0, 0))
    dst = pl.BlockSpec((bt, DN_H, DN_DK, DN_DV), lambda b, l: (b, 0, 0, 0))
    slab = lambda a: pl.BlockSpec((a.shape[0] // n_steps,) + a.shape[1:],
                                  lambda b, l: (b * nl + l,) + (0,) * (a.ndim - 1))
    outs = pl.pallas_call(
        functools.partial(_seq_kernel, bt=bt, tt=tt, ret_cu=ret_cu, dn_cu=dn_cu, n_cast=len(cast_weights)),
        grid=(bsz // bt, nl), name="seq",
        in_specs=([tok(RET_QK), tok(RET_QK), tok(RET_VW), tok(RET_VW), rst]
                  + [const(t) for t in ret_tables]
                  + [tok(DN_QK), tok(DN_QK), tok(DN_VW), tok(LANES),
                     pl.BlockSpec((GB_ROWS, r), lambda b, l: (0, b * nl + l)),
                     tok(DN_VW), const(dn_norm), dst] + [const(t) for t in dn_tables]
                  + [slab(a) for a in cast_weights]),
        out_specs=(tok(RET_VW), rst, tok(DN_VW), dst) + tuple(slab(a) for a in cast_weights),
        out_shape=(jax.ShapeDtypeStruct((bsz * seq, RET_VW), BF16), jax.ShapeDtypeStruct(rs0.shape, F32),
                   jax.ShapeDtypeStruct((bsz * seq, DN_VW), BF16), jax.ShapeDtypeStruct(ds0.shape, F32))
        + tuple(jax.ShapeDtypeStruct(a.shape, BF16) for a in cast_weights),
        compiler_params=pltpu.CompilerParams(dimension_semantics=("parallel", "arbitrary"),
                                             vmem_limit_bytes=VMEM_LIMIT),
    )(rq, rk, rv, rg, rs0, *ret_tables, dq, dk, dv, gbc, gbr, dz, dn_norm, ds0, *dn_tables, *cast_weights)
    return outs[:4], outs[4:]


_R_EXP = N_GROUPS


def _ffn_kernel(x_ref, og_ref, od_ref, ga_ref, gb_ref, p_ref,
                wru_ref, wdu_ref, wo_ref, nffn_ref, wr_ref, br_ref, wgu_ref, wdn_ref, nple_ref, wpg_ref,
                wpp_ref, nfin_ref, y_ref, hid_ref):
    branch_a = _dot(og_ref[...], wru_ref[...])
    branch_b = _dot(od_ref[...], wdu_ref[...])
    merged = (jax.nn.sigmoid(ga_ref[...].astype(F32)) * branch_a
              + jax.nn.sigmoid(gb_ref[...].astype(F32)) * branch_b)
    h = x_ref[...] + _dot(merged.astype(BF16), wo_ref[...])

    u = _rms_plain(h) * nffn_ref[...]
    ub = u.astype(BF16)

    logits = _dot(ub, wr_ref[...]) + br_ref[...]
    lane = lax.broadcasted_iota(jnp.int32, logits.shape, 1)
    neg = jnp.float32(-jnp.inf)
    big = jnp.int32(LANES)
    gl = jnp.where(lane < N_GROUPS, logits, neg)
    g_max = jnp.max(gl, axis=-1, keepdims=True)
    p_grp = 1.0 / jnp.sum(jnp.exp(gl - g_max), axis=-1, keepdims=True)
    grp = jnp.min(jnp.where(gl == g_max, lane, big), axis=-1, keepdims=True)
    e_idx = lane - _R_EXP
    in_grp = (e_idx >= 0) & (e_idx < N_EXPERTS) & ((e_idx // EXPERTS_PER_GROUP) == grp)
    el = jnp.where(in_grp, logits, neg)
    v1 = jnp.max(el, axis=-1, keepdims=True)
    i1 = jnp.min(jnp.where(el == v1, lane, big), axis=-1, keepdims=True)
    el2 = jnp.where(lane == i1, neg, el)
    v2 = jnp.max(el2, axis=-1, keepdims=True)
    i2 = jnp.min(jnp.where(el2 == v2, lane, big), axis=-1, keepdims=True)
    e2 = jnp.exp(v2 - v1)
    w1 = 1.0 / (1.0 + e2)
    w2 = e2 / (1.0 + e2)
    combine = jnp.where(lane == i1, w1 * p_grp, 0.0) + jnp.where(lane == i2, w2 * p_grp, 0.0)

    for e in range(N_EXPERTS):
        gu = _dot(ub, wgu_ref[e])
        hid = _silu(gu[:, :D_EXPERT]) * gu[:, D_EXPERT:]
        hid_ref[:, e * D_EXPERT:(e + 1) * D_EXPERT] = (
            combine[:, _R_EXP + e:_R_EXP + e + 1] * hid).astype(BF16)
    h = h + _dot(hid_ref[...], wdn_ref[...])

    u3 = (_rms_plain(h) * nple_ref[...]).astype(BF16)
    gate = jax.nn.sigmoid(_dot(u3, wpg_ref[...]))
    h = h + gate * _dot(p_ref[...].astype(BF16), wpp_ref[...])
    y_ref[...] = _rms_plain(h) * nfin_ref[...]


def _ffn_call(tok_in, weights, tm):
    n = tok_in[0].shape[0]
    tok = lambda a: pl.BlockSpec((tm, a.shape[1]), lambda i: (i, 0))
    const = lambda a: pl.BlockSpec(a.shape, lambda i: (0,) * a.ndim, pipeline_mode=pl.Buffered(1))
    return pl.pallas_call(
        _ffn_kernel, grid=(n // tm,), name="ffn",
        in_specs=[tok(a) for a in tok_in] + [const(w) for w in weights],
        out_specs=tok(tok_in[0]), out_shape=jax.ShapeDtypeStruct((n, D_MODEL), F32),
        scratch_shapes=[pltpu.VMEM((tm, N_EXPERTS * D_EXPERT), BF16)],
        compiler_params=pltpu.CompilerParams(dimension_semantics=("parallel",),
                                             vmem_limit_bytes=VMEM_LIMIT),
    )(*tok_in, *weights)


_SEQ_CAST = ("w_gate_up", "w_down", "w_out", "w_ple_gate", "w_ret_up", "w_dn_up")

def _rope_tables(pos):
    half = RET_DK // 2
    inv = 1.0 / (ROPE_BASE ** (jnp.arange(half, dtype=F32) / half))
    ang = pos.astype(F32)[:, None] * inv[None, :]
    cos, sin = jnp.cos(ang), jnp.sin(ang)
    cos_t = jnp.tile(jnp.concatenate([cos, cos], axis=-1), (1, RET_H))
    sin_t = jnp.tile(jnp.concatenate([-sin, sin], axis=-1), (1, RET_H))
    return cos_t, sin_t


def _ret_tables(r, c):
    log_gamma = jnp.log(1.0 - 2.0 ** (-5.0 - jnp.arange(RET_H, dtype=F32)))
    row = jnp.arange(r)
    idx = (row % c).astype(F32)
    diff = idx[:, None] - idx[None, :]
    causal = (diff >= 0) & ((row[:, None] // c) == (row[None, :] // c))
    lg = log_gamma[:, None, None]
    decay = jnp.where(causal, jnp.exp(jnp.where(causal, diff, 0.0) * lg), 0.0)
    q_dec = jnp.exp((idx + 1.0)[None, :] * log_gamma[:, None])[..., None]
    k_dec = jnp.exp((c - 1.0 - idx)[None, :] * log_gamma[:, None])[..., None]
    chunk_dec = jnp.exp(c * log_gamma)
    lane_head = np.arange(RET_QK) // RET_DK
    head_mask = jnp.asarray((lane_head[None, :] == np.arange(RET_H)[:, None])[:, None, :], F32)
    return (decay.reshape(RET_H * r, r), head_mask, q_dec * head_mask, k_dec * head_mask,
            jnp.broadcast_to(chunk_dec[lane_head][:, None], (RET_QK, RET_DV)))


def _dn_tables(r, c):
    i = np.arange(r)
    ri, ci = i[:, None], i[None, :]
    same = (ri // c) == (ci // c)
    lblk = ((ri >= ci) & same).astype(np.float32)
    x = ri ^ ci
    lvl, m = [], 1
    while m < c:
        lvl.append((same & (ri > ci) & (x >= m) & (x < 2 * m)).astype(np.float32))
        m *= 2
    return jnp.asarray(lblk, BF16), jnp.asarray(lblk.T, BF16), jnp.asarray(np.stack(lvl))


def _token_mixers(x, p, s_ret, s_dn, s_conv, pos_table, wts, cfg, cast_weights=()):
    bsz, seq, _ = x.shape
    n = bsz * seq
    x2d = x.reshape(n, D_MODEL)
    cos_t, sin_t = pos_table
    (q, k, v, rg, dq, dk, dv, dz, ga, gb, gbc, gbr, conv_new) = _proj_call(
        x2d, wts["norm_mix"], wts["w_in"], wts["w_gates"], cos_t, sin_t,
        wts["alog_c"], wts["dtb_c"], wts["alog_r"], wts["dtb_r"], wts["conv_w"], s_conv, cfg["tm_proj"])

    ret_tables = _ret_tables(cfg["ret_cu"], min(cfg["tt"], cfg["ret_cu"]))
    dn_tables = _dn_tables(cfg["dn_cu"], min(cfg["tt"], cfg["dn_cu"]))
    (og, ret_new, od, dn_new), cast_out = _seq_call(
        q, k, v, rg, s_ret.reshape(bsz, RET_QK, RET_DV), ret_tables,
        dq, dk, dv, gbc, gbr, dz, wts["dn_norm"], s_dn, dn_tables,
        cfg["bt"], cfg["tt"], cfg["ret_cu"], cfg["dn_cu"], cast_weights)
    ret_new = ret_new.reshape(bsz, RET_H, RET_DK, RET_DV)

    return (x2d, og, od, ga, gb, p.reshape(n, PLE_DIM)), ret_new, dn_new, conv_new, cast_out


def _prep_weights(norm_mix, w_in, conv_w, dn_a_log, dn_dt_bias, dn_norm, w_ret_up, w_dn_up, w_out,
                  norm_ffn, w_router_group, b_router_group, w_router_expert, b_router_expert,
                  w_gate_up, w_down, norm_ple, w_ple_gate, w_ple_proj, norm_final):
    assert sum(IN_WIDTHS[:6]) == _C_FRONT and sum(IN_WIDTHS[:8]) == _C_GATES
    w_in = w_in.astype(BF16)
    w_gates = jnp.pad(jnp.concatenate([w_in[:, _C_GATES:], w_in[:, _C_FRONT:_C_GATES]], axis=-1),
                      ((0, 0), (0, LANES - GB_ROWS)))
    pad_lanes = lambda a: jnp.pad(a.astype(F32), (0, LANES - a.shape[0]))[None, :]
    pad_rows = lambda a: jnp.pad(a.astype(F32), (0, GB_ROWS - a.shape[0]))[:, None]
    w_router = jnp.pad(jnp.concatenate([w_router_group, w_router_expert], axis=-1),
                       ((0, 0), (0, LANES - N_GROUPS - N_EXPERTS))).astype(BF16)
    b_router = pad_lanes(jnp.concatenate([b_router_group, b_router_expert]))
    row = lambda a: a.astype(F32)[None, :]
    return dict(
        norm_mix=row(norm_mix), w_in=w_in, w_gates=w_gates,
        alog_c=pad_lanes(dn_a_log), dtb_c=pad_lanes(dn_dt_bias),
        alog_r=pad_rows(dn_a_log), dtb_r=pad_rows(dn_dt_bias),
        conv_w=conv_w.astype(F32), dn_norm=row(dn_norm),
        w_ret_up=w_ret_up, w_dn_up=w_dn_up, w_out=w_out,
        norm_ffn=row(norm_ffn), w_router=w_router, b_router=b_router,
        w_gate_up=w_gate_up, w_down=w_down,
        norm_ple=row(norm_ple), w_ple_gate=w_ple_gate, w_ple_proj=w_ple_proj.astype(BF16),
        norm_final=row(norm_final),
    )


def kernel(x_prompt, x_sample, p_prompt, p_sample, state_ret, state_dn, state_conv, norm_mix, w_in, conv_w, dn_a_log, dn_dt_bias, dn_norm, w_ret_up, w_dn_up, w_out, norm_ffn, w_router_group, b_router_group, w_router_expert, b_router_expert, w_gate_up, w_down, norm_ple, w_ple_gate, w_ple_proj, norm_final):
    depth = w_in.shape[0]
    assert depth == 1, "one layer: the final norm is fused into the layer's last kernel"
    bp, lp, _ = x_prompt.shape
    bs, ls, _ = x_sample.shape
    wts = _prep_weights(norm_mix[0], w_in[0], conv_w[0], dn_a_log[0], dn_dt_bias[0], dn_norm[0],
                        w_ret_up[0], w_dn_up[0], w_out[0], norm_ffn[0], w_router_group[0],
                        b_router_group[0], w_router_expert[0], b_router_expert[0], w_gate_up[0],
                        w_down[0], norm_ple[0], w_ple_gate[0], w_ple_proj[0], norm_final)

    tm = 512
    bt_s = LANES // ls
    cfg_p = dict(tm=tm, tm_proj=256, bt=1, tt=512, ret_cu=256, dn_cu=128)
    cfg_s = dict(tm=tm, tm_proj=256, bt=bt_s, tt=ls, ret_cu=bt_s * ls, dn_cu=bt_s * ls)

    pos_p = _rope_tables(jnp.arange(lp, dtype=jnp.int32))
    cos_s, sin_s = _rope_tables(PAST_LEN + jnp.arange(ls, dtype=jnp.int32))
    pos_s = (jnp.tile(cos_s, (bs, 1)), jnp.tile(sin_s, (bs, 1)))

    zero_ret = jnp.zeros((bp, RET_H, RET_DK, RET_DV), F32)
    zero_dn = jnp.zeros((bp, DN_H, DN_DK, DN_DV), F32)
    zero_conv = jnp.zeros((bp, CONV_W - 1, DN_CONV_CH), F32)

    seq_steps_p = (bp // cfg_p["bt"]) * (lp // cfg_p["tt"])
    slabs = lambda w: w.reshape((seq_steps_p, -1) + w.shape[-1:])
    tok_p, r_p, d_p, c_p, cast_out = _token_mixers(
        x_prompt, p_prompt[0], zero_ret, zero_dn, zero_conv, pos_p, wts, cfg_p,
        cast_weights=tuple(slabs(wts[k]) for k in _SEQ_CAST))
    tok_s, r_s, d_s, c_s, _ = _token_mixers(x_sample, p_sample[0], state_ret[0], state_dn[0], state_conv[0],
                                            pos_s, wts, cfg_s)
    for k, w in zip(_SEQ_CAST, cast_out):
        wts[k] = w.reshape(wts[k].shape)
    wts["w_down"] = wts["w_down"].reshape(N_EXPERTS * D_EXPERT, D_MODEL)
    ffn_weights = [wts[k] for k in ("w_ret_up", "w_dn_up", "w_out", "norm_ffn", "w_router", "b_router",
                                    "w_gate_up", "w_down", "norm_ple", "w_ple_gate", "w_ple_proj",
                                    "norm_final")]
    y_p = _ffn_call(tok_p, ffn_weights, tm)
    y_s = _ffn_call(tok_s, ffn_weights, tm)
    return (y_p.reshape(x_prompt.shape), y_s.reshape(x_sample.shape),
            r_p[None], d_p[None], c_p[None], r_s[None], d_s[None], c_s[None])
```

```python
import functools

import jax
import jax.numpy as jnp
import numpy as np
from jax import lax
from jax.experimental import pallas as pl
from jax.experimental.pallas import tpu as pltpu

F32 = jnp.float32
BF16 = jnp.bfloat16

D_MODEL = 1024
RET_H, RET_DK, RET_DV = 4, 64, 128
DN_H, DN_DK, DN_DV = 4, 128, 128
CONV_W = 4
ROPE_BASE = 10000.0
PAST_LEN = 16384
N_GROUPS, EXPERTS_PER_GROUP = 4, 4
N_EXPERTS = N_GROUPS * EXPERTS_PER_GROUP
D_EXPERT = 256
PLE_DIM = 256
EPS = 1e-6

RET_QK = RET_H * RET_DK
RET_VW = RET_H * RET_DV
DN_QK = DN_H * DN_DK
DN_VW = DN_H * DN_DV
DN_CONV_CH = 2 * DN_QK + DN_VW
IN_WIDTHS = (RET_QK, RET_QK, RET_VW, RET_VW, DN_CONV_CH, DN_VW, DN_H, DN_H, D_MODEL, D_MODEL)

LANES = 128
SUBLANES = 8
VMEM_LIMIT = 56 * 1024 * 1024
GB_ROWS = 2 * DN_H
assert GB_ROWS == SUBLANES

_C_Q, _C_K, _C_V, _C_RG = 0, 256, 512, 1024
_C_DQKV, _C_DZ, _C_FRONT = 1536, 3072, 3584
_C_GATES = _C_FRONT + GB_ROWS


def _silu(x):
    return x * jax.nn.sigmoid(x)


def _softplus(x):
    return jnp.maximum(x, 0.0) + jnp.log1p(jnp.exp(-jnp.abs(x)))


def _dot(a, b):
    return jnp.dot(a, b, preferred_element_type=F32)


def _dot_nt(a, b):
    return lax.dot_general(a, b, (((1,), (1,)), ((), ())), preferred_element_type=F32)


def _dot_tn(a, b):
    return lax.dot_general(a, b, (((0,), (0,)), ((), ())), preferred_element_type=F32)


def _split3(x):
    hi = x.astype(BF16)
    r = x - hi.astype(F32)
    mid = r.astype(BF16)
    lo = (r - mid.astype(F32)).astype(BF16)
    return hi, mid, lo


def _rms_plain(x):
    return x * lax.rsqrt(jnp.mean(x * x, axis=-1, keepdims=True) + EPS)


def _proj_kernel(x_ref, gain_ref, w_ref, wg_ref, cos_ref, sin_ref,
                 alog_c_ref, dtb_c_ref, alog_r_ref, dtb_r_ref, cw_ref, c0_ref,
                 q_ref, k_ref, v_ref, rg_ref, dq_ref, dk_ref, dv_ref, dz_ref, ga_ref, gb_ref,
                 gbc_ref, gbr_ref, c_ref, ext_ref, *, bt, tt, nl):
    pad = SUBLANES
    tail = CONV_W - 1

    @pl.when(pl.program_id(0) % nl == 0)
    def _():
        ext_ref[:, pad - tail:pad, :] = c0_ref[...]

    x = x_ref[...]
    u = x * lax.rsqrt(jnp.mean(x * x, axis=-1, keepdims=True) + EPS) * gain_ref[...]
    ub = u.astype(BF16)

    def mm(lo, hi):
        return _dot(ub, w_ref[:, lo:hi])

    ext_ref[:, pad:pad + tt, :] = mm(_C_DQKV, _C_DZ).reshape(bt, tt, DN_CONV_CH)

    def conv_silu(c0, c1):
        acc = ext_ref[:, pad - tail:pad - tail + tt, c0:c1] * cw_ref[0:1, c0:c1]
        for j in range(1, CONV_W):
            acc = acc + ext_ref[:, pad - tail + j:pad - tail + j + tt, c0:c1] * cw_ref[j:j + 1, c0:c1]
        return _silu(acc).reshape(bt * tt, c1 - c0)

    def l2_heads(out_ref, c0, scale):
        units = []
        for h in range(DN_H):
            ch = conv_silu(c0 + h * DN_DK, c0 + (h + 1) * DN_DK)
            unit = ch * lax.rsqrt(jnp.sum(ch * ch, axis=-1, keepdims=True) + EPS)
            units.append(unit if scale is None else unit * scale)
        out_ref[...] = jnp.concatenate(units, axis=-1).astype(BF16)

    cos = cos_ref[...]
    sin = sin_ref[...]
    lane = lax.broadcasted_iota(jnp.int32, cos.shape, 1)
    first_half = (lane % RET_DK) < (RET_DK // 2)

    def rot(t):
        swapped = jnp.where(first_half, pltpu.roll(t, RET_QK - RET_DK // 2, 1),
                            pltpu.roll(t, RET_DK // 2, 1))
        return t * cos + swapped * sin

    gates = _dot(ub, wg_ref[...])
    ga_ref[...] = gates[:, :D_MODEL].astype(BF16)
    gb_ref[...] = gates[:, D_MODEL:2 * D_MODEL].astype(BF16)
    dab = gates[:, 2 * D_MODEL:]
    l2_heads(dq_ref, 0, DN_DK ** -0.5)
    l2_heads(dk_ref, DN_QK, None)
    v_ref[...] = mm(_C_V, _C_RG).astype(BF16)
    dv_ref[...] = conv_silu(2 * DN_QK, DN_CONV_CH).astype(BF16)
    new_tail = ext_ref[:, pad + tt - tail:pad + tt, :]
    ext_ref[:, pad - tail:pad, :] = new_tail
    c_ref[...] = new_tail
    qk = mm(_C_Q, _C_V)
    q_ref[...] = (rot(qk[:, :RET_QK]) * (RET_DK ** -0.5)).astype(BF16)
    k_ref[...] = rot(qk[:, RET_QK:]).astype(BF16)
    rg_ref[...] = _silu(mm(_C_RG, _C_DQKV)).astype(BF16)
    dz_ref[...] = _silu(mm(_C_DZ, _C_FRONT)).astype(BF16)

    lane_c = lax.broadcasted_iota(jnp.int32, dab.shape, 1)
    g_c = -jnp.exp(alog_c_ref[...]) * _softplus(dab + dtb_c_ref[...])
    gbc_ref[...] = jnp.where(lane_c < DN_H, g_c, jax.nn.sigmoid(dab))
    dabt = dab.T[:GB_ROWS]
    row_r = lax.broadcasted_iota(jnp.int32, dabt.shape, 0)
    g_r = -jnp.exp(alog_r_ref[...]) * _softplus(dabt + dtb_r_ref[...])
    gbr_ref[...] = jnp.where(row_r < DN_H, g_r, jax.nn.sigmoid(dabt))


def _proj_call(x2d, gain, w_in, w_gates, cos_t, sin_t, alog_c, dtb_c, alog_r, dtb_r, conv_w, c0, tm):
    n = x2d.shape[0]
    bsz = c0.shape[0]
    seq = n // bsz
    tt = min(tm, seq)
    bt = tm // tt
    nl = seq // tt
    table_blocks = cos_t.shape[0] // tm
    tok = lambda w: pl.BlockSpec((tm, w), lambda i: (i, 0))
    const = lambda a: pl.BlockSpec(a.shape, lambda i: (0,) * a.ndim, pipeline_mode=pl.Buffered(1))
    tab = pl.BlockSpec((tm, RET_QK), lambda i: (i % table_blocks, 0))
    cst = pl.BlockSpec((bt, CONV_W - 1, DN_CONV_CH), lambda i: (i // nl, 0, 0))
    bf = lambda w: jax.ShapeDtypeStruct((n, w), BF16)
    out_shapes = (bf(RET_QK), bf(RET_QK), bf(RET_VW), bf(RET_VW), bf(DN_QK), bf(DN_QK), bf(DN_VW),
                  bf(DN_VW), bf(D_MODEL), bf(D_MODEL),
                  jax.ShapeDtypeStruct((n, LANES), F32), jax.ShapeDtypeStruct((GB_ROWS, n), F32),
                  jax.ShapeDtypeStruct(c0.shape, F32))
    out_specs = (tok(RET_QK), tok(RET_QK), tok(RET_VW), tok(RET_VW), tok(DN_QK), tok(DN_QK), tok(DN_VW),
                 tok(DN_VW), tok(D_MODEL), tok(D_MODEL), tok(LANES),
                 pl.BlockSpec((GB_ROWS, tm), lambda i: (0, i)), cst)
    return pl.pallas_call(
        functools.partial(_proj_kernel, bt=bt, tt=tt, nl=nl), grid=(n // tm,), name="proj",
        in_specs=[tok(D_MODEL), const(gain),
                  pl.BlockSpec((D_MODEL, _C_FRONT), lambda i: (0, 0), pipeline_mode=pl.Buffered(1)),
                  const(w_gates), tab, tab,
                  const(alog_c), const(dtb_c), const(alog_r), const(dtb_r), const(conv_w), cst],
        out_specs=out_specs, out_shape=out_shapes,
        scratch_shapes=[pltpu.VMEM((bt, tt + SUBLANES, DN_CONV_CH), F32)],
        compiler_params=pltpu.CompilerParams(dimension_semantics=("arbitrary",),
                                             vmem_limit_bytes=VMEM_LIMIT),
    )(x2d, gain, w_in, w_gates, cos_t, sin_t, alog_c, dtb_c, alog_r, dtb_r, conv_w, c0)


def _ret_tile(q_ref, k_ref, v_ref, rg_ref, dec_ref, hm_ref, qd_ref, kd_ref, cd_ref,
              o_ref, s_ref, *, bt, tt, cu):
    r = bt * tt
    blk = min(tt, cu)
    n_units = r // cu
    n_seg = cu // blk

    heads = range(RET_H)
    hrow = lambda a, h, m: a[h * m:(h + 1) * m]
    for n in range(n_units):
        rows = slice(n * cu, (n + 1) * cu)
        qf = q_ref[rows, :].astype(F32)
        kf = k_ref[rows, :].astype(F32)
        vf = v_ref[rows, :].astype(F32)
        q_m = [qf * hm_ref[h] for h in heads]
        q_d = [qf * qd_ref[h] for h in heads]
        k_d = [kf * kd_ref[h] for h in heads]
        v_h = [vf[:, h * RET_DV:(h + 1) * RET_DV] for h in heads]
        scores = (_dot_nt(jnp.concatenate(q_m, axis=0).astype(BF16), k_ref[rows, :])
                  * dec_ref[...]).astype(BF16)
        intra = []
        for h in heads:
            intra.append(_dot(hrow(scores, h, cu), v_h[h].astype(BF16)))
        inters = []
        for j in range(n_seg):
            seg = slice(j * blk, (j + 1) * blk)
            b = (n * cu + j * blk) // tt
            s = s_ref[b]
            q_seg = jnp.concatenate([a[seg] for a in q_d], axis=0).astype(BF16)
            k_seg = jnp.concatenate([a[seg] for a in k_d], axis=0).astype(BF16)
            v_seg = jnp.concatenate([a[seg] for a in v_h], axis=0).astype(BF16)
            inters.append(_dot(q_seg, s.astype(BF16)))
            s_ref[b] = s * cd_ref[...] + _dot_tn(k_seg, v_seg)
        for h in heads:
            inter = jnp.concatenate([hrow(a, h, blk) for a in inters], axis=0)
            gate = rg_ref[rows, h * RET_DV:(h + 1) * RET_DV].astype(F32)
            o_ref[rows, h * RET_DV:(h + 1) * RET_DV] = (
                _rms_plain(intra[h] + inter) * gate).astype(BF16)


def _dn_tile(q_ref, k_ref, v_ref, gc_ref, gr_ref, dz_ref, dnorm_ref, lblk_ref, ublk_ref, lvl_ref,
             o_ref, s_ref, *, bt, tt, cu):
    r = bt * tt
    blk = min(tt, cu)
    n_units = r // cu
    n_seg = cu // blk
    n_lvl = lvl_ref.shape[0]

    ri = lax.broadcasted_iota(jnp.int32, (cu, cu), 0)
    ci = lax.broadcasted_iota(jnp.int32, (cu, cu), 1)
    same = (ri // blk) == (ci // blk)
    incl = same & (ri >= ci)
    strict = same & (ri > ci)

    gc = gc_ref[...]
    g_hi, g_mid, g_lo = _split3(gc)
    r_hi, r_mid, r_lo = _split3(gr_ref[...])
    lblk = lblk_ref[...]
    ublk = ublk_ref[...]

    units = [(n, h) for n in range(n_units) for h in range(DN_H)]
    cum_c, cum_r = [], []
    for n in range(n_units):
        rows = slice(n * cu, (n + 1) * cu)
        cum_c.append(_dot(lblk, g_hi[rows]) + _dot(lblk, g_mid[rows]) + _dot(lblk, g_lo[rows]))
        cum_r.append(_dot(r_hi[:, rows], ublk) + _dot(r_mid[:, rows], ublk) + _dot(r_lo[:, rows], ublk))

    qn, kn, kb, g_col, dec, a_mat, ab, d, rhs = {}, {}, {}, {}, {}, {}, {}, {}, {}
    for u in units:
        n, h = u
        rows = slice(n * cu, (n + 1) * cu)
        qb = q_ref[rows, h * DN_DK:(h + 1) * DN_DK]
        kb[u] = k_ref[rows, h * DN_DK:(h + 1) * DN_DK]
        kn[u] = kb[u].astype(F32)
        vc = v_ref[rows, h * DN_DV:(h + 1) * DN_DV].astype(F32)
        g_col[u] = cum_c[n][:, h:h + 1]
        g_row = cum_r[n][h:h + 1, :]
        beta = gc[rows, DN_H + h:DN_H + h + 1]
        dec[u] = jnp.where(incl, jnp.exp(jnp.where(incl, g_col[u] - g_row, 0.0)), 0.0)
        a_mat[u] = jnp.where(strict, _dot_nt(kb[u], kb[u]) * dec[u] * beta, 0.0)
        ab[u] = a_mat[u].astype(BF16)
        e_g = jnp.exp(g_col[u])
        rhs[u] = jnp.concatenate([beta * vc, beta * e_g * kn[u]], axis=-1)
        qn[u] = (qb, qb.astype(F32) * e_g)
        d[u] = -(lvl_ref[0] * a_mat[u])
    for lv in range(1, n_lvl):
        m = 1 << lv
        by_rows = m % SUBLANES == 0

        def lower(t):
            return t.reshape(cu // (2 * m), 2, m, t.shape[-1])[:, 1].reshape(cu // 2, t.shape[-1])

        for u in units:
            db = d[u].astype(BF16)
            if not by_rows:
                w = a_mat[u] + _dot(db, ab[u])
                z = w + _dot(w.astype(BF16), db)
                d[u] = d[u] - lvl_ref[lv] * z
                continue
            w = lower(a_mat[u]) + _dot(lower(d[u]).astype(BF16), ab[u])
            z = w + _dot(w.astype(BF16), db)
            d_low = lower(d[u]) - lower(lvl_ref[lv]) * z
            d4 = d[u].reshape(cu // (2 * m), 2, m, cu)
            d[u] = jnp.concatenate([d4[:, :1], d_low.reshape(cu // (2 * m), 1, m, cu)],
                                   axis=1).reshape(cu, cu)
    sol, p_mat = {}, {}
    for u in units:
        sol[u] = rhs[u] + _dot(d[u].astype(BF16), rhs[u].astype(BF16))
        p_mat[u] = (_dot_nt(qn[u][0], kb[u]) * dec[u]).astype(BF16)

    for n in range(n_units):
        for h in range(DN_H):
            u = (n, h)
            u_mat, w_mat = sol[u][:, :DN_DV], sol[u][:, DN_DV:]
            qe = qn[u][1]
            deltas, qss = [], []
            for j in range(n_seg):
                r0 = j * blk
                b = (n * cu + r0) // tt
                sb = s_ref[b, h].astype(BF16)
                lhs = jnp.concatenate([w_mat[r0:r0 + blk], qe[r0:r0 + blk]], axis=0).astype(BF16)
                ws_qs = _dot(lhs, sb)
                deltas.append(u_mat[r0:r0 + blk] - ws_qs[:blk])
                qss.append(ws_qs[blk:])
            delta = deltas[0] if n_seg == 1 else jnp.concatenate(deltas, axis=0)
            q_s = qss[0] if n_seg == 1 else jnp.concatenate(qss, axis=0)
            delta_b = delta.astype(BF16)
            o = q_s + _dot(p_mat[u], delta_b)
            for j in range(n_seg):
                r0 = j * blk
                b = (n * cu + r0) // tt
                g_last = g_col[u][r0 + blk - 1:r0 + blk, :]
                k_dec = (kn[u][r0:r0 + blk] * jnp.exp(g_last - g_col[u][r0:r0 + blk])).astype(BF16)
                s_ref[b, h] = s_ref[b, h] * jnp.exp(g_last) + _dot_tn(k_dec, delta_b[r0:r0 + blk])
            gate = dz_ref[n * cu:(n + 1) * cu, h * DN_DV:(h + 1) * DN_DV].astype(F32)
            o_ref[n * cu:(n + 1) * cu, h * DN_DV:(h + 1) * DN_DV] = (
                _rms_plain(o) * dnorm_ref[...] * gate).astype(BF16)


def _seq_kernel(rq_ref, rk_ref, rv_ref, rg_ref, rs0_ref, dec_ref, hm_ref, qd_ref, kd_ref, cd_ref,
                dq_ref, dk_ref, dv_ref, gc_ref, gr_ref, dz_ref, dnorm_ref, ds0_ref, lblk_ref, ublk_ref,
                lvl_ref, *rest, bt, tt, ret_cu, dn_cu, n_cast):
    cast_in, (og_ref, rs_ref, od_ref, ds_ref), cast_out = rest[:n_cast], rest[n_cast:n_cast + 4], rest[n_cast + 4:]
    for src, dst in zip(cast_in, cast_out):
        dst[...] = src[...].astype(BF16)

    @pl.when(pl.program_id(1) == 0)
    def _():
        rs_ref[...] = rs0_ref[...]
        ds_ref[...] = ds0_ref[...]

    _dn_tile(dq_ref, dk_ref, dv_ref, gc_ref, gr_ref, dz_ref, dnorm_ref, lblk_ref, ublk_ref, lvl_ref,
             od_ref, ds_ref, bt=bt, tt=tt, cu=dn_cu)
    _ret_tile(rq_ref, rk_ref, rv_ref, rg_ref, dec_ref, hm_ref, qd_ref, kd_ref, cd_ref,
              og_ref, rs_ref, bt=bt, tt=tt, cu=ret_cu)


def _seq_call(rq, rk, rv, rg, rs0, ret_tables, dq, dk, dv, gbc, gbr, dz, dn_norm, ds0, dn_tables,
              bt, tt, ret_cu, dn_cu, cast_weights=()):
    bsz = ds0.shape[0]
    seq = dq.shape[0] // bsz
    nl = seq // tt
    r = bt * tt
    n_steps = (bsz // bt) * nl
    tok = lambda w: pl.BlockSpec((r, w), lambda b, l: (b * nl + l, 0))
    const = lambda a: pl.BlockSpec(a.shape, lambda b, l: (0,) * a.ndim)
    rst = pl.BlockSpec((bt, RET_QK, RET_DV), lambda b, l: (b, 0, 0))
    dst = pl.BlockSpec((bt, DN_H, DN_DK, DN_DV), lambda b, l: (b, 0, 0, 0))
    slab = lambda a: pl.BlockSpec((a.shape[0] // n_steps,) + a.shape[1:],
                                  lambda b, l: (b * nl + l,) + (0,) * (a.ndim - 1))
    outs = pl.pallas_call(
        functools.partial(_seq_kernel, bt=bt, tt=tt, ret_cu=ret_cu, dn_cu=dn_cu, n_cast=len(cast_weights)),
        grid=(bsz // bt, nl), name="seq",
        in_specs=([tok(RET_QK), tok(RET_QK), tok(RET_VW), tok(RET_VW), rst]
                  + [const(t) for t in ret_tables]
                  + [tok(DN_QK), tok(DN_QK), tok(DN_VW), tok(LANES),
                     pl.BlockSpec((GB_ROWS, r), lambda b, l: (0, b * nl + l)),
                     tok(DN_VW), const(dn_norm), dst] + [const(t) for t in dn_tables]
                  + [slab(a) for a in cast_weights]),
        out_specs=(tok(RET_VW), rst, tok(DN_VW), dst) + tuple(slab(a) for a in cast_weights),
        out_shape=(jax.ShapeDtypeStruct((bsz * seq, RET_VW), BF16), jax.ShapeDtypeStruct(rs0.shape, F32),
                   jax.ShapeDtypeStruct((bsz * seq, DN_VW), BF16), jax.ShapeDtypeStruct(ds0.shape, F32))
        + tuple(jax.ShapeDtypeStruct(a.shape, BF16) for a in cast_weights),
        compiler_params=pltpu.CompilerParams(dimension_semantics=("parallel", "arbitrary"),
                                             vmem_limit_bytes=VMEM_LIMIT),
    )(rq, rk, rv, rg, rs0, *ret_tables, dq, dk, dv, gbc, gbr, dz, dn_norm, ds0, *dn_tables, *cast_weights)
    return outs[:4], outs[4:]


_R_EXP = N_GROUPS


def _ffn_kernel(x_ref, og_ref, od_ref, ga_ref, gb_ref, p_ref,
                wru_ref, wdu_ref, wo_ref, nffn_ref, wr_ref, br_ref, wgu_ref, wdn_ref, nple_ref, wpg_ref,
                wpp_ref, nfin_ref, y_ref, hid_ref):
    branch_a = _dot(og_ref[...], wru_ref[...])
    branch_b = _dot(od_ref[...], wdu_ref[...])
    merged = (jax.nn.sigmoid(ga_ref[...].astype(F32)) * branch_a
              + jax.nn.sigmoid(gb_ref[...].astype(F32)) * branch_b)
    h = x_ref[...] + _dot(merged.astype(BF16), wo_ref[...])

    u = _rms_plain(h) * nffn_ref[...]
    ub = u.astype(BF16)

    logits = _dot(ub, wr_ref[...]) + br_ref[...]
    lane = lax.broadcasted_iota(jnp.int32, logits.shape, 1)
    neg = jnp.float32(-jnp.inf)
    big = jnp.int32(LANES)
    gl = jnp.where(lane < N_GROUPS, logits, neg)
    g_max = jnp.max(gl, axis=-1, keepdims=True)
    p_grp = 1.0 / jnp.sum(jnp.exp(gl - g_max), axis=-1, keepdims=True)
    grp = jnp.min(jnp.where(gl == g_max, lane, big), axis=-1, keepdims=True)
    e_idx = lane - _R_EXP
    in_grp = (e_idx >= 0) & (e_idx < N_EXPERTS) & ((e_idx // EXPERTS_PER_GROUP) == grp)
    el = jnp.where(in_grp, logits, neg)
    v1 = jnp.max(el, axis=-1, keepdims=True)
    i1 = jnp.min(jnp.where(el == v1, lane, big), axis=-1, keepdims=True)
    el2 = jnp.where(lane == i1, neg, el)
    v2 = jnp.max(el2, axis=-1, keepdims=True)
    i2 = jnp.min(jnp.where(el2 == v2, lane, big), axis=-1, keepdims=True)
    e2 = jnp.exp(v2 - v1)
    w1 = 1.0 / (1.0 + e2)
    w2 = e2 / (1.0 + e2)
    combine = jnp.where(lane == i1, w1 * p_grp, 0.0) + jnp.where(lane == i2, w2 * p_grp, 0.0)

    for e in range(N_EXPERTS):
        gu = _dot(ub, wgu_ref[e])
        hid = _silu(gu[:, :D_EXPERT]) * gu[:, D_EXPERT:]
        hid_ref[:, e * D_EXPERT:(e + 1) * D_EXPERT] = (
            combine[:, _R_EXP + e:_R_EXP + e + 1] * hid).astype(BF16)
    h = h + _dot(hid_ref[...], wdn_ref[...])

    u3 = (_rms_plain(h) * nple_ref[...]).astype(BF16)
    gate = jax.nn.sigmoid(_dot(u3, wpg_ref[...]))
    h = h + gate * _dot(p_ref[...].astype(BF16), wpp_ref[...])
    y_ref[...] = _rms_plain(h) * nfin_ref[...]


def _ffn_call(tok_in, weights, tm):
    n = tok_in[0].shape[0]
    tok = lambda a: pl.BlockSpec((tm, a.shape[1]), lambda i: (i, 0))
    const = lambda a: pl.BlockSpec(a.shape, lambda i: (0,) * a.ndim, pipeline_mode=pl.Buffered(1))
    return pl.pallas_call(
        _ffn_kernel, grid=(n // tm,), name="ffn",
        in_specs=[tok(a) for a in tok_in] + [const(w) for w in weights],
        out_specs=tok(tok_in[0]), out_shape=jax.ShapeDtypeStruct((n, D_MODEL), F32),
        scratch_shapes=[pltpu.VMEM((tm, N_EXPERTS * D_EXPERT), BF16)],
        compiler_params=pltpu.CompilerParams(dimension_semantics=("parallel",),
                                             vmem_limit_bytes=VMEM_LIMIT),
    )(*tok_in, *weights)


_SEQ_CAST = ("w_gate_up", "w_down", "w_out", "w_ple_gate", "w_ret_up", "w_dn_up")

def _rope_tables(pos):
    half = RET_DK // 2
    inv = 1.0 / (ROPE_BASE ** (jnp.arange(half, dtype=F32) / half))
    ang = pos.astype(F32)[:, None] * inv[None, :]
    cos, sin = jnp.cos(ang), jnp.sin(ang)
    cos_t = jnp.tile(jnp.concatenate([cos, cos], axis=-1), (1, RET_H))
    sin_t = jnp.tile(jnp.concatenate([-sin, sin], axis=-1), (1, RET_H))
    return cos_t, sin_t


def _ret_tables(r, c):
    f32 = np.float32
    log_gamma = np.log(f32(1.0) - f32(2.0) ** (f32(-5.0) - np.arange(RET_H, dtype=f32)))
    row = np.arange(r)
    idx = (row % c).astype(f32)
    diff = idx[:, None] - idx[None, :]
    causal = (diff >= 0) & ((row[:, None] // c) == (row[None, :] // c))
    lg = log_gamma[:, None, None]
    decay = np.where(causal, np.exp(np.where(causal, diff, f32(0.0)) * lg), f32(0.0))
    q_dec = np.exp((idx + f32(1.0))[None, :] * log_gamma[:, None])[..., None]
    k_dec = np.exp((f32(c - 1.0) - idx)[None, :] * log_gamma[:, None])[..., None]
    chunk_dec = np.exp(f32(c) * log_gamma)
    lane_head = np.arange(RET_QK) // RET_DK
    head_mask = (lane_head[None, :] == np.arange(RET_H)[:, None]).astype(f32)[:, None, :]
    tables = (decay.reshape(RET_H * r, r), head_mask, q_dec * head_mask, k_dec * head_mask,
              np.broadcast_to(chunk_dec[lane_head][:, None], (RET_QK, RET_DV)))
    return tuple(jnp.asarray(t, F32) for t in tables)


def _dn_tables(r, c):
    i = np.arange(r)
    ri, ci = i[:, None], i[None, :]
    same = (ri // c) == (ci // c)
    lblk = ((ri >= ci) & same).astype(np.float32)
    x = ri ^ ci
    lvl, m = [], 1
    while m < c:
        lvl.append((same & (ri > ci) & (x >= m) & (x < 2 * m)).astype(np.float32))
        m *= 2
    return jnp.asarray(lblk, BF16), jnp.asarray(lblk.T, BF16), jnp.asarray(np.stack(lvl))


def _token_mixers(x, p, s_ret, s_dn, s_conv, pos_table, wts, cfg, cast_weights=()):
    bsz, seq, _ = x.shape
    n = bsz * seq
    x2d = x.reshape(n, D_MODEL)
    cos_t, sin_t = pos_table
    (q, k, v, rg, dq, dk, dv, dz, ga, gb, gbc, gbr, conv_new) = _proj_call(
        x2d, wts["norm_mix"], wts["w_in"], wts["w_gates"], cos_t, sin_t,
        wts["alog_c"], wts["dtb_c"], wts["alog_r"], wts["dtb_r"], wts["conv_w"], s_conv, cfg["tm_proj"])

    ret_tables = _ret_tables(cfg["ret_cu"], min(cfg["tt"], cfg["ret_cu"]))
    dn_tables = _dn_tables(cfg["dn_cu"], min(cfg["tt"], cfg["dn_cu"]))
    (og, ret_new, od, dn_new), cast_out = _seq_call(
        q, k, v, rg, s_ret.reshape(bsz, RET_QK, RET_DV), ret_tables,
        dq, dk, dv, gbc, gbr, dz, wts["dn_norm"], s_dn, dn_tables,
        cfg["bt"], cfg["tt"], cfg["ret_cu"], cfg["dn_cu"], cast_weights)
    ret_new = ret_new.reshape(bsz, RET_H, RET_DK, RET_DV)

    return (x2d, og, od, ga, gb, p.reshape(n, PLE_DIM)), ret_new, dn_new, conv_new, cast_out


def _prep_weights(norm_mix, w_in, conv_w, dn_a_log, dn_dt_bias, dn_norm, w_ret_up, w_dn_up, w_out,
                  norm_ffn, w_router_group, b_router_group, w_router_expert, b_router_expert,
                  w_gate_up, w_down, norm_ple, w_ple_gate, w_ple_proj, norm_final):
    assert sum(IN_WIDTHS[:6]) == _C_FRONT and sum(IN_WIDTHS[:8]) == _C_GATES
    w_in = w_in.astype(BF16)
    w_gates = jnp.pad(jnp.concatenate([w_in[:, _C_GATES:], w_in[:, _C_FRONT:_C_GATES]], axis=-1),
                      ((0, 0), (0, LANES - GB_ROWS)))
    pad_lanes = lambda a: jnp.pad(a.astype(F32), (0, LANES - a.shape[0]))[None, :]
    pad_rows = lambda a: jnp.pad(a.astype(F32), (0, GB_ROWS - a.shape[0]))[:, None]
    w_router = jnp.pad(jnp.concatenate([w_router_group, w_router_expert], axis=-1),
                       ((0, 0), (0, LANES - N_GROUPS - N_EXPERTS))).astype(BF16)
    b_router = pad_lanes(jnp.concatenate([b_router_group, b_router_expert]))
    row = lambda a: a.astype(F32)[None, :]
    return dict(
        norm_mix=row(norm_mix), w_in=w_in, w_gates=w_gates,
        alog_c=pad_lanes(dn_a_log), dtb_c=pad_lanes(dn_dt_bias),
        alog_r=pad_rows(dn_a_log), dtb_r=pad_rows(dn_dt_bias),
        conv_w=conv_w.astype(F32), dn_norm=row(dn_norm),
        w_ret_up=w_ret_up, w_dn_up=w_dn_up, w_out=w_out,
        norm_ffn=row(norm_ffn), w_router=w_router, b_router=b_router,
        w_gate_up=w_gate_up, w_down=w_down,
        norm_ple=row(norm_ple), w_ple_gate=w_ple_gate, w_ple_proj=w_ple_proj.astype(BF16),
        norm_final=row(norm_final),
    )


def kernel(x_prompt, x_sample, p_prompt, p_sample, state_ret, state_dn, state_conv, norm_mix, w_in, conv_w, dn_a_log, dn_dt_bias, dn_norm, w_ret_up, w_dn_up, w_out, norm_ffn, w_router_group, b_router_group, w_router_expert, b_router_expert, w_gate_up, w_down, norm_ple, w_ple_gate, w_ple_proj, norm_final):
    depth = w_in.shape[0]
    assert depth == 1, "one layer: the final norm is fused into the layer's last kernel"
    bp, lp, _ = x_prompt.shape
    bs, ls, _ = x_sample.shape
    wts = _prep_weights(norm_mix[0], w_in[0], conv_w[0], dn_a_log[0], dn_dt_bias[0], dn_norm[0],
                        w_ret_up[0], w_dn_up[0], w_out[0], norm_ffn[0], w_router_group[0],
                        b_router_group[0], w_router_expert[0], b_router_expert[0], w_gate_up[0],
                        w_down[0], norm_ple[0], w_ple_gate[0], w_ple_proj[0], norm_final)

    tm = 512
    bt_s = LANES // ls
    cfg_p = dict(tm=tm, tm_proj=256, bt=1, tt=512, ret_cu=256, dn_cu=128)
    cfg_s = dict(tm=tm, tm_proj=256, bt=bt_s, tt=ls, ret_cu=bt_s * ls, dn_cu=bt_s * ls)

    pos_p = _rope_tables(jnp.arange(lp, dtype=jnp.int32))
    cos_s, sin_s = _rope_tables(PAST_LEN + jnp.arange(ls, dtype=jnp.int32))
    pos_s = (jnp.tile(cos_s, (bs, 1)), jnp.tile(sin_s, (bs, 1)))

    zero_ret = jnp.zeros((bp, RET_H, RET_DK, RET_DV), F32)
    zero_dn = jnp.zeros((bp, DN_H, DN_DK, DN_DV), F32)
    zero_conv = jnp.zeros((bp, CONV_W - 1, DN_CONV_CH), F32)

    seq_steps_p = (bp // cfg_p["bt"]) * (lp // cfg_p["tt"])
    slabs = lambda w: w.reshape((seq_steps_p, -1) + w.shape[-1:])
    tok_p, r_p, d_p, c_p, cast_out = _token_mixers(
        x_prompt, p_prompt[0], zero_ret, zero_dn, zero_conv, pos_p, wts, cfg_p,
        cast_weights=tuple(slabs(wts[k]) for k in _SEQ_CAST))
    tok_s, r_s, d_s, c_s, _ = _token_mixers(x_sample, p_sample[0], state_ret[0], state_dn[0], state_conv[0],
                                            pos_s, wts, cfg_s)
    for k, w in zip(_SEQ_CAST, cast_out):
        wts[k] = w.reshape(wts[k].shape)
    wts["w_down"] = wts["w_down"].reshape(N_EXPERTS * D_EXPERT, D_MODEL)
    ffn_weights = [wts[k] for k in ("w_ret_up", "w_dn_up", "w_out", "norm_ffn", "w_router", "b_router",
                                    "w_gate_up", "w_down", "norm_ple", "w_ple_gate", "w_ple_proj",
                                    "norm_final")]
    y_p = _ffn_call(tok_p, ffn_weights, tm)
    y_s = _ffn_call(tok_s, ffn_weights, tm)
    return (y_p.reshape(x_prompt.shape), y_s.reshape(x_sample.shape),
            r_p[None], d_p[None], c_p[None], r_s[None], d_s[None], c_s[None])
```

```python
import functools

import jax
import jax.numpy as jnp
import numpy as np
from jax import lax
from jax.experimental import pallas as pl
from jax.experimental.pallas import tpu as pltpu

F32 = jnp.float32
BF16 = jnp.bfloat16

D_MODEL = 1024
RET_H, RET_DK, RET_DV = 4, 64, 128
DN_H, DN_DK, DN_DV = 4, 128, 128
CONV_W = 4
ROPE_BASE = 10000.0
PAST_LEN = 16384
N_GROUPS, EXPERTS_PER_GROUP = 4, 4
N_EXPERTS = N_GROUPS * EXPERTS_PER_GROUP
D_EXPERT = 256
PLE_DIM = 256
EPS = 1e-6

RET_QK = RET_H * RET_DK
RET_VW = RET_H * RET_DV
DN_QK = DN_H * DN_DK
DN_VW = DN_H * DN_DV
DN_CONV_CH = 2 * DN_QK + DN_VW
IN_WIDTHS = (RET_QK, RET_QK, RET_VW, RET_VW, DN_CONV_CH, DN_VW, DN_H, DN_H, D_MODEL, D_MODEL)

LANES = 128
SUBLANES = 8
VMEM_LIMIT = 56 * 1024 * 1024
GB_ROWS = 2 * DN_H
assert GB_ROWS == SUBLANES

_C_Q, _C_K, _C_V, _C_RG = 0, 256, 512, 1024
_C_DQKV, _C_DZ, _C_FRONT = 1536, 3072, 3584
_C_GATES = _C_FRONT + GB_ROWS


def _silu(x):
    return x * jax.nn.sigmoid(x)


def _softplus(x):
    return jnp.maximum(x, 0.0) + jnp.log1p(jnp.exp(-jnp.abs(x)))


def _dot(a, b):
    return jnp.dot(a, b, preferred_element_type=F32)


def _dot_nt(a, b):
    return lax.dot_general(a, b, (((1,), (1,)), ((), ())), preferred_element_type=F32)


def _dot_tn(a, b):
    return lax.dot_general(a, b, (((0,), (0,)), ((), ())), preferred_element_type=F32)


def _split3(x):
    hi = x.astype(BF16)
    r = x - hi.astype(F32)
    mid = r.astype(BF16)
    lo = (r - mid.astype(F32)).astype(BF16)
    return hi, mid, lo


def _rms_plain(x):
    return x * lax.rsqrt(jnp.mean(x * x, axis=-1, keepdims=True) + EPS)


def _proj_kernel(x_ref, gain_ref, w_ref, wg_ref, cos_ref, sin_ref,
                 alog_c_ref, dtb_c_ref, alog_r_ref, dtb_r_ref, cw_ref, c0_ref,
                 q_ref, k_ref, v_ref, rg_ref, dq_ref, dk_ref, dv_ref, dz_ref, ga_ref, gb_ref,
                 gbc_ref, gbr_ref, c_ref, ext_ref, *, bt, tt, nl):
    pad = SUBLANES
    tail = CONV_W - 1

    @pl.when(pl.program_id(0) % nl == 0)
    def _():
        ext_ref[:, pad - tail:pad, :] = c0_ref[...]

    x = x_ref[...]
    u = x * lax.rsqrt(jnp.mean(x * x, axis=-1, keepdims=True) + EPS) * gain_ref[...]
    ub = u.astype(BF16)

    def mm(lo, hi):
        return _dot(ub, w_ref[:, lo:hi])

    ext_ref[:, pad:pad + tt, :] = mm(_C_DQKV, _C_DZ).reshape(bt, tt, DN_CONV_CH)

    def conv_silu(c0, c1):
        acc = ext_ref[:, pad - tail:pad - tail + tt, c0:c1] * cw_ref[0:1, c0:c1]
        for j in range(1, CONV_W):
            acc = acc + ext_ref[:, pad - tail + j:pad - tail + j + tt, c0:c1] * cw_ref[j:j + 1, c0:c1]
        return _silu(acc).reshape(bt * tt, c1 - c0)

    def l2_heads(out_ref, c0, scale):
        units = []
        for h in range(DN_H):
            ch = conv_silu(c0 + h * DN_DK, c0 + (h + 1) * DN_DK)
            unit = ch * lax.rsqrt(jnp.sum(ch * ch, axis=-1, keepdims=True) + EPS)
            units.append(unit if scale is None else unit * scale)
        out_ref[...] = jnp.concatenate(units, axis=-1).astype(BF16)

    cos = cos_ref[...]
    sin = sin_ref[...]
    lane = lax.broadcasted_iota(jnp.int32, cos.shape, 1)
    first_half = (lane % RET_DK) < (RET_DK // 2)

    def rot(t):
        swapped = jnp.where(first_half, pltpu.roll(t, RET_QK - RET_DK // 2, 1),
                            pltpu.roll(t, RET_DK // 2, 1))
        return t * cos + swapped * sin

    gates = _dot(ub, wg_ref[...])
    ga_ref[...] = gates[:, :D_MODEL].astype(BF16)
    gb_ref[...] = gates[:, D_MODEL:2 * D_MODEL].astype(BF16)
    dab = gates[:, 2 * D_MODEL:]
    l2_heads(dq_ref, 0, DN_DK ** -0.5)
    l2_heads(dk_ref, DN_QK, None)
    v_ref[...] = mm(_C_V, _C_RG).astype(BF16)
    dv_ref[...] = conv_silu(2 * DN_QK, DN_CONV_CH).astype(BF16)
    new_tail = ext_ref[:, pad + tt - tail:pad + tt, :]
    ext_ref[:, pad - tail:pad, :] = new_tail
    c_ref[...] = new_tail
    qk = mm(_C_Q, _C_V)
    q_ref[...] = (rot(qk[:, :RET_QK]) * (RET_DK ** -0.5)).astype(BF16)
    k_ref[...] = rot(qk[:, RET_QK:]).astype(BF16)
    rg_ref[...] = _silu(mm(_C_RG, _C_DQKV)).astype(BF16)
    dz_ref[...] = _silu(mm(_C_DZ, _C_FRONT)).astype(BF16)

    lane_c = lax.broadcasted_iota(jnp.int32, dab.shape, 1)
    g_c = -jnp.exp(alog_c_ref[...]) * _softplus(dab + dtb_c_ref[...])
    gbc_ref[...] = jnp.where(lane_c < DN_H, g_c, jax.nn.sigmoid(dab))
    dabt = dab.T[:GB_ROWS]
    row_r = lax.broadcasted_iota(jnp.int32, dabt.shape, 0)
    g_r = -jnp.exp(alog_r_ref[...]) * _softplus(dabt + dtb_r_ref[...])
    gbr_ref[...] = jnp.where(row_r < DN_H, g_r, jax.nn.sigmoid(dabt))


def _proj_call(x2d, gain, w_in, w_gates, cos_t, sin_t, alog_c, dtb_c, alog_r, dtb_r, conv_w, c0, tm):
    n = x2d.shape[0]
    bsz = c0.shape[0]
    seq = n // bsz
    tt = min(tm, seq)
    bt = tm // tt
    nl = seq // tt
    table_blocks = cos_t.shape[0] // tm
    tok = lambda w: pl.BlockSpec((tm, w), lambda i: (i, 0))
    const = lambda a: pl.BlockSpec(a.shape, lambda i: (0,) * a.ndim, pipeline_mode=pl.Buffered(1))
    tab = pl.BlockSpec((tm, RET_QK), lambda i: (i % table_blocks, 0))
    cst = pl.BlockSpec((bt, CONV_W - 1, DN_CONV_CH), lambda i: (i // nl, 0, 0))
    bf = lambda w: jax.ShapeDtypeStruct((n, w), BF16)
    out_shapes = (bf(RET_QK), bf(RET_QK), bf(RET_VW), bf(RET_VW), bf(DN_QK), bf(DN_QK), bf(DN_VW),
                  bf(DN_VW), bf(D_MODEL), bf(D_MODEL),
                  jax.ShapeDtypeStruct((n, LANES), F32), jax.ShapeDtypeStruct((GB_ROWS, n), F32),
                  jax.ShapeDtypeStruct(c0.shape, F32))
    out_specs = (tok(RET_QK), tok(RET_QK), tok(RET_VW), tok(RET_VW), tok(DN_QK), tok(DN_QK), tok(DN_VW),
                 tok(DN_VW), tok(D_MODEL), tok(D_MODEL), tok(LANES),
                 pl.BlockSpec((GB_ROWS, tm), lambda i: (0, i)), cst)
    return pl.pallas_call(
        functools.partial(_proj_kernel, bt=bt, tt=tt, nl=nl), grid=(n // tm,), name="proj",
        in_specs=[tok(D_MODEL), const(gain),
                  pl.BlockSpec((D_MODEL, _C_FRONT), lambda i: (0, 0), pipeline_mode=pl.Buffered(1)),
                  const(w_gates), tab, tab,
                  const(alog_c), const(dtb_c), const(alog_r), const(dtb_r), const(conv_w), cst],
        out_specs=out_specs, out_shape=out_shapes,
        scratch_shapes=[pltpu.VMEM((bt, tt + SUBLANES, DN_CONV_CH), F32)],
        compiler_params=pltpu.CompilerParams(dimension_semantics=("arbitrary",),
                                             vmem_limit_bytes=VMEM_LIMIT),
    )(x2d, gain, w_in, w_gates, cos_t, sin_t, alog_c, dtb_c, alog_r, dtb_r, conv_w, c0)


def _ret_tile(q_ref, k_ref, v_ref, rg_ref, dec_ref, hm_ref, qd_ref, kd_ref, cd_ref,
              o_ref, s_ref, *, bt, tt, cu):
    r = bt * tt
    blk = min(tt, cu)
    n_units = r // cu
    n_seg = cu // blk

    heads = range(RET_H)
    hrow = lambda a, h, m: a[h * m:(h + 1) * m]
    for n in range(n_units):
        rows = slice(n * cu, (n + 1) * cu)
        qf = q_ref[rows, :].astype(F32)
        kf = k_ref[rows, :].astype(F32)
        vf = v_ref[rows, :].astype(F32)
        q_m = [qf * hm_ref[h] for h in heads]
        q_d = [qf * qd_ref[h] for h in heads]
        k_d = [kf * kd_ref[h] for h in heads]
        v_h = [vf[:, h * RET_DV:(h + 1) * RET_DV] for h in heads]
        scores = (_dot_nt(jnp.concatenate(q_m, axis=0).astype(BF16), k_ref[rows, :])
                  * dec_ref[...]).astype(BF16)
        intra = []
        for h in heads:
            intra.append(_dot(hrow(scores, h, cu), v_h[h].astype(BF16)))
        inters = []
        for j in range(n_seg):
            seg = slice(j * blk, (j + 1) * blk)
            b = (n * cu + j * blk) // tt
            s = s_ref[b]
            q_seg = jnp.concatenate([a[seg] for a in q_d], axis=0).astype(BF16)
            k_seg = jnp.concatenate([a[seg] for a in k_d], axis=0).astype(BF16)
            v_seg = jnp.concatenate([a[seg] for a in v_h], axis=0).astype(BF16)
            inters.append(_dot(q_seg, s.astype(BF16)))
            s_ref[b] = s * cd_ref[...] + _dot_tn(k_seg, v_seg)
        for h in heads:
            inter = jnp.concatenate([hrow(a, h, blk) for a in inters], axis=0)
            gate = rg_ref[rows, h * RET_DV:(h + 1) * RET_DV].astype(F32)
            o_ref[rows, h * RET_DV:(h + 1) * RET_DV] = (
                _rms_plain(intra[h] + inter) * gate).astype(BF16)


def _dn_tile(q_ref, k_ref, v_ref, gc_ref, gr_ref, dz_ref, dnorm_ref, lblk_ref, ublk_ref, lvl_ref,
             o_ref, s_ref, *, bt, tt, cu):
    r = bt * tt
    blk = min(tt, cu)
    n_units = r // cu
    n_seg = cu // blk
    n_lvl = lvl_ref.shape[0]

    ri = lax.broadcasted_iota(jnp.int32, (cu, cu), 0)
    ci = lax.broadcasted_iota(jnp.int32, (cu, cu), 1)
    same = (ri // blk) == (ci // blk)
    incl = same & (ri >= ci)
    strict = same & (ri > ci)

    gc = gc_ref[...]
    g_hi, g_mid, g_lo = _split3(gc)
    r_hi, r_mid, r_lo = _split3(gr_ref[...])
    lblk = lblk_ref[...]
    ublk = ublk_ref[...]

    units = [(n, h) for n in range(n_units) for h in range(DN_H)]
    cum_c, cum_r = [], []
    for n in range(n_units):
        rows = slice(n * cu, (n + 1) * cu)
        cum_c.append(_dot(lblk, g_hi[rows]) + _dot(lblk, g_mid[rows]) + _dot(lblk, g_lo[rows]))
        cum_r.append(_dot(r_hi[:, rows], ublk) + _dot(r_mid[:, rows], ublk) + _dot(r_lo[:, rows], ublk))

    qn, kn, kb, g_col, dec, a_mat, ab, d, rhs = {}, {}, {}, {}, {}, {}, {}, {}, {}
    for u in units:
        n, h = u
        rows = slice(n * cu, (n + 1) * cu)
        qb = q_ref[rows, h * DN_DK:(h + 1) * DN_DK]
        kb[u] = k_ref[rows, h * DN_DK:(h + 1) * DN_DK]
        kn[u] = kb[u].astype(F32)
        vc = v_ref[rows, h * DN_DV:(h + 1) * DN_DV].astype(F32)
        g_col[u] = cum_c[n][:, h:h + 1]
        g_row = cum_r[n][h:h + 1, :]
        beta = gc[rows, DN_H + h:DN_H + h + 1]
        dec[u] = jnp.where(incl, jnp.exp(jnp.where(incl, g_col[u] - g_row, 0.0)), 0.0)
        a_mat[u] = jnp.where(strict, _dot_nt(kb[u], kb[u]) * dec[u] * beta, 0.0)
        ab[u] = a_mat[u].astype(BF16)
        e_g = jnp.exp(g_col[u])
        rhs[u] = jnp.concatenate([beta * vc, beta * e_g * kn[u]], axis=-1)
        qn[u] = (qb, qb.astype(F32) * e_g)
        d[u] = -(lvl_ref[0] * a_mat[u])
    for lv in range(1, n_lvl):
        m = 1 << lv
        by_rows = m % SUBLANES == 0

        def lower(t):
            return t.reshape(cu // (2 * m), 2, m, t.shape[-1])[:, 1].reshape(cu // 2, t.shape[-1])

        for u in units:
            db = d[u].astype(BF16)
            if not by_rows:
                w = a_mat[u] + _dot(db, ab[u])
                z = w + _dot(w.astype(BF16), db)
                d[u] = d[u] - lvl_ref[lv] * z
                continue
            w = lower(a_mat[u]) + _dot(lower(d[u]).astype(BF16), ab[u])
            z = w + _dot(w.astype(BF16), db)
            d_low = lower(d[u]) - lower(lvl_ref[lv]) * z
            d4 = d[u].reshape(cu // (2 * m), 2, m, cu)
            d[u] = jnp.concatenate([d4[:, :1], d_low.reshape(cu // (2 * m), 1, m, cu)],
                                   axis=1).reshape(cu, cu)
    sol, p_mat = {}, {}
    for u in units:
        sol[u] = rhs[u] + _dot(d[u].astype(BF16), rhs[u].astype(BF16))
        p_mat[u] = (_dot_nt(qn[u][0], kb[u]) * dec[u]).astype(BF16)

    for n in range(n_units):
        for h in range(DN_H):
            u = (n, h)
            u_mat, w_mat = sol[u][:, :DN_DV], sol[u][:, DN_DV:]
            qe = qn[u][1]
            deltas, qss = [], []
            for j in range(n_seg):
                r0 = j * blk
                b = (n * cu + r0) // tt
                sb = s_ref[b, h].astype(BF16)
                lhs = jnp.concatenate([w_mat[r0:r0 + blk], qe[r0:r0 + blk]], axis=0).astype(BF16)
                ws_qs = _dot(lhs, sb)
                deltas.append(u_mat[r0:r0 + blk] - ws_qs[:blk])
                qss.append(ws_qs[blk:])
            delta = deltas[0] if n_seg == 1 else jnp.concatenate(deltas, axis=0)
            q_s = qss[0] if n_seg == 1 else jnp.concatenate(qss, axis=0)
            delta_b = delta.astype(BF16)
            o = q_s + _dot(p_mat[u], delta_b)
            for j in range(n_seg):
                r0 = j * blk
                b = (n * cu + r0) // tt
                g_last = g_col[u][r0 + blk - 1:r0 + blk, :]
                k_dec = (kn[u][r0:r0 + blk] * jnp.exp(g_last - g_col[u][r0:r0 + blk])).astype(BF16)
                s_ref[b, h] = s_ref[b, h] * jnp.exp(g_last) + _dot_tn(k_dec, delta_b[r0:r0 + blk])
            gate = dz_ref[n * cu:(n + 1) * cu, h * DN_DV:(h + 1) * DN_DV].astype(F32)
            o_ref[n * cu:(n + 1) * cu, h * DN_DV:(h + 1) * DN_DV] = (
                _rms_plain(o) * dnorm_ref[...] * gate).astype(BF16)


def _seq_kernel(rq_ref, rk_ref, rv_ref, rg_ref, rs0_ref, dec_ref, hm_ref, qd_ref, kd_ref, cd_ref,
                dq_ref, dk_ref, dv_ref, gc_ref, gr_ref, dz_ref, dnorm_ref, ds0_ref, lblk_ref, ublk_ref,
                lvl_ref, *rest, bt, tt, ret_cu, dn_cu, n_cast):
    cast_in, (og_ref, rs_ref, od_ref, ds_ref), cast_out = rest[:n_cast], rest[n_cast:n_cast + 4], rest[n_cast + 4:]
    for src, dst in zip(cast_in, cast_out):
        dst[...] = src[...].astype(BF16)

    @pl.when(pl.program_id(1) == 0)
    def _():
        rs_ref[...] = rs0_ref[...]
        ds_ref[...] = ds0_ref[...]

    _dn_tile(dq_ref, dk_ref, dv_ref, gc_ref, gr_ref, dz_ref, dnorm_ref, lblk_ref, ublk_ref, lvl_ref,
             od_ref, ds_ref, bt=bt, tt=tt, cu=dn_cu)
    _ret_tile(rq_ref, rk_ref, rv_ref, rg_ref, dec_ref, hm_ref, qd_ref, kd_ref, cd_ref,
              og_ref, rs_ref, bt=bt, tt=tt, cu=ret_cu)


def _seq_call(rq, rk, rv, rg, rs0, ret_tables, dq, dk, dv, gbc, gbr, dz, dn_norm, ds0, dn_tables,
              bt, tt, ret_cu, dn_cu, cast_weights=()):
    bsz = ds0.shape[0]
    seq = dq.shape[0] // bsz
    nl = seq // tt
    r = bt * tt
    n_steps = (bsz // bt) * nl
    tok = lambda w: pl.BlockSpec((r, w), lambda b, l: (b * nl + l, 0))
    const = lambda a: pl.BlockSpec(a.shape, lambda b, l: (0,) * a.ndim)
    rst = pl.BlockSpec((bt, RET_QK, RET_DV), lambda b, l: (b, 0, 0))
    dst = pl.BlockSpec((bt, DN_H, DN_DK, DN_DV), lambda b, l: (b, 0, 0, 0))
    slab = lambda a: pl.BlockSpec((a.shape[0] // n_steps,) + a.shape[1:],
                                  lambda b, l: (b * nl + l,) + (0,) * (a.ndim - 1))
    outs = pl.pallas_call(
        functools.partial(_seq_kernel, bt=bt, tt=tt, ret_cu=ret_cu, dn_cu=dn_cu, n_cast=len(cast_weights)),
        grid=(bsz // bt, nl), name="seq",
        in_specs=([tok(RET_QK), tok(RET_QK), tok(RET_VW), tok(RET_VW), rst]
                  + [const(t) for t in ret_tables]
                  + [tok(DN_QK), tok(DN_QK), tok(DN_VW), tok(LANES),
                     pl.BlockSpec((GB_ROWS, r), lambda b, l: (0, b * nl + l)),
                     tok(DN_VW), const(dn_norm), dst] + [const(t) for t in dn_tables]
                  + [slab(a) for a in cast_weights]),
        out_specs=(tok(RET_VW), rst, tok(DN_VW), dst) + tuple(slab(a) for a in cast_weights),
        out_shape=(jax.ShapeDtypeStruct((bsz * seq, RET_VW), BF16), jax.ShapeDtypeStruct(rs0.shape, F32),
                   jax.ShapeDtypeStruct((bsz * seq, DN_VW), BF16), jax.ShapeDtypeStruct(ds0.shape, F32))
        + tuple(jax.ShapeDtypeStruct(a.shape, BF16) for a in cast_weights),
        compiler_params=pltpu.CompilerParams(dimension_semantics=("parallel", "arbitrary"),
                                             vmem_limit_bytes=VMEM_LIMIT),
    )(rq, rk, rv, rg, rs0, *ret_tables, dq, dk, dv, gbc, gbr, dz, dn_norm, ds0, *dn_tables, *cast_weights)
    return outs[:4], outs[4:]


_R_EXP = N_GROUPS


def _ffn_kernel(x_ref, og_ref, od_ref, ga_ref, gb_ref, p_ref,
                wru_ref, wdu_ref, wo_ref, nffn_ref, wr_ref, br_ref, wgu_ref, wdn_ref, nple_ref, wpg_ref,
                wpp_ref, nfin_ref, y_ref, hid_ref):
    branch_a = _dot(og_ref[...], wru_ref[...])
    branch_b = _dot(od_ref[...], wdu_ref[...])
    merged = (jax.nn.sigmoid(ga_ref[...].astype(F32)) * branch_a
              + jax.nn.sigmoid(gb_ref[...].astype(F32)) * branch_b)
    h = x_ref[...] + _dot(merged.astype(BF16), wo_ref[...])

    u = _rms_plain(h) * nffn_ref[...]
    ub = u.astype(BF16)

    logits = _dot(ub, wr_ref[...]) + br_ref[...]
    lane = lax.broadcasted_iota(jnp.int32, logits.shape, 1)
    neg = jnp.float32(-jnp.inf)
    big = jnp.int32(LANES)
    gl = jnp.where(lane < N_GROUPS, logits, neg)
    g_max = jnp.max(gl, axis=-1, keepdims=True)
    p_grp = 1.0 / jnp.sum(jnp.exp(gl - g_max), axis=-1, keepdims=True)
    grp = jnp.min(jnp.where(gl == g_max, lane, big), axis=-1, keepdims=True)
    e_idx = lane - _R_EXP
    in_grp = (e_idx >= 0) & (e_idx < N_EXPERTS) & ((e_idx // EXPERTS_PER_GROUP) == grp)
    el = jnp.where(in_grp, logits, neg)
    v1 = jnp.max(el, axis=-1, keepdims=True)
    i1 = jnp.min(jnp.where(el == v1, lane, big), axis=-1, keepdims=True)
    el2 = jnp.where(lane == i1, neg, el)
    v2 = jnp.max(el2, axis=-1, keepdims=True)
    i2 = jnp.min(jnp.where(el2 == v2, lane, big), axis=-1, keepdims=True)
    e2 = jnp.exp(v2 - v1)
    w1 = 1.0 / (1.0 + e2)
    w2 = e2 / (1.0 + e2)
    combine = jnp.where(lane == i1, w1 * p_grp, 0.0) + jnp.where(lane == i2, w2 * p_grp, 0.0)

    for e in range(N_EXPERTS):
        gu = _dot(ub, wgu_ref[e])
        hid = _silu(gu[:, :D_EXPERT]) * gu[:, D_EXPERT:]
        hid_ref[:, e * D_EXPERT:(e + 1) * D_EXPERT] = (
            combine[:, _R_EXP + e:_R_EXP + e + 1] * hid).astype(BF16)
    h = h + _dot(hid_ref[...], wdn_ref[...])

    u3 = (_rms_plain(h) * nple_ref[...]).astype(BF16)
    gate = jax.nn.sigmoid(_dot(u3, wpg_ref[...]))
    h = h + gate * _dot(p_ref[...].astype(BF16), wpp_ref[...])
    y_ref[...] = _rms_plain(h) * nfin_ref[...]


def _ffn_call(tok_in, weights, tm):
    n = tok_in[0].shape[0]
    tok = lambda a: pl.BlockSpec((tm, a.shape[1]), lambda i: (i, 0))
    const = lambda a: pl.BlockSpec(a.shape, lambda i: (0,) * a.ndim, pipeline_mode=pl.Buffered(1))
    return pl.pallas_call(
        _ffn_kernel, grid=(n // tm,), name="ffn",
        in_specs=[tok(a) for a in tok_in] + [const(w) for w in weights],
        out_specs=tok(tok_in[0]), out_shape=jax.ShapeDtypeStruct((n, D_MODEL), F32),
        scratch_shapes=[pltpu.VMEM((tm, N_EXPERTS * D_EXPERT), BF16)],
        compiler_params=pltpu.CompilerParams(dimension_semantics=("parallel",),
                                             vmem_limit_bytes=VMEM_LIMIT),
    )(*tok_in, *weights)


_SEQ_CAST = ("w_gate_up", "w_down", "w_out", "w_ple_gate", "w_ret_up", "w_dn_up")

def _rope_tables(pos):
    half = RET_DK // 2
    inv = 1.0 / (ROPE_BASE ** (jnp.arange(half, dtype=F32) / half))
    ang = pos.astype(F32)[:, None] * inv[None, :]
    cos, sin = jnp.cos(ang), jnp.sin(ang)
    cos_t = jnp.tile(jnp.concatenate([cos, cos], axis=-1), (1, RET_H))
    sin_t = jnp.tile(jnp.concatenate([-sin, sin], axis=-1), (1, RET_H))
    return cos_t, sin_t


def _ret_tables(r, c):
    f32 = np.float32
    log_gamma = np.log(f32(1.0) - f32(2.0) ** (f32(-5.0) - np.arange(RET_H, dtype=f32)))
    row = np.arange(r)
    idx = (row % c).astype(f32)
    diff = idx[:, None] - idx[None, :]
    causal = (diff >= 0) & ((row[:, None] // c) == (row[None, :] // c))
    lg = log_gamma[:, None, None]
    decay = np.where(causal, np.exp(np.where(causal, diff, f32(0.0)) * lg), f32(0.0))
    q_dec = np.exp((idx + f32(1.0))[None, :] * log_gamma[:, None])[..., None]
    k_dec = np.exp((f32(c - 1.0) - idx)[None, :] * log_gamma[:, None])[..., None]
    chunk_dec = np.exp(f32(c) * log_gamma)
    lane_head = np.arange(RET_QK) // RET_DK
    head_mask = (lane_head[None, :] == np.arange(RET_H)[:, None]).astype(f32)[:, None, :]
    tables = (decay.reshape(RET_H * r, r), head_mask, q_dec * head_mask, k_dec * head_mask,
              np.broadcast_to(chunk_dec[lane_head][:, None], (RET_QK, RET_DV)))
    return tuple(jnp.asarray(t, F32) for t in tables)


def _dn_tables(r, c):
    i = np.arange(r)
    ri, ci = i[:, None], i[None, :]
    same = (ri // c) == (ci // c)
    lblk = ((ri >= ci) & same).astype(np.float32)
    x = ri ^ ci
    lvl, m = [], 1
    while m < c:
        lvl.append((same & (ri > ci) & (x >= m) & (x < 2 * m)).astype(np.float32))
        m *= 2
    return jnp.asarray(lblk, BF16), jnp.asarray(lblk.T, BF16), jnp.asarray(np.stack(lvl))


def _token_mixers(x, p, s_ret, s_dn, s_conv, pos_table, wts, cfg, cast_weights=()):
    bsz, seq, _ = x.shape
    n = bsz * seq
    x2d = x.reshape(n, D_MODEL)
    cos_t, sin_t = pos_table
    (q, k, v, rg, dq, dk, dv, dz, ga, gb, gbc, gbr, conv_new) = _proj_call(
        x2d, wts["norm_mix"], wts["w_in"], wts["w_gates"], cos_t, sin_t,
        wts["alog_c"], wts["dtb_c"], wts["alog_r"], wts["dtb_r"], wts["conv_w"], s_conv, cfg["tm_proj"])

    ret_tables = _ret_tables(cfg["ret_cu"], min(cfg["tt"], cfg["ret_cu"]))
    dn_tables = _dn_tables(cfg["dn_cu"], min(cfg["tt"], cfg["dn_cu"]))
    (og, ret_new, od, dn_new), cast_out = _seq_call(
        q, k, v, rg, s_ret.reshape(bsz, RET_QK, RET_DV), ret_tables,
        dq, dk, dv, gbc, gbr, dz, wts["dn_norm"], s_dn, dn_tables,
        cfg["bt"], cfg["tt"], cfg["ret_cu"], cfg["dn_cu"], cast_weights)
    ret_new = ret_new.reshape(bsz, RET_H, RET_DK, RET_DV)

    return (x2d, og, od, ga, gb, p.reshape(n, PLE_DIM)), ret_new, dn_new, conv_new, cast_out


def _prep_weights(norm_mix, w_in, conv_w, dn_a_log, dn_dt_bias, dn_norm, w_ret_up, w_dn_up, w_out,
                  norm_ffn, w_router_group, b_router_group, w_router_expert, b_router_expert,
                  w_gate_up, w_down, norm_ple, w_ple_gate, w_ple_proj, norm_final):
    assert sum(IN_WIDTHS[:6]) == _C_FRONT and sum(IN_WIDTHS[:8]) == _C_GATES
    w_in = w_in.astype(BF16)
    w_gates = jnp.pad(jnp.concatenate([w_in[:, _C_GATES:], w_in[:, _C_FRONT:_C_GATES]], axis=-1),
                      ((0, 0), (0, LANES - GB_ROWS)))
    pad_lanes = lambda a: jnp.pad(a.astype(F32), (0, LANES - a.shape[0]))[None, :]
    pad_rows = lambda a: jnp.pad(a.astype(F32), (0, GB_ROWS - a.shape[0]))[:, None]
    w_router = jnp.pad(jnp.concatenate([w_router_group, w_router_expert], axis=-1),
                       ((0, 0), (0, LANES - N_GROUPS - N_EXPERTS))).astype(BF16)
    b_router = pad_lanes(jnp.concatenate([b_router_group, b_router_expert]))
    row = lambda a: a.astype(F32)[None, :]
    return dict(
        norm_mix=row(norm_mix), w_in=w_in, w_gates=w_gates,
        alog_c=pad_lanes(dn_a_log), dtb_c=pad_lanes(dn_dt_bias),
        alog_r=pad_rows(dn_a_log), dtb_r=pad_rows(dn_dt_bias),
        conv_w=conv_w.astype(F32), dn_norm=row(dn_norm),
        w_ret_up=w_ret_up, w_dn_up=w_dn_up, w_out=w_out,
        norm_ffn=row(norm_ffn), w_router=w_router, b_router=b_router,
        w_gate_up=w_gate_up, w_down=w_down,
        norm_ple=row(norm_ple), w_ple_gate=w_ple_gate, w_ple_proj=w_ple_proj.astype(BF16),
        norm_final=row(norm_final),
    )


def kernel(x_prompt, x_sample, p_prompt, p_sample, state_ret, state_dn, state_conv, norm_mix, w_in, conv_w, dn_a_log, dn_dt_bias, dn_norm, w_ret_up, w_dn_up, w_out, norm_ffn, w_router_group, b_router_group, w_router_expert, b_router_expert, w_gate_up, w_down, norm_ple, w_ple_gate, w_ple_proj, norm_final):
    depth = w_in.shape[0]
    assert depth == 1, "one layer: the final norm is fused into the layer's last kernel"
    bp, lp, _ = x_prompt.shape
    bs, ls, _ = x_sample.shape
    wts = _prep_weights(norm_mix[0], w_in[0], conv_w[0], dn_a_log[0], dn_dt_bias[0], dn_norm[0],
                        w_ret_up[0], w_dn_up[0], w_out[0], norm_ffn[0], w_router_group[0],
                        b_router_group[0], w_router_expert[0], b_router_expert[0], w_gate_up[0],
                        w_down[0], norm_ple[0], w_ple_gate[0], w_ple_proj[0], norm_final)

    tm = 512
    bt_s = LANES // ls
    cfg_p = dict(tm=tm, tm_proj=512, bt=1, tt=512, ret_cu=256, dn_cu=128)
    cfg_s = dict(tm=tm, tm_proj=256, bt=bt_s, tt=ls, ret_cu=bt_s * ls, dn_cu=bt_s * ls)

    pos_p = _rope_tables(jnp.arange(lp, dtype=jnp.int32))
    cos_s, sin_s = _rope_tables(PAST_LEN + jnp.arange(ls, dtype=jnp.int32))
    pos_s = (jnp.tile(cos_s, (bs, 1)), jnp.tile(sin_s, (bs, 1)))

    zero_ret = jnp.zeros((bp, RET_H, RET_DK, RET_DV), F32)
    zero_dn = jnp.zeros((bp, DN_H, DN_DK, DN_DV), F32)
    zero_conv = jnp.zeros((bp, CONV_W - 1, DN_CONV_CH), F32)

    seq_steps_p = (bp // cfg_p["bt"]) * (lp // cfg_p["tt"])
    slabs = lambda w: w.reshape((seq_steps_p, -1) + w.shape[-1:])
    tok_p, r_p, d_p, c_p, cast_out = _token_mixers(
        x_prompt, p_prompt[0], zero_ret, zero_dn, zero_conv, pos_p, wts, cfg_p,
        cast_weights=tuple(slabs(wts[k]) for k in _SEQ_CAST))
    tok_s, r_s, d_s, c_s, _ = _token_mixers(x_sample, p_sample[0], state_ret[0], state_dn[0], state_conv[0],
                                            pos_s, wts, cfg_s)
    for k, w in zip(_SEQ_CAST, cast_out):
        wts[k] = w.reshape(wts[k].shape)
    wts["w_down"] = wts["w_down"].reshape(N_EXPERTS * D_EXPERT, D_MODEL)
    ffn_weights = [wts[k] for k in ("w_ret_up", "w_dn_up", "w_out", "norm_ffn", "w_router", "b_router",
                                    "w_gate_up", "w_down", "norm_ple", "w_ple_gate", "w_ple_proj",
                                    "norm_final")]
    y_p = _ffn_call(tok_p, ffn_weights, tm)
    y_s = _ffn_call(tok_s, ffn_weights, tm)
    return (y_p.reshape(x_prompt.shape), y_s.reshape(x_sample.shape),
            r_p[None], d_p[None], c_p[None], r_s[None], d_s[None], c_s[None])
```

```python
import functools

import jax
import jax.numpy as jnp
import numpy as np
from jax import lax
from jax.experimental import pallas as pl
from jax.experimental.pallas import tpu as pltpu

F32 = jnp.float32
BF16 = jnp.bfloat16

D_MODEL = 1024
RET_H, RET_DK, RET_DV = 4, 64, 128
DN_H, DN_DK, DN_DV = 4, 128, 128
CONV_W = 4
ROPE_BASE = 10000.0
PAST_LEN = 16384
N_GROUPS, EXPERTS_PER_GROUP = 4, 4
N_EXPERTS = N_GROUPS * EXPERTS_PER_GROUP
D_EXPERT = 256
PLE_DIM = 256
EPS = 1e-6

RET_QK = RET_H * RET_DK
RET_VW = RET_H * RET_DV
DN_QK = DN_H * DN_DK
DN_VW = DN_H * DN_DV
DN_CONV_CH = 2 * DN_QK + DN_VW
IN_WIDTHS = (RET_QK, RET_QK, RET_VW, RET_VW, DN_CONV_CH, DN_VW, DN_H, DN_H, D_MODEL, D_MODEL)

LANES = 128
SUBLANES = 8
VMEM_LIMIT = 56 * 1024 * 1024
GB_ROWS = 2 * DN_H
assert GB_ROWS == SUBLANES

_C_Q, _C_K, _C_V, _C_RG = 0, 256, 512, 1024
_C_DQKV, _C_DZ, _C_FRONT = 1536, 3072, 3584
_C_GATES = _C_FRONT + GB_ROWS


def _silu(x):
    return x * jax.nn.sigmoid(x)


def _softplus(x):
    return jnp.maximum(x, 0.0) + jnp.log1p(jnp.exp(-jnp.abs(x)))


def _dot(a, b):
    return jnp.dot(a, b, preferred_element_type=F32)


def _dot_nt(a, b):
    return lax.dot_general(a, b, (((1,), (1,)), ((), ())), preferred_element_type=F32)


def _dot_tn(a, b):
    return lax.dot_general(a, b, (((0,), (0,)), ((), ())), preferred_element_type=F32)


def _split3(x):
    hi = x.astype(BF16)
    r = x - hi.astype(F32)
    mid = r.astype(BF16)
    lo = (r - mid.astype(F32)).astype(BF16)
    return hi, mid, lo


def _rms_plain(x):
    return x * lax.rsqrt(jnp.mean(x * x, axis=-1, keepdims=True) + EPS)


def _proj_kernel(x_ref, gain_ref, w_ref, wg_ref, cos_ref, sin_ref,
                 alog_c_ref, dtb_c_ref, alog_r_ref, dtb_r_ref, cw_ref, c0_ref,
                 q_ref, k_ref, v_ref, rg_ref, dq_ref, dk_ref, dv_ref, dz_ref, ga_ref, gb_ref,
                 gbc_ref, gbr_ref, c_ref, ext_ref, *, bt, tt, nl):
    pad = SUBLANES
    tail = CONV_W - 1

    @pl.when(pl.program_id(0) % nl == 0)
    def _():
        ext_ref[:, pad - tail:pad, :] = c0_ref[...]

    x = x_ref[...]
    u = x * lax.rsqrt(jnp.mean(x * x, axis=-1, keepdims=True) + EPS) * gain_ref[...]
    ub = u.astype(BF16)

    def mm(lo, hi):
        return _dot(ub, w_ref[:, lo:hi])

    ext_ref[:, pad:pad + tt, :] = mm(_C_DQKV, _C_DZ).reshape(bt, tt, DN_CONV_CH)

    def conv_silu(c0, c1):
        acc = ext_ref[:, pad - tail:pad - tail + tt, c0:c1] * cw_ref[0:1, c0:c1]
        for j in range(1, CONV_W):
            acc = acc + ext_ref[:, pad - tail + j:pad - tail + j + tt, c0:c1] * cw_ref[j:j + 1, c0:c1]
        return _silu(acc).reshape(bt * tt, c1 - c0)

    def l2_heads(out_ref, c0, scale):
        units = []
        for h in range(DN_H):
            ch = conv_silu(c0 + h * DN_DK, c0 + (h + 1) * DN_DK)
            unit = ch * lax.rsqrt(jnp.sum(ch * ch, axis=-1, keepdims=True) + EPS)
            units.append(unit if scale is None else unit * scale)
        out_ref[...] = jnp.concatenate(units, axis=-1).astype(BF16)

    cos = cos_ref[...]
    sin = sin_ref[...]
    lane = lax.broadcasted_iota(jnp.int32, cos.shape, 1)
    first_half = (lane % RET_DK) < (RET_DK // 2)

    def rot(t):
        swapped = jnp.where(first_half, pltpu.roll(t, RET_QK - RET_DK // 2, 1),
                            pltpu.roll(t, RET_DK // 2, 1))
        return t * cos + swapped * sin

    gates = _dot(ub, wg_ref[...])
    ga_ref[...] = gates[:, :D_MODEL].astype(BF16)
    gb_ref[...] = gates[:, D_MODEL:2 * D_MODEL].astype(BF16)
    dab = gates[:, 2 * D_MODEL:]
    l2_heads(dq_ref, 0, DN_DK ** -0.5)
    l2_heads(dk_ref, DN_QK, None)
    v_ref[...] = mm(_C_V, _C_RG).astype(BF16)
    dv_ref[...] = conv_silu(2 * DN_QK, DN_CONV_CH).astype(BF16)
    new_tail = ext_ref[:, pad + tt - tail:pad + tt, :]
    ext_ref[:, pad - tail:pad, :] = new_tail
    c_ref[...] = new_tail
    qk = mm(_C_Q, _C_V)
    q_ref[...] = (rot(qk[:, :RET_QK]) * (RET_DK ** -0.5)).astype(BF16)
    k_ref[...] = rot(qk[:, RET_QK:]).astype(BF16)
    rg_ref[...] = _silu(mm(_C_RG, _C_DQKV)).astype(BF16)
    dz_ref[...] = _silu(mm(_C_DZ, _C_FRONT)).astype(BF16)

    lane_c = lax.broadcasted_iota(jnp.int32, dab.shape, 1)
    g_c = -jnp.exp(alog_c_ref[...]) * _softplus(dab + dtb_c_ref[...])
    gbc_ref[...] = jnp.where(lane_c < DN_H, g_c, jax.nn.sigmoid(dab))
    dabt = dab.T[:GB_ROWS]
    row_r = lax.broadcasted_iota(jnp.int32, dabt.shape, 0)
    g_r = -jnp.exp(alog_r_ref[...]) * _softplus(dabt + dtb_r_ref[...])
    gbr_ref[...] = jnp.where(row_r < DN_H, g_r, jax.nn.sigmoid(dabt))


def _proj_call(x2d, gain, w_in, w_gates, cos_t, sin_t, alog_c, dtb_c, alog_r, dtb_r, conv_w, c0, tm):
    n = x2d.shape[0]
    bsz = c0.shape[0]
    seq = n // bsz
    tt = min(tm, seq)
    bt = tm // tt
    nl = seq // tt
    table_blocks = cos_t.shape[0] // tm
    tok = lambda w: pl.BlockSpec((tm, w), lambda i: (i, 0))
    const = lambda a: pl.BlockSpec(a.shape, lambda i: (0,) * a.ndim, pipeline_mode=pl.Buffered(1))
    tab = pl.BlockSpec((tm, RET_QK), lambda i: (i % table_blocks, 0))
    cst = pl.BlockSpec((bt, CONV_W - 1, DN_CONV_CH), lambda i: (i // nl, 0, 0))
    bf = lambda w: jax.ShapeDtypeStruct((n, w), BF16)
    out_shapes = (bf(RET_QK), bf(RET_QK), bf(RET_VW), bf(RET_VW), bf(DN_QK), bf(DN_QK), bf(DN_VW),
                  bf(DN_VW), bf(D_MODEL), bf(D_MODEL),
                  jax.ShapeDtypeStruct((n, LANES), F32), jax.ShapeDtypeStruct((GB_ROWS, n), F32),
                  jax.ShapeDtypeStruct(c0.shape, F32))
    out_specs = (tok(RET_QK), tok(RET_QK), tok(RET_VW), tok(RET_VW), tok(DN_QK), tok(DN_QK), tok(DN_VW),
                 tok(DN_VW), tok(D_MODEL), tok(D_MODEL), tok(LANES),
                 pl.BlockSpec((GB_ROWS, tm), lambda i: (0, i)), cst)
    return pl.pallas_call(
        functools.partial(_proj_kernel, bt=bt, tt=tt, nl=nl), grid=(n // tm,), name="proj",
        in_specs=[tok(D_MODEL), const(gain),
                  pl.BlockSpec((D_MODEL, _C_FRONT), lambda i: (0, 0), pipeline_mode=pl.Buffered(1)),
                  const(w_gates), tab, tab,
                  const(alog_c), const(dtb_c), const(alog_r), const(dtb_r), const(conv_w), cst],
        out_specs=out_specs, out_shape=out_shapes,
        scratch_shapes=[pltpu.VMEM((bt, tt + SUBLANES, DN_CONV_CH), F32)],
        compiler_params=pltpu.CompilerParams(dimension_semantics=("arbitrary",),
                                             vmem_limit_bytes=VMEM_LIMIT),
    )(x2d, gain, w_in, w_gates, cos_t, sin_t, alog_c, dtb_c, alog_r, dtb_r, conv_w, c0)


def _ret_tile(q_ref, k_ref, v_ref, rg_ref, dec_ref, hm_ref, qd_ref, kd_ref, cd_ref,
              o_ref, s_ref, *, bt, tt, cu):
    r = bt * tt
    blk = min(tt, cu)
    n_units = r // cu
    n_seg = cu // blk

    heads = range(RET_H)
    hrow = lambda a, h, m: a[h * m:(h + 1) * m]
    for n in range(n_units):
        rows = slice(n * cu, (n + 1) * cu)
        qf = q_ref[rows, :].astype(F32)
        kf = k_ref[rows, :].astype(F32)
        vf = v_ref[rows, :].astype(F32)
        q_m = [qf * hm_ref[h] for h in heads]
        q_d = [qf * qd_ref[h] for h in heads]
        k_d = [kf * kd_ref[h] for h in heads]
        v_h = [vf[:, h * RET_DV:(h + 1) * RET_DV] for h in heads]
        scores = (_dot_nt(jnp.concatenate(q_m, axis=0).astype(BF16), k_ref[rows, :])
                  * dec_ref[...]).astype(BF16)
        intra = []
        for h in heads:
            intra.append(_dot(hrow(scores, h, cu), v_h[h].astype(BF16)))
        inters = []
        for j in range(n_seg):
            seg = slice(j * blk, (j + 1) * blk)
            b = (n * cu + j * blk) // tt
            s = s_ref[b]
            q_seg = jnp.concatenate([a[seg] for a in q_d], axis=0).astype(BF16)
            k_seg = jnp.concatenate([a[seg] for a in k_d], axis=0).astype(BF16)
            v_seg = jnp.concatenate([a[seg] for a in v_h], axis=0).astype(BF16)
            inters.append(_dot(q_seg, s.astype(BF16)))
            s_ref[b] = s * cd_ref[...] + _dot_tn(k_seg, v_seg)
        for h in heads:
            inter = jnp.concatenate([hrow(a, h, blk) for a in inters], axis=0)
            gate = rg_ref[rows, h * RET_DV:(h + 1) * RET_DV].astype(F32)
            o_ref[rows, h * RET_DV:(h + 1) * RET_DV] = (
                _rms_plain(intra[h] + inter) * gate).astype(BF16)


def _dn_tile(q_ref, k_ref, v_ref, gc_ref, gr_ref, dz_ref, dnorm_ref, lblk_ref, ublk_ref, lvl_ref,
             o_ref, s_ref, *, bt, tt, cu):
    r = bt * tt
    blk = min(tt, cu)
    n_units = r // cu
    n_seg = cu // blk
    n_lvl = lvl_ref.shape[0]

    ri = lax.broadcasted_iota(jnp.int32, (cu, cu), 0)
    ci = lax.broadcasted_iota(jnp.int32, (cu, cu), 1)
    same = (ri // blk) == (ci // blk)
    incl = same & (ri >= ci)
    strict = same & (ri > ci)

    gc = gc_ref[...]
    g_hi, g_mid, g_lo = _split3(gc)
    r_hi, r_mid, r_lo = _split3(gr_ref[...])
    lblk = lblk_ref[...]
    ublk = ublk_ref[...]

    units = [(n, h) for n in range(n_units) for h in range(DN_H)]
    cum_c, cum_r = [], []
    for n in range(n_units):
        rows = slice(n * cu, (n + 1) * cu)
        cum_c.append(_dot(lblk, g_hi[rows]) + _dot(lblk, g_mid[rows]) + _dot(lblk, g_lo[rows]))
        cum_r.append(_dot(r_hi[:, rows], ublk) + _dot(r_mid[:, rows], ublk) + _dot(r_lo[:, rows], ublk))

    qn, kn, kb, g_col, dec, a_mat, ab, d, rhs = {}, {}, {}, {}, {}, {}, {}, {}, {}
    for u in units:
        n, h = u
        rows = slice(n * cu, (n + 1) * cu)
        qb = q_ref[rows, h * DN_DK:(h + 1) * DN_DK]
        kb[u] = k_ref[rows, h * DN_DK:(h + 1) * DN_DK]
        kn[u] = kb[u].astype(F32)
        vc = v_ref[rows, h * DN_DV:(h + 1) * DN_DV].astype(F32)
        g_col[u] = cum_c[n][:, h:h + 1]
        g_row = cum_r[n][h:h + 1, :]
        beta = gc[rows, DN_H + h:DN_H + h + 1]
        dec[u] = jnp.where(incl, jnp.exp(jnp.where(incl, g_col[u] - g_row, 0.0)), 0.0)
        a_mat[u] = jnp.where(strict, _dot_nt(kb[u], kb[u]) * dec[u] * beta, 0.0)
        ab[u] = a_mat[u].astype(BF16)
        e_g = jnp.exp(g_col[u])
        rhs[u] = jnp.concatenate([beta * vc, beta * e_g * kn[u]], axis=-1)
        qn[u] = (qb, qb.astype(F32) * e_g)
        d[u] = -(lvl_ref[0] * a_mat[u])
    for lv in range(1, n_lvl):
        m = 1 << lv
        by_rows = m % SUBLANES == 0

        def lower(t):
            return t.reshape(cu // (2 * m), 2, m, t.shape[-1])[:, 1].reshape(cu // 2, t.shape[-1])

        for u in units:
            db = d[u].astype(BF16)
            if not by_rows:
                w = a_mat[u] + _dot(db, ab[u])
                z = w + _dot(w.astype(BF16), db)
                d[u] = d[u] - lvl_ref[lv] * z
                continue
            w = lower(a_mat[u]) + _dot(lower(d[u]).astype(BF16), ab[u])
            z = w + _dot(w.astype(BF16), db)
            d_low = lower(d[u]) - lower(lvl_ref[lv]) * z
            d4 = d[u].reshape(cu // (2 * m), 2, m, cu)
            d[u] = jnp.concatenate([d4[:, :1], d_low.reshape(cu // (2 * m), 1, m, cu)],
                                   axis=1).reshape(cu, cu)
    sol, p_mat = {}, {}
    for u in units:
        sol[u] = rhs[u] + _dot(d[u].astype(BF16), rhs[u].astype(BF16))
        p_mat[u] = (_dot_nt(qn[u][0], kb[u]) * dec[u]).astype(BF16)

    for n in range(n_units):
        for h in range(DN_H):
            u = (n, h)
            u_mat, w_mat = sol[u][:, :DN_DV], sol[u][:, DN_DV:]
            qe = qn[u][1]
            deltas, qss = [], []
            for j in range(n_seg):
                r0 = j * blk
                b = (n * cu + r0) // tt
                sb = s_ref[b, h].astype(BF16)
                lhs = jnp.concatenate([w_mat[r0:r0 + blk], qe[r0:r0 + blk]], axis=0).astype(BF16)
                ws_qs = _dot(lhs, sb)
                deltas.append(u_mat[r0:r0 + blk] - ws_qs[:blk])
                qss.append(ws_qs[blk:])
            delta = deltas[0] if n_seg == 1 else jnp.concatenate(deltas, axis=0)
            q_s = qss[0] if n_seg == 1 else jnp.concatenate(qss, axis=0)
            delta_b = delta.astype(BF16)
            o = q_s + _dot(p_mat[u], delta_b)
            for j in range(n_seg):
                r0 = j * blk
                b = (n * cu + r0) // tt
                g_last = g_col[u][r0 + blk - 1:r0 + blk, :]
                k_dec = (kn[u][r0:r0 + blk] * jnp.exp(g_last - g_col[u][r0:r0 + blk])).astype(BF16)
                s_ref[b, h] = s_ref[b, h] * jnp.exp(g_last) + _dot_tn(k_dec, delta_b[r0:r0 + blk])
            gate = dz_ref[n * cu:(n + 1) * cu, h * DN_DV:(h + 1) * DN_DV].astype(F32)
            o_ref[n * cu:(n + 1) * cu, h * DN_DV:(h + 1) * DN_DV] = (
                _rms_plain(o) * dnorm_ref[...] * gate).astype(BF16)


def _seq_kernel(rq_ref, rk_ref, rv_ref, rg_ref, rs0_ref, dec_ref, hm_ref, qd_ref, kd_ref, cd_ref,
                dq_ref, dk_ref, dv_ref, gc_ref, gr_ref, dz_ref, dnorm_ref, ds0_ref, lblk_ref, ublk_ref,
                lvl_ref, *rest, bt, tt, ret_cu, dn_cu, n_cast):
    cast_in, (og_ref, rs_ref, od_ref, ds_ref), cast_out = rest[:n_cast], rest[n_cast:n_cast + 4], rest[n_cast + 4:]
    for src, dst in zip(cast_in, cast_out):
        dst[...] = src[...].astype(BF16)

    @pl.when(pl.program_id(1) == 0)
    def _():
        rs_ref[...] = rs0_ref[...]
        ds_ref[...] = ds0_ref[...]

    _dn_tile(dq_ref, dk_ref, dv_ref, gc_ref, gr_ref, dz_ref, dnorm_ref, lblk_ref, ublk_ref, lvl_ref,
             od_ref, ds_ref, bt=bt, tt=tt, cu=dn_cu)
    _ret_tile(rq_ref, rk_ref, rv_ref, rg_ref, dec_ref, hm_ref, qd_ref, kd_ref, cd_ref,
              og_ref, rs_ref, bt=bt, tt=tt, cu=ret_cu)


def _seq_call(rq, rk, rv, rg, rs0, ret_tables, dq, dk, dv, gbc, gbr, dz, dn_norm, ds0, dn_tables,
              bt, tt, ret_cu, dn_cu, cast_weights=()):
    bsz = ds0.shape[0]
    seq = dq.shape[0] // bsz
    nl = seq // tt
    r = bt * tt
    n_steps = (bsz // bt) * nl
    tok = lambda w: pl.BlockSpec((r, w), lambda b, l: (b * nl + l, 0))
    const = lambda a: pl.BlockSpec(a.shape, lambda b, l: (0,) * a.ndim)
    rst = pl.BlockSpec((bt, RET_QK, RET_DV), lambda b, l: (b, 0, 0))
    dst = pl.BlockSpec((bt, DN_H, DN_DK, DN_DV), lambda b, l: (b, 0, 0, 0))
    slab = lambda a: pl.BlockSpec((a.shape[0] // n_steps,) + a.shape[1:],
                                  lambda b, l: (b * nl + l,) + (0,) * (a.ndim - 1))
    outs = pl.pallas_call(
        functools.partial(_seq_kernel, bt=bt, tt=tt, ret_cu=ret_cu, dn_cu=dn_cu, n_cast=len(cast_weights)),
        grid=(bsz // bt, nl), name="seq",
        in_specs=([tok(RET_QK), tok(RET_QK), tok(RET_VW), tok(RET_VW), rst]
                  + [const(t) for t in ret_tables]
                  + [tok(DN_QK), tok(DN_QK), tok(DN_VW), tok(LANES),
                     pl.BlockSpec((GB_ROWS, r), lambda b, l: (0, b * nl + l)),
                     tok(DN_VW), const(dn_norm), dst] + [const(t) for t in dn_tables]
                  + [slab(a) for a in cast_weights]),
        out_specs=(tok(RET_VW), rst, tok(DN_VW), dst) + tuple(slab(a) for a in cast_weights),
        out_shape=(jax.ShapeDtypeStruct((bsz * seq, RET_VW), BF16), jax.ShapeDtypeStruct(rs0.shape, F32),
                   jax.ShapeDtypeStruct((bsz * seq, DN_VW), BF16), jax.ShapeDtypeStruct(ds0.shape, F32))
        + tuple(jax.ShapeDtypeStruct(a.shape, BF16) for a in cast_weights),
        compiler_params=pltpu.CompilerParams(dimension_semantics=("parallel", "arbitrary"),
                                             vmem_limit_bytes=VMEM_LIMIT),
    )(rq, rk, rv, rg, rs0, *ret_tables, dq, dk, dv, gbc, gbr, dz, dn_norm, ds0, *dn_tables, *cast_weights)
    return outs[:4], outs[4:]


_R_EXP = N_GROUPS


def _ffn_kernel(x_ref, og_ref, od_ref, ga_ref, gb_ref, p_ref,
                wru_ref, wdu_ref, wo_ref, nffn_ref, wr_ref, br_ref, wgu_ref, wdn_ref, nple_ref, wpg_ref,
                wpp_ref, nfin_ref, y_ref, hid_ref):
    branch_a = _dot(og_ref[...], wru_ref[...])
    branch_b = _dot(od_ref[...], wdu_ref[...])
    merged = (jax.nn.sigmoid(ga_ref[...].astype(F32)) * branch_a
              + jax.nn.sigmoid(gb_ref[...].astype(F32)) * branch_b)
    h = x_ref[...] + _dot(merged.astype(BF16), wo_ref[...])

    u = _rms_plain(h) * nffn_ref[...]
    ub = u.astype(BF16)

    logits = _dot(ub, wr_ref[...]) + br_ref[...]
    lane = lax.broadcasted_iota(jnp.int32, logits.shape, 1)
    neg = jnp.float32(-jnp.inf)
    big = jnp.int32(LANES)
    gl = jnp.where(lane < N_GROUPS, logits, neg)
    g_max = jnp.max(gl, axis=-1, keepdims=True)
    p_grp = 1.0 / jnp.sum(jnp.exp(gl - g_max), axis=-1, keepdims=True)
    grp = jnp.min(jnp.where(gl == g_max, lane, big), axis=-1, keepdims=True)
    e_idx = lane - _R_EXP
    in_grp = (e_idx >= 0) & (e_idx < N_EXPERTS) & ((e_idx // EXPERTS_PER_GROUP) == grp)
    el = jnp.where(in_grp, logits, neg)
    v1 = jnp.max(el, axis=-1, keepdims=True)
    i1 = jnp.min(jnp.where(el == v1, lane, big), axis=-1, keepdims=True)
    el2 = jnp.where(lane == i1, neg, el)
    v2 = jnp.max(el2, axis=-1, keepdims=True)
    i2 = jnp.min(jnp.where(el2 == v2, lane, big), axis=-1, keepdims=True)
    e2 = jnp.exp(v2 - v1)
    w1 = 1.0 / (1.0 + e2)
    w2 = e2 / (1.0 + e2)
    combine = jnp.where(lane == i1, w1 * p_grp, 0.0) + jnp.where(lane == i2, w2 * p_grp, 0.0)

    for e in range(N_EXPERTS):
        gu = _dot(ub, wgu_ref[e])
        hid = _silu(gu[:, :D_EXPERT]) * gu[:, D_EXPERT:]
        hid_ref[:, e * D_EXPERT:(e + 1) * D_EXPERT] = (
            combine[:, _R_EXP + e:_R_EXP + e + 1] * hid).astype(BF16)
    h = h + _dot(hid_ref[...], wdn_ref[...])

    u3 = (_rms_plain(h) * nple_ref[...]).astype(BF16)
    gate = jax.nn.sigmoid(_dot(u3, wpg_ref[...]))
    h = h + gate * _dot(p_ref[...].astype(BF16), wpp_ref[...])
    y_ref[...] = _rms_plain(h) * nfin_ref[...]


def _ffn_call(tok_in, weights, tm):
    n = tok_in[0].shape[0]
    tok = lambda a: pl.BlockSpec((tm, a.shape[1]), lambda i: (i, 0))
    const = lambda a: pl.BlockSpec(a.shape, lambda i: (0,) * a.ndim, pipeline_mode=pl.Buffered(1))
    return pl.pallas_call(
        _ffn_kernel, grid=(n // tm,), name="ffn",
        in_specs=[tok(a) for a in tok_in] + [const(w) for w in weights],
        out_specs=tok(tok_in[0]), out_shape=jax.ShapeDtypeStruct((n, D_MODEL), F32),
        scratch_shapes=[pltpu.VMEM((tm, N_EXPERTS * D_EXPERT), BF16)],
        compiler_params=pltpu.CompilerParams(dimension_semantics=("parallel",),
                                             vmem_limit_bytes=VMEM_LIMIT),
    )(*tok_in, *weights)


_SEQ_CAST = ("w_gate_up", "w_down", "w_out", "w_ple_gate", "w_ret_up", "w_dn_up")

def _rope_tables(pos):
    half = RET_DK // 2
    inv = 1.0 / (ROPE_BASE ** (jnp.arange(half, dtype=F32) / half))
    lane = np.arange(RET_QK) % RET_DK
    sign = np.where(lane < half, -1.0, 1.0).astype(np.float32)
    ang = pos.astype(F32)[:, None] * inv[lane % half][None, :]
    return jnp.cos(ang), jnp.sin(ang) * sign


def _ret_tables(r, c):
    f32 = np.float32
    log_gamma = np.log(f32(1.0) - f32(2.0) ** (f32(-5.0) - np.arange(RET_H, dtype=f32)))
    row = np.arange(r)
    idx = (row % c).astype(f32)
    diff = idx[:, None] - idx[None, :]
    causal = (diff >= 0) & ((row[:, None] // c) == (row[None, :] // c))
    lg = log_gamma[:, None, None]
    decay = np.where(causal, np.exp(np.where(causal, diff, f32(0.0)) * lg), f32(0.0))
    q_dec = np.exp((idx + f32(1.0))[None, :] * log_gamma[:, None])[..., None]
    k_dec = np.exp((f32(c - 1.0) - idx)[None, :] * log_gamma[:, None])[..., None]
    chunk_dec = np.exp(f32(c) * log_gamma)
    lane_head = np.arange(RET_QK) // RET_DK
    head_mask = (lane_head[None, :] == np.arange(RET_H)[:, None]).astype(f32)[:, None, :]
    tables = (decay.reshape(RET_H * r, r), head_mask, q_dec * head_mask, k_dec * head_mask,
              np.broadcast_to(chunk_dec[lane_head][:, None], (RET_QK, RET_DV)))
    return tuple(jnp.asarray(t, F32) for t in tables)


def _dn_tables(r, c):
    i = np.arange(r)
    ri, ci = i[:, None], i[None, :]
    same = (ri // c) == (ci // c)
    lblk = ((ri >= ci) & same).astype(np.float32)
    x = ri ^ ci
    lvl, m = [], 1
    while m < c:
        lvl.append((same & (ri > ci) & (x >= m) & (x < 2 * m)).astype(np.float32))
        m *= 2
    return jnp.asarray(lblk, BF16), jnp.asarray(lblk.T, BF16), jnp.asarray(np.stack(lvl))


def _token_mixers(x, p, s_ret, s_dn, s_conv, pos_table, wts, cfg, cast_weights=()):
    bsz, seq, _ = x.shape
    n = bsz * seq
    x2d = x.reshape(n, D_MODEL)
    cos_t, sin_t = pos_table
    (q, k, v, rg, dq, dk, dv, dz, ga, gb, gbc, gbr, conv_new) = _proj_call(
        x2d, wts["norm_mix"], wts["w_in"], wts["w_gates"], cos_t, sin_t,
        wts["alog_c"], wts["dtb_c"], wts["alog_r"], wts["dtb_r"], wts["conv_w"], s_conv, cfg["tm_proj"])

    ret_tables = _ret_tables(cfg["ret_cu"], min(cfg["tt"], cfg["ret_cu"]))
    dn_tables = _dn_tables(cfg["dn_cu"], min(cfg["tt"], cfg["dn_cu"]))
    (og, ret_new, od, dn_new), cast_out = _seq_call(
        q, k, v, rg, s_ret.reshape(bsz, RET_QK, RET_DV), ret_tables,
        dq, dk, dv, gbc, gbr, dz, wts["dn_norm"], s_dn, dn_tables,
        cfg["bt"], cfg["tt"], cfg["ret_cu"], cfg["dn_cu"], cast_weights)
    ret_new = ret_new.reshape(bsz, RET_H, RET_DK, RET_DV)

    return (x2d, og, od, ga, gb, p.reshape(n, PLE_DIM)), ret_new, dn_new, conv_new, cast_out


def _prep_weights(norm_mix, w_in, conv_w, dn_a_log, dn_dt_bias, dn_norm, w_ret_up, w_dn_up, w_out,
                  norm_ffn, w_router_group, b_router_group, w_router_expert, b_router_expert,
                  w_gate_up, w_down, norm_ple, w_ple_gate, w_ple_proj, norm_final):
    assert sum(IN_WIDTHS[:6]) == _C_FRONT and sum(IN_WIDTHS[:8]) == _C_GATES
    w_in = w_in.astype(BF16)
    w_gates = jnp.pad(jnp.concatenate([w_in[:, _C_GATES:], w_in[:, _C_FRONT:_C_GATES]], axis=-1),
                      ((0, 0), (0, LANES - GB_ROWS)))
    pad_lanes = lambda a: jnp.pad(a.astype(F32), (0, LANES - a.shape[0]))[None, :]
    pad_rows = lambda a: jnp.pad(a.astype(F32), (0, GB_ROWS - a.shape[0]))[:, None]
    w_router = jnp.pad(jnp.concatenate([w_router_group, w_router_expert], axis=-1),
                       ((0, 0), (0, LANES - N_GROUPS - N_EXPERTS))).astype(BF16)
    b_router = pad_lanes(jnp.concatenate([b_router_group, b_router_expert]))
    row = lambda a: a.astype(F32)[None, :]
    return dict(
        norm_mix=row(norm_mix), w_in=w_in, w_gates=w_gates,
        alog_c=pad_lanes(dn_a_log), dtb_c=pad_lanes(dn_dt_bias),
        alog_r=pad_rows(dn_a_log), dtb_r=pad_rows(dn_dt_bias),
        conv_w=conv_w.astype(F32), dn_norm=row(dn_norm),
        w_ret_up=w_ret_up, w_dn_up=w_dn_up, w_out=w_out,
        norm_ffn=row(norm_ffn), w_router=w_router, b_router=b_router,
        w_gate_up=w_gate_up, w_down=w_down,
        norm_ple=row(norm_ple), w_ple_gate=w_ple_gate, w_ple_proj=w_ple_proj.astype(BF16),
        norm_final=row(norm_final),
    )


def kernel(x_prompt, x_sample, p_prompt, p_sample, state_ret, state_dn, state_conv, norm_mix, w_in, conv_w, dn_a_log, dn_dt_bias, dn_norm, w_ret_up, w_dn_up, w_out, norm_ffn, w_router_group, b_router_group, w_router_expert, b_router_expert, w_gate_up, w_down, norm_ple, w_ple_gate, w_ple_proj, norm_final):
    depth = w_in.shape[0]
    assert depth == 1, "one layer: the final norm is fused into the layer's last kernel"
    bp, lp, _ = x_prompt.shape
    bs, ls, _ = x_sample.shape
    wts = _prep_weights(norm_mix[0], w_in[0], conv_w[0], dn_a_log[0], dn_dt_bias[0], dn_norm[0],
                        w_ret_up[0], w_dn_up[0], w_out[0], norm_ffn[0], w_router_group[0],
                        b_router_group[0], w_router_expert[0], b_router_expert[0], w_gate_up[0],
                        w_down[0], norm_ple[0], w_ple_gate[0], w_ple_proj[0], norm_final)

    tm = 512
    bt_s = LANES // ls
    cfg_p = dict(tm=tm, tm_proj=512, bt=1, tt=1024, ret_cu=256, dn_cu=128)
    cfg_s = dict(tm=tm, tm_proj=256, bt=bt_s, tt=ls, ret_cu=bt_s * ls, dn_cu=bt_s * ls)

    pos_p = _rope_tables(jnp.arange(lp, dtype=jnp.int32))
    pos_s = _rope_tables(jnp.asarray(PAST_LEN + np.arange(bs * ls) % ls, jnp.int32))

    zero_ret = jnp.zeros((bp, RET_H, RET_DK, RET_DV), F32)
    zero_dn = jnp.zeros((bp, DN_H, DN_DK, DN_DV), F32)
    zero_conv = jnp.zeros((bp, CONV_W - 1, DN_CONV_CH), F32)

    seq_steps_p = (bp // cfg_p["bt"]) * (lp // cfg_p["tt"])
    slabs = lambda w: w.reshape((seq_steps_p, -1) + w.shape[-1:])
    tok_p, r_p, d_p, c_p, cast_out = _token_mixers(
        x_prompt, p_prompt[0], zero_ret, zero_dn, zero_conv, pos_p, wts, cfg_p,
        cast_weights=tuple(slabs(wts[k]) for k in _SEQ_CAST))
    tok_s, r_s, d_s, c_s, _ = _token_mixers(x_sample, p_sample[0], state_ret[0], state_dn[0], state_conv[0],
                                            pos_s, wts, cfg_s)
    for k, w in zip(_SEQ_CAST, cast_out):
        wts[k] = w.reshape(wts[k].shape)
    wts["w_down"] = wts["w_down"].reshape(N_EXPERTS * D_EXPERT, D_MODEL)
    ffn_weights = [wts[k] for k in ("w_ret_up", "w_dn_up", "w_out", "norm_ffn", "w_router", "b_router",
                                    "w_gate_up", "w_down", "norm_ple", "w_ple_gate", "w_ple_proj",
                                    "norm_final")]
    y_p = _ffn_call(tok_p, ffn_weights, tm)
    y_s = _ffn_call(tok_s, ffn_weights, tm)
    return (y_p.reshape(x_prompt.shape), y_s.reshape(x_sample.shape),
            r_p[None], d_p[None], c_p[None], r_s[None], d_s[None], c_s[None])
```

```python
import functools

import jax
import jax.numpy as jnp
import numpy as np
from jax import lax
from jax.experimental import pallas as pl
from jax.experimental.pallas import tpu as pltpu

F32 = jnp.float32
BF16 = jnp.bfloat16

D_MODEL = 1024
RET_H, RET_DK, RET_DV = 4, 64, 128
DN_H, DN_DK, DN_DV = 4, 128, 128
CONV_W = 4
ROPE_BASE = 10000.0
PAST_LEN = 16384
N_GROUPS, EXPERTS_PER_GROUP = 4, 4
N_EXPERTS = N_GROUPS * EXPERTS_PER_GROUP
D_EXPERT = 256
PLE_DIM = 256
EPS = 1e-6

RET_QK = RET_H * RET_DK
RET_VW = RET_H * RET_DV
DN_QK = DN_H * DN_DK
DN_VW = DN_H * DN_DV
DN_CONV_CH = 2 * DN_QK + DN_VW
IN_WIDTHS = (RET_QK, RET_QK, RET_VW, RET_VW, DN_CONV_CH, DN_VW, DN_H, DN_H, D_MODEL, D_MODEL)

LANES = 128
SUBLANES = 8
VMEM_LIMIT = 56 * 1024 * 1024
GB_ROWS = 2 * DN_H
assert GB_ROWS == SUBLANES

_C_Q, _C_K, _C_V, _C_RG = 0, 256, 512, 1024
_C_DQKV, _C_DZ, _C_FRONT = 1536, 3072, 3584
_C_GATES = _C_FRONT + GB_ROWS


def _silu(x):
    return x * jax.nn.sigmoid(x)


def _softplus(x):
    return jnp.maximum(x, 0.0) + jnp.log1p(jnp.exp(-jnp.abs(x)))


def _dot(a, b):
    return jnp.dot(a, b, preferred_element_type=F32)


def _dot_nt(a, b):
    return lax.dot_general(a, b, (((1,), (1,)), ((), ())), preferred_element_type=F32)


def _dot_tn(a, b):
    return lax.dot_general(a, b, (((0,), (0,)), ((), ())), preferred_element_type=F32)


def _split3(x):
    hi = x.astype(BF16)
    r = x - hi.astype(F32)
    mid = r.astype(BF16)
    lo = (r - mid.astype(F32)).astype(BF16)
    return hi, mid, lo


def _rms_plain(x):
    return x * lax.rsqrt(jnp.mean(x * x, axis=-1, keepdims=True) + EPS)


def _proj_kernel(x_ref, gain_ref, w_ref, wg_ref, cos_ref, sin_ref,
                 alog_c_ref, dtb_c_ref, alog_r_ref, dtb_r_ref, cw_ref, c0_ref,
                 q_ref, k_ref, v_ref, rg_ref, dq_ref, dk_ref, dv_ref, dz_ref, ga_ref, gb_ref,
                 gbc_ref, gbr_ref, c_ref, ext_ref, *, bt, tt, nl):
    pad = SUBLANES
    tail = CONV_W - 1

    @pl.when(pl.program_id(0) % nl == 0)
    def _():
        ext_ref[:, pad - tail:pad, :] = c0_ref[...]

    x = x_ref[...]
    u = x * lax.rsqrt(jnp.mean(x * x, axis=-1, keepdims=True) + EPS) * gain_ref[...]
    ub = u.astype(BF16)

    def mm(lo, hi):
        return _dot(ub, w_ref[:, lo:hi])

    ext_ref[:, pad:pad + tt, :] = mm(_C_DQKV, _C_DZ).reshape(bt, tt, DN_CONV_CH)

    def conv_silu(c0, c1):
        acc = ext_ref[:, pad - tail:pad - tail + tt, c0:c1] * cw_ref[0:1, c0:c1]
        for j in range(1, CONV_W):
            acc = acc + ext_ref[:, pad - tail + j:pad - tail + j + tt, c0:c1] * cw_ref[j:j + 1, c0:c1]
        return _silu(acc).reshape(bt * tt, c1 - c0)

    def l2_heads(out_ref, c0, scale):
        units = []
        for h in range(DN_H):
            ch = conv_silu(c0 + h * DN_DK, c0 + (h + 1) * DN_DK)
            unit = ch * lax.rsqrt(jnp.sum(ch * ch, axis=-1, keepdims=True) + EPS)
            units.append(unit if scale is None else unit * scale)
        out_ref[...] = jnp.concatenate(units, axis=-1).astype(BF16)

    cos = cos_ref[...]
    sin = sin_ref[...]
    lane = lax.broadcasted_iota(jnp.int32, cos.shape, 1)
    first_half = (lane % RET_DK) < (RET_DK // 2)

    def rot(t):
        swapped = jnp.where(first_half, pltpu.roll(t, RET_QK - RET_DK // 2, 1),
                            pltpu.roll(t, RET_DK // 2, 1))
        return t * cos + swapped * sin

    gates = _dot(ub, wg_ref[...])
    ga_ref[...] = gates[:, :D_MODEL].astype(BF16)
    gb_ref[...] = gates[:, D_MODEL:2 * D_MODEL].astype(BF16)
    dab = gates[:, 2 * D_MODEL:]
    l2_heads(dq_ref, 0, DN_DK ** -0.5)
    l2_heads(dk_ref, DN_QK, None)
    v_ref[...] = mm(_C_V, _C_RG).astype(BF16)
    dv_ref[...] = conv_silu(2 * DN_QK, DN_CONV_CH).astype(BF16)
    new_tail = ext_ref[:, pad + tt - tail:pad + tt, :]
    ext_ref[:, pad - tail:pad, :] = new_tail
    c_ref[...] = new_tail
    qk = mm(_C_Q, _C_V)
    q_ref[...] = (rot(qk[:, :RET_QK]) * (RET_DK ** -0.5)).astype(BF16)
    k_ref[...] = rot(qk[:, RET_QK:]).astype(BF16)
    rg_ref[...] = _silu(mm(_C_RG, _C_DQKV)).astype(BF16)
    dz_ref[...] = _silu(mm(_C_DZ, _C_FRONT)).astype(BF16)

    lane_c = lax.broadcasted_iota(jnp.int32, dab.shape, 1)
    g_c = -jnp.exp(alog_c_ref[...]) * _softplus(dab + dtb_c_ref[...])
    gbc_ref[...] = jnp.where(lane_c < DN_H, g_c, jax.nn.sigmoid(dab))
    dabt = dab.T[:GB_ROWS]
    row_r = lax.broadcasted_iota(jnp.int32, dabt.shape, 0)
    g_r = -jnp.exp(alog_r_ref[...]) * _softplus(dabt + dtb_r_ref[...])
    gbr_ref[...] = jnp.where(row_r < DN_H, g_r, jax.nn.sigmoid(dabt))


def _proj_call(x2d, gain, w_in, w_gates, cos_t, sin_t, alog_c, dtb_c, alog_r, dtb_r, conv_w, c0, tm):
    n = x2d.shape[0]
    bsz = c0.shape[0]
    seq = n // bsz
    tt = min(tm, seq)
    bt = tm // tt
    nl = seq // tt
    table_blocks = cos_t.shape[0] // tm
    tok = lambda w: pl.BlockSpec((tm, w), lambda i: (i, 0))
    const = lambda a: pl.BlockSpec(a.shape, lambda i: (0,) * a.ndim, pipeline_mode=pl.Buffered(1))
    tab = pl.BlockSpec((tm, RET_QK), lambda i: (i % table_blocks, 0))
    cst = pl.BlockSpec((bt, CONV_W - 1, DN_CONV_CH), lambda i: (i // nl, 0, 0))
    bf = lambda w: jax.ShapeDtypeStruct((n, w), BF16)
    out_shapes = (bf(RET_QK), bf(RET_QK), bf(RET_VW), bf(RET_VW), bf(DN_QK), bf(DN_QK), bf(DN_VW),
                  bf(DN_VW), bf(D_MODEL), bf(D_MODEL),
                  jax.ShapeDtypeStruct((n, LANES), F32), jax.ShapeDtypeStruct((GB_ROWS, n), F32),
                  jax.ShapeDtypeStruct(c0.shape, F32))
    out_specs = (tok(RET_QK), tok(RET_QK), tok(RET_VW), tok(RET_VW), tok(DN_QK), tok(DN_QK), tok(DN_VW),
                 tok(DN_VW), tok(D_MODEL), tok(D_MODEL), tok(LANES),
                 pl.BlockSpec((GB_ROWS, tm), lambda i: (0, i)), cst)
    return pl.pallas_call(
        functools.partial(_proj_kernel, bt=bt, tt=tt, nl=nl), grid=(n // tm,), name="proj",
        in_specs=[tok(D_MODEL), const(gain),
                  pl.BlockSpec((D_MODEL, _C_FRONT), lambda i: (0, 0), pipeline_mode=pl.Buffered(1)),
                  const(w_gates), tab, tab,
                  const(alog_c), const(dtb_c), const(alog_r), const(dtb_r), const(conv_w), cst],
        out_specs=out_specs, out_shape=out_shapes,
        scratch_shapes=[pltpu.VMEM((bt, tt + SUBLANES, DN_CONV_CH), F32)],
        compiler_params=pltpu.CompilerParams(dimension_semantics=("arbitrary",),
                                             vmem_limit_bytes=VMEM_LIMIT),
    )(x2d, gain, w_in, w_gates, cos_t, sin_t, alog_c, dtb_c, alog_r, dtb_r, conv_w, c0)


def _ret_tile(q_ref, k_ref, v_ref, rg_ref, dec_ref, hm_ref, qd_ref, kd_ref, cd_ref,
              o_ref, s_ref, *, bt, tt, cu):
    r = bt * tt
    blk = min(tt, cu)
    n_units = r // cu
    n_seg = cu // blk

    heads = range(RET_H)
    hrow = lambda a, h, m: a[h * m:(h + 1) * m]
    for n in range(n_units):
        rows = slice(n * cu, (n + 1) * cu)
        qf = q_ref[rows, :].astype(F32)
        kf = k_ref[rows, :].astype(F32)
        vf = v_ref[rows, :].astype(F32)
        q_m = [qf * hm_ref[h] for h in heads]
        q_d = [qf * qd_ref[h] for h in heads]
        k_d = [kf * kd_ref[h] for h in heads]
        v_h = [vf[:, h * RET_DV:(h + 1) * RET_DV] for h in heads]
        scores = (_dot_nt(jnp.concatenate(q_m, axis=0).astype(BF16), k_ref[rows, :])
                  * dec_ref[...]).astype(BF16)
        intra = []
        for h in heads:
            intra.append(_dot(hrow(scores, h, cu), v_h[h].astype(BF16)))
        inters = []
        for j in range(n_seg):
            seg = slice(j * blk, (j + 1) * blk)
            b = (n * cu + j * blk) // tt
            s = s_ref[b]
            q_seg = jnp.concatenate([a[seg] for a in q_d], axis=0).astype(BF16)
            k_seg = jnp.concatenate([a[seg] for a in k_d], axis=0).astype(BF16)
            v_seg = jnp.concatenate([a[seg] for a in v_h], axis=0).astype(BF16)
            inters.append(_dot(q_seg, s.astype(BF16)))
            s_ref[b] = s * cd_ref[...] + _dot_tn(k_seg, v_seg)
        for h in heads:
            inter = jnp.concatenate([hrow(a, h, blk) for a in inters], axis=0)
            gate = rg_ref[rows, h * RET_DV:(h + 1) * RET_DV].astype(F32)
            o_ref[rows, h * RET_DV:(h + 1) * RET_DV] = (
                _rms_plain(intra[h] + inter) * gate).astype(BF16)


def _dn_tile(q_ref, k_ref, v_ref, gc_ref, gr_ref, dz_ref, dnorm_ref, lblk_ref, ublk_ref, lvl_ref,
             o_ref, s_ref, *, bt, tt, cu):
    r = bt * tt
    blk = min(tt, cu)
    n_units = r // cu
    n_seg = cu // blk
    n_lvl = lvl_ref.shape[0]

    ri = lax.broadcasted_iota(jnp.int32, (cu, cu), 0)
    ci = lax.broadcasted_iota(jnp.int32, (cu, cu), 1)
    same = (ri // blk) == (ci // blk)
    incl = same & (ri >= ci)
    strict = same & (ri > ci)

    gc = gc_ref[...]
    g_hi, g_mid, g_lo = _split3(gc)
    r_hi, r_mid, r_lo = _split3(gr_ref[...])
    lblk = lblk_ref[...]
    ublk = ublk_ref[...]

    units = [(n, h) for n in range(n_units) for h in range(DN_H)]
    cum_c, cum_r = [], []
    for n in range(n_units):
        rows = slice(n * cu, (n + 1) * cu)
        cum_c.append(_dot(lblk, g_hi[rows]) + _dot(lblk, g_mid[rows]) + _dot(lblk, g_lo[rows]))
        cum_r.append(_dot(r_hi[:, rows], ublk) + _dot(r_mid[:, rows], ublk) + _dot(r_lo[:, rows], ublk))

    qn, kn, kb, g_col, dec, a_mat, ab, d, rhs = {}, {}, {}, {}, {}, {}, {}, {}, {}
    for u in units:
        n, h = u
        rows = slice(n * cu, (n + 1) * cu)
        qb = q_ref[rows, h * DN_DK:(h + 1) * DN_DK]
        kb[u] = k_ref[rows, h * DN_DK:(h + 1) * DN_DK]
        kn[u] = kb[u].astype(F32)
        vc = v_ref[rows, h * DN_DV:(h + 1) * DN_DV].astype(F32)
        g_col[u] = cum_c[n][:, h:h + 1]
        g_row = cum_r[n][h:h + 1, :]
        beta = gc[rows, DN_H + h:DN_H + h + 1]
        dec[u] = jnp.where(incl, jnp.exp(jnp.where(incl, g_col[u] - g_row, 0.0)), 0.0)
        a_mat[u] = jnp.where(strict, _dot_nt(kb[u], kb[u]) * dec[u] * beta, 0.0)
        ab[u] = a_mat[u].astype(BF16)
        e_g = jnp.exp(g_col[u])
        rhs[u] = jnp.concatenate([beta * vc, beta * e_g * kn[u]], axis=-1)
        qn[u] = (qb, qb.astype(F32) * e_g)
        d[u] = -(lvl_ref[0] * a_mat[u])
    for lv in range(1, n_lvl):
        m = 1 << lv
        by_rows = m % SUBLANES == 0

        def lower(t):
            return t.reshape(cu // (2 * m), 2, m, t.shape[-1])[:, 1].reshape(cu // 2, t.shape[-1])

        for u in units:
            db = d[u].astype(BF16)
            if not by_rows:
                w = a_mat[u] + _dot(db, ab[u])
                z = w + _dot(w.astype(BF16), db)
                d[u] = d[u] - lvl_ref[lv] * z
                continue
            w = lower(a_mat[u]) + _dot(lower(d[u]).astype(BF16), ab[u])
            z = w + _dot(w.astype(BF16), db)
            d_low = lower(d[u]) - lower(lvl_ref[lv]) * z
            d4 = d[u].reshape(cu // (2 * m), 2, m, cu)
            d[u] = jnp.concatenate([d4[:, :1], d_low.reshape(cu // (2 * m), 1, m, cu)],
                                   axis=1).reshape(cu, cu)
    sol, p_mat, e_last = {}, {}, {}
    for u in units:
        sol[u] = rhs[u] + _dot(d[u].astype(BF16), rhs[u].astype(BF16))
        p_mat[u] = (_dot_nt(qn[u][0], kb[u]) * dec[u]).astype(BF16)
        if n_seg == 1:
            g_last = g_col[u][cu - 1:cu, :]
            k_dec_t = (kn[u] * jnp.exp(g_last - g_col[u])).T.astype(BF16)
            p_mat[u] = jnp.concatenate([p_mat[u], k_dec_t], axis=0)
            e_last[u] = jnp.exp(g_last)

    for n in range(n_units):
        for h in range(DN_H):
            u = (n, h)
            u_mat, w_mat = sol[u][:, :DN_DV], sol[u][:, DN_DV:]
            qe = qn[u][1]
            if n_seg == 1:
                b = (n * cu) // tt
                s = s_ref[b, h]
                ws_qs = _dot(jnp.concatenate([w_mat, qe], axis=0).astype(BF16), s.astype(BF16))
                both = _dot(p_mat[u], (u_mat - ws_qs[:cu]).astype(BF16))
                s_ref[b, h] = s * e_last[u] + both[cu:]
                gate = dz_ref[n * cu:(n + 1) * cu, h * DN_DV:(h + 1) * DN_DV].astype(F32)
                o_ref[n * cu:(n + 1) * cu, h * DN_DV:(h + 1) * DN_DV] = (
                    _rms_plain(ws_qs[cu:] + both[:cu]) * dnorm_ref[...] * gate).astype(BF16)
                continue
            deltas, qss = [], []
            for j in range(n_seg):
                r0 = j * blk
                b = (n * cu + r0) // tt
                sb = s_ref[b, h].astype(BF16)
                lhs = jnp.concatenate([w_mat[r0:r0 + blk], qe[r0:r0 + blk]], axis=0).astype(BF16)
                ws_qs = _dot(lhs, sb)
                deltas.append(u_mat[r0:r0 + blk] - ws_qs[:blk])
                qss.append(ws_qs[blk:])
            delta = deltas[0] if n_seg == 1 else jnp.concatenate(deltas, axis=0)
            q_s = qss[0] if n_seg == 1 else jnp.concatenate(qss, axis=0)
            delta_b = delta.astype(BF16)
            o = q_s + _dot(p_mat[u], delta_b)
            for j in range(n_seg):
                r0 = j * blk
                b = (n * cu + r0) // tt
                g_last = g_col[u][r0 + blk - 1:r0 + blk, :]
                k_dec = (kn[u][r0:r0 + blk] * jnp.exp(g_last - g_col[u][r0:r0 + blk])).astype(BF16)
                s_ref[b, h] = s_ref[b, h] * jnp.exp(g_last) + _dot_tn(k_dec, delta_b[r0:r0 + blk])
            gate = dz_ref[n * cu:(n + 1) * cu, h * DN_DV:(h + 1) * DN_DV].astype(F32)
            o_ref[n * cu:(n + 1) * cu, h * DN_DV:(h + 1) * DN_DV] = (
                _rms_plain(o) * dnorm_ref[...] * gate).astype(BF16)


def _seq_kernel(rq_ref, rk_ref, rv_ref, rg_ref, rs0_ref, dec_ref, hm_ref, qd_ref, kd_ref, cd_ref,
                dq_ref, dk_ref, dv_ref, gc_ref, gr_ref, dz_ref, dnorm_ref, ds0_ref, lblk_ref, ublk_ref,
                lvl_ref, *rest, bt, tt, ret_cu, dn_cu, n_cast):
    cast_in, (og_ref, rs_ref, od_ref, ds_ref), cast_out = rest[:n_cast], rest[n_cast:n_cast + 4], rest[n_cast + 4:]
    for src, dst in zip(cast_in, cast_out):
        dst[...] = src[...].astype(BF16)

    @pl.when(pl.program_id(1) == 0)
    def _():
        rs_ref[...] = rs0_ref[...]
        ds_ref[...] = ds0_ref[...]

    _dn_tile(dq_ref, dk_ref, dv_ref, gc_ref, gr_ref, dz_ref, dnorm_ref, lblk_ref, ublk_ref, lvl_ref,
             od_ref, ds_ref, bt=bt, tt=tt, cu=dn_cu)
    _ret_tile(rq_ref, rk_ref, rv_ref, rg_ref, dec_ref, hm_ref, qd_ref, kd_ref, cd_ref,
              og_ref, rs_ref, bt=bt, tt=tt, cu=ret_cu)


def _seq_call(rq, rk, rv, rg, rs0, ret_tables, dq, dk, dv, gbc, gbr, dz, dn_norm, ds0, dn_tables,
              bt, tt, ret_cu, dn_cu, cast_weights=()):
    bsz = ds0.shape[0]
    seq = dq.shape[0] // bsz
    nl = seq // tt
    r = bt * tt
    n_steps = (bsz // bt) * nl
    tok = lambda w: pl.BlockSpec((r, w), lambda b, l: (b * nl + l, 0))
    const = lambda a: pl.BlockSpec(a.shape, lambda b, l: (0,) * a.ndim)
    rst = pl.BlockSpec((bt, RET_QK, RET_DV), lambda b, l: (b, 0, 0))
    dst = pl.BlockSpec((bt, DN_H, DN_DK, DN_DV), lambda b, l: (b, 0, 0, 0))
    slab = lambda a: pl.BlockSpec((a.shape[0] // n_steps,) + a.shape[1:],
                                  lambda b, l: (b * nl + l,) + (0,) * (a.ndim - 1))
    outs = pl.pallas_call(
        functools.partial(_seq_kernel, bt=bt, tt=tt, ret_cu=ret_cu, dn_cu=dn_cu, n_cast=len(cast_weights)),
        grid=(bsz // bt, nl), name="seq",
        in_specs=([tok(RET_QK), tok(RET_QK), tok(RET_VW), tok(RET_VW), rst]
                  + [const(t) for t in ret_tables]
                  + [tok(DN_QK), tok(DN_QK), tok(DN_VW), tok(LANES),
                     pl.BlockSpec((GB_ROWS, r), lambda b, l: (0, b * nl + l)),
                     tok(DN_VW), const(dn_norm), dst] + [const(t) for t in dn_tables]
                  + [slab(a) for a in cast_weights]),
        out_specs=(tok(RET_VW), rst, tok(DN_VW), dst) + tuple(slab(a) for a in cast_weights),
        out_shape=(jax.ShapeDtypeStruct((bsz * seq, RET_VW), BF16), jax.ShapeDtypeStruct(rs0.shape, F32),
                   jax.ShapeDtypeStruct((bsz * seq, DN_VW), BF16), jax.ShapeDtypeStruct(ds0.shape, F32))
        + tuple(jax.ShapeDtypeStruct(a.shape, BF16) for a in cast_weights),
        compiler_params=pltpu.CompilerParams(dimension_semantics=("parallel", "arbitrary"),
                                             vmem_limit_bytes=VMEM_LIMIT),
    )(rq, rk, rv, rg, rs0, *ret_tables, dq, dk, dv, gbc, gbr, dz, dn_norm, ds0, *dn_tables, *cast_weights)
    return outs[:4], outs[4:]


_R_EXP = N_GROUPS


def _ffn_kernel(x_ref, og_ref, od_ref, ga_ref, gb_ref, p_ref,
                wru_ref, wdu_ref, wo_ref, nffn_ref, wr_ref, br_ref, wgu_ref, wdn_ref, nple_ref, wpg_ref,
                wpp_ref, nfin_ref, y_ref, hid_ref):
    branch_a = _dot(og_ref[...], wru_ref[...])
    branch_b = _dot(od_ref[...], wdu_ref[...])
    merged = (jax.nn.sigmoid(ga_ref[...].astype(F32)) * branch_a
              + jax.nn.sigmoid(gb_ref[...].astype(F32)) * branch_b)
    h = x_ref[...] + _dot(merged.astype(BF16), wo_ref[...])

    u = _rms_plain(h) * nffn_ref[...]
    ub = u.astype(BF16)

    logits = _dot(ub, wr_ref[...]) + br_ref[...]
    lane = lax.broadcasted_iota(jnp.int32, logits.shape, 1)
    neg = jnp.float32(-jnp.inf)
    big = jnp.int32(LANES)
    gl = jnp.where(lane < N_GROUPS, logits, neg)
    g_max = jnp.max(gl, axis=-1, keepdims=True)
    p_grp = 1.0 / jnp.sum(jnp.exp(gl - g_max), axis=-1, keepdims=True)
    grp = jnp.min(jnp.where(gl == g_max, lane, big), axis=-1, keepdims=True)
    e_idx = lane - _R_EXP
    in_grp = (e_idx >= 0) & (e_idx < N_EXPERTS) & ((e_idx // EXPERTS_PER_GROUP) == grp)
    el = jnp.where(in_grp, logits, neg)
    v1 = jnp.max(el, axis=-1, keepdims=True)
    i1 = jnp.min(jnp.where(el == v1, lane, big), axis=-1, keepdims=True)
    el2 = jnp.where(lane == i1, neg, el)
    v2 = jnp.max(el2, axis=-1, keepdims=True)
    i2 = jnp.min(jnp.where(el2 == v2, lane, big), axis=-1, keepdims=True)
    e2 = jnp.exp(v2 - v1)
    w1 = 1.0 / (1.0 + e2)
    w2 = e2 / (1.0 + e2)
    combine = jnp.where(lane == i1, w1 * p_grp, 0.0) + jnp.where(lane == i2, w2 * p_grp, 0.0)

    for e in range(N_EXPERTS):
        gu = _dot(ub, wgu_ref[e])
        hid = _silu(gu[:, :D_EXPERT]) * gu[:, D_EXPERT:]
        hid_ref[:, e * D_EXPERT:(e + 1) * D_EXPERT] = (
            combine[:, _R_EXP + e:_R_EXP + e + 1] * hid).astype(BF16)
    h = h + _dot(hid_ref[...], wdn_ref[...])

    u3 = (_rms_plain(h) * nple_ref[...]).astype(BF16)
    gate = jax.nn.sigmoid(_dot(u3, wpg_ref[...]))
    h = h + gate * _dot(p_ref[...].astype(BF16), wpp_ref[...])
    y_ref[...] = _rms_plain(h) * nfin_ref[...]


def _ffn_call(tok_in, weights, tm):
    n = tok_in[0].shape[0]
    tok = lambda a: pl.BlockSpec((tm, a.shape[1]), lambda i: (i, 0))
    const = lambda a: pl.BlockSpec(a.shape, lambda i: (0,) * a.ndim, pipeline_mode=pl.Buffered(1))
    return pl.pallas_call(
        _ffn_kernel, grid=(n // tm,), name="ffn",
        in_specs=[tok(a) for a in tok_in] + [const(w) for w in weights],
        out_specs=tok(tok_in[0]), out_shape=jax.ShapeDtypeStruct((n, D_MODEL), F32),
        scratch_shapes=[pltpu.VMEM((tm, N_EXPERTS * D_EXPERT), BF16)],
        compiler_params=pltpu.CompilerParams(dimension_semantics=("parallel",),
                                             vmem_limit_bytes=VMEM_LIMIT),
    )(*tok_in, *weights)


_SEQ_CAST = ("w_gate_up", "w_down", "w_out", "w_ple_gate", "w_ret_up", "w_dn_up")

def _rope_tables(pos):
    half = RET_DK // 2
    inv = 1.0 / (ROPE_BASE ** (jnp.arange(half, dtype=F32) / half))
    ang = pos.astype(F32)[:, None] * inv[None, :]
    cos, sin = jnp.cos(ang), jnp.sin(ang)
    cos_t = jnp.tile(jnp.concatenate([cos, cos], axis=-1), (1, RET_H))
    sin_t = jnp.tile(jnp.concatenate([-sin, sin], axis=-1), (1, RET_H))
    return cos_t, sin_t


def _ret_tables(r, c):
    f32 = np.float32
    log_gamma = np.log(f32(1.0) - f32(2.0) ** (f32(-5.0) - np.arange(RET_H, dtype=f32)))
    row = np.arange(r)
    idx = (row % c).astype(f32)
    diff = idx[:, None] - idx[None, :]
    causal = (diff >= 0) & ((row[:, None] // c) == (row[None, :] // c))
    lg = log_gamma[:, None, None]
    decay = np.where(causal, np.exp(np.where(causal, diff, f32(0.0)) * lg), f32(0.0))
    q_dec = np.exp((idx + f32(1.0))[None, :] * log_gamma[:, None])[..., None]
    k_dec = np.exp((f32(c - 1.0) - idx)[None, :] * log_gamma[:, None])[..., None]
    chunk_dec = np.exp(f32(c) * log_gamma)
    lane_head = np.arange(RET_QK) // RET_DK
    head_mask = (lane_head[None, :] == np.arange(RET_H)[:, None]).astype(f32)[:, None, :]
    tables = (decay.reshape(RET_H * r, r), head_mask, q_dec * head_mask, k_dec * head_mask,
              np.broadcast_to(chunk_dec[lane_head][:, None], (RET_QK, RET_DV)))
    return tuple(jnp.asarray(t, F32) for t in tables)


def _dn_tables(r, c):
    i = np.arange(r)
    ri, ci = i[:, None], i[None, :]
    same = (ri // c) == (ci // c)
    lblk = ((ri >= ci) & same).astype(np.float32)
    x = ri ^ ci
    lvl, m = [], 1
    while m < c:
        lvl.append((same & (ri > ci) & (x >= m) & (x < 2 * m)).astype(np.float32))
        m *= 2
    return jnp.asarray(lblk, BF16), jnp.asarray(lblk.T, BF16), jnp.asarray(np.stack(lvl))


def _token_mixers(x, p, s_ret, s_dn, s_conv, pos_table, wts, cfg, cast_weights=()):
    bsz, seq, _ = x.shape
    n = bsz * seq
    x2d = x.reshape(n, D_MODEL)
    cos_t, sin_t = pos_table
    (q, k, v, rg, dq, dk, dv, dz, ga, gb, gbc, gbr, conv_new) = _proj_call(
        x2d, wts["norm_mix"], wts["w_in"], wts["w_gates"], cos_t, sin_t,
        wts["alog_c"], wts["dtb_c"], wts["alog_r"], wts["dtb_r"], wts["conv_w"], s_conv, cfg["tm_proj"])

    ret_tables = _ret_tables(cfg["ret_cu"], min(cfg["tt"], cfg["ret_cu"]))
    dn_tables = _dn_tables(cfg["dn_cu"], min(cfg["tt"], cfg["dn_cu"]))
    (og, ret_new, od, dn_new), cast_out = _seq_call(
        q, k, v, rg, s_ret.reshape(bsz, RET_QK, RET_DV), ret_tables,
        dq, dk, dv, gbc, gbr, dz, wts["dn_norm"], s_dn, dn_tables,
        cfg["bt"], cfg["tt"], cfg["ret_cu"], cfg["dn_cu"], cast_weights)
    ret_new = ret_new.reshape(bsz, RET_H, RET_DK, RET_DV)

    return (x2d, og, od, ga, gb, p.reshape(n, PLE_DIM)), ret_new, dn_new, conv_new, cast_out


def _prep_weights(norm_mix, w_in, conv_w, dn_a_log, dn_dt_bias, dn_norm, w_ret_up, w_dn_up, w_out,
                  norm_ffn, w_router_group, b_router_group, w_router_expert, b_router_expert,
                  w_gate_up, w_down, norm_ple, w_ple_gate, w_ple_proj, norm_final):
    assert sum(IN_WIDTHS[:6]) == _C_FRONT and sum(IN_WIDTHS[:8]) == _C_GATES
    w_in = w_in.astype(BF16)
    w_gates = jnp.pad(jnp.concatenate([w_in[:, _C_GATES:], w_in[:, _C_FRONT:_C_GATES]], axis=-1),
                      ((0, 0), (0, LANES - GB_ROWS)))
    pad_lanes = lambda a: jnp.pad(a.astype(F32), (0, LANES - a.shape[0]))[None, :]
    pad_rows = lambda a: jnp.pad(a.astype(F32), (0, GB_ROWS - a.shape[0]))[:, None]
    w_router = jnp.pad(jnp.concatenate([w_router_group, w_router_expert], axis=-1),
                       ((0, 0), (0, LANES - N_GROUPS - N_EXPERTS))).astype(BF16)
    b_router = pad_lanes(jnp.concatenate([b_router_group, b_router_expert]))
    row = lambda a: a.astype(F32)[None, :]
    return dict(
        norm_mix=row(norm_mix), w_in=w_in, w_gates=w_gates,
        alog_c=pad_lanes(dn_a_log), dtb_c=pad_lanes(dn_dt_bias),
        alog_r=pad_rows(dn_a_log), dtb_r=pad_rows(dn_dt_bias),
        conv_w=conv_w.astype(F32), dn_norm=row(dn_norm),
        w_ret_up=w_ret_up, w_dn_up=w_dn_up, w_out=w_out,
        norm_ffn=row(norm_ffn), w_router=w_router, b_router=b_router,
        w_gate_up=w_gate_up, w_down=w_down,
        norm_ple=row(norm_ple), w_ple_gate=w_ple_gate, w_ple_proj=w_ple_proj.astype(BF16),
        norm_final=row(norm_final),
    )


def kernel(x_prompt, x_sample, p_prompt, p_sample, state_ret, state_dn, state_conv, norm_mix, w_in, conv_w, dn_a_log, dn_dt_bias, dn_norm, w_ret_up, w_dn_up, w_out, norm_ffn, w_router_group, b_router_group, w_router_expert, b_router_expert, w_gate_up, w_down, norm_ple, w_ple_gate, w_ple_proj, norm_final):
    depth = w_in.shape[0]
    assert depth == 1, "one layer: the final norm is fused into the layer's last kernel"
    bp, lp, _ = x_prompt.shape
    bs, ls, _ = x_sample.shape
    wts = _prep_weights(norm_mix[0], w_in[0], conv_w[0], dn_a_log[0], dn_dt_bias[0], dn_norm[0],
                        w_ret_up[0], w_dn_up[0], w_out[0], norm_ffn[0], w_router_group[0],
                        b_router_group[0], w_router_expert[0], b_router_expert[0], w_gate_up[0],
                        w_down[0], norm_ple[0], w_ple_gate[0], w_ple_proj[0], norm_final)

    tm = 512
    bt_s = LANES // ls
    cfg_p = dict(tm=tm, tm_proj=512, bt=1, tt=512, ret_cu=256, dn_cu=128)
    cfg_s = dict(tm=tm, tm_proj=256, bt=bt_s, tt=ls, ret_cu=bt_s * ls, dn_cu=bt_s * ls)

    pos_p = _rope_tables(jnp.arange(lp, dtype=jnp.int32))
    cos_s, sin_s = _rope_tables(PAST_LEN + jnp.arange(ls, dtype=jnp.int32))
    pos_s = (jnp.tile(cos_s, (bs, 1)), jnp.tile(sin_s, (bs, 1)))

    zero_ret = jnp.zeros((bp, RET_H, RET_DK, RET_DV), F32)
    zero_dn = jnp.zeros((bp, DN_H, DN_DK, DN_DV), F32)
    zero_conv = jnp.zeros((bp, CONV_W - 1, DN_CONV_CH), F32)

    seq_steps_p = (bp // cfg_p["bt"]) * (lp // cfg_p["tt"])
    slabs = lambda w: w.reshape((seq_steps_p, -1) + w.shape[-1:])
    tok_p, r_p, d_p, c_p, cast_out = _token_mixers(
        x_prompt, p_prompt[0], zero_ret, zero_dn, zero_conv, pos_p, wts, cfg_p,
        cast_weights=tuple(slabs(wts[k]) for k in _SEQ_CAST))
    tok_s, r_s, d_s, c_s, _ = _token_mixers(x_sample, p_sample[0], state_ret[0], state_dn[0], state_conv[0],
                                            pos_s, wts, cfg_s)
    for k, w in zip(_SEQ_CAST, cast_out):
        wts[k] = w.reshape(wts[k].shape)
    wts["w_down"] = wts["w_down"].reshape(N_EXPERTS * D_EXPERT, D_MODEL)
    ffn_weights = [wts[k] for k in ("w_ret_up", "w_dn_up", "w_out", "norm_ffn", "w_router", "b_router",
                                    "w_gate_up", "w_down", "norm_ple", "w_ple_gate", "w_ple_proj",
                                    "norm_final")]
    y_p = _ffn_call(tok_p, ffn_weights, tm)
    y_s = _ffn_call(tok_s, ffn_weights, tm)
    return (y_p.reshape(x_prompt.shape), y_s.reshape(x_sample.shape),
            r_p[None], d_p[None], c_p[None], r_s[None], d_s[None], c_s[None])
```

```python
import functools

import jax
import jax.numpy as jnp
import numpy as np
from jax import lax
from jax.experimental import pallas as pl
from jax.experimental.pallas import tpu as pltpu

F32 = jnp.float32
BF16 = jnp.bfloat16

D_MODEL = 1024
RET_H, RET_DK, RET_DV = 4, 64, 128
DN_H, DN_DK, DN_DV = 4, 128, 128
CONV_W = 4
ROPE_BASE = 10000.0
PAST_LEN = 16384
N_GROUPS, EXPERTS_PER_GROUP = 4, 4
N_EXPERTS = N_GROUPS * EXPERTS_PER_GROUP
D_EXPERT = 256
PLE_DIM = 256
EPS = 1e-6

RET_QK = RET_H * RET_DK
RET_VW = RET_H * RET_DV
DN_QK = DN_H * DN_DK
DN_VW = DN_H * DN_DV
DN_CONV_CH = 2 * DN_QK + DN_VW
IN_WIDTHS = (RET_QK, RET_QK, RET_VW, RET_VW, DN_CONV_CH, DN_VW, DN_H, DN_H, D_MODEL, D_MODEL)

LANES = 128
SUBLANES = 8
VMEM_LIMIT = 56 * 1024 * 1024
GB_ROWS = 2 * DN_H
assert GB_ROWS == SUBLANES

_C_Q, _C_K, _C_V, _C_RG = 0, 256, 512, 1024
_C_DQKV, _C_DZ, _C_FRONT = 1536, 3072, 3584
_C_GATES = _C_FRONT + GB_ROWS


def _silu(x):
    return x * jax.nn.sigmoid(x)


def _softplus(x):
    return jnp.maximum(x, 0.0) + jnp.log1p(jnp.exp(-jnp.abs(x)))


def _dot(a, b):
    return jnp.dot(a, b, preferred_element_type=F32)


def _dot_nt(a, b):
    return lax.dot_general(a, b, (((1,), (1,)), ((), ())), preferred_element_type=F32)


def _dot_tn(a, b):
    return lax.dot_general(a, b, (((0,), (0,)), ((), ())), preferred_element_type=F32)


def _split3(x):
    hi = x.astype(BF16)
    r = x - hi.astype(F32)
    mid = r.astype(BF16)
    lo = (r - mid.astype(F32)).astype(BF16)
    return hi, mid, lo


def _rms_plain(x):
    return x * lax.rsqrt(jnp.mean(x * x, axis=-1, keepdims=True) + EPS)


def _proj_kernel(x_ref, gain_ref, w_ref, wg_ref, cos_ref, sin_ref,
                 alog_c_ref, dtb_c_ref, alog_r_ref, dtb_r_ref, cw_ref, c0_ref,
                 q_ref, k_ref, v_ref, rg_ref, dq_ref, dk_ref, dv_ref, dz_ref, ga_ref, gb_ref,
                 gbc_ref, gbr_ref, c_ref, ext_ref, *, bt, tt, nl):
    pad = SUBLANES
    tail = CONV_W - 1

    @pl.when(pl.program_id(0) % nl == 0)
    def _():
        ext_ref[:, pad - tail:pad, :] = c0_ref[...]

    x = x_ref[...]
    u = x * lax.rsqrt(jnp.mean(x * x, axis=-1, keepdims=True) + EPS) * gain_ref[...]
    ub = u.astype(BF16)

    def mm(lo, hi):
        return _dot(ub, w_ref[:, lo:hi])

    ext_ref[:, pad:pad + tt, :] = mm(_C_DQKV, _C_DZ).reshape(bt, tt, DN_CONV_CH)

    def conv_silu(c0, c1):
        acc = ext_ref[:, pad - tail:pad - tail + tt, c0:c1] * cw_ref[0:1, c0:c1]
        for j in range(1, CONV_W):
            acc = acc + ext_ref[:, pad - tail + j:pad - tail + j + tt, c0:c1] * cw_ref[j:j + 1, c0:c1]
        return _silu(acc).reshape(bt * tt, c1 - c0)

    def l2_heads(out_ref, c0, scale):
        units = []
        for h in range(DN_H):
            ch = conv_silu(c0 + h * DN_DK, c0 + (h + 1) * DN_DK)
            unit = ch * lax.rsqrt(jnp.sum(ch * ch, axis=-1, keepdims=True) + EPS)
            units.append(unit if scale is None else unit * scale)
        out_ref[...] = jnp.concatenate(units, axis=-1).astype(BF16)

    cos = cos_ref[...]
    sin = sin_ref[...]
    lane = lax.broadcasted_iota(jnp.int32, cos.shape, 1)
    first_half = (lane % RET_DK) < (RET_DK // 2)

    def rot(t):
        swapped = jnp.where(first_half, pltpu.roll(t, RET_QK - RET_DK // 2, 1),
                            pltpu.roll(t, RET_DK // 2, 1))
        return t * cos + swapped * sin

    gates = _dot(ub, wg_ref[...])
    ga_ref[...] = gates[:, :D_MODEL].astype(BF16)
    gb_ref[...] = gates[:, D_MODEL:2 * D_MODEL].astype(BF16)
    dab = gates[:, 2 * D_MODEL:]
    l2_heads(dq_ref, 0, DN_DK ** -0.5)
    l2_heads(dk_ref, DN_QK, None)
    v_ref[...] = mm(_C_V, _C_RG).astype(BF16)
    dv_ref[...] = conv_silu(2 * DN_QK, DN_CONV_CH).astype(BF16)
    new_tail = ext_ref[:, pad + tt - tail:pad + tt, :]
    ext_ref[:, pad - tail:pad, :] = new_tail
    c_ref[...] = new_tail
    qk = mm(_C_Q, _C_V)
    q_ref[...] = (rot(qk[:, :RET_QK]) * (RET_DK ** -0.5)).astype(BF16)
    k_ref[...] = rot(qk[:, RET_QK:]).astype(BF16)
    rg_ref[...] = _silu(mm(_C_RG, _C_DQKV)).astype(BF16)
    dz_ref[...] = _silu(mm(_C_DZ, _C_FRONT)).astype(BF16)

    lane_c = lax.broadcasted_iota(jnp.int32, dab.shape, 1)
    g_c = -jnp.exp(alog_c_ref[...]) * _softplus(dab + dtb_c_ref[...])
    gbc_ref[...] = jnp.where(lane_c < DN_H, g_c, jax.nn.sigmoid(dab))
    dabt = dab.T[:GB_ROWS]
    row_r = lax.broadcasted_iota(jnp.int32, dabt.shape, 0)
    g_r = -jnp.exp(alog_r_ref[...]) * _softplus(dabt + dtb_r_ref[...])
    gbr_ref[...] = jnp.where(row_r < DN_H, g_r, jax.nn.sigmoid(dabt))


def _proj_call(x2d, gain, w_in, w_gates, cos_t, sin_t, alog_c, dtb_c, alog_r, dtb_r, conv_w, c0, tm):
    n = x2d.shape[0]
    bsz = c0.shape[0]
    seq = n // bsz
    tt = min(tm, seq)
    bt = tm // tt
    nl = seq // tt
    table_blocks = cos_t.shape[0] // tm
    tok = lambda w: pl.BlockSpec((tm, w), lambda i: (i, 0))
    const = lambda a: pl.BlockSpec(a.shape, lambda i: (0,) * a.ndim, pipeline_mode=pl.Buffered(1))
    tab = pl.BlockSpec((tm, RET_QK), lambda i: (i % table_blocks, 0))
    cst = pl.BlockSpec((bt, CONV_W - 1, DN_CONV_CH), lambda i: (i // nl, 0, 0))
    bf = lambda w: jax.ShapeDtypeStruct((n, w), BF16)
    out_shapes = (bf(RET_QK), bf(RET_QK), bf(RET_VW), bf(RET_VW), bf(DN_QK), bf(DN_QK), bf(DN_VW),
                  bf(DN_VW), bf(D_MODEL), bf(D_MODEL),
                  jax.ShapeDtypeStruct((n, LANES), F32), jax.ShapeDtypeStruct((GB_ROWS, n), F32),
                  jax.ShapeDtypeStruct(c0.shape, F32))
    out_specs = (tok(RET_QK), tok(RET_QK), tok(RET_VW), tok(RET_VW), tok(DN_QK), tok(DN_QK), tok(DN_VW),
                 tok(DN_VW), tok(D_MODEL), tok(D_MODEL), tok(LANES),
                 pl.BlockSpec((GB_ROWS, tm), lambda i: (0, i)), cst)
    return pl.pallas_call(
        functools.partial(_proj_kernel, bt=bt, tt=tt, nl=nl), grid=(n // tm,), name="proj",
        in_specs=[tok(D_MODEL), const(gain),
                  pl.BlockSpec((D_MODEL, _C_FRONT), lambda i: (0, 0), pipeline_mode=pl.Buffered(1)),
                  const(w_gates), tab, tab,
                  const(alog_c), const(dtb_c), const(alog_r), const(dtb_r), const(conv_w), cst],
        out_specs=out_specs, out_shape=out_shapes,
        scratch_shapes=[pltpu.VMEM((bt, tt + SUBLANES, DN_CONV_CH), F32)],
        compiler_params=pltpu.CompilerParams(dimension_semantics=("arbitrary",),
                                             vmem_limit_bytes=VMEM_LIMIT),
    )(x2d, gain, w_in, w_gates, cos_t, sin_t, alog_c, dtb_c, alog_r, dtb_r, conv_w, c0)


def _ret_tile(q_ref, k_ref, v_ref, rg_ref, dec_ref, hm_ref, qd_ref, kd_ref, cd_ref,
              o_ref, s_ref, *, bt, tt, cu):
    r = bt * tt
    blk = min(tt, cu)
    n_units = r // cu
    n_seg = cu // blk

    heads = range(RET_H)
    hrow = lambda a, h, m: a[h * m:(h + 1) * m]
    for n in range(n_units):
        rows = slice(n * cu, (n + 1) * cu)
        qf = q_ref[rows, :].astype(F32)
        kf = k_ref[rows, :].astype(F32)
        vf = v_ref[rows, :].astype(F32)
        q_m = [qf * hm_ref[h] for h in heads]
        q_d = [qf * qd_ref[h] for h in heads]
        k_d = [kf * kd_ref[h] for h in heads]
        v_h = [vf[:, h * RET_DV:(h + 1) * RET_DV] for h in heads]
        scores = (_dot_nt(jnp.concatenate(q_m, axis=0).astype(BF16), k_ref[rows, :])
                  * dec_ref[...]).astype(BF16)
        intra = []
        for h in heads:
            intra.append(_dot(hrow(scores, h, cu), v_h[h].astype(BF16)))
        inters = []
        for j in range(n_seg):
            seg = slice(j * blk, (j + 1) * blk)
            b = (n * cu + j * blk) // tt
            s = s_ref[b]
            q_seg = jnp.concatenate([a[seg] for a in q_d], axis=0).astype(BF16)
            k_seg = jnp.concatenate([a[seg] for a in k_d], axis=0).astype(BF16)
            v_seg = jnp.concatenate([a[seg] for a in v_h], axis=0).astype(BF16)
            inters.append(_dot(q_seg, s.astype(BF16)))
            s_ref[b] = s * cd_ref[...] + _dot_tn(k_seg, v_seg)
        for h in heads:
            inter = jnp.concatenate([hrow(a, h, blk) for a in inters], axis=0)
            gate = rg_ref[rows, h * RET_DV:(h + 1) * RET_DV].astype(F32)
            o_ref[rows, h * RET_DV:(h + 1) * RET_DV] = (
                _rms_plain(intra[h] + inter) * gate).astype(BF16)


def _dn_tile(q_ref, k_ref, v_ref, gc_ref, gr_ref, dz_ref, dnorm_ref, lblk_ref, ublk_ref, lvl_ref,
             o_ref, s_ref, *, bt, tt, cu):
    r = bt * tt
    blk = min(tt, cu)
    n_units = r // cu
    n_seg = cu // blk
    n_lvl = lvl_ref.shape[0]

    ri = lax.broadcasted_iota(jnp.int32, (cu, cu), 0)
    ci = lax.broadcasted_iota(jnp.int32, (cu, cu), 1)
    same = (ri // blk) == (ci // blk)
    incl = same & (ri >= ci)
    strict = same & (ri > ci)

    gc = gc_ref[...]
    g_hi, g_mid, g_lo = _split3(gc)
    r_hi, r_mid, r_lo = _split3(gr_ref[...])
    lblk = lblk_ref[...]
    ublk = ublk_ref[...]

    units = [(n, h) for n in range(n_units) for h in range(DN_H)]
    cum_c, cum_r = [], []
    for n in range(n_units):
        rows = slice(n * cu, (n + 1) * cu)
        cum_c.append(_dot(lblk, g_hi[rows]) + _dot(lblk, g_mid[rows]) + _dot(lblk, g_lo[rows]))
        cum_r.append(_dot(r_hi[:, rows], ublk) + _dot(r_mid[:, rows], ublk) + _dot(r_lo[:, rows], ublk))

    qe_all, kn, kb, g_col, dec, a_mat, ab, d, rhs, p_mat = {}, {}, {}, {}, {}, {}, {}, {}, {}, {}
    for u in units:
        n, h = u
        rows = slice(n * cu, (n + 1) * cu)
        qb = q_ref[rows, h * DN_DK:(h + 1) * DN_DK]
        kb[u] = k_ref[rows, h * DN_DK:(h + 1) * DN_DK]
        kn[u] = kb[u].astype(F32)
        vc = v_ref[rows, h * DN_DV:(h + 1) * DN_DV].astype(F32)
        g_col[u] = cum_c[n][:, h:h + 1]
        g_row = cum_r[n][h:h + 1, :]
        beta = gc[rows, DN_H + h:DN_H + h + 1]
        dec[u] = jnp.where(incl, jnp.exp(jnp.where(incl, g_col[u] - g_row, 0.0)), 0.0)
        kq_k = _dot_nt(jnp.concatenate([kb[u], qb], axis=0), kb[u])
        a_mat[u] = jnp.where(strict, kq_k[:cu] * dec[u] * beta, 0.0)
        p_mat[u] = (kq_k[cu:] * dec[u]).astype(BF16)
        ab[u] = a_mat[u].astype(BF16)
        e_g = jnp.exp(g_col[u])
        rhs[u] = jnp.concatenate([beta * vc, beta * e_g * kn[u]], axis=-1)
        qe_all[u] = qb.astype(F32) * e_g
        d[u] = -(lvl_ref[0] * a_mat[u])
    for lv in range(1, n_lvl):
        m = 1 << lv
        by_rows = m % SUBLANES == 0

        def lower(t):
            return t.reshape(cu // (2 * m), 2, m, t.shape[-1])[:, 1].reshape(cu // 2, t.shape[-1])

        for u in units:
            db = d[u].astype(BF16)
            if not by_rows:
                w = a_mat[u] + _dot(db, ab[u])
                z = w + _dot(w.astype(BF16), db)
                d[u] = d[u] - lvl_ref[lv] * z
                continue
            w = lower(a_mat[u]) + _dot(lower(d[u]).astype(BF16), ab[u])
            z = w + _dot(w.astype(BF16), db)
            d_low = lower(d[u]) - lower(lvl_ref[lv]) * z
            d4 = d[u].reshape(cu // (2 * m), 2, m, cu)
            d[u] = jnp.concatenate([d4[:, :1], d_low.reshape(cu // (2 * m), 1, m, cu)],
                                   axis=1).reshape(cu, cu)
    sol, e_last = {}, {}
    for u in units:
        sol[u] = rhs[u] + _dot(d[u].astype(BF16), rhs[u].astype(BF16))
        if n_seg == 1:
            g_last = g_col[u][cu - 1:cu, :]
            k_dec_t = (kn[u] * jnp.exp(g_last - g_col[u])).T.astype(BF16)
            p_mat[u] = jnp.concatenate([p_mat[u], k_dec_t], axis=0)
            e_last[u] = jnp.exp(g_last)

    for n in range(n_units):
        for h in range(DN_H):
            u = (n, h)
            u_mat, w_mat = sol[u][:, :DN_DV], sol[u][:, DN_DV:]
            qe = qe_all[u]
            if n_seg == 1:
                b = (n * cu) // tt
                s = s_ref[b, h]
                ws_qs = _dot(jnp.concatenate([w_mat, qe], axis=0).astype(BF16), s.astype(BF16))
                both = _dot(p_mat[u], (u_mat - ws_qs[:cu]).astype(BF16))
                s_ref[b, h] = s * e_last[u] + both[cu:]
                gate = dz_ref[n * cu:(n + 1) * cu, h * DN_DV:(h + 1) * DN_DV].astype(F32)
                o_ref[n * cu:(n + 1) * cu, h * DN_DV:(h + 1) * DN_DV] = (
                    _rms_plain(ws_qs[cu:] + both[:cu]) * dnorm_ref[...] * gate).astype(BF16)
                continue
            deltas, qss = [], []
            for j in range(n_seg):
                r0 = j * blk
                b = (n * cu + r0) // tt
                sb = s_ref[b, h].astype(BF16)
                lhs = jnp.concatenate([w_mat[r0:r0 + blk], qe[r0:r0 + blk]], axis=0).astype(BF16)
                ws_qs = _dot(lhs, sb)
                deltas.append(u_mat[r0:r0 + blk] - ws_qs[:blk])
                qss.append(ws_qs[blk:])
            delta = deltas[0] if n_seg == 1 else jnp.concatenate(deltas, axis=0)
            q_s = qss[0] if n_seg == 1 else jnp.concatenate(qss, axis=0)
            delta_b = delta.astype(BF16)
            o = q_s + _dot(p_mat[u], delta_b)
            for j in range(n_seg):
                r0 = j * blk
                b = (n * cu + r0) // tt
                g_last = g_col[u][r0 + blk - 1:r0 + blk, :]
                k_dec = (kn[u][r0:r0 + blk] * jnp.exp(g_last - g_col[u][r0:r0 + blk])).astype(BF16)
                s_ref[b, h] = s_ref[b, h] * jnp.exp(g_last) + _dot_tn(k_dec, delta_b[r0:r0 + blk])
            gate = dz_ref[n * cu:(n + 1) * cu, h * DN_DV:(h + 1) * DN_DV].astype(F32)
            o_ref[n * cu:(n + 1) * cu, h * DN_DV:(h + 1) * DN_DV] = (
                _rms_plain(o) * dnorm_ref[...] * gate).astype(BF16)


def _seq_kernel(rq_ref, rk_ref, rv_ref, rg_ref, rs0_ref, dec_ref, hm_ref, qd_ref, kd_ref, cd_ref,
                dq_ref, dk_ref, dv_ref, gc_ref, gr_ref, dz_ref, dnorm_ref, ds0_ref, lblk_ref, ublk_ref,
                lvl_ref, *rest, bt, tt, ret_cu, dn_cu, n_cast):
    cast_in, (og_ref, rs_ref, od_ref, ds_ref), cast_out = rest[:n_cast], rest[n_cast:n_cast + 4], rest[n_cast + 4:]
    for src, dst in zip(cast_in, cast_out):
        dst[...] = src[...].astype(BF16)

    @pl.when(pl.program_id(1) == 0)
    def _():
        rs_ref[...] = rs0_ref[...]
        ds_ref[...] = ds0_ref[...]

    _dn_tile(dq_ref, dk_ref, dv_ref, gc_ref, gr_ref, dz_ref, dnorm_ref, lblk_ref, ublk_ref, lvl_ref,
             od_ref, ds_ref, bt=bt, tt=tt, cu=dn_cu)
    _ret_tile(rq_ref, rk_ref, rv_ref, rg_ref, dec_ref, hm_ref, qd_ref, kd_ref, cd_ref,
              og_ref, rs_ref, bt=bt, tt=tt, cu=ret_cu)


def _seq_call(rq, rk, rv, rg, rs0, ret_tables, dq, dk, dv, gbc, gbr, dz, dn_norm, ds0, dn_tables,
              bt, tt, ret_cu, dn_cu, cast_weights=()):
    bsz = ds0.shape[0]
    seq = dq.shape[0] // bsz
    nl = seq // tt
    r = bt * tt
    n_steps = (bsz // bt) * nl
    tok = lambda w: pl.BlockSpec((r, w), lambda b, l: (b * nl + l, 0))
    const = lambda a: pl.BlockSpec(a.shape, lambda b, l: (0,) * a.ndim)
    rst = pl.BlockSpec((bt, RET_QK, RET_DV), lambda b, l: (b, 0, 0))
    dst = pl.BlockSpec((bt, DN_H, DN_DK, DN_DV), lambda b, l: (b, 0, 0, 0))
    slab = lambda a: pl.BlockSpec((a.shape[0] // n_steps,) + a.shape[1:],
                                  lambda b, l: (b * nl + l,) + (0,) * (a.ndim - 1))
    outs = pl.pallas_call(
        functools.partial(_seq_kernel, bt=bt, tt=tt, ret_cu=ret_cu, dn_cu=dn_cu, n_cast=len(cast_weights)),
        grid=(bsz // bt, nl), name="seq",
        in_specs=([tok(RET_QK), tok(RET_QK), tok(RET_VW), tok(RET_VW), rst]
                  + [const(t) for t in ret_tables]
                  + [tok(DN_QK), tok(DN_QK), tok(DN_VW), tok(LANES),
                     pl.BlockSpec((GB_ROWS, r), lambda b, l: (0, b * nl + l)),
                     tok(DN_VW), const(dn_norm), dst] + [const(t) for t in dn_tables]
                  + [slab(a) for a in cast_weights]),
        out_specs=(tok(RET_VW), rst, tok(DN_VW), dst) + tuple(slab(a) for a in cast_weights),
        out_shape=(jax.ShapeDtypeStruct((bsz * seq, RET_VW), BF16), jax.ShapeDtypeStruct(rs0.shape, F32),
                   jax.ShapeDtypeStruct((bsz * seq, DN_VW), BF16), jax.ShapeDtypeStruct(ds0.shape, F32))
        + tuple(jax.ShapeDtypeStruct(a.shape, BF16) for a in cast_weights),
        compiler_params=pltpu.CompilerParams(dimension_semantics=("parallel", "arbitrary"),
                                             vmem_limit_bytes=VMEM_LIMIT),
    )(rq, rk, rv, rg, rs0, *ret_tables, dq, dk, dv, gbc, gbr, dz, dn_norm, ds0, *dn_tables, *cast_weights)
    return outs[:4], outs[4:]


_R_EXP = N_GROUPS


def _ffn_kernel(x_ref, og_ref, od_ref, ga_ref, gb_ref, p_ref,
                wru_ref, wdu_ref, wo_ref, nffn_ref, wr_ref, br_ref, wgu_ref, wdn_ref, nple_ref, wpg_ref,
                wpp_ref, nfin_ref, y_ref, hid_ref):
    branch_a = _dot(og_ref[...], wru_ref[...])
    branch_b = _dot(od_ref[...], wdu_ref[...])
    merged = (jax.nn.sigmoid(ga_ref[...].astype(F32)) * branch_a
              + jax.nn.sigmoid(gb_ref[...].astype(F32)) * branch_b)
    h = x_ref[...] + _dot(merged.astype(BF16), wo_ref[...])

    u = _rms_plain(h) * nffn_ref[...]
    ub = u.astype(BF16)

    logits = _dot(ub, wr_ref[...]) + br_ref[...]
    lane = lax.broadcasted_iota(jnp.int32, logits.shape, 1)
    neg = jnp.float32(-jnp.inf)
    big = jnp.int32(LANES)
    gl = jnp.where(lane < N_GROUPS, logits, neg)
    g_max = jnp.max(gl, axis=-1, keepdims=True)
    p_grp = 1.0 / jnp.sum(jnp.exp(gl - g_max), axis=-1, keepdims=True)
    grp = jnp.min(jnp.where(gl == g_max, lane, big), axis=-1, keepdims=True)
    e_idx = lane - _R_EXP
    in_grp = (e_idx >= 0) & (e_idx < N_EXPERTS) & ((e_idx // EXPERTS_PER_GROUP) == grp)
    el = jnp.where(in_grp, logits, neg)
    v1 = jnp.max(el, axis=-1, keepdims=True)
    i1 = jnp.min(jnp.where(el == v1, lane, big), axis=-1, keepdims=True)
    el2 = jnp.where(lane == i1, neg, el)
    v2 = jnp.max(el2, axis=-1, keepdims=True)
    i2 = jnp.min(jnp.where(el2 == v2, lane, big), axis=-1, keepdims=True)
    e2 = jnp.exp(v2 - v1)
    w1 = 1.0 / (1.0 + e2)
    w2 = e2 / (1.0 + e2)
    combine = jnp.where(lane == i1, w1 * p_grp, 0.0) + jnp.where(lane == i2, w2 * p_grp, 0.0)

    for e in range(N_EXPERTS):
        gu = _dot(ub, wgu_ref[e])
        hid = _silu(gu[:, :D_EXPERT]) * gu[:, D_EXPERT:]
        hid_ref[:, e * D_EXPERT:(e + 1) * D_EXPERT] = (
            combine[:, _R_EXP + e:_R_EXP + e + 1] * hid).astype(BF16)
    h = h + _dot(hid_ref[...], wdn_ref[...])

    u3 = (_rms_plain(h) * nple_ref[...]).astype(BF16)
    gate = jax.nn.sigmoid(_dot(u3, wpg_ref[...]))
    h = h + gate * _dot(p_ref[...].astype(BF16), wpp_ref[...])
    y_ref[...] = _rms_plain(h) * nfin_ref[...]


def _ffn_call(tok_in, weights, tm):
    n = tok_in[0].shape[0]
    tok = lambda a: pl.BlockSpec((tm, a.shape[1]), lambda i: (i, 0))
    const = lambda a: pl.BlockSpec(a.shape, lambda i: (0,) * a.ndim, pipeline_mode=pl.Buffered(1))
    return pl.pallas_call(
        _ffn_kernel, grid=(n // tm,), name="ffn",
        in_specs=[tok(a) for a in tok_in] + [const(w) for w in weights],
        out_specs=tok(tok_in[0]), out_shape=jax.ShapeDtypeStruct((n, D_MODEL), F32),
        scratch_shapes=[pltpu.VMEM((tm, N_EXPERTS * D_EXPERT), BF16)],
        compiler_params=pltpu.CompilerParams(dimension_semantics=("parallel",),
                                             vmem_limit_bytes=VMEM_LIMIT),
    )(*tok_in, *weights)


_SEQ_CAST = ("w_gate_up", "w_down", "w_out", "w_ple_gate", "w_ret_up", "w_dn_up")

def _rope_tables(pos):
    half = RET_DK // 2
    inv = 1.0 / (ROPE_BASE ** (jnp.arange(half, dtype=F32) / half))
    ang = pos.astype(F32)[:, None] * inv[None, :]
    cos, sin = jnp.cos(ang), jnp.sin(ang)
    cos_t = jnp.tile(jnp.concatenate([cos, cos], axis=-1), (1, RET_H))
    sin_t = jnp.tile(jnp.concatenate([-sin, sin], axis=-1), (1, RET_H))
    return cos_t, sin_t


def _ret_tables(r, c):
    f32 = np.float32
    log_gamma = np.log(f32(1.0) - f32(2.0) ** (f32(-5.0) - np.arange(RET_H, dtype=f32)))
    row = np.arange(r)
    idx = (row % c).astype(f32)
    diff = idx[:, None] - idx[None, :]
    causal = (diff >= 0) & ((row[:, None] // c) == (row[None, :] // c))
    lg = log_gamma[:, None, None]
    decay = np.where(causal, np.exp(np.where(causal, diff, f32(0.0)) * lg), f32(0.0))
    q_dec = np.exp((idx + f32(1.0))[None, :] * log_gamma[:, None])[..., None]
    k_dec = np.exp((f32(c - 1.0) - idx)[None, :] * log_gamma[:, None])[..., None]
    chunk_dec = np.exp(f32(c) * log_gamma)
    lane_head = np.arange(RET_QK) // RET_DK
    head_mask = (lane_head[None, :] == np.arange(RET_H)[:, None]).astype(f32)[:, None, :]
    tables = (decay.reshape(RET_H * r, r), head_mask, q_dec * head_mask, k_dec * head_mask,
              np.broadcast_to(chunk_dec[lane_head][:, None], (RET_QK, RET_DV)))
    return tuple(jnp.asarray(t, F32) for t in tables)


def _dn_tables(r, c):
    i = np.arange(r)
    ri, ci = i[:, None], i[None, :]
    same = (ri // c) == (ci // c)
    lblk = ((ri >= ci) & same).astype(np.float32)
    x = ri ^ ci
    lvl, m = [], 1
    while m < c:
        lvl.append((same & (ri > ci) & (x >= m) & (x < 2 * m)).astype(np.float32))
        m *= 2
    return jnp.asarray(lblk, BF16), jnp.asarray(lblk.T, BF16), jnp.asarray(np.stack(lvl))


def _token_mixers(x, p, s_ret, s_dn, s_conv, pos_table, wts, cfg, cast_weights=()):
    bsz, seq, _ = x.shape
    n = bsz * seq
    x2d = x.reshape(n, D_MODEL)
    cos_t, sin_t = pos_table
    (q, k, v, rg, dq, dk, dv, dz, ga, gb, gbc, gbr, conv_new) = _proj_call(
        x2d, wts["norm_mix"], wts["w_in"], wts["w_gates"], cos_t, sin_t,
        wts["alog_c"], wts["dtb_c"], wts["alog_r"], wts["dtb_r"], wts["conv_w"], s_conv, cfg["tm_proj"])

    ret_tables = _ret_tables(cfg["ret_cu"], min(cfg["tt"], cfg["ret_cu"]))
    dn_tables = _dn_tables(cfg["dn_cu"], min(cfg["tt"], cfg["dn_cu"]))
    (og, ret_new, od, dn_new), cast_out = _seq_call(
        q, k, v, rg, s_ret.reshape(bsz, RET_QK, RET_DV), ret_tables,
        dq, dk, dv, gbc, gbr, dz, wts["dn_norm"], s_dn, dn_tables,
        cfg["bt"], cfg["tt"], cfg["ret_cu"], cfg["dn_cu"], cast_weights)
    ret_new = ret_new.reshape(bsz, RET_H, RET_DK, RET_DV)

    return (x2d, og, od, ga, gb, p.reshape(n, PLE_DIM)), ret_new, dn_new, conv_new, cast_out


def _prep_weights(norm_mix, w_in, conv_w, dn_a_log, dn_dt_bias, dn_norm, w_ret_up, w_dn_up, w_out,
                  norm_ffn, w_router_group, b_router_group, w_router_expert, b_router_expert,
                  w_gate_up, w_down, norm_ple, w_ple_gate, w_ple_proj, norm_final):
    assert sum(IN_WIDTHS[:6]) == _C_FRONT and sum(IN_WIDTHS[:8]) == _C_GATES
    w_in = w_in.astype(BF16)
    w_gates = jnp.pad(jnp.concatenate([w_in[:, _C_GATES:], w_in[:, _C_FRONT:_C_GATES]], axis=-1),
                      ((0, 0), (0, LANES - GB_ROWS)))
    pad_lanes = lambda a: jnp.pad(a.astype(F32), (0, LANES - a.shape[0]))[None, :]
    pad_rows = lambda a: jnp.pad(a.astype(F32), (0, GB_ROWS - a.shape[0]))[:, None]
    w_router = jnp.pad(jnp.concatenate([w_router_group, w_router_expert], axis=-1),
                       ((0, 0), (0, LANES - N_GROUPS - N_EXPERTS))).astype(BF16)
    b_router = pad_lanes(jnp.concatenate([b_router_group, b_router_expert]))
    row = lambda a: a.astype(F32)[None, :]
    return dict(
        norm_mix=row(norm_mix), w_in=w_in, w_gates=w_gates,
        alog_c=pad_lanes(dn_a_log), dtb_c=pad_lanes(dn_dt_bias),
        alog_r=pad_rows(dn_a_log), dtb_r=pad_rows(dn_dt_bias),
        conv_w=conv_w.astype(F32), dn_norm=row(dn_norm),
        w_ret_up=w_ret_up, w_dn_up=w_dn_up, w_out=w_out,
        norm_ffn=row(norm_ffn), w_router=w_router, b_router=b_router,
        w_gate_up=w_gate_up, w_down=w_down,
        norm_ple=row(norm_ple), w_ple_gate=w_ple_gate, w_ple_proj=w_ple_proj.astype(BF16),
        norm_final=row(norm_final),
    )


def kernel(x_prompt, x_sample, p_prompt, p_sample, state_ret, state_dn, state_conv, norm_mix, w_in, conv_w, dn_a_log, dn_dt_bias, dn_norm, w_ret_up, w_dn_up, w_out, norm_ffn, w_router_group, b_router_group, w_router_expert, b_router_expert, w_gate_up, w_down, norm_ple, w_ple_gate, w_ple_proj, norm_final):
    depth = w_in.shape[0]
    assert depth == 1, "one layer: the final norm is fused into the layer's last kernel"
    bp, lp, _ = x_prompt.shape
    bs, ls, _ = x_sample.shape
    wts = _prep_weights(norm_mix[0], w_in[0], conv_w[0], dn_a_log[0], dn_dt_bias[0], dn_norm[0],
                        w_ret_up[0], w_dn_up[0], w_out[0], norm_ffn[0], w_router_group[0],
                        b_router_group[0], w_router_expert[0], b_router_expert[0], w_gate_up[0],
                        w_down[0], norm_ple[0], w_ple_gate[0], w_ple_proj[0], norm_final)

    tm = 512
    bt_s = LANES // ls
    cfg_p = dict(tm=tm, tm_proj=512, bt=1, tt=512, ret_cu=256, dn_cu=128)
    cfg_s = dict(tm=tm, tm_proj=256, bt=bt_s, tt=ls, ret_cu=bt_s * ls, dn_cu=bt_s * ls)

    pos_p = _rope_tables(jnp.arange(lp, dtype=jnp.int32))
    cos_s, sin_s = _rope_tables(PAST_LEN + jnp.arange(ls, dtype=jnp.int32))
    pos_s = (jnp.tile(cos_s, (bs, 1)), jnp.tile(sin_s, (bs, 1)))

    zero_ret = jnp.zeros((bp, RET_H, RET_DK, RET_DV), F32)
    zero_dn = jnp.zeros((bp, DN_H, DN_DK, DN_DV), F32)
    zero_conv = jnp.zeros((bp, CONV_W - 1, DN_CONV_CH), F32)

    seq_steps_p = (bp // cfg_p["bt"]) * (lp // cfg_p["tt"])
    slabs = lambda w: w.reshape((seq_steps_p, -1) + w.shape[-1:])
    tok_p, r_p, d_p, c_p, cast_out = _token_mixers(
        x_prompt, p_prompt[0], zero_ret, zero_dn, zero_conv, pos_p, wts, cfg_p,
        cast_weights=tuple(slabs(wts[k]) for k in _SEQ_CAST))
    tok_s, r_s, d_s, c_s, _ = _token_mixers(x_sample, p_sample[0], state_ret[0], state_dn[0], state_conv[0],
                                            pos_s, wts, cfg_s)
    for k, w in zip(_SEQ_CAST, cast_out):
        wts[k] = w.reshape(wts[k].shape)
    wts["w_down"] = wts["w_down"].reshape(N_EXPERTS * D_EXPERT, D_MODEL)
    ffn_weights = [wts[k] for k in ("w_ret_up", "w_dn_up", "w_out", "norm_ffn", "w_router", "b_router",
                                    "w_gate_up", "w_down", "norm_ple", "w_ple_gate", "w_ple_proj",
                                    "norm_final")]
    y_p = _ffn_call(tok_p, ffn_weights, tm)
    y_s = _ffn_call(tok_s, ffn_weights, tm)
    return (y_p.reshape(x_prompt.shape), y_s.reshape(x_sample.shape),
            r_p[None], d_p[None], c_p[None], r_s[None], d_s[None], c_s[None])
```

```python
import functools

import jax
import jax.numpy as jnp
import numpy as np
from jax import lax
from jax.experimental import pallas as pl
from jax.experimental.pallas import tpu as pltpu

F32 = jnp.float32
BF16 = jnp.bfloat16

D_MODEL = 1024
RET_H, RET_DK, RET_DV = 4, 64, 128
DN_H, DN_DK, DN_DV = 4, 128, 128
CONV_W = 4
ROPE_BASE = 10000.0
PAST_LEN = 16384
N_GROUPS, EXPERTS_PER_GROUP = 4, 4
N_EXPERTS = N_GROUPS * EXPERTS_PER_GROUP
D_EXPERT = 256
PLE_DIM = 256
EPS = 1e-6

RET_QK = RET_H * RET_DK
RET_VW = RET_H * RET_DV
DN_QK = DN_H * DN_DK
DN_VW = DN_H * DN_DV
DN_CONV_CH = 2 * DN_QK + DN_VW
IN_WIDTHS = (RET_QK, RET_QK, RET_VW, RET_VW, DN_CONV_CH, DN_VW, DN_H, DN_H, D_MODEL, D_MODEL)

LANES = 128
SUBLANES = 8
VMEM_LIMIT = 56 * 1024 * 1024
GB_ROWS = 2 * DN_H
assert GB_ROWS == SUBLANES

_C_Q, _C_K, _C_V, _C_RG = 0, 256, 512, 1024
_C_DQKV, _C_DZ, _C_FRONT = 1536, 3072, 3584
_C_GATES = _C_FRONT + GB_ROWS


def _silu(x):
    return x * jax.nn.sigmoid(x)


def _softplus(x):
    return jnp.maximum(x, 0.0) + jnp.log1p(jnp.exp(-jnp.abs(x)))


def _dot(a, b):
    return jnp.dot(a, b, preferred_element_type=F32)


def _dot_nt(a, b):
    return lax.dot_general(a, b, (((1,), (1,)), ((), ())), preferred_element_type=F32)


def _dot_tn(a, b):
    return lax.dot_general(a, b, (((0,), (0,)), ((), ())), preferred_element_type=F32)


def _split3(x):
    hi = x.astype(BF16)
    r = x - hi.astype(F32)
    mid = r.astype(BF16)
    lo = (r - mid.astype(F32)).astype(BF16)
    return hi, mid, lo


def _rms_plain(x):
    return x * lax.rsqrt(jnp.mean(x * x, axis=-1, keepdims=True) + EPS)


def _proj_kernel(x_ref, gain_ref, w_ref, wg_ref, cos_ref, sin_ref,
                 alog_c_ref, dtb_c_ref, alog_r_ref, dtb_r_ref, cw_ref, c0_ref,
                 q_ref, k_ref, v_ref, rg_ref, dq_ref, dk_ref, dv_ref, dz_ref, ga_ref, gb_ref,
                 gbc_ref, gbr_ref, c_ref, ext_ref, *, bt, tt, nl):
    pad = SUBLANES
    tail = CONV_W - 1

    @pl.when(pl.program_id(0) % nl == 0)
    def _():
        ext_ref[:, pad - tail:pad, :] = c0_ref[...]

    x = x_ref[...]
    u = x * lax.rsqrt(jnp.mean(x * x, axis=-1, keepdims=True) + EPS) * gain_ref[...]
    ub = u.astype(BF16)

    def mm(lo, hi):
        return _dot(ub, w_ref[:, lo:hi])

    ext_ref[:, pad:pad + tt, :] = mm(_C_DQKV, _C_DZ).reshape(bt, tt, DN_CONV_CH)

    def conv_silu(c0, c1):
        acc = ext_ref[:, pad - tail:pad - tail + tt, c0:c1] * cw_ref[0:1, c0:c1]
        for j in range(1, CONV_W):
            acc = acc + ext_ref[:, pad - tail + j:pad - tail + j + tt, c0:c1] * cw_ref[j:j + 1, c0:c1]
        return _silu(acc).reshape(bt * tt, c1 - c0)

    def l2_heads(out_ref, c0, scale):
        units = []
        for h in range(DN_H):
            ch = conv_silu(c0 + h * DN_DK, c0 + (h + 1) * DN_DK)
            unit = ch * lax.rsqrt(jnp.sum(ch * ch, axis=-1, keepdims=True) + EPS)
            units.append(unit if scale is None else unit * scale)
        out_ref[...] = jnp.concatenate(units, axis=-1).astype(BF16)

    cos = cos_ref[...]
    sin = sin_ref[...]
    lane = lax.broadcasted_iota(jnp.int32, cos.shape, 1)
    first_half = (lane % RET_DK) < (RET_DK // 2)

    def rot(t):
        swapped = jnp.where(first_half, pltpu.roll(t, RET_QK - RET_DK // 2, 1),
                            pltpu.roll(t, RET_DK // 2, 1))
        return t * cos + swapped * sin

    gates = _dot(ub, wg_ref[...])
    ga_ref[...] = gates[:, :D_MODEL].astype(BF16)
    gb_ref[...] = gates[:, D_MODEL:2 * D_MODEL].astype(BF16)
    dab = gates[:, 2 * D_MODEL:]
    l2_heads(dq_ref, 0, DN_DK ** -0.5)
    l2_heads(dk_ref, DN_QK, None)
    v_ref[...] = mm(_C_V, _C_RG).astype(BF16)
    dv_ref[...] = conv_silu(2 * DN_QK, DN_CONV_CH).astype(BF16)
    new_tail = ext_ref[:, pad + tt - tail:pad + tt, :]
    ext_ref[:, pad - tail:pad, :] = new_tail
    c_ref[...] = new_tail
    qk = mm(_C_Q, _C_V)
    q_ref[...] = (rot(qk[:, :RET_QK]) * (RET_DK ** -0.5)).astype(BF16)
    k_ref[...] = rot(qk[:, RET_QK:]).astype(BF16)
    rg_ref[...] = _silu(mm(_C_RG, _C_DQKV)).astype(BF16)
    dz_ref[...] = _silu(mm(_C_DZ, _C_FRONT)).astype(BF16)

    lane_c = lax.broadcasted_iota(jnp.int32, dab.shape, 1)
    g_c = -jnp.exp(alog_c_ref[...]) * _softplus(dab + dtb_c_ref[...])
    gbc_ref[...] = jnp.where(lane_c < DN_H, g_c, jax.nn.sigmoid(dab))
    dabt = dab.T[:GB_ROWS]
    row_r = lax.broadcasted_iota(jnp.int32, dabt.shape, 0)
    g_r = -jnp.exp(alog_r_ref[...]) * _softplus(dabt + dtb_r_ref[...])
    gbr_ref[...] = jnp.where(row_r < DN_H, g_r, jax.nn.sigmoid(dabt))


def _proj_call(x2d, gain, w_in, w_gates, cos_t, sin_t, alog_c, dtb_c, alog_r, dtb_r, conv_w, c0, tm):
    n = x2d.shape[0]
    bsz = c0.shape[0]
    seq = n // bsz
    tt = min(tm, seq)
    bt = tm // tt
    nl = seq // tt
    table_blocks = cos_t.shape[0] // tm
    tok = lambda w: pl.BlockSpec((tm, w), lambda i: (i, 0))
    const = lambda a: pl.BlockSpec(a.shape, lambda i: (0,) * a.ndim, pipeline_mode=pl.Buffered(1))
    tab = pl.BlockSpec((tm, RET_QK), lambda i: (i % table_blocks, 0))
    cst = pl.BlockSpec((bt, CONV_W - 1, DN_CONV_CH), lambda i: (i // nl, 0, 0))
    bf = lambda w: jax.ShapeDtypeStruct((n, w), BF16)
    out_shapes = (bf(RET_QK), bf(RET_QK), bf(RET_VW), bf(RET_VW), bf(DN_QK), bf(DN_QK), bf(DN_VW),
                  bf(DN_VW), bf(D_MODEL), bf(D_MODEL),
                  jax.ShapeDtypeStruct((n, LANES), F32), jax.ShapeDtypeStruct((GB_ROWS, n), F32),
                  jax.ShapeDtypeStruct(c0.shape, F32))
    out_specs = (tok(RET_QK), tok(RET_QK), tok(RET_VW), tok(RET_VW), tok(DN_QK), tok(DN_QK), tok(DN_VW),
                 tok(DN_VW), tok(D_MODEL), tok(D_MODEL), tok(LANES),
                 pl.BlockSpec((GB_ROWS, tm), lambda i: (0, i)), cst)
    return pl.pallas_call(
        functools.partial(_proj_kernel, bt=bt, tt=tt, nl=nl), grid=(n // tm,), name="proj",
        in_specs=[tok(D_MODEL), const(gain),
                  pl.BlockSpec((D_MODEL, _C_FRONT), lambda i: (0, 0), pipeline_mode=pl.Buffered(1)),
                  const(w_gates), tab, tab,
                  const(alog_c), const(dtb_c), const(alog_r), const(dtb_r), const(conv_w), cst],
        out_specs=out_specs, out_shape=out_shapes,
        scratch_shapes=[pltpu.VMEM((bt, tt + SUBLANES, DN_CONV_CH), F32)],
        compiler_params=pltpu.CompilerParams(dimension_semantics=("arbitrary",),
                                             vmem_limit_bytes=VMEM_LIMIT),
    )(x2d, gain, w_in, w_gates, cos_t, sin_t, alog_c, dtb_c, alog_r, dtb_r, conv_w, c0)


def _ret_tile(q_ref, k_ref, v_ref, rg_ref, dec_ref, hm_ref, qd_ref, kd_ref, cd_ref,
              o_ref, s_ref, *, bt, tt, cu):
    r = bt * tt
    blk = min(tt, cu)
    n_units = r // cu
    n_seg = cu // blk

    heads = range(RET_H)
    hrow = lambda a, h, m: a[h * m:(h + 1) * m]
    for n in range(n_units):
        rows = slice(n * cu, (n + 1) * cu)
        qf = q_ref[rows, :].astype(F32)
        kf = k_ref[rows, :].astype(F32)
        vf = v_ref[rows, :].astype(F32)
        q_m = [qf * hm_ref[h] for h in heads]
        q_d = [qf * qd_ref[h] for h in heads]
        k_d = [kf * kd_ref[h] for h in heads]
        v_h = [vf[:, h * RET_DV:(h + 1) * RET_DV] for h in heads]
        scores = (_dot_nt(jnp.concatenate(q_m, axis=0).astype(BF16), k_ref[rows, :])
                  * dec_ref[...]).astype(BF16)
        intra = []
        for h in heads:
            intra.append(_dot(hrow(scores, h, cu), v_h[h].astype(BF16)))
        inters = []
        for j in range(n_seg):
            seg = slice(j * blk, (j + 1) * blk)
            b = (n * cu + j * blk) // tt
            s = s_ref[b]
            q_seg = jnp.concatenate([a[seg] for a in q_d], axis=0).astype(BF16)
            k_seg = jnp.concatenate([a[seg] for a in k_d], axis=0).astype(BF16)
            v_seg = jnp.concatenate([a[seg] for a in v_h], axis=0).astype(BF16)
            inters.append(_dot(q_seg, s.astype(BF16)))
            s_ref[b] = s * cd_ref[...] + _dot_tn(k_seg, v_seg)
        for h in heads:
            inter = jnp.concatenate([hrow(a, h, blk) for a in inters], axis=0)
            gate = rg_ref[rows, h * RET_DV:(h + 1) * RET_DV].astype(F32)
            o_ref[rows, h * RET_DV:(h + 1) * RET_DV] = (
                _rms_plain(intra[h] + inter) * gate).astype(BF16)


def _dn_tile(q_ref, k_ref, v_ref, gc_ref, gr_ref, dz_ref, dnorm_ref, lblk_ref, ublk_ref, lvl_ref,
             o_ref, s_ref, *, bt, tt, cu):
    r = bt * tt
    blk = min(tt, cu)
    n_units = r // cu
    n_seg = cu // blk
    n_lvl = lvl_ref.shape[0]

    ri = lax.broadcasted_iota(jnp.int32, (cu, cu), 0)
    ci = lax.broadcasted_iota(jnp.int32, (cu, cu), 1)
    same = (ri // blk) == (ci // blk)
    incl = same & (ri >= ci)
    strict = same & (ri > ci)

    gc = gc_ref[...]
    g_hi, g_mid, g_lo = _split3(gc)
    r_hi, r_mid, r_lo = _split3(gr_ref[...])
    lblk = lblk_ref[...]
    ublk = ublk_ref[...]

    units = [(n, h) for n in range(n_units) for h in range(DN_H)]
    cum_c, cum_r = [], []
    for n in range(n_units):
        rows = slice(n * cu, (n + 1) * cu)
        cum_c.append(_dot(lblk, g_hi[rows]) + _dot(lblk, g_mid[rows]) + _dot(lblk, g_lo[rows]))
        cum_r.append(_dot(r_hi[:, rows], ublk) + _dot(r_mid[:, rows], ublk) + _dot(r_lo[:, rows], ublk))

    qe_all, kn, kb, g_col, dec, a_mat, ab, d, rhs, p_mat = {}, {}, {}, {}, {}, {}, {}, {}, {}, {}
    for u in units:
        n, h = u
        rows = slice(n * cu, (n + 1) * cu)
        qb = q_ref[rows, h * DN_DK:(h + 1) * DN_DK]
        kb[u] = k_ref[rows, h * DN_DK:(h + 1) * DN_DK]
        kn[u] = kb[u].astype(F32)
        vc = v_ref[rows, h * DN_DV:(h + 1) * DN_DV].astype(F32)
        g_col[u] = cum_c[n][:, h:h + 1]
        g_row = cum_r[n][h:h + 1, :]
        beta = gc[rows, DN_H + h:DN_H + h + 1]
        dec[u] = jnp.where(incl, jnp.exp(jnp.where(incl, g_col[u] - g_row, 0.0)), 0.0)
        kq_k = _dot_nt(jnp.concatenate([kb[u], qb], axis=0), kb[u])
        a_mat[u] = jnp.where(strict, kq_k[:cu] * dec[u] * beta, 0.0)
        p_mat[u] = (kq_k[cu:] * dec[u]).astype(BF16)
        ab[u] = a_mat[u].astype(BF16)
        e_g = jnp.exp(g_col[u])
        rhs[u] = jnp.concatenate([beta * vc, beta * e_g * kn[u]], axis=-1)
        qe_all[u] = qb.astype(F32) * e_g
        d[u] = -(lvl_ref[0] * a_mat[u])
    for lv in range(1, n_lvl):
        m = 1 << lv
        by_rows = m % SUBLANES == 0

        def lower(t):
            return t.reshape(cu // (2 * m), 2, m, t.shape[-1])[:, 1].reshape(cu // 2, t.shape[-1])

        for u in units:
            db = d[u].astype(BF16)
            if not by_rows:
                w = a_mat[u] + _dot(db, ab[u])
                z = w + _dot(w.astype(BF16), db)
                d[u] = d[u] - lvl_ref[lv] * z
                continue
            w = lower(a_mat[u]) + _dot(lower(d[u]).astype(BF16), ab[u])
            z = w + _dot(w.astype(BF16), db)
            d_low = lower(d[u]) - lower(lvl_ref[lv]) * z
            d4 = d[u].reshape(cu // (2 * m), 2, m, cu)
            d[u] = jnp.concatenate([d4[:, :1], d_low.reshape(cu // (2 * m), 1, m, cu)],
                                   axis=1).reshape(cu, cu)
    sol, e_last = {}, {}
    for u in units:
        sol[u] = rhs[u] + _dot(d[u].astype(BF16), rhs[u].astype(BF16))
        if n_seg == 1:
            g_last = g_col[u][cu - 1:cu, :]
            k_dec_t = (kn[u] * jnp.exp(g_last - g_col[u])).T.astype(BF16)
            p_mat[u] = jnp.concatenate([p_mat[u], k_dec_t], axis=0)
            e_last[u] = jnp.exp(g_last)

    for n in range(n_units):
        for h in range(DN_H):
            u = (n, h)
            u_mat, w_mat = sol[u][:, :DN_DV], sol[u][:, DN_DV:]
            qe = qe_all[u]
            if n_seg == 1:
                b = (n * cu) // tt
                s = s_ref[b, h]
                ws_qs = _dot(jnp.concatenate([w_mat, qe], axis=0).astype(BF16), s.astype(BF16))
                both = _dot(p_mat[u], (u_mat - ws_qs[:cu]).astype(BF16))
                s_ref[b, h] = s * e_last[u] + both[cu:]
                gate = dz_ref[n * cu:(n + 1) * cu, h * DN_DV:(h + 1) * DN_DV].astype(F32)
                o_ref[n * cu:(n + 1) * cu, h * DN_DV:(h + 1) * DN_DV] = (
                    _rms_plain(ws_qs[cu:] + both[:cu]) * dnorm_ref[...] * gate).astype(BF16)
                continue
            deltas, qss = [], []
            for j in range(n_seg):
                r0 = j * blk
                b = (n * cu + r0) // tt
                sb = s_ref[b, h].astype(BF16)
                lhs = jnp.concatenate([w_mat[r0:r0 + blk], qe[r0:r0 + blk]], axis=0).astype(BF16)
                ws_qs = _dot(lhs, sb)
                deltas.append(u_mat[r0:r0 + blk] - ws_qs[:blk])
                qss.append(ws_qs[blk:])
            delta_b = jnp.concatenate(deltas, axis=0).astype(BF16)
            o = jnp.concatenate(qss, axis=0) + _dot(p_mat[u], delta_b)
            for j in range(n_seg):
                r0 = j * blk
                b = (n * cu + r0) // tt
                g_last = g_col[u][r0 + blk - 1:r0 + blk, :]
                k_dec = (kn[u][r0:r0 + blk] * jnp.exp(g_last - g_col[u][r0:r0 + blk])).astype(BF16)
                s_ref[b, h] = s_ref[b, h] * jnp.exp(g_last) + _dot_tn(k_dec, delta_b[r0:r0 + blk])
            gate = dz_ref[n * cu:(n + 1) * cu, h * DN_DV:(h + 1) * DN_DV].astype(F32)
            o_ref[n * cu:(n + 1) * cu, h * DN_DV:(h + 1) * DN_DV] = (
                _rms_plain(o) * dnorm_ref[...] * gate).astype(BF16)


def _seq_kernel(rq_ref, rk_ref, rv_ref, rg_ref, rs0_ref, dec_ref, hm_ref, qd_ref, kd_ref, cd_ref,
                dq_ref, dk_ref, dv_ref, gc_ref, gr_ref, dz_ref, dnorm_ref, ds0_ref, lblk_ref, ublk_ref,
                lvl_ref, *rest, bt, tt, ret_cu, dn_cu, n_cast):
    cast_in, (og_ref, rs_ref, od_ref, ds_ref), cast_out = rest[:n_cast], rest[n_cast:n_cast + 4], rest[n_cast + 4:]
    for src, dst in zip(cast_in, cast_out):
        dst[...] = src[...].astype(BF16)

    @pl.when(pl.program_id(1) == 0)
    def _():
        rs_ref[...] = rs0_ref[...]
        ds_ref[...] = ds0_ref[...]

    _dn_tile(dq_ref, dk_ref, dv_ref, gc_ref, gr_ref, dz_ref, dnorm_ref, lblk_ref, ublk_ref, lvl_ref,
             od_ref, ds_ref, bt=bt, tt=tt, cu=dn_cu)
    _ret_tile(rq_ref, rk_ref, rv_ref, rg_ref, dec_ref, hm_ref, qd_ref, kd_ref, cd_ref,
              og_ref, rs_ref, bt=bt, tt=tt, cu=ret_cu)


def _seq_call(rq, rk, rv, rg, rs0, ret_tables, dq, dk, dv, gbc, gbr, dz, dn_norm, ds0, dn_tables,
              bt, tt, ret_cu, dn_cu, cast_weights=()):
    bsz = ds0.shape[0]
    seq = dq.shape[0] // bsz
    nl = seq // tt
    r = bt * tt
    n_steps = (bsz // bt) * nl
    tok = lambda w: pl.BlockSpec((r, w), lambda b, l: (b * nl + l, 0))
    const = lambda a: pl.BlockSpec(a.shape, lambda b, l: (0,) * a.ndim)
    rst = pl.BlockSpec((bt, RET_QK, RET_DV), lambda b, l: (b, 0, 0))
    dst = pl.BlockSpec((bt, DN_H, DN_DK, DN_DV), lambda b, l: (b, 0, 0, 0))
    slab = lambda a: pl.BlockSpec((a.shape[0] // n_steps,) + a.shape[1:],
                                  lambda b, l: (b * nl + l,) + (0,) * (a.ndim - 1))
    outs = pl.pallas_call(
        functools.partial(_seq_kernel, bt=bt, tt=tt, ret_cu=ret_cu, dn_cu=dn_cu, n_cast=len(cast_weights)),
        grid=(bsz // bt, nl), name="seq",
        in_specs=([tok(RET_QK), tok(RET_QK), tok(RET_VW), tok(RET_VW), rst]
                  + [const(t) for t in ret_tables]
                  + [tok(DN_QK), tok(DN_QK), tok(DN_VW), tok(LANES),
                     pl.BlockSpec((GB_ROWS, r), lambda b, l: (0, b * nl + l)),
                     tok(DN_VW), const(dn_norm), dst] + [const(t) for t in dn_tables]
                  + [slab(a) for a in cast_weights]),
        out_specs=(tok(RET_VW), rst, tok(DN_VW), dst) + tuple(slab(a) for a in cast_weights),
        out_shape=(jax.ShapeDtypeStruct((bsz * seq, RET_VW), BF16), jax.ShapeDtypeStruct(rs0.shape, F32),
                   jax.ShapeDtypeStruct((bsz * seq, DN_VW), BF16), jax.ShapeDtypeStruct(ds0.shape, F32))
        + tuple(jax.ShapeDtypeStruct(a.shape, BF16) for a in cast_weights),
        compiler_params=pltpu.CompilerParams(dimension_semantics=("parallel", "arbitrary"),
                                             vmem_limit_bytes=VMEM_LIMIT),
    )(rq, rk, rv, rg, rs0, *ret_tables, dq, dk, dv, gbc, gbr, dz, dn_norm, ds0, *dn_tables, *cast_weights)
    return outs[:4], outs[4:]


_R_EXP = N_GROUPS


def _ffn_kernel(x_ref, og_ref, od_ref, ga_ref, gb_ref, p_ref,
                wru_ref, wdu_ref, wo_ref, nffn_ref, wr_ref, br_ref, wgu_ref, wdn_ref, nple_ref, wpg_ref,
                wpp_ref, nfin_ref, y_ref, hid_ref):
    branch_a = _dot(og_ref[...], wru_ref[...])
    branch_b = _dot(od_ref[...], wdu_ref[...])
    merged = (jax.nn.sigmoid(ga_ref[...].astype(F32)) * branch_a
              + jax.nn.sigmoid(gb_ref[...].astype(F32)) * branch_b)
    h = x_ref[...] + _dot(merged.astype(BF16), wo_ref[...])

    u = _rms_plain(h) * nffn_ref[...]
    ub = u.astype(BF16)

    logits = _dot(ub, wr_ref[...]) + br_ref[...]
    lane = lax.broadcasted_iota(jnp.int32, logits.shape, 1)
    neg = jnp.float32(-jnp.inf)
    big = jnp.int32(LANES)
    gl = jnp.where(lane < N_GROUPS, logits, neg)
    g_max = jnp.max(gl, axis=-1, keepdims=True)
    p_grp = 1.0 / jnp.sum(jnp.exp(gl - g_max), axis=-1, keepdims=True)
    grp = jnp.min(jnp.where(gl == g_max, lane, big), axis=-1, keepdims=True)
    e_idx = lane - _R_EXP
    in_grp = (e_idx >= 0) & (e_idx < N_EXPERTS) & ((e_idx // EXPERTS_PER_GROUP) == grp)
    el = jnp.where(in_grp, logits, neg)
    v1 = jnp.max(el, axis=-1, keepdims=True)
    i1 = jnp.min(jnp.where(el == v1, lane, big), axis=-1, keepdims=True)
    el2 = jnp.where(lane == i1, neg, el)
    v2 = jnp.max(el2, axis=-1, keepdims=True)
    i2 = jnp.min(jnp.where(el2 == v2, lane, big), axis=-1, keepdims=True)
    e2 = jnp.exp(v2 - v1)
    w1 = 1.0 / (1.0 + e2)
    w2 = e2 / (1.0 + e2)
    combine = jnp.where(lane == i1, w1 * p_grp, 0.0) + jnp.where(lane == i2, w2 * p_grp, 0.0)

    for e in range(N_EXPERTS):
        gu = _dot(ub, wgu_ref[e])
        hid = _silu(gu[:, :D_EXPERT]) * gu[:, D_EXPERT:]
        hid_ref[:, e * D_EXPERT:(e + 1) * D_EXPERT] = (
            combine[:, _R_EXP + e:_R_EXP + e + 1] * hid).astype(BF16)
    h = h + _dot(hid_ref[...], wdn_ref[...])

    u3 = (_rms_plain(h) * nple_ref[...]).astype(BF16)
    gate = jax.nn.sigmoid(_dot(u3, wpg_ref[...]))
    h = h + gate * _dot(p_ref[...].astype(BF16), wpp_ref[...])
    y_ref[...] = _rms_plain(h) * nfin_ref[...]


def _ffn_call(tok_in, weights, tm):
    n = tok_in[0].shape[0]
    tok = lambda a: pl.BlockSpec((tm, a.shape[1]), lambda i: (i, 0))
    const = lambda a: pl.BlockSpec(a.shape, lambda i: (0,) * a.ndim, pipeline_mode=pl.Buffered(1))
    return pl.pallas_call(
        _ffn_kernel, grid=(n // tm,), name="ffn",
        in_specs=[tok(a) for a in tok_in] + [const(w) for w in weights],
        out_specs=tok(tok_in[0]), out_shape=jax.ShapeDtypeStruct((n, D_MODEL), F32),
        scratch_shapes=[pltpu.VMEM((tm, N_EXPERTS * D_EXPERT), BF16)],
        compiler_params=pltpu.CompilerParams(dimension_semantics=("parallel",),
                                             vmem_limit_bytes=VMEM_LIMIT),
    )(*tok_in, *weights)


_SEQ_CAST = ("w_gate_up", "w_down", "w_out", "w_ple_gate", "w_ret_up", "w_dn_up")

def _rope_tables(pos):
    half = RET_DK // 2
    inv = 1.0 / (ROPE_BASE ** (jnp.arange(half, dtype=F32) / half))
    ang = pos.astype(F32)[:, None] * inv[None, :]
    cos, sin = jnp.cos(ang), jnp.sin(ang)
    cos_t = jnp.tile(jnp.concatenate([cos, cos], axis=-1), (1, RET_H))
    sin_t = jnp.tile(jnp.concatenate([-sin, sin], axis=-1), (1, RET_H))
    return cos_t, sin_t


def _ret_tables(r, c):
    f32 = np.float32
    log_gamma = np.log(f32(1.0) - f32(2.0) ** (f32(-5.0) - np.arange(RET_H, dtype=f32)))
    row = np.arange(r)
    idx = (row % c).astype(f32)
    diff = idx[:, None] - idx[None, :]
    causal = (diff >= 0) & ((row[:, None] // c) == (row[None, :] // c))
    lg = log_gamma[:, None, None]
    decay = np.where(causal, np.exp(np.where(causal, diff, f32(0.0)) * lg), f32(0.0))
    q_dec = np.exp((idx + f32(1.0))[None, :] * log_gamma[:, None])[..., None]
    k_dec = np.exp((f32(c - 1.0) - idx)[None, :] * log_gamma[:, None])[..., None]
    chunk_dec = np.exp(f32(c) * log_gamma)
    lane_head = np.arange(RET_QK) // RET_DK
    head_mask = (lane_head[None, :] == np.arange(RET_H)[:, None]).astype(f32)[:, None, :]
    tables = (decay.reshape(RET_H * r, r), head_mask, q_dec * head_mask, k_dec * head_mask,
              np.broadcast_to(chunk_dec[lane_head][:, None], (RET_QK, RET_DV)))
    return tuple(jnp.asarray(t, F32) for t in tables)


def _dn_tables(r, c):
    i = np.arange(r)
    ri, ci = i[:, None], i[None, :]
    same = (ri // c) == (ci // c)
    lblk = ((ri >= ci) & same).astype(np.float32)
    x = ri ^ ci
    lvl, m = [], 1
    while m < c:
        lvl.append((same & (ri > ci) & (x >= m) & (x < 2 * m)).astype(np.float32))
        m *= 2
    return jnp.asarray(lblk, BF16), jnp.asarray(lblk.T, BF16), jnp.asarray(np.stack(lvl))


def _token_mixers(x, p, s_ret, s_dn, s_conv, pos_table, wts, cfg, cast_weights=()):
    bsz, seq, _ = x.shape
    n = bsz * seq
    x2d = x.reshape(n, D_MODEL)
    cos_t, sin_t = pos_table
    (q, k, v, rg, dq, dk, dv, dz, ga, gb, gbc, gbr, conv_new) = _proj_call(
        x2d, wts["norm_mix"], wts["w_in"], wts["w_gates"], cos_t, sin_t,
        wts["alog_c"], wts["dtb_c"], wts["alog_r"], wts["dtb_r"], wts["conv_w"], s_conv, cfg["tm_proj"])

    ret_tables = _ret_tables(cfg["ret_cu"], min(cfg["tt"], cfg["ret_cu"]))
    dn_tables = _dn_tables(cfg["dn_cu"], min(cfg["tt"], cfg["dn_cu"]))
    (og, ret_new, od, dn_new), cast_out = _seq_call(
        q, k, v, rg, s_ret.reshape(bsz, RET_QK, RET_DV), ret_tables,
        dq, dk, dv, gbc, gbr, dz, wts["dn_norm"], s_dn, dn_tables,
        cfg["bt"], cfg["tt"], cfg["ret_cu"], cfg["dn_cu"], cast_weights)
    ret_new = ret_new.reshape(bsz, RET_H, RET_DK, RET_DV)

    return (x2d, og, od, ga, gb, p.reshape(n, PLE_DIM)), ret_new, dn_new, conv_new, cast_out


def _prep_weights(norm_mix, w_in, conv_w, dn_a_log, dn_dt_bias, dn_norm, w_ret_up, w_dn_up, w_out,
                  norm_ffn, w_router_group, b_router_group, w_router_expert, b_router_expert,
                  w_gate_up, w_down, norm_ple, w_ple_gate, w_ple_proj, norm_final):
    assert sum(IN_WIDTHS[:6]) == _C_FRONT and sum(IN_WIDTHS[:8]) == _C_GATES
    w_in = w_in.astype(BF16)
    w_gates = jnp.pad(jnp.concatenate([w_in[:, _C_GATES:], w_in[:, _C_FRONT:_C_GATES]], axis=-1),
                      ((0, 0), (0, LANES - GB_ROWS)))
    pad_lanes = lambda a: jnp.pad(a.astype(F32), (0, LANES - a.shape[0]))[None, :]
    pad_rows = lambda a: jnp.pad(a.astype(F32), (0, GB_ROWS - a.shape[0]))[:, None]
    w_router = jnp.pad(jnp.concatenate([w_router_group, w_router_expert], axis=-1),
                       ((0, 0), (0, LANES - N_GROUPS - N_EXPERTS))).astype(BF16)
    b_router = pad_lanes(jnp.concatenate([b_router_group, b_router_expert]))
    row = lambda a: a.astype(F32)[None, :]
    return dict(
        norm_mix=row(norm_mix), w_in=w_in, w_gates=w_gates,
        alog_c=pad_lanes(dn_a_log), dtb_c=pad_lanes(dn_dt_bias),
        alog_r=pad_rows(dn_a_log), dtb_r=pad_rows(dn_dt_bias),
        conv_w=conv_w.astype(F32), dn_norm=row(dn_norm),
        w_ret_up=w_ret_up, w_dn_up=w_dn_up, w_out=w_out,
        norm_ffn=row(norm_ffn), w_router=w_router, b_router=b_router,
        w_gate_up=w_gate_up, w_down=w_down,
        norm_ple=row(norm_ple), w_ple_gate=w_ple_gate, w_ple_proj=w_ple_proj.astype(BF16),
        norm_final=row(norm_final),
    )


def kernel(x_prompt, x_sample, p_prompt, p_sample, state_ret, state_dn, state_conv, norm_mix, w_in, conv_w, dn_a_log, dn_dt_bias, dn_norm, w_ret_up, w_dn_up, w_out, norm_ffn, w_router_group, b_router_group, w_router_expert, b_router_expert, w_gate_up, w_down, norm_ple, w_ple_gate, w_ple_proj, norm_final):
    depth = w_in.shape[0]
    assert depth == 1, "one layer: the final norm is fused into the layer's last kernel"
    bp, lp, _ = x_prompt.shape
    bs, ls, _ = x_sample.shape
    wts = _prep_weights(norm_mix[0], w_in[0], conv_w[0], dn_a_log[0], dn_dt_bias[0], dn_norm[0],
                        w_ret_up[0], w_dn_up[0], w_out[0], norm_ffn[0], w_router_group[0],
                        b_router_group[0], w_router_expert[0], b_router_expert[0], w_gate_up[0],
                        w_down[0], norm_ple[0], w_ple_gate[0], w_ple_proj[0], norm_final)

    tm = 512
    bt_s = LANES // ls
    cfg_p = dict(tm=tm, tm_proj=512, bt=1, tt=512, ret_cu=256, dn_cu=128)
    cfg_s = dict(tm=tm, tm_proj=256, bt=bt_s, tt=ls, ret_cu=bt_s * ls, dn_cu=bt_s * ls)

    pos_p = _rope_tables(jnp.arange(lp, dtype=jnp.int32))
    cos_s, sin_s = _rope_tables(PAST_LEN + jnp.arange(ls, dtype=jnp.int32))
    pos_s = (jnp.tile(cos_s, (bs, 1)), jnp.tile(sin_s, (bs, 1)))

    zero_ret = jnp.zeros((bp, RET_H, RET_DK, RET_DV), F32)
    zero_dn = jnp.zeros((bp, DN_H, DN_DK, DN_DV), F32)
    zero_conv = jnp.zeros((bp, CONV_W - 1, DN_CONV_CH), F32)

    seq_steps_p = (bp // cfg_p["bt"]) * (lp // cfg_p["tt"])
    slabs = lambda w: w.reshape((seq_steps_p, -1) + w.shape[-1:])
    tok_p, r_p, d_p, c_p, cast_out = _token_mixers(
        x_prompt, p_prompt[0], zero_ret, zero_dn, zero_conv, pos_p, wts, cfg_p,
        cast_weights=tuple(slabs(wts[k]) for k in _SEQ_CAST))
    tok_s, r_s, d_s, c_s, _ = _token_mixers(x_sample, p_sample[0], state_ret[0], state_dn[0], state_conv[0],
                                            pos_s, wts, cfg_s)
    for k, w in zip(_SEQ_CAST, cast_out):
        wts[k] = w.reshape(wts[k].shape)
    wts["w_down"] = wts["w_down"].reshape(N_EXPERTS * D_EXPERT, D_MODEL)
    ffn_weights = [wts[k] for k in ("w_ret_up", "w_dn_up", "w_out", "norm_ffn", "w_router", "b_router",
                                    "w_gate_up", "w_down", "norm_ple", "w_ple_gate", "w_ple_proj",
                                    "norm_final")]
    y_p = _ffn_call(tok_p, ffn_weights, tm)
    y_s = _ffn_call(tok_s, ffn_weights, tm)
    return (y_p.reshape(x_prompt.shape), y_s.reshape(x_sample.shape),
            r_p[None], d_p[None], c_p[None], r_s[None], d_s[None], c_s[None])
```

```python
import functools

import jax
import jax.numpy as jnp
import numpy as np
from jax import lax
from jax.experimental import pallas as pl
from jax.experimental.pallas import tpu as pltpu

F32 = jnp.float32
BF16 = jnp.bfloat16

D_MODEL = 1024
RET_H, RET_DK, RET_DV = 4, 64, 128
DN_H, DN_DK, DN_DV = 4, 128, 128
CONV_W = 4
ROPE_BASE = 10000.0
PAST_LEN = 16384
N_GROUPS, EXPERTS_PER_GROUP = 4, 4
N_EXPERTS = N_GROUPS * EXPERTS_PER_GROUP
D_EXPERT = 256
PLE_DIM = 256
EPS = 1e-6

RET_QK = RET_H * RET_DK
RET_VW = RET_H * RET_DV
DN_QK = DN_H * DN_DK
DN_VW = DN_H * DN_DV
DN_CONV_CH = 2 * DN_QK + DN_VW
IN_WIDTHS = (RET_QK, RET_QK, RET_VW, RET_VW, DN_CONV_CH, DN_VW, DN_H, DN_H, D_MODEL, D_MODEL)

LANES = 128
SUBLANES = 8
VMEM_LIMIT = 56 * 1024 * 1024
GB_ROWS = 2 * DN_H
assert GB_ROWS == SUBLANES

_C_Q, _C_K, _C_V, _C_RG = 0, 256, 512, 1024
_C_DQKV, _C_DZ, _C_FRONT = 1536, 3072, 3584
_C_GATES = _C_FRONT + GB_ROWS


def _silu(x):
    return x * jax.nn.sigmoid(x)


def _softplus(x):
    return jnp.maximum(x, 0.0) + jnp.log1p(jnp.exp(-jnp.abs(x)))


def _dot(a, b):
    return jnp.dot(a, b, preferred_element_type=F32)


def _dot_nt(a, b):
    return lax.dot_general(a, b, (((1,), (1,)), ((), ())), preferred_element_type=F32)


def _dot_tn(a, b):
    return lax.dot_general(a, b, (((0,), (0,)), ((), ())), preferred_element_type=F32)


def _split3(x):
    hi = x.astype(BF16)
    r = x - hi.astype(F32)
    mid = r.astype(BF16)
    lo = (r - mid.astype(F32)).astype(BF16)
    return hi, mid, lo


def _rms_plain(x):
    return x * lax.rsqrt(jnp.mean(x * x, axis=-1, keepdims=True) + EPS)


def _proj_kernel(x_ref, gain_ref, w_ref, wg_ref, cos_ref, sin_ref,
                 alog_c_ref, dtb_c_ref, alog_r_ref, dtb_r_ref, cw_ref, c0_ref,
                 q_ref, k_ref, v_ref, rg_ref, dq_ref, dk_ref, dv_ref, dz_ref, ga_ref, gb_ref,
                 gbc_ref, gbr_ref, c_ref, ext_ref, *, bt, tt, nl):
    pad = SUBLANES
    tail = CONV_W - 1

    @pl.when(pl.program_id(0) % nl == 0)
    def _():
        ext_ref[:, pad - tail:pad, :] = c0_ref[...]

    x = x_ref[...]
    u = x * lax.rsqrt(jnp.mean(x * x, axis=-1, keepdims=True) + EPS) * gain_ref[...]
    ub = u.astype(BF16)

    def mm(lo, hi):
        return _dot(ub, w_ref[:, lo:hi])

    ext_ref[:, pad:pad + tt, :] = mm(_C_DQKV, _C_DZ).reshape(bt, tt, DN_CONV_CH)

    def conv_silu(c0, c1):
        acc = ext_ref[:, pad - tail:pad - tail + tt, c0:c1] * cw_ref[0:1, c0:c1]
        for j in range(1, CONV_W):
            acc = acc + ext_ref[:, pad - tail + j:pad - tail + j + tt, c0:c1] * cw_ref[j:j + 1, c0:c1]
        return _silu(acc).reshape(bt * tt, c1 - c0)

    def l2_heads(out_ref, c0, scale):
        units = []
        for h in range(DN_H):
            ch = conv_silu(c0 + h * DN_DK, c0 + (h + 1) * DN_DK)
            unit = ch * lax.rsqrt(jnp.sum(ch * ch, axis=-1, keepdims=True) + EPS)
            units.append(unit if scale is None else unit * scale)
        out_ref[...] = jnp.concatenate(units, axis=-1).astype(BF16)

    cos = cos_ref[...]
    sin = sin_ref[...]
    lane = lax.broadcasted_iota(jnp.int32, cos.shape, 1)
    first_half = (lane % RET_DK) < (RET_DK // 2)

    def rot(t):
        swapped = jnp.where(first_half, pltpu.roll(t, RET_QK - RET_DK // 2, 1),
                            pltpu.roll(t, RET_DK // 2, 1))
        return t * cos + swapped * sin

    gates = _dot(ub, wg_ref[...])
    ga_ref[...] = gates[:, :D_MODEL].astype(BF16)
    gb_ref[...] = gates[:, D_MODEL:2 * D_MODEL].astype(BF16)
    dab = gates[:, 2 * D_MODEL:]
    l2_heads(dq_ref, 0, DN_DK ** -0.5)
    l2_heads(dk_ref, DN_QK, None)
    v_ref[...] = mm(_C_V, _C_RG).astype(BF16)
    dv_ref[...] = conv_silu(2 * DN_QK, DN_CONV_CH).astype(BF16)
    new_tail = ext_ref[:, pad + tt - tail:pad + tt, :]
    ext_ref[:, pad - tail:pad, :] = new_tail
    c_ref[...] = new_tail
    qk = mm(_C_Q, _C_V)
    q_ref[...] = (rot(qk[:, :RET_QK]) * (RET_DK ** -0.5)).astype(BF16)
    k_ref[...] = rot(qk[:, RET_QK:]).astype(BF16)
    rg_ref[...] = _silu(mm(_C_RG, _C_DQKV)).astype(BF16)
    dz_ref[...] = _silu(mm(_C_DZ, _C_FRONT)).astype(BF16)

    lane_c = lax.broadcasted_iota(jnp.int32, dab.shape, 1)
    g_c = -jnp.exp(alog_c_ref[...]) * _softplus(dab + dtb_c_ref[...])
    gbc_ref[...] = jnp.where(lane_c < DN_H, g_c, jax.nn.sigmoid(dab))
    dabt = dab.T[:GB_ROWS]
    row_r = lax.broadcasted_iota(jnp.int32, dabt.shape, 0)
    g_r = -jnp.exp(alog_r_ref[...]) * _softplus(dabt + dtb_r_ref[...])
    gbr_ref[...] = jnp.where(row_r < DN_H, g_r, jax.nn.sigmoid(dabt))


def _proj_call(x2d, gain, w_in, w_gates, cos_t, sin_t, alog_c, dtb_c, alog_r, dtb_r, conv_w, c0, tm):
    n = x2d.shape[0]
    bsz = c0.shape[0]
    seq = n // bsz
    tt = min(tm, seq)
    bt = tm // tt
    nl = seq // tt
    table_blocks = cos_t.shape[0] // tm
    tok = lambda w: pl.BlockSpec((tm, w), lambda i: (i, 0))
    const = lambda a: pl.BlockSpec(a.shape, lambda i: (0,) * a.ndim, pipeline_mode=pl.Buffered(1))
    tab = pl.BlockSpec((tm, RET_QK), lambda i: (i % table_blocks, 0))
    cst = pl.BlockSpec((bt, CONV_W - 1, DN_CONV_CH), lambda i: (i // nl, 0, 0))
    bf = lambda w: jax.ShapeDtypeStruct((n, w), BF16)
    out_shapes = (bf(RET_QK), bf(RET_QK), bf(RET_VW), bf(RET_VW), bf(DN_QK), bf(DN_QK), bf(DN_VW),
                  bf(DN_VW), bf(D_MODEL), bf(D_MODEL),
                  jax.ShapeDtypeStruct((n, LANES), F32), jax.ShapeDtypeStruct((GB_ROWS, n), F32),
                  jax.ShapeDtypeStruct(c0.shape, F32))
    out_specs = (tok(RET_QK), tok(RET_QK), tok(RET_VW), tok(RET_VW), tok(DN_QK), tok(DN_QK), tok(DN_VW),
                 tok(DN_VW), tok(D_MODEL), tok(D_MODEL), tok(LANES),
                 pl.BlockSpec((GB_ROWS, tm), lambda i: (0, i)), cst)
    return pl.pallas_call(
        functools.partial(_proj_kernel, bt=bt, tt=tt, nl=nl), grid=(n // tm,), name="proj",
        in_specs=[tok(D_MODEL), const(gain),
                  pl.BlockSpec((D_MODEL, _C_FRONT), lambda i: (0, 0), pipeline_mode=pl.Buffered(1)),
                  const(w_gates), tab, tab,
                  const(alog_c), const(dtb_c), const(alog_r), const(dtb_r), const(conv_w), cst],
        out_specs=out_specs, out_shape=out_shapes,
        scratch_shapes=[pltpu.VMEM((bt, tt + SUBLANES, DN_CONV_CH), F32)],
        compiler_params=pltpu.CompilerParams(dimension_semantics=("arbitrary",),
                                             vmem_limit_bytes=VMEM_LIMIT),
    )(x2d, gain, w_in, w_gates, cos_t, sin_t, alog_c, dtb_c, alog_r, dtb_r, conv_w, c0)


def _ret_tile(q_ref, k_ref, v_ref, rg_ref, dec_ref, hm_ref, qd_ref, kd_ref, cd_ref,
              o_ref, s_ref, *, bt, tt, cu):
    r = bt * tt
    blk = min(tt, cu)
    n_units = r // cu
    n_seg = cu // blk

    heads = range(RET_H)
    hrow = lambda a, h, m: a[h * m:(h + 1) * m]
    for n in range(n_units):
        rows = slice(n * cu, (n + 1) * cu)
        qf = q_ref[rows, :].astype(F32)
        kf = k_ref[rows, :].astype(F32)
        vf = v_ref[rows, :].astype(F32)
        q_m = [qf * hm_ref[h] for h in heads]
        q_d = [qf * qd_ref[h] for h in heads]
        k_d = [kf * kd_ref[h] for h in heads]
        v_h = [vf[:, h * RET_DV:(h + 1) * RET_DV] for h in heads]
        scores = (_dot_nt(jnp.concatenate(q_m, axis=0).astype(BF16), k_ref[rows, :])
                  * dec_ref[...]).astype(BF16)
        intra = []
        for h in heads:
            intra.append(_dot(hrow(scores, h, cu), v_h[h].astype(BF16)))
        inters = []
        for j in range(n_seg):
            seg = slice(j * blk, (j + 1) * blk)
            b = (n * cu + j * blk) // tt
            s = s_ref[b]
            q_seg = jnp.concatenate([a[seg] for a in q_d], axis=0).astype(BF16)
            k_seg = jnp.concatenate([a[seg] for a in k_d], axis=0).astype(BF16)
            v_seg = jnp.concatenate([a[seg] for a in v_h], axis=0).astype(BF16)
            inters.append(_dot(q_seg, s.astype(BF16)))
            s_ref[b] = s * cd_ref[...] + _dot_tn(k_seg, v_seg)
        for h in heads:
            inter = jnp.concatenate([hrow(a, h, blk) for a in inters], axis=0)
            gate = rg_ref[rows, h * RET_DV:(h + 1) * RET_DV].astype(F32)
            o_ref[rows, h * RET_DV:(h + 1) * RET_DV] = (
                _rms_plain(intra[h] + inter) * gate).astype(BF16)


def _dn_tile(q_ref, k_ref, v_ref, gc_ref, gr_ref, dz_ref, dnorm_ref, lblk_ref, ublk_ref, lvl_ref,
             o_ref, s_ref, *, bt, tt, cu):
    r = bt * tt
    blk = min(tt, cu)
    n_units = r // cu
    n_seg = cu // blk
    n_lvl = lvl_ref.shape[0]

    ri = lax.broadcasted_iota(jnp.int32, (cu, cu), 0)
    ci = lax.broadcasted_iota(jnp.int32, (cu, cu), 1)
    same = (ri // blk) == (ci // blk)
    incl = same & (ri >= ci)
    strict = same & (ri > ci)

    gc = gc_ref[...]
    g_hi, g_mid, g_lo = _split3(gc)
    r_hi, r_mid, r_lo = _split3(gr_ref[...])
    lblk = lblk_ref[...]
    ublk = ublk_ref[...]

    units = [(n, h) for n in range(n_units) for h in range(DN_H)]
    cum_c, cum_r = [], []
    for n in range(n_units):
        rows = slice(n * cu, (n + 1) * cu)
        cum_c.append(_dot(lblk, g_hi[rows]) + _dot(lblk, g_mid[rows]) + _dot(lblk, g_lo[rows]))
        cum_r.append(_dot(r_hi[:, rows], ublk) + _dot(r_mid[:, rows], ublk) + _dot(r_lo[:, rows], ublk))

    qe_all, kn, kb, g_col, dec, a_mat, ab, d, rhs, p_mat = {}, {}, {}, {}, {}, {}, {}, {}, {}, {}
    for u in units:
        n, h = u
        rows = slice(n * cu, (n + 1) * cu)
        qb = q_ref[rows, h * DN_DK:(h + 1) * DN_DK]
        kb[u] = k_ref[rows, h * DN_DK:(h + 1) * DN_DK]
        kn[u] = kb[u].astype(F32)
        vc = v_ref[rows, h * DN_DV:(h + 1) * DN_DV].astype(F32)
        g_col[u] = cum_c[n][:, h:h + 1]
        g_row = cum_r[n][h:h + 1, :]
        beta = gc[rows, DN_H + h:DN_H + h + 1]
        dec[u] = jnp.where(incl, jnp.exp(jnp.where(incl, g_col[u] - g_row, 0.0)), 0.0)
        kq_k = _dot_nt(jnp.concatenate([kb[u], qb], axis=0), kb[u])
        a_mat[u] = jnp.where(strict, kq_k[:cu] * dec[u] * beta, 0.0)
        p_mat[u] = (kq_k[cu:] * dec[u]).astype(BF16)
        ab[u] = a_mat[u].astype(BF16)
        e_g = jnp.exp(g_col[u])
        rhs[u] = jnp.concatenate([beta * vc, beta * e_g * kn[u]], axis=-1)
        qe_all[u] = qb.astype(F32) * e_g
        d[u] = -(lvl_ref[0] * a_mat[u])
    for lv in range(1, n_lvl):
        m = 1 << lv
        by_rows = m % SUBLANES == 0

        def lower(t):
            return t.reshape(cu // (2 * m), 2, m, t.shape[-1])[:, 1].reshape(cu // 2, t.shape[-1])

        for u in units:
            db = d[u].astype(BF16)
            if not by_rows:
                w = a_mat[u] + _dot(db, ab[u])
                z = w + _dot(w.astype(BF16), db)
                d[u] = d[u] - lvl_ref[lv] * z
                continue
            w = lower(a_mat[u]) + _dot(lower(d[u]).astype(BF16), ab[u])
            z = w + _dot(w.astype(BF16), db)
            d_low = lower(d[u]) - lower(lvl_ref[lv]) * z
            d4 = d[u].reshape(cu // (2 * m), 2, m, cu)
            d[u] = jnp.concatenate([d4[:, :1], d_low.reshape(cu // (2 * m), 1, m, cu)],
                                   axis=1).reshape(cu, cu)
    sol, e_last = {}, {}
    for u in units:
        sol[u] = rhs[u] + _dot(d[u].astype(BF16), rhs[u].astype(BF16))
        if n_seg == 1:
            g_last = g_col[u][cu - 1:cu, :]
            k_dec_t = (kn[u] * jnp.exp(g_last - g_col[u])).T.astype(BF16)
            p_mat[u] = jnp.concatenate([p_mat[u], k_dec_t], axis=0)
            e_last[u] = jnp.exp(g_last)

    for n in range(n_units):
        for h in range(DN_H):
            u = (n, h)
            u_mat, w_mat = sol[u][:, :DN_DV], sol[u][:, DN_DV:]
            qe = qe_all[u]
            if n_seg == 1:
                b = (n * cu) // tt
                s = s_ref[b, h]
                ws_qs = _dot(jnp.concatenate([w_mat, qe], axis=0).astype(BF16), s.astype(BF16))
                both = _dot(p_mat[u], (u_mat - ws_qs[:cu]).astype(BF16))
                s_ref[b, h] = s * e_last[u] + both[cu:]
                gate = dz_ref[n * cu:(n + 1) * cu, h * DN_DV:(h + 1) * DN_DV].astype(F32)
                o_ref[n * cu:(n + 1) * cu, h * DN_DV:(h + 1) * DN_DV] = (
                    _rms_plain(ws_qs[cu:] + both[:cu]) * dnorm_ref[...] * gate).astype(BF16)
                continue
            deltas, qss = [], []
            for j in range(n_seg):
                r0 = j * blk
                b = (n * cu + r0) // tt
                sb = s_ref[b, h].astype(BF16)
                lhs = jnp.concatenate([w_mat[r0:r0 + blk], qe[r0:r0 + blk]], axis=0).astype(BF16)
                ws_qs = _dot(lhs, sb)
                deltas.append(u_mat[r0:r0 + blk] - ws_qs[:blk])
                qss.append(ws_qs[blk:])
            delta_b = jnp.concatenate(deltas, axis=0).astype(BF16)
            o = jnp.concatenate(qss, axis=0) + _dot(p_mat[u], delta_b)
            for j in range(n_seg):
                r0 = j * blk
                b = (n * cu + r0) // tt
                g_last = g_col[u][r0 + blk - 1:r0 + blk, :]
                k_dec = (kn[u][r0:r0 + blk] * jnp.exp(g_last - g_col[u][r0:r0 + blk])).astype(BF16)
                s_ref[b, h] = s_ref[b, h] * jnp.exp(g_last) + _dot_tn(k_dec, delta_b[r0:r0 + blk])
            gate = dz_ref[n * cu:(n + 1) * cu, h * DN_DV:(h + 1) * DN_DV].astype(F32)
            o_ref[n * cu:(n + 1) * cu, h * DN_DV:(h + 1) * DN_DV] = (
                _rms_plain(o) * dnorm_ref[...] * gate).astype(BF16)


def _seq_kernel(rq_ref, rk_ref, rv_ref, rg_ref, rs0_ref, dec_ref, hm_ref, qd_ref, kd_ref, cd_ref,
                dq_ref, dk_ref, dv_ref, gc_ref, gr_ref, dz_ref, dnorm_ref, ds0_ref, lblk_ref, ublk_ref,
                lvl_ref, *rest, bt, tt, ret_cu, dn_cu, n_cast):
    cast_in, (og_ref, rs_ref, od_ref, ds_ref), cast_out = rest[:n_cast], rest[n_cast:n_cast + 4], rest[n_cast + 4:]
    for src, dst in zip(cast_in, cast_out):
        dst[...] = src[...].astype(BF16)

    @pl.when(pl.program_id(1) == 0)
    def _():
        rs_ref[...] = rs0_ref[...]
        ds_ref[...] = ds0_ref[...]

    _dn_tile(dq_ref, dk_ref, dv_ref, gc_ref, gr_ref, dz_ref, dnorm_ref, lblk_ref, ublk_ref, lvl_ref,
             od_ref, ds_ref, bt=bt, tt=tt, cu=dn_cu)
    _ret_tile(rq_ref, rk_ref, rv_ref, rg_ref, dec_ref, hm_ref, qd_ref, kd_ref, cd_ref,
              og_ref, rs_ref, bt=bt, tt=tt, cu=ret_cu)


def _seq_call(rq, rk, rv, rg, rs0, ret_tables, dq, dk, dv, gbc, gbr, dz, dn_norm, ds0, dn_tables,
              bt, tt, ret_cu, dn_cu, cast_weights=()):
    bsz = ds0.shape[0]
    seq = dq.shape[0] // bsz
    nl = seq // tt
    r = bt * tt
    n_steps = (bsz // bt) * nl
    tok = lambda w: pl.BlockSpec((r, w), lambda b, l: (b * nl + l, 0))
    const = lambda a: pl.BlockSpec(a.shape, lambda b, l: (0,) * a.ndim)
    rst = pl.BlockSpec((bt, RET_QK, RET_DV), lambda b, l: (b, 0, 0))
    dst = pl.BlockSpec((bt, DN_H, DN_DK, DN_DV), lambda b, l: (b, 0, 0, 0))
    slab = lambda a: pl.BlockSpec((a.shape[0] // n_steps,) + a.shape[1:],
                                  lambda b, l: (b * nl + l,) + (0,) * (a.ndim - 1))
    outs = pl.pallas_call(
        functools.partial(_seq_kernel, bt=bt, tt=tt, ret_cu=ret_cu, dn_cu=dn_cu, n_cast=len(cast_weights)),
        grid=(bsz // bt, nl), name="seq",
        in_specs=([tok(RET_QK), tok(RET_QK), tok(RET_VW), tok(RET_VW), rst]
                  + [const(t) for t in ret_tables]
                  + [tok(DN_QK), tok(DN_QK), tok(DN_VW), tok(LANES),
                     pl.BlockSpec((GB_ROWS, r), lambda b, l: (0, b * nl + l)),
                     tok(DN_VW), const(dn_norm), dst] + [const(t) for t in dn_tables]
                  + [slab(a) for a in cast_weights]),
        out_specs=(tok(RET_VW), rst, tok(DN_VW), dst) + tuple(slab(a) for a in cast_weights),
        out_shape=(jax.ShapeDtypeStruct((bsz * seq, RET_VW), BF16), jax.ShapeDtypeStruct(rs0.shape, F32),
                   jax.ShapeDtypeStruct((bsz * seq, DN_VW), BF16), jax.ShapeDtypeStruct(ds0.shape, F32))
        + tuple(jax.ShapeDtypeStruct(a.shape, BF16) for a in cast_weights),
        compiler_params=pltpu.CompilerParams(dimension_semantics=("parallel", "arbitrary"),
                                             vmem_limit_bytes=VMEM_LIMIT),
    )(rq, rk, rv, rg, rs0, *ret_tables, dq, dk, dv, gbc, gbr, dz, dn_norm, ds0, *dn_tables, *cast_weights)
    return outs[:4], outs[4:]


_R_EXP = N_GROUPS


def _ffn_kernel(x_ref, og_ref, od_ref, ga_ref, gb_ref, p_ref,
                wru_ref, wdu_ref, wo_ref, nffn_ref, wr_ref, br_ref, wgu_ref, wdn_ref, nple_ref, wpg_ref,
                wpp_ref, nfin_ref, y_ref, hid_ref):
    branch_a = _dot(og_ref[...], wru_ref[...])
    branch_b = _dot(od_ref[...], wdu_ref[...])
    merged = (jax.nn.sigmoid(ga_ref[...].astype(F32)) * branch_a
              + jax.nn.sigmoid(gb_ref[...].astype(F32)) * branch_b)
    h = x_ref[...] + _dot(merged.astype(BF16), wo_ref[...])

    u = _rms_plain(h) * nffn_ref[...]
    ub = u.astype(BF16)

    logits = _dot(ub, wr_ref[...]) + br_ref[...]
    lane = lax.broadcasted_iota(jnp.int32, logits.shape, 1)
    neg = jnp.float32(-jnp.inf)
    big = jnp.int32(LANES)
    gl = jnp.where(lane < N_GROUPS, logits, neg)
    g_max = jnp.max(gl, axis=-1, keepdims=True)
    p_grp = 1.0 / jnp.sum(jnp.exp(gl - g_max), axis=-1, keepdims=True)
    grp = jnp.min(jnp.where(gl == g_max, lane, big), axis=-1, keepdims=True)
    e_idx = lane - _R_EXP
    in_grp = (e_idx >= 0) & (e_idx < N_EXPERTS) & ((e_idx // EXPERTS_PER_GROUP) == grp)
    el = jnp.where(in_grp, logits, neg)
    v1 = jnp.max(el, axis=-1, keepdims=True)
    i1 = jnp.min(jnp.where(el == v1, lane, big), axis=-1, keepdims=True)
    el2 = jnp.where(lane == i1, neg, el)
    v2 = jnp.max(el2, axis=-1, keepdims=True)
    i2 = jnp.min(jnp.where(el2 == v2, lane, big), axis=-1, keepdims=True)
    e2 = jnp.exp(v2 - v1)
    w1 = 1.0 / (1.0 + e2)
    w2 = e2 / (1.0 + e2)
    combine = jnp.where(lane == i1, w1 * p_grp, 0.0) + jnp.where(lane == i2, w2 * p_grp, 0.0)

    for e in range(N_EXPERTS):
        gu = _dot(ub, wgu_ref[e])
        hid = _silu(gu[:, :D_EXPERT]) * gu[:, D_EXPERT:]
        hid_ref[:, e * D_EXPERT:(e + 1) * D_EXPERT] = (
            combine[:, _R_EXP + e:_R_EXP + e + 1] * hid).astype(BF16)
    h = h + _dot(hid_ref[...], wdn_ref[...])

    u3 = (_rms_plain(h) * nple_ref[...]).astype(BF16)
    gate = jax.nn.sigmoid(_dot(u3, wpg_ref[...]))
    h = h + gate * _dot(p_ref[...].astype(BF16), wpp_ref[...])
    y_ref[...] = _rms_plain(h) * nfin_ref[...]


def _ffn_call(tok_in, weights, tm):
    n = tok_in[0].shape[0]
    tok = lambda a: pl.BlockSpec((tm, a.shape[1]), lambda i: (i, 0))
    const = lambda a: pl.BlockSpec(a.shape, lambda i: (0,) * a.ndim, pipeline_mode=pl.Buffered(1))
    return pl.pallas_call(
        _ffn_kernel, grid=(n // tm,), name="ffn",
        in_specs=[tok(a) for a in tok_in] + [const(w) for w in weights],
        out_specs=tok(tok_in[0]), out_shape=jax.ShapeDtypeStruct((n, D_MODEL), F32),
        scratch_shapes=[pltpu.VMEM((tm, N_EXPERTS * D_EXPERT), BF16)],
        compiler_params=pltpu.CompilerParams(dimension_semantics=("parallel",),
                                             vmem_limit_bytes=VMEM_LIMIT),
    )(*tok_in, *weights)


_SEQ_CAST = ("w_gate_up", "w_down", "w_out", "w_ple_gate", "w_ret_up", "w_dn_up")

def _rope_tables(pos):
    half = RET_DK // 2
    inv = 1.0 / (ROPE_BASE ** (jnp.arange(half, dtype=F32) / half))
    ang = pos.astype(F32)[:, None] * inv[None, :]
    cos, sin = jnp.cos(ang), jnp.sin(ang)
    cos_t = jnp.tile(jnp.concatenate([cos, cos], axis=-1), (1, RET_H))
    sin_t = jnp.tile(jnp.concatenate([-sin, sin], axis=-1), (1, RET_H))
    return cos_t, sin_t


def _ret_tables(r, c):
    f32 = np.float32
    log_gamma = np.log(f32(1.0) - f32(2.0) ** (f32(-5.0) - np.arange(RET_H, dtype=f32)))
    row = np.arange(r)
    idx = (row % c).astype(f32)
    diff = idx[:, None] - idx[None, :]
    causal = (diff >= 0) & ((row[:, None] // c) == (row[None, :] // c))
    lg = log_gamma[:, None, None]
    decay = np.where(causal, np.exp(np.where(causal, diff, f32(0.0)) * lg), f32(0.0))
    q_dec = np.exp((idx + f32(1.0))[None, :] * log_gamma[:, None])[..., None]
    k_dec = np.exp((f32(c - 1.0) - idx)[None, :] * log_gamma[:, None])[..., None]
    chunk_dec = np.exp(f32(c) * log_gamma)
    lane_head = np.arange(RET_QK) // RET_DK
    head_mask = (lane_head[None, :] == np.arange(RET_H)[:, None]).astype(f32)[:, None, :]
    tables = (decay.reshape(RET_H * r, r), head_mask, q_dec * head_mask, k_dec * head_mask,
              np.broadcast_to(chunk_dec[lane_head][:, None], (RET_QK, RET_DV)))
    return tuple(jnp.asarray(t, F32) for t in tables)


def _dn_tables(r, c):
    i = np.arange(r)
    ri, ci = i[:, None], i[None, :]
    same = (ri // c) == (ci // c)
    lblk = ((ri >= ci) & same).astype(np.float32)
    x = ri ^ ci
    lvl, m = [], 1
    while m < c:
        lvl.append((same & (ri > ci) & (x >= m) & (x < 2 * m)).astype(np.float32))
        m *= 2
    return jnp.asarray(lblk, BF16), jnp.asarray(lblk.T, BF16), jnp.asarray(np.stack(lvl))


def _token_mixers(x, p, s_ret, s_dn, s_conv, pos_table, wts, cfg, cast_weights=()):
    bsz, seq, _ = x.shape
    n = bsz * seq
    x2d = x.reshape(n, D_MODEL)
    cos_t, sin_t = pos_table
    (q, k, v, rg, dq, dk, dv, dz, ga, gb, gbc, gbr, conv_new) = _proj_call(
        x2d, wts["norm_mix"], wts["w_in"], wts["w_gates"], cos_t, sin_t,
        wts["alog_c"], wts["dtb_c"], wts["alog_r"], wts["dtb_r"], wts["conv_w"], s_conv, cfg["tm_proj"])

    ret_tables = _ret_tables(cfg["ret_cu"], min(cfg["tt"], cfg["ret_cu"]))
    dn_tables = _dn_tables(cfg["dn_cu"], min(cfg["tt"], cfg["dn_cu"]))
    (og, ret_new, od, dn_new), cast_out = _seq_call(
        q, k, v, rg, s_ret.reshape(bsz, RET_QK, RET_DV), ret_tables,
        dq, dk, dv, gbc, gbr, dz, wts["dn_norm"], s_dn, dn_tables,
        cfg["bt"], cfg["tt"], cfg["ret_cu"], cfg["dn_cu"], cast_weights)
    ret_new = ret_new.reshape(bsz, RET_H, RET_DK, RET_DV)

    return (x2d, og, od, ga, gb, p.reshape(n, PLE_DIM)), ret_new, dn_new, conv_new, cast_out


def _prep_weights(norm_mix, w_in, conv_w, dn_a_log, dn_dt_bias, dn_norm, w_ret_up, w_dn_up, w_out,
                  norm_ffn, w_router_group, b_router_group, w_router_expert, b_router_expert,
                  w_gate_up, w_down, norm_ple, w_ple_gate, w_ple_proj, norm_final):
    assert sum(IN_WIDTHS[:6]) == _C_FRONT and sum(IN_WIDTHS[:8]) == _C_GATES
    w_in = w_in.astype(BF16)
    w_gates = jnp.pad(jnp.concatenate([w_in[:, _C_GATES:], w_in[:, _C_FRONT:_C_GATES]], axis=-1),
                      ((0, 0), (0, LANES - GB_ROWS)))
    pad_lanes = lambda a: jnp.pad(a.astype(F32), (0, LANES - a.shape[0]))[None, :]
    pad_rows = lambda a: jnp.pad(a.astype(F32), (0, GB_ROWS - a.shape[0]))[:, None]
    w_router = jnp.pad(jnp.concatenate([w_router_group, w_router_expert], axis=-1),
                       ((0, 0), (0, LANES - N_GROUPS - N_EXPERTS))).astype(BF16)
    b_router = pad_lanes(jnp.concatenate([b_router_group, b_router_expert]))
    row = lambda a: a.astype(F32)[None, :]
    return dict(
        norm_mix=row(norm_mix), w_in=w_in, w_gates=w_gates,
        alog_c=pad_lanes(dn_a_log), dtb_c=pad_lanes(dn_dt_bias),
        alog_r=pad_rows(dn_a_log), dtb_r=pad_rows(dn_dt_bias),
        conv_w=conv_w.astype(F32), dn_norm=row(dn_norm),
        w_ret_up=w_ret_up, w_dn_up=w_dn_up, w_out=w_out,
        norm_ffn=row(norm_ffn), w_router=w_router, b_router=b_router,
        w_gate_up=w_gate_up, w_down=w_down,
        norm_ple=row(norm_ple), w_ple_gate=w_ple_gate, w_ple_proj=w_ple_proj.astype(BF16),
        norm_final=row(norm_final),
    )


def kernel(x_prompt, x_sample, p_prompt, p_sample, state_ret, state_dn, state_conv, norm_mix, w_in, conv_w, dn_a_log, dn_dt_bias, dn_norm, w_ret_up, w_dn_up, w_out, norm_ffn, w_router_group, b_router_group, w_router_expert, b_router_expert, w_gate_up, w_down, norm_ple, w_ple_gate, w_ple_proj, norm_final):
    depth = w_in.shape[0]
    assert depth == 1, "one layer: the final norm is fused into the layer's last kernel"
    bp, lp, _ = x_prompt.shape
    bs, ls, _ = x_sample.shape
    wts = _prep_weights(norm_mix[0], w_in[0], conv_w[0], dn_a_log[0], dn_dt_bias[0], dn_norm[0],
                        w_ret_up[0], w_dn_up[0], w_out[0], norm_ffn[0], w_router_group[0],
                        b_router_group[0], w_router_expert[0], b_router_expert[0], w_gate_up[0],
                        w_down[0], norm_ple[0], w_ple_gate[0], w_ple_proj[0], norm_final)

    tm = 512
    bt_s = LANES // ls
    cfg_p = dict(tm=tm, tm_proj=512, bt=1, tt=1024, ret_cu=256, dn_cu=128)
    cfg_s = dict(tm=tm, tm_proj=256, bt=bt_s, tt=ls, ret_cu=bt_s * ls, dn_cu=bt_s * ls)

    pos_p = _rope_tables(jnp.arange(lp, dtype=jnp.int32))
    cos_s, sin_s = _rope_tables(PAST_LEN + jnp.arange(ls, dtype=jnp.int32))
    pos_s = (jnp.tile(cos_s, (bs, 1)), jnp.tile(sin_s, (bs, 1)))

    zero_ret = jnp.zeros((bp, RET_H, RET_DK, RET_DV), F32)
    zero_dn = jnp.zeros((bp, DN_H, DN_DK, DN_DV), F32)
    zero_conv = jnp.zeros((bp, CONV_W - 1, DN_CONV_CH), F32)

    seq_steps_p = (bp // cfg_p["bt"]) * (lp // cfg_p["tt"])
    slabs = lambda w: w.reshape((seq_steps_p, -1) + w.shape[-1:])
    tok_p, r_p, d_p, c_p, cast_out = _token_mixers(
        x_prompt, p_prompt[0], zero_ret, zero_dn, zero_conv, pos_p, wts, cfg_p,
        cast_weights=tuple(slabs(wts[k]) for k in _SEQ_CAST))
    tok_s, r_s, d_s, c_s, _ = _token_mixers(x_sample, p_sample[0], state_ret[0], state_dn[0], state_conv[0],
                                            pos_s, wts, cfg_s)
    for k, w in zip(_SEQ_CAST, cast_out):
        wts[k] = w.reshape(wts[k].shape)
    wts["w_down"] = wts["w_down"].reshape(N_EXPERTS * D_EXPERT, D_MODEL)
    ffn_weights = [wts[k] for k in ("w_ret_up", "w_dn_up", "w_out", "norm_ffn", "w_router", "b_router",
                                    "w_gate_up", "w_down", "norm_ple", "w_ple_gate", "w_ple_proj",
                                    "norm_final")]
    y_p = _ffn_call(tok_p, ffn_weights, tm)
    y_s = _ffn_call(tok_s, ffn_weights, tm)
    return (y_p.reshape(x_prompt.shape), y_s.reshape(x_sample.shape),
            r_p[None], d_p[None], c_p[None], r_s[None], d_s[None], c_s[None])
```

```python
import functools

import jax
import jax.numpy as jnp
import numpy as np
from jax import lax
from jax.experimental import pallas as pl
from jax.experimental.pallas import tpu as pltpu

F32 = jnp.float32
BF16 = jnp.bfloat16

D_MODEL = 1024
RET_H, RET_DK, RET_DV = 4, 64, 128
DN_H, DN_DK, DN_DV = 4, 128, 128
CONV_W = 4
ROPE_BASE = 10000.0
PAST_LEN = 16384
N_GROUPS, EXPERTS_PER_GROUP = 4, 4
N_EXPERTS = N_GROUPS * EXPERTS_PER_GROUP
D_EXPERT = 256
PLE_DIM = 256
EPS = 1e-6

RET_QK = RET_H * RET_DK
RET_VW = RET_H * RET_DV
DN_QK = DN_H * DN_DK
DN_VW = DN_H * DN_DV
DN_CONV_CH = 2 * DN_QK + DN_VW
IN_WIDTHS = (RET_QK, RET_QK, RET_VW, RET_VW, DN_CONV_CH, DN_VW, DN_H, DN_H, D_MODEL, D_MODEL)

LANES = 128
SUBLANES = 8
VMEM_LIMIT = 56 * 1024 * 1024
GB_ROWS = 2 * DN_H
assert GB_ROWS == SUBLANES

_C_Q, _C_K, _C_V, _C_RG = 0, 256, 512, 1024
_C_DQKV, _C_DZ, _C_FRONT = 1536, 3072, 3584
_C_GATES = _C_FRONT + GB_ROWS


def _silu(x):
    return x * jax.nn.sigmoid(x)


def _softplus(x):
    return jnp.maximum(x, 0.0) + jnp.log1p(jnp.exp(-jnp.abs(x)))


def _dot(a, b):
    return jnp.dot(a, b, preferred_element_type=F32)


def _dot_nt(a, b):
    return lax.dot_general(a, b, (((1,), (1,)), ((), ())), preferred_element_type=F32)


def _dot_tn(a, b):
    return lax.dot_general(a, b, (((0,), (0,)), ((), ())), preferred_element_type=F32)


def _split3(x):
    hi = x.astype(BF16)
    r = x - hi.astype(F32)
    mid = r.astype(BF16)
    lo = (r - mid.astype(F32)).astype(BF16)
    return hi, mid, lo


def _rms_plain(x):
    return x * lax.rsqrt(jnp.mean(x * x, axis=-1, keepdims=True) + EPS)


def _proj_kernel(x_ref, gain_ref, w_ref, wg_ref, cos_ref, sin_ref,
                 alog_c_ref, dtb_c_ref, alog_r_ref, dtb_r_ref, cw_ref, c0_ref,
                 q_ref, k_ref, v_ref, rg_ref, dq_ref, dk_ref, dv_ref, dz_ref, ga_ref, gb_ref,
                 gbc_ref, gbr_ref, c_ref, ext_ref, *, bt, tt, nl):
    pad = SUBLANES
    tail = CONV_W - 1

    @pl.when(pl.program_id(0) % nl == 0)
    def _():
        ext_ref[:, pad - tail:pad, :] = c0_ref[...]

    x = x_ref[...]
    u = x * lax.rsqrt(jnp.mean(x * x, axis=-1, keepdims=True) + EPS) * gain_ref[...]
    ub = u.astype(BF16)

    def mm(lo, hi):
        return _dot(ub, w_ref[:, lo:hi])

    ext_ref[:, pad:pad + tt, :] = mm(_C_DQKV, _C_DZ).reshape(bt, tt, DN_CONV_CH)

    def conv_silu(c0, c1):
        acc = ext_ref[:, pad - tail:pad - tail + tt, c0:c1] * cw_ref[0:1, c0:c1]
        for j in range(1, CONV_W):
            acc = acc + ext_ref[:, pad - tail + j:pad - tail + j + tt, c0:c1] * cw_ref[j:j + 1, c0:c1]
        return _silu(acc).reshape(bt * tt, c1 - c0)

    def l2_heads(out_ref, c0, scale):
        units = []
        for h in range(DN_H):
            ch = conv_silu(c0 + h * DN_DK, c0 + (h + 1) * DN_DK)
            unit = ch * lax.rsqrt(jnp.sum(ch * ch, axis=-1, keepdims=True) + EPS)
            units.append(unit if scale is None else unit * scale)
        out_ref[...] = jnp.concatenate(units, axis=-1).astype(BF16)

    cos = cos_ref[...]
    sin = sin_ref[...]
    lane = lax.broadcasted_iota(jnp.int32, cos.shape, 1)
    first_half = (lane % RET_DK) < (RET_DK // 2)

    def rot(t):
        swapped = jnp.where(first_half, pltpu.roll(t, RET_QK - RET_DK // 2, 1),
                            pltpu.roll(t, RET_DK // 2, 1))
        return t * cos + swapped * sin

    gates = _dot(ub, wg_ref[...])
    ga_ref[...] = gates[:, :D_MODEL].astype(BF16)
    gb_ref[...] = gates[:, D_MODEL:2 * D_MODEL].astype(BF16)
    dab = gates[:, 2 * D_MODEL:]
    l2_heads(dq_ref, 0, DN_DK ** -0.5)
    l2_heads(dk_ref, DN_QK, None)
    v_ref[...] = mm(_C_V, _C_RG).astype(BF16)
    dv_ref[...] = conv_silu(2 * DN_QK, DN_CONV_CH).astype(BF16)
    new_tail = ext_ref[:, pad + tt - tail:pad + tt, :]
    ext_ref[:, pad - tail:pad, :] = new_tail
    c_ref[...] = new_tail
    qk = mm(_C_Q, _C_V)
    q_ref[...] = (rot(qk[:, :RET_QK]) * (RET_DK ** -0.5)).astype(BF16)
    k_ref[...] = rot(qk[:, RET_QK:]).astype(BF16)
    rg_ref[...] = _silu(mm(_C_RG, _C_DQKV)).astype(BF16)
    dz_ref[...] = _silu(mm(_C_DZ, _C_FRONT)).astype(BF16)

    lane_c = lax.broadcasted_iota(jnp.int32, dab.shape, 1)
    g_c = -jnp.exp(alog_c_ref[...]) * _softplus(dab + dtb_c_ref[...])
    gbc_ref[...] = jnp.where(lane_c < DN_H, g_c, jax.nn.sigmoid(dab))
    dabt = dab.T[:GB_ROWS]
    row_r = lax.broadcasted_iota(jnp.int32, dabt.shape, 0)
    g_r = -jnp.exp(alog_r_ref[...]) * _softplus(dabt + dtb_r_ref[...])
    gbr_ref[...] = jnp.where(row_r < DN_H, g_r, jax.nn.sigmoid(dabt))


def _proj_call(x2d, gain, w_in, w_gates, cos_t, sin_t, alog_c, dtb_c, alog_r, dtb_r, conv_w, c0, tm):
    n = x2d.shape[0]
    bsz = c0.shape[0]
    seq = n // bsz
    tt = min(tm, seq)
    bt = tm // tt
    nl = seq // tt
    table_blocks = cos_t.shape[0] // tm
    tok = lambda w: pl.BlockSpec((tm, w), lambda i: (i, 0))
    const = lambda a: pl.BlockSpec(a.shape, lambda i: (0,) * a.ndim, pipeline_mode=pl.Buffered(1))
    tab = pl.BlockSpec((tm, RET_QK), lambda i: (i % table_blocks, 0))
    cst = pl.BlockSpec((bt, CONV_W - 1, DN_CONV_CH), lambda i: (i // nl, 0, 0))
    bf = lambda w: jax.ShapeDtypeStruct((n, w), BF16)
    out_shapes = (bf(RET_QK), bf(RET_QK), bf(RET_VW), bf(RET_VW), bf(DN_QK), bf(DN_QK), bf(DN_VW),
                  bf(DN_VW), bf(D_MODEL), bf(D_MODEL),
                  jax.ShapeDtypeStruct((n, LANES), F32), jax.ShapeDtypeStruct((GB_ROWS, n), F32),
                  jax.ShapeDtypeStruct(c0.shape, F32))
    out_specs = (tok(RET_QK), tok(RET_QK), tok(RET_VW), tok(RET_VW), tok(DN_QK), tok(DN_QK), tok(DN_VW),
                 tok(DN_VW), tok(D_MODEL), tok(D_MODEL), tok(LANES),
                 pl.BlockSpec((GB_ROWS, tm), lambda i: (0, i)), cst)
    return pl.pallas_call(
        functools.partial(_proj_kernel, bt=bt, tt=tt, nl=nl), grid=(n // tm,), name="proj",
        in_specs=[tok(D_MODEL), const(gain),
                  pl.BlockSpec((D_MODEL, _C_FRONT), lambda i: (0, 0), pipeline_mode=pl.Buffered(1)),
                  const(w_gates), tab, tab,
                  const(alog_c), const(dtb_c), const(alog_r), const(dtb_r), const(conv_w), cst],
        out_specs=out_specs, out_shape=out_shapes,
        scratch_shapes=[pltpu.VMEM((bt, tt + SUBLANES, DN_CONV_CH), F32)],
        compiler_params=pltpu.CompilerParams(dimension_semantics=("arbitrary",),
                                             vmem_limit_bytes=VMEM_LIMIT),
    )(x2d, gain, w_in, w_gates, cos_t, sin_t, alog_c, dtb_c, alog_r, dtb_r, conv_w, c0)


def _ret_tile(q_ref, k_ref, v_ref, rg_ref, dec_ref, hm_ref, qd_ref, kd_ref, cd_ref,
              o_ref, s_ref, *, bt, tt, cu):
    r = bt * tt
    blk = min(tt, cu)
    n_units = r // cu
    n_seg = cu // blk

    heads = range(RET_H)
    hrow = lambda a, h, m: a[h * m:(h + 1) * m]
    for n in range(n_units):
        rows = slice(n * cu, (n + 1) * cu)
        qf = q_ref[rows, :].astype(F32)
        kf = k_ref[rows, :].astype(F32)
        vf = v_ref[rows, :].astype(F32)
        q_m = [qf * hm_ref[h] for h in heads]
        q_d = [qf * qd_ref[h] for h in heads]
        k_d = [kf * kd_ref[h] for h in heads]
        v_h = [vf[:, h * RET_DV:(h + 1) * RET_DV] for h in heads]
        scores = (_dot_nt(jnp.concatenate(q_m, axis=0).astype(BF16), k_ref[rows, :])
                  * dec_ref[...]).astype(BF16)
        intra = []
        for h in heads:
            intra.append(_dot(hrow(scores, h, cu), v_h[h].astype(BF16)))
        inters = []
        for j in range(n_seg):
            seg = slice(j * blk, (j + 1) * blk)
            b = (n * cu + j * blk) // tt
            s = s_ref[b]
            q_seg = jnp.concatenate([a[seg] for a in q_d], axis=0).astype(BF16)
            k_seg = jnp.concatenate([a[seg] for a in k_d], axis=0).astype(BF16)
            v_seg = jnp.concatenate([a[seg] for a in v_h], axis=0).astype(BF16)
            inters.append(_dot(q_seg, s.astype(BF16)))
            s_ref[b] = s * cd_ref[...] + _dot_tn(k_seg, v_seg)
        for h in heads:
            inter = jnp.concatenate([hrow(a, h, blk) for a in inters], axis=0)
            gate = rg_ref[rows, h * RET_DV:(h + 1) * RET_DV].astype(F32)
            o_ref[rows, h * RET_DV:(h + 1) * RET_DV] = (
                _rms_plain(intra[h] + inter) * gate).astype(BF16)


def _dn_tile(q_ref, k_ref, v_ref, gc_ref, gr_ref, dz_ref, dnorm_ref, lblk_ref, ublk_ref, lvl_ref,
             o_ref, s_ref, *, bt, tt, cu):
    r = bt * tt
    blk = min(tt, cu)
    n_units = r // cu
    n_seg = cu // blk
    n_lvl = lvl_ref.shape[0]

    ri = lax.broadcasted_iota(jnp.int32, (cu, cu), 0)
    ci = lax.broadcasted_iota(jnp.int32, (cu, cu), 1)
    same = (ri // blk) == (ci // blk)
    incl = same & (ri >= ci)
    strict = same & (ri > ci)

    gc = gc_ref[...]
    g_hi, g_mid, g_lo = _split3(gc)
    r_hi, r_mid, r_lo = _split3(gr_ref[...])
    lblk = lblk_ref[...]
    ublk = ublk_ref[...]

    units = [(n, h) for n in range(n_units) for h in range(DN_H)]
    cum_c, cum_r = [], []
    for n in range(n_units):
        rows = slice(n * cu, (n + 1) * cu)
        cum_c.append(_dot(lblk, g_hi[rows]) + _dot(lblk, g_mid[rows]) + _dot(lblk, g_lo[rows]))
        cum_r.append(_dot(r_hi[:, rows], ublk) + _dot(r_mid[:, rows], ublk) + _dot(r_lo[:, rows], ublk))

    qe_all, kn, kb, g_col, dec, a_mat, ab, d, rhs, p_mat = {}, {}, {}, {}, {}, {}, {}, {}, {}, {}
    for u in units:
        n, h = u
        rows = slice(n * cu, (n + 1) * cu)
        qb = q_ref[rows, h * DN_DK:(h + 1) * DN_DK]
        kb[u] = k_ref[rows, h * DN_DK:(h + 1) * DN_DK]
        kn[u] = kb[u].astype(F32)
        vc = v_ref[rows, h * DN_DV:(h + 1) * DN_DV].astype(F32)
        g_col[u] = cum_c[n][:, h:h + 1]
        g_row = cum_r[n][h:h + 1, :]
        beta = gc[rows, DN_H + h:DN_H + h + 1]
        dec[u] = jnp.where(incl, jnp.exp(jnp.where(incl, g_col[u] - g_row, 0.0)), 0.0)
        kq_k = _dot_nt(jnp.concatenate([kb[u], qb], axis=0), kb[u])
        a_mat[u] = jnp.where(strict, kq_k[:cu] * dec[u] * beta, 0.0)
        p_mat[u] = (kq_k[cu:] * dec[u]).astype(BF16)
        ab[u] = a_mat[u].astype(BF16)
        e_g = jnp.exp(g_col[u])
        rhs[u] = jnp.concatenate([beta * vc, beta * e_g * kn[u]], axis=-1)
        qe_all[u] = qb.astype(F32) * e_g
        d[u] = -(lvl_ref[0] * a_mat[u])
    for lv in range(1, n_lvl):
        m = 1 << lv
        by_rows = m % SUBLANES == 0

        def lower(t):
            return t.reshape(cu // (2 * m), 2, m, t.shape[-1])[:, 1].reshape(cu // 2, t.shape[-1])

        for u in units:
            db = d[u].astype(BF16)
            if not by_rows:
                w = a_mat[u] + _dot(db, ab[u])
                z = w + _dot(w.astype(BF16), db)
                d[u] = d[u] - lvl_ref[lv] * z
                continue
            w = lower(a_mat[u]) + _dot(lower(d[u]).astype(BF16), ab[u])
            z = w + _dot(w.astype(BF16), db)
            d_low = lower(d[u]) - lower(lvl_ref[lv]) * z
            d4 = d[u].reshape(cu // (2 * m), 2, m, cu)
            d[u] = jnp.concatenate([d4[:, :1], d_low.reshape(cu // (2 * m), 1, m, cu)],
                                   axis=1).reshape(cu, cu)
    sol, e_last = {}, {}
    for u in units:
        sol[u] = rhs[u] + _dot(d[u].astype(BF16), rhs[u].astype(BF16))
        if n_seg == 1:
            g_last = g_col[u][cu - 1:cu, :]
            k_dec_t = (kn[u] * jnp.exp(g_last - g_col[u])).T.astype(BF16)
            p_mat[u] = jnp.concatenate([p_mat[u], k_dec_t], axis=0)
            e_last[u] = jnp.exp(g_last)

    for n in range(n_units):
        for h in range(DN_H):
            u = (n, h)
            u_mat, w_mat = sol[u][:, :DN_DV], sol[u][:, DN_DV:]
            qe = qe_all[u]
            if n_seg == 1:
                b = (n * cu) // tt
                s = s_ref[b, h]
                ws_qs = _dot(jnp.concatenate([w_mat, qe], axis=0).astype(BF16), s.astype(BF16))
                both = _dot(p_mat[u], (u_mat - ws_qs[:cu]).astype(BF16))
                s_ref[b, h] = s * e_last[u] + both[cu:]
                gate = dz_ref[n * cu:(n + 1) * cu, h * DN_DV:(h + 1) * DN_DV].astype(F32)
                o_ref[n * cu:(n + 1) * cu, h * DN_DV:(h + 1) * DN_DV] = (
                    _rms_plain(ws_qs[cu:] + both[:cu]) * dnorm_ref[...] * gate).astype(BF16)
                continue
            deltas, qss = [], []
            for j in range(n_seg):
                r0 = j * blk
                b = (n * cu + r0) // tt
                sb = s_ref[b, h].astype(BF16)
                lhs = jnp.concatenate([w_mat[r0:r0 + blk], qe[r0:r0 + blk]], axis=0).astype(BF16)
                ws_qs = _dot(lhs, sb)
                deltas.append(u_mat[r0:r0 + blk] - ws_qs[:blk])
                qss.append(ws_qs[blk:])
            delta_b = jnp.concatenate(deltas, axis=0).astype(BF16)
            o = jnp.concatenate(qss, axis=0) + _dot(p_mat[u], delta_b)
            for j in range(n_seg):
                r0 = j * blk
                b = (n * cu + r0) // tt
                g_last = g_col[u][r0 + blk - 1:r0 + blk, :]
                k_dec = (kn[u][r0:r0 + blk] * jnp.exp(g_last - g_col[u][r0:r0 + blk])).astype(BF16)
                s_ref[b, h] = s_ref[b, h] * jnp.exp(g_last) + _dot_tn(k_dec, delta_b[r0:r0 + blk])
            gate = dz_ref[n * cu:(n + 1) * cu, h * DN_DV:(h + 1) * DN_DV].astype(F32)
            o_ref[n * cu:(n + 1) * cu, h * DN_DV:(h + 1) * DN_DV] = (
                _rms_plain(o) * dnorm_ref[...] * gate).astype(BF16)


def _seq_kernel(rq_ref, rk_ref, rv_ref, rg_ref, rs0_ref, dec_ref, hm_ref, qd_ref, kd_ref, cd_ref,
                dq_ref, dk_ref, dv_ref, gc_ref, gr_ref, dz_ref, dnorm_ref, ds0_ref, lblk_ref, ublk_ref,
                lvl_ref, *rest, bt, tt, ret_cu, dn_cu, n_cast):
    cast_in, (og_ref, rs_ref, od_ref, ds_ref), cast_out = rest[:n_cast], rest[n_cast:n_cast + 4], rest[n_cast + 4:]
    for src, dst in zip(cast_in, cast_out):
        dst[...] = src[...].astype(BF16)

    @pl.when(pl.program_id(1) == 0)
    def _():
        rs_ref[...] = rs0_ref[...]
        ds_ref[...] = ds0_ref[...]

    _dn_tile(dq_ref, dk_ref, dv_ref, gc_ref, gr_ref, dz_ref, dnorm_ref, lblk_ref, ublk_ref, lvl_ref,
             od_ref, ds_ref, bt=bt, tt=tt, cu=dn_cu)
    _ret_tile(rq_ref, rk_ref, rv_ref, rg_ref, dec_ref, hm_ref, qd_ref, kd_ref, cd_ref,
              og_ref, rs_ref, bt=bt, tt=tt, cu=ret_cu)


def _seq_call(rq, rk, rv, rg, rs0, ret_tables, dq, dk, dv, gbc, gbr, dz, dn_norm, ds0, dn_tables,
              bt, tt, ret_cu, dn_cu, cast_weights=()):
    bsz = ds0.shape[0]
    seq = dq.shape[0] // bsz
    nl = seq // tt
    r = bt * tt
    n_steps = (bsz // bt) * nl
    tok = lambda w: pl.BlockSpec((r, w), lambda b, l: (b * nl + l, 0))
    const = lambda a: pl.BlockSpec(a.shape, lambda b, l: (0,) * a.ndim)
    rst = pl.BlockSpec((bt, RET_QK, RET_DV), lambda b, l: (b, 0, 0))
    dst = pl.BlockSpec((bt, DN_H, DN_DK, DN_DV), lambda b, l: (b, 0, 0, 0))
    slab = lambda a: pl.BlockSpec((a.shape[0] // n_steps,) + a.shape[1:],
                                  lambda b, l: (b * nl + l,) + (0,) * (a.ndim - 1))
    outs = pl.pallas_call(
        functools.partial(_seq_kernel, bt=bt, tt=tt, ret_cu=ret_cu, dn_cu=dn_cu, n_cast=len(cast_weights)),
        grid=(bsz // bt, nl), name="seq",
        in_specs=([tok(RET_QK), tok(RET_QK), tok(RET_VW), tok(RET_VW), rst]
                  + [const(t) for t in ret_tables]
                  + [tok(DN_QK), tok(DN_QK), tok(DN_VW), tok(LANES),
                     pl.BlockSpec((GB_ROWS, r), lambda b, l: (0, b * nl + l)),
                     tok(DN_VW), const(dn_norm), dst] + [const(t) for t in dn_tables]
                  + [slab(a) for a in cast_weights]),
        out_specs=(tok(RET_VW), rst, tok(DN_VW), dst) + tuple(slab(a) for a in cast_weights),
        out_shape=(jax.ShapeDtypeStruct((bsz * seq, RET_VW), BF16), jax.ShapeDtypeStruct(rs0.shape, F32),
                   jax.ShapeDtypeStruct((bsz * seq, DN_VW), BF16), jax.ShapeDtypeStruct(ds0.shape, F32))
        + tuple(jax.ShapeDtypeStruct(a.shape, BF16) for a in cast_weights),
        compiler_params=pltpu.CompilerParams(dimension_semantics=("parallel", "arbitrary"),
                                             vmem_limit_bytes=VMEM_LIMIT),
    )(rq, rk, rv, rg, rs0, *ret_tables, dq, dk, dv, gbc, gbr, dz, dn_norm, ds0, *dn_tables, *cast_weights)
    return outs[:4], outs[4:]


_R_EXP = N_GROUPS


def _ffn_kernel(x_ref, og_ref, od_ref, ga_ref, gb_ref, p_ref,
                wru_ref, wdu_ref, wo_ref, nffn_ref, wr_ref, br_ref, wgu_ref, wdn_ref, nple_ref, wpg_ref,
                wpp_ref, nfin_ref, y_ref, hid_ref):
    branch_a = _dot(og_ref[...], wru_ref[...])
    branch_b = _dot(od_ref[...], wdu_ref[...])
    merged = (jax.nn.sigmoid(ga_ref[...].astype(F32)) * branch_a
              + jax.nn.sigmoid(gb_ref[...].astype(F32)) * branch_b)
    h = x_ref[...] + _dot(merged.astype(BF16), wo_ref[...])

    u = _rms_plain(h) * nffn_ref[...]
    ub = u.astype(BF16)

    logits = _dot(ub, wr_ref[...]) + br_ref[...]
    lane = lax.broadcasted_iota(jnp.int32, logits.shape, 1)
    neg = jnp.float32(-jnp.inf)
    big = jnp.int32(LANES)
    gl = jnp.where(lane < N_GROUPS, logits, neg)
    g_max = jnp.max(gl, axis=-1, keepdims=True)
    p_grp = 1.0 / jnp.sum(jnp.exp(gl - g_max), axis=-1, keepdims=True)
    grp = jnp.min(jnp.where(gl == g_max, lane, big), axis=-1, keepdims=True)
    e_idx = lane - _R_EXP
    in_grp = (e_idx >= 0) & (e_idx < N_EXPERTS) & ((e_idx // EXPERTS_PER_GROUP) == grp)
    el = jnp.where(in_grp, logits, neg)
    v1 = jnp.max(el, axis=-1, keepdims=True)
    i1 = jnp.min(jnp.where(el == v1, lane, big), axis=-1, keepdims=True)
    el2 = jnp.where(lane == i1, neg, el)
    v2 = jnp.max(el2, axis=-1, keepdims=True)
    i2 = jnp.min(jnp.where(el2 == v2, lane, big), axis=-1, keepdims=True)
    e2 = jnp.exp(v2 - v1)
    w1 = 1.0 / (1.0 + e2)
    w2 = e2 / (1.0 + e2)
    combine = jnp.where(lane == i1, w1 * p_grp, 0.0) + jnp.where(lane == i2, w2 * p_grp, 0.0)

    for e in range(N_EXPERTS):
        gu = _dot(ub, wgu_ref[e])
        hid = _silu(gu[:, :D_EXPERT]) * gu[:, D_EXPERT:]
        hid_ref[:, e * D_EXPERT:(e + 1) * D_EXPERT] = (
            combine[:, _R_EXP + e:_R_EXP + e + 1] * hid).astype(BF16)
    h = h + _dot(hid_ref[...], wdn_ref[...])

    u3 = (_rms_plain(h) * nple_ref[...]).astype(BF16)
    gate = jax.nn.sigmoid(_dot(u3, wpg_ref[...]))
    h = h + gate * _dot(p_ref[...].astype(BF16), wpp_ref[...])
    y_ref[...] = _rms_plain(h) * nfin_ref[...]


def _ffn_both_kernel(*refs, n_w, tm):
    n_tok = 6
    tok_p, tok_s = refs[:n_tok], refs[n_tok:2 * n_tok]
    w_hbm = refs[2 * n_tok:2 * n_tok + n_w]
    y_p, y_s = refs[2 * n_tok + n_w:2 * n_tok + n_w + 2]
    w_vmem = refs[2 * n_tok + n_w + 2:2 * n_tok + 2 * n_w + 2]
    hid_ref = refs[-1]
    for src, dst in zip(w_hbm, w_vmem):
        pltpu.sync_copy(src, dst)

    def body(x_ref, og_ref, od_ref, ga_ref, gb_ref, p_ref, y_ref):
        _ffn_kernel(x_ref, og_ref, od_ref, ga_ref, gb_ref, p_ref, *w_vmem, y_ref, hid_ref)

    for toks, y in ((tok_p, y_p), (tok_s, y_s)):
        pltpu.emit_pipeline(
            body, grid=(toks[0].shape[0] // tm,),
            in_specs=[pl.BlockSpec((tm, a.shape[1]), lambda i: (i, 0)) for a in toks],
            out_specs=[pl.BlockSpec((tm, D_MODEL), lambda i: (i, 0))],
        )(*toks, y)


def _ffn_both_call(tok_p, tok_s, weights, tm):
    any_spec = pl.BlockSpec(memory_space=pl.ANY)
    n_in = len(tok_p) + len(tok_s) + len(weights)
    return pl.pallas_call(
        functools.partial(_ffn_both_kernel, n_w=len(weights), tm=tm), name="ffn",
        in_specs=[any_spec] * n_in, out_specs=(any_spec, any_spec),
        out_shape=(jax.ShapeDtypeStruct((tok_p[0].shape[0], D_MODEL), F32),
                   jax.ShapeDtypeStruct((tok_s[0].shape[0], D_MODEL), F32)),
        scratch_shapes=[pltpu.VMEM(w.shape, w.dtype) for w in weights]
        + [pltpu.VMEM((tm, N_EXPERTS * D_EXPERT), BF16)],
        compiler_params=pltpu.CompilerParams(vmem_limit_bytes=VMEM_LIMIT + 2 * 1024 * 1024),
    )(*tok_p, *tok_s, *weights)


def _ffn_call(tok_in, weights, tm):
    n = tok_in[0].shape[0]
    tok = lambda a: pl.BlockSpec((tm, a.shape[1]), lambda i: (i, 0))
    const = lambda a: pl.BlockSpec(a.shape, lambda i: (0,) * a.ndim, pipeline_mode=pl.Buffered(1))
    return pl.pallas_call(
        _ffn_kernel, grid=(n // tm,), name="ffn",
        in_specs=[tok(a) for a in tok_in] + [const(w) for w in weights],
        out_specs=tok(tok_in[0]), out_shape=jax.ShapeDtypeStruct((n, D_MODEL), F32),
        scratch_shapes=[pltpu.VMEM((tm, N_EXPERTS * D_EXPERT), BF16)],
        compiler_params=pltpu.CompilerParams(dimension_semantics=("parallel",),
                                             vmem_limit_bytes=VMEM_LIMIT),
    )(*tok_in, *weights)


_SEQ_CAST = ("w_gate_up", "w_down", "w_out", "w_ple_gate", "w_ret_up", "w_dn_up")

def _rope_tables(pos):
    half = RET_DK // 2
    inv = 1.0 / (ROPE_BASE ** (jnp.arange(half, dtype=F32) / half))
    ang = pos.astype(F32)[:, None] * inv[None, :]
    cos, sin = jnp.cos(ang), jnp.sin(ang)
    cos_t = jnp.tile(jnp.concatenate([cos, cos], axis=-1), (1, RET_H))
    sin_t = jnp.tile(jnp.concatenate([-sin, sin], axis=-1), (1, RET_H))
    return cos_t, sin_t


def _ret_tables(r, c):
    f32 = np.float32
    log_gamma = np.log(f32(1.0) - f32(2.0) ** (f32(-5.0) - np.arange(RET_H, dtype=f32)))
    row = np.arange(r)
    idx = (row % c).astype(f32)
    diff = idx[:, None] - idx[None, :]
    causal = (diff >= 0) & ((row[:, None] // c) == (row[None, :] // c))
    lg = log_gamma[:, None, None]
    decay = np.where(causal, np.exp(np.where(causal, diff, f32(0.0)) * lg), f32(0.0))
    q_dec = np.exp((idx + f32(1.0))[None, :] * log_gamma[:, None])[..., None]
    k_dec = np.exp((f32(c - 1.0) - idx)[None, :] * log_gamma[:, None])[..., None]
    chunk_dec = np.exp(f32(c) * log_gamma)
    lane_head = np.arange(RET_QK) // RET_DK
    head_mask = (lane_head[None, :] == np.arange(RET_H)[:, None]).astype(f32)[:, None, :]
    tables = (decay.reshape(RET_H * r, r), head_mask, q_dec * head_mask, k_dec * head_mask,
              np.broadcast_to(chunk_dec[lane_head][:, None], (RET_QK, RET_DV)))
    return tuple(jnp.asarray(t, F32) for t in tables)


def _dn_tables(r, c):
    i = np.arange(r)
    ri, ci = i[:, None], i[None, :]
    same = (ri // c) == (ci // c)
    lblk = ((ri >= ci) & same).astype(np.float32)
    x = ri ^ ci
    lvl, m = [], 1
    while m < c:
        lvl.append((same & (ri > ci) & (x >= m) & (x < 2 * m)).astype(np.float32))
        m *= 2
    return jnp.asarray(lblk, BF16), jnp.asarray(lblk.T, BF16), jnp.asarray(np.stack(lvl))


def _token_mixers(x, p, s_ret, s_dn, s_conv, pos_table, wts, cfg, cast_weights=()):
    bsz, seq, _ = x.shape
    n = bsz * seq
    x2d = x.reshape(n, D_MODEL)
    cos_t, sin_t = pos_table
    (q, k, v, rg, dq, dk, dv, dz, ga, gb, gbc, gbr, conv_new) = _proj_call(
        x2d, wts["norm_mix"], wts["w_in"], wts["w_gates"], cos_t, sin_t,
        wts["alog_c"], wts["dtb_c"], wts["alog_r"], wts["dtb_r"], wts["conv_w"], s_conv, cfg["tm_proj"])

    ret_tables = _ret_tables(cfg["ret_cu"], min(cfg["tt"], cfg["ret_cu"]))
    dn_tables = _dn_tables(cfg["dn_cu"], min(cfg["tt"], cfg["dn_cu"]))
    (og, ret_new, od, dn_new), cast_out = _seq_call(
        q, k, v, rg, s_ret.reshape(bsz, RET_QK, RET_DV), ret_tables,
        dq, dk, dv, gbc, gbr, dz, wts["dn_norm"], s_dn, dn_tables,
        cfg["bt"], cfg["tt"], cfg["ret_cu"], cfg["dn_cu"], cast_weights)
    ret_new = ret_new.reshape(bsz, RET_H, RET_DK, RET_DV)

    return (x2d, og, od, ga, gb, p.reshape(n, PLE_DIM)), ret_new, dn_new, conv_new, cast_out


def _prep_weights(norm_mix, w_in, conv_w, dn_a_log, dn_dt_bias, dn_norm, w_ret_up, w_dn_up, w_out,
                  norm_ffn, w_router_group, b_router_group, w_router_expert, b_router_expert,
                  w_gate_up, w_down, norm_ple, w_ple_gate, w_ple_proj, norm_final):
    assert sum(IN_WIDTHS[:6]) == _C_FRONT and sum(IN_WIDTHS[:8]) == _C_GATES
    w_in = w_in.astype(BF16)
    w_gates = jnp.pad(jnp.concatenate([w_in[:, _C_GATES:], w_in[:, _C_FRONT:_C_GATES]], axis=-1),
                      ((0, 0), (0, LANES - GB_ROWS)))
    pad_lanes = lambda a: jnp.pad(a.astype(F32), (0, LANES - a.shape[0]))[None, :]
    pad_rows = lambda a: jnp.pad(a.astype(F32), (0, GB_ROWS - a.shape[0]))[:, None]
    w_router = jnp.pad(jnp.concatenate([w_router_group, w_router_expert], axis=-1),
                       ((0, 0), (0, LANES - N_GROUPS - N_EXPERTS))).astype(BF16)
    b_router = pad_lanes(jnp.concatenate([b_router_group, b_router_expert]))
    row = lambda a: a.astype(F32)[None, :]
    return dict(
        norm_mix=row(norm_mix), w_in=w_in, w_gates=w_gates,
        alog_c=pad_lanes(dn_a_log), dtb_c=pad_lanes(dn_dt_bias),
        alog_r=pad_rows(dn_a_log), dtb_r=pad_rows(dn_dt_bias),
        conv_w=conv_w.astype(F32), dn_norm=row(dn_norm),
        w_ret_up=w_ret_up, w_dn_up=w_dn_up, w_out=w_out,
        norm_ffn=row(norm_ffn), w_router=w_router, b_router=b_router,
        w_gate_up=w_gate_up, w_down=w_down,
        norm_ple=row(norm_ple), w_ple_gate=w_ple_gate, w_ple_proj=w_ple_proj.astype(BF16),
        norm_final=row(norm_final),
    )


def kernel(x_prompt, x_sample, p_prompt, p_sample, state_ret, state_dn, state_conv, norm_mix, w_in, conv_w, dn_a_log, dn_dt_bias, dn_norm, w_ret_up, w_dn_up, w_out, norm_ffn, w_router_group, b_router_group, w_router_expert, b_router_expert, w_gate_up, w_down, norm_ple, w_ple_gate, w_ple_proj, norm_final):
    depth = w_in.shape[0]
    assert depth == 1, "one layer: the final norm is fused into the layer's last kernel"
    bp, lp, _ = x_prompt.shape
    bs, ls, _ = x_sample.shape
    wts = _prep_weights(norm_mix[0], w_in[0], conv_w[0], dn_a_log[0], dn_dt_bias[0], dn_norm[0],
                        w_ret_up[0], w_dn_up[0], w_out[0], norm_ffn[0], w_router_group[0],
                        b_router_group[0], w_router_expert[0], b_router_expert[0], w_gate_up[0],
                        w_down[0], norm_ple[0], w_ple_gate[0], w_ple_proj[0], norm_final)

    tm = 512
    bt_s = LANES // ls
    cfg_p = dict(tm=tm, tm_proj=512, bt=1, tt=1024, ret_cu=256, dn_cu=128)
    cfg_s = dict(tm=tm, tm_proj=256, bt=bt_s, tt=ls, ret_cu=bt_s * ls, dn_cu=bt_s * ls)

    pos_p = _rope_tables(jnp.arange(lp, dtype=jnp.int32))
    cos_s, sin_s = _rope_tables(PAST_LEN + jnp.arange(ls, dtype=jnp.int32))
    pos_s = (jnp.tile(cos_s, (bs, 1)), jnp.tile(sin_s, (bs, 1)))

    zero_ret = jnp.zeros((bp, RET_H, RET_DK, RET_DV), F32)
    zero_dn = jnp.zeros((bp, DN_H, DN_DK, DN_DV), F32)
    zero_conv = jnp.zeros((bp, CONV_W - 1, DN_CONV_CH), F32)

    seq_steps_p = (bp // cfg_p["bt"]) * (lp // cfg_p["tt"])
    slabs = lambda w: w.reshape((seq_steps_p, -1) + w.shape[-1:])
    tok_p, r_p, d_p, c_p, cast_out = _token_mixers(
        x_prompt, p_prompt[0], zero_ret, zero_dn, zero_conv, pos_p, wts, cfg_p,
        cast_weights=tuple(slabs(wts[k]) for k in _SEQ_CAST))
    tok_s, r_s, d_s, c_s, _ = _token_mixers(x_sample, p_sample[0], state_ret[0], state_dn[0], state_conv[0],
                                            pos_s, wts, cfg_s)
    for k, w in zip(_SEQ_CAST, cast_out):
        wts[k] = w.reshape(wts[k].shape)
    wts["w_down"] = wts["w_down"].reshape(N_EXPERTS * D_EXPERT, D_MODEL)
    ffn_weights = [wts[k] for k in ("w_ret_up", "w_dn_up", "w_out", "norm_ffn", "w_router", "b_router",
                                    "w_gate_up", "w_down", "norm_ple", "w_ple_gate", "w_ple_proj",
                                    "norm_final")]
    y_p, y_s = _ffn_both_call(tok_p, tok_s, ffn_weights, tm)
    return (y_p.reshape(x_prompt.shape), y_s.reshape(x_sample.shape),
            r_p[None], d_p[None], c_p[None], r_s[None], d_s[None], c_s[None])
```
